```python
import math
import jax, jax.numpy as jnp
from jax import lax
import numpy as np

D_MODEL = 1024
BATCH = 2
SEQ = 8192
DEPTH = 4
DEC_BATCH = 32
DEC_SEQ = 32
PAST_LEN = 1024

CHUNK = 64
N_HEADS = 8
HEAD_DIM = D_MODEL // N_HEADS
N_KV_HEADS = 2
N_IDX_HEADS = 8
IDX_DIM = 64
TOPK_MAX = 256
Q_BLOCK = 128
SGU_CHUNK = 128
SGU_GROUPS = 8
SGU_WIDTH = D_MODEL
SGU_GROUP_DIM = SGU_WIDTH // SGU_GROUPS
ATT_WIDTH = N_HEADS * HEAD_DIM
KV_WIDTH = N_KV_HEADS * HEAD_DIM
D_FF = ((8 * D_MODEL + 3 * 256 - 1) // (3 * 256)) * 256
ROPE_THETA = 10000.0
EPS = 1e-6
IN_WIDTHS = (ATT_WIDTH, KV_WIDTH, KV_WIDTH, N_IDX_HEADS * IDX_DIM, IDX_DIM, N_IDX_HEADS,
             SGU_WIDTH, SGU_WIDTH, D_MODEL, D_MODEL)
IN_DIM = sum(IN_WIDTHS)

kernel_name = "streaming_dsa_gmlp_hybrid"


def _split_points():
    pts, acc = [], 0
    for w in IN_WIDTHS[:-1]:
        acc += w
        pts.append(acc)
    return pts


def rms_norm(x, g):
    xf = x.astype(jnp.float32)
    y = xf * lax.rsqrt(jnp.mean(xf * xf, axis=-1, keepdims=True) + EPS)
    return (y * g.astype(jnp.float32)).astype(x.dtype)


def layer_norm(x, g, b):
    xf = x.astype(jnp.float32)
    mu = jnp.mean(xf, axis=-1, keepdims=True)
    var = jnp.mean(jnp.square(xf - mu), axis=-1, keepdims=True)
    y = (xf - mu) * lax.rsqrt(var + EPS) * g.astype(jnp.float32) + b.astype(jnp.float32)
    return y.astype(x.dtype)


def rope(x, pos):
    d = x.shape[-1]
    inv = ROPE_THETA ** (-jnp.arange(0, d, 2, dtype=jnp.float32) / d)
    ang = pos.astype(jnp.float32)[:, None] * inv[None, :]
    cos = jnp.cos(ang)[:, None, :]
    sin = jnp.sin(ang)[:, None, :]
    xf = x.astype(jnp.float32)
    x1, x2 = xf[..., : d // 2], xf[..., d // 2:]
    return jnp.concatenate([x1 * cos - x2 * sin, x2 * cos + x1 * sin], axis=-1).astype(x.dtype)


def dsa_attend(q, qi, wi, q_pos, k, v, ki, k_pos, k_top):
    B, T = q.shape[0], q.shape[1]
    f32 = jnp.float32
    vis = (k_pos[None, :] // CHUNK) <= (q_pos[:, None] // CHUNK)
    dots = jnp.einsum('bthd,bsd->bths', qi.astype(f32), ki.astype(f32)) * (IDX_DIM ** -0.5)
    score = jnp.einsum('bth,bths->bts', wi.astype(f32), jax.nn.relu(dots))
    score = jnp.where(vis[None], score, -jnp.inf)
    _, idx = lax.top_k(score, k_top)
    valid = (k_pos[idx] // CHUNK) <= (q_pos[None, :, None] // CHUNK)
    k_sel = jax.vmap(lambda kb, ib: kb[ib])(k, idx)
    v_sel = jax.vmap(lambda vb, ib: vb[ib])(v, idx)
    g = N_HEADS // N_KV_HEADS
    qg = q.reshape(B, T, N_KV_HEADS, g, HEAD_DIM).astype(f32)
    s = jnp.einsum('btngd,btsnd->btngs', qg, k_sel.astype(f32)) * (HEAD_DIM ** -0.5)
    s = jnp.where(valid[:, :, None, None, :], s, -jnp.inf)
    p = jax.nn.softmax(s, axis=-1)
    o = jnp.einsum('btngs,btsnd->btngd', p, v_sel.astype(f32))
    return o.reshape(B, T, ATT_WIDTH).astype(q.dtype)


def prompt_sparse_attention(q, qi, wi, pos, k, v, ki, k_top):
    B, T = q.shape[0], q.shape[1]
    nb = T // Q_BLOCK

    def blocks(t):
        return jnp.moveaxis(t.reshape((B, nb, Q_BLOCK) + t.shape[2:]), 1, 0)

    def one(args):
        qb, qib, wib, pb = args
        return dsa_attend(qb, qib, wib, pb, k, v, ki, pos, k_top)

    out = lax.map(one, (blocks(q), blocks(qi), blocks(wi), pos.reshape(nb, Q_BLOCK)))
    return jnp.moveaxis(out, 0, 1).reshape(B, T, ATT_WIDTH)


def spatial_gating(u, vn, w_s, b_s, n):
    B, T = u.shape[0], u.shape[1]
    p = jnp.arange(n)
    mask = (p[None, :] // CHUNK) <= (p[:, None] // CHUNK)
    w = jnp.where(mask[None], w_s[:, :n, :n], 0.0)
    vc = vn.reshape(B, T // n, n, SGU_GROUPS, SGU_GROUP_DIM)
    mixed = jnp.einsum('gij,bcjgd->bcigd', w, vc) + b_s[:, :n].T[None, None, :, :, None]
    return u * mixed.reshape(B, T, SGU_WIDTH)


def block_forward(x, pos, past, k_top, sgu_n, norm1_g, w_in, ln_v_g, ln_v_b, sgu_w, sgu_b,
                  w_out, norm2_g, w_gate, w_up, w_down):
    B, T = x.shape[0], x.shape[1]
    h = rms_norm(x, norm1_g)
    proj = h @ w_in
    q, k, v, qi, ki, wi, u, vb, ga, gb = jnp.split(proj, _split_points(), axis=-1)
    q = rope(q.reshape(B, T, N_HEADS, HEAD_DIM), pos)
    k = rope(k.reshape(B, T, N_KV_HEADS, HEAD_DIM), pos)
    v = v.reshape(B, T, N_KV_HEADS, HEAD_DIM)
    qi = rope(qi.reshape(B, T, N_IDX_HEADS, IDX_DIM), pos)
    ki = rope(ki.reshape(B, T, 1, IDX_DIM), pos)[:, :, 0]
    wi = wi * (N_IDX_HEADS ** -0.5)
    if past is None:
        a = prompt_sparse_attention(q, qi, wi, pos, k, v, ki, k_top)
    else:
        pk, pv, pki = past
        k_all = jnp.concatenate([pk, k], axis=1)
        v_all = jnp.concatenate([pv, v], axis=1)
        ki_all = jnp.concatenate([pki, ki], axis=1)
        k_pos = jnp.arange(pk.shape[1] + T)
        a = dsa_attend(q, qi, wi, pos, k_all, v_all, ki_all, k_pos, k_top)
    u = jax.nn.gelu(u)
    vn = layer_norm(jax.nn.gelu(vb), ln_v_g, ln_v_b)
    b_out = spatial_gating(u, vn, sgu_w, sgu_b, sgu_n)
    z = jax.nn.sigmoid(ga) * a + jax.nn.sigmoid(gb) * b_out
    x = x + z @ w_out
    h2 = rms_norm(x, norm2_g)
    x = x + (jax.nn.silu(h2 @ w_gate) * (h2 @ w_up)) @ w_down
    return x, k, v, ki, vn


def setup_inputs(seed: int = 0) -> dict:
    key = jax.random.key(seed)
    ks = jax.random.split(key, 20)
    f32 = jnp.float32
    nrm = lambda k, shape, s: jax.random.normal(k, shape, f32) * s
    return {
        "x_prompt": nrm(ks[0], (BATCH, SEQ, D_MODEL), 1.0),
        "x_sample": nrm(ks[1], (DEC_BATCH, DEC_SEQ, D_MODEL), 1.0),
        "cache_k": nrm(ks[2], (DEPTH, DEC_BATCH, PAST_LEN, N_KV_HEADS, HEAD_DIM), 1.0),
        "cache_v": nrm(ks[3], (DEPTH, DEC_BATCH, PAST_LEN, N_KV_HEADS, HEAD_DIM), 1.0),
        "cache_kidx": nrm(ks[4], (DEPTH, DEC_BATCH, PAST_LEN, IDX_DIM), 1.0),
        "norm1_g": 1.0 + nrm(ks[5], (DEPTH, D_MODEL), 0.02),
        "w_in": nrm(ks[6], (DEPTH, D_MODEL, IN_DIM), D_MODEL ** -0.5),
        "ln_v_g": 1.0 + nrm(ks[7], (DEPTH, SGU_WIDTH), 0.02),
        "ln_v_b": nrm(ks[8], (DEPTH, SGU_WIDTH), 0.02),
        "sgu_w": nrm(ks[9], (DEPTH, SGU_GROUPS, SGU_CHUNK, SGU_CHUNK), SGU_CHUNK ** -0.5),
        "sgu_b": 1.0 + nrm(ks[10], (DEPTH, SGU_GROUPS, SGU_CHUNK), 0.1),
        "w_out": nrm(ks[11], (DEPTH, D_MODEL, D_MODEL), D_MODEL ** -0.5),
        "norm2_g": 1.0 + nrm(ks[12], (DEPTH, D_MODEL), 0.02),
        "w_gate": nrm(ks[13], (DEPTH, D_MODEL, D_FF), D_MODEL ** -0.5),
        "w_up": nrm(ks[14], (DEPTH, D_MODEL, D_FF), D_MODEL ** -0.5),
        "w_down": nrm(ks[15], (DEPTH, D_FF, D_MODEL), D_FF ** -0.5),
        "final_norm_g": 1.0 + nrm(ks[16], (D_MODEL,), 0.02),
    }


def reference(x_prompt, x_sample, cache_k, cache_v, cache_kidx, norm1_g, w_in, ln_v_g, ln_v_b,
              sgu_w, sgu_b, w_out, norm2_g, w_gate, w_up, w_down, final_norm_g):
    S = x_prompt.shape[1]
    Tn = x_sample.shape[1]
    P = cache_k.shape[2]
    pos_p = jnp.arange(S)
    pos_s = P + jnp.arange(Tn)
    k_top_p = min(TOPK_MAX, S // 4)
    k_top_s = min(TOPK_MAX, (P + Tn) // 4)
    xp, xs = x_prompt, x_sample
    kp, vp, kip, ksm, vsm, kism, vns = [], [], [], [], [], [], []
    for l in range(DEPTH):
        lw = (norm1_g[l], w_in[l], ln_v_g[l], ln_v_b[l], sgu_w[l], sgu_b[l], w_out[l],
              norm2_g[l], w_gate[l], w_up[l], w_down[l])
        xp, k1, v1, ki1, _ = block_forward(xp, pos_p, None, k_top_p, SGU_CHUNK, *lw)
        xs, k2, v2, ki2, vn2 = block_forward(
            xs, pos_s, (cache_k[l], cache_v[l], cache_kidx[l]), k_top_s, Tn, *lw)
        kp.append(k1); vp.append(v1); kip.append(ki1)
        ksm.append(k2); vsm.append(v2); kism.append(ki2); vns.append(vn2)
    y_prompt = rms_norm(xp, final_norm_g)
    y_sample = rms_norm(xs, final_norm_g)
    return (y_prompt, y_sample, jnp.stack(kp), jnp.stack(vp), jnp.stack(kip),
            jnp.stack(ksm), jnp.stack(vsm), jnp.stack(kism), jnp.stack(vns))
```

```python
import functools
import math

import jax
import jax.numpy as jnp
from jax import lax
from jax.experimental import pallas as pl
from jax.experimental.pallas import tpu as pltpu

CHUNK = 64
N_HEADS = 8
HEAD_DIM = 128
N_KV_HEADS = 2
KV_GROUP = N_HEADS // N_KV_HEADS
N_IDX_HEADS = 8
IDX_DIM = 64
TOPK_MAX = 256
SGU_CHUNK = 128
SGU_GROUPS = 8
ROPE_THETA = 10000.0
EPS = 1e-6

LANES = 128
TOKEN_TILE = 256
Q_TILE = 256
KEY_TILE = 256
VMEM_LIMIT = 56 * 1024 * 1024

INT_MIN = -(2 ** 31)
INT_MAX = 2 ** 31 - 1
MASK_BIAS = -1e30
LOG2E = 1.4426950408889634

F32 = jnp.float32
BF16 = jnp.bfloat16
NT_DIMS = (((1,), (1,)), ((), ()))


def _const_spec(shape):
    nd = len(shape)
    return pl.BlockSpec(shape, lambda *_: (0,) * nd, pipeline_mode=pl.Buffered(1))


def _sigmoid(x):
    return 1.0 / (1.0 + jnp.exp(-x))


def _rms_norm(x, g):
    return x * lax.rsqrt(jnp.mean(x * x, axis=-1, keepdims=True) + EPS) * g


def _order_key(score):
    bits = pltpu.bitcast(score, jnp.int32)
    return bits ^ ((bits >> 31) & INT_MAX)


def _midpoint(lo, hi):
    return (lo >> 1) + (hi >> 1) + (lo & hi & 1)


_C_Q = 0
_C_K = _C_Q + N_HEADS * HEAD_DIM
_C_V = _C_K + N_KV_HEADS * HEAD_DIM
_C_QI = _C_V + N_KV_HEADS * HEAD_DIM
_C_KI = _C_QI + N_IDX_HEADS * IDX_DIM
_C_WI = _C_KI + LANES
_C_U = _C_WI + LANES


def _inproj_kernel(x_ref, g_ref, wa_ref, wb_ref, cosh_ref, sinh_ref, cosi_ref, sini_ref,
                   lng_ref, lnb_ref,
                   q_ref, kf_ref, kb_ref, vf_ref, vt_ref, qi_ref, kif_ref, kze_ref, kzo_ref,
                   wi_ref, wit_ref, u_ref, vn_ref, sga_ref, sgb_ref, *, d_model):
    hb = _rms_norm(x_ref[...], g_ref[...]).astype(BF16)

    def proj(c0, width):
        return jnp.dot(hb, wa_ref[:, c0:c0 + width], preferred_element_type=F32)

    cosh, sinh = cosh_ref[...], sinh_ref[...]
    cosi, sini = cosi_ref[...], sini_ref[...]
    lane = lax.broadcasted_iota(jnp.int32, cosi.shape, 1)
    first_half = (lane % IDX_DIM) < (IDX_DIM // 2)

    def rope_head(x):
        return x * cosh + pltpu.roll(x, HEAD_DIM // 2, 1) * sinh

    def rope_idx(x):
        partner = jnp.where(first_half, pltpu.roll(x, LANES - IDX_DIM // 2, 1),
                            pltpu.roll(x, IDX_DIM // 2, 1))
        return x * cosi + partner * sini

    xq = proj(_C_Q, N_HEADS * HEAD_DIM)
    for h in range(N_HEADS):
        sl = slice(h * HEAD_DIM, (h + 1) * HEAD_DIM)
        q_ref[:, sl] = rope_head(xq[:, sl]).astype(BF16)

    xk = proj(_C_K, N_KV_HEADS * HEAD_DIM)
    for h in range(N_KV_HEADS):
        sl = slice(h * HEAD_DIM, (h + 1) * HEAD_DIM)
        kr = rope_head(xk[:, sl])
        kf_ref[:, sl] = kr
        kb_ref[:, sl] = kr.astype(BF16)

    vf_ref[...] = proj(_C_V, N_KV_HEADS * HEAD_DIM)

    xqi = proj(_C_QI, N_IDX_HEADS * IDX_DIM)
    for p in range(N_IDX_HEADS * IDX_DIM // LANES):
        sl = slice(p * LANES, (p + 1) * LANES)
        qi_ref[:, sl] = (rope_idx(xqi[:, sl]) * (IDX_DIM ** -0.5)).astype(BF16)

    kk = rope_idx(proj(_C_KI, LANES))
    kif_ref[...] = kk[:, :IDX_DIM]
    low = lane < IDX_DIM
    kze_ref[...] = jnp.where(low, kk, 0.0).astype(BF16)
    kzo_ref[...] = jnp.where(low, 0.0, kk).astype(BF16)

    wi_ref[...] = proj(_C_WI, LANES) * (N_IDX_HEADS ** -0.5)

    tb = lax.dot_general(wb_ref[...], hb, NT_DIMS, preferred_element_type=F32)
    kvw = N_KV_HEADS * HEAD_DIM
    vt_ref[0] = tb[:kvw].astype(BF16)
    wit_ref[...] = tb[kvw:] * (N_IDX_HEADS ** -0.5)

    u_ref[...] = jax.nn.gelu(proj(_C_U, d_model), approximate=True)
    gv = jax.nn.gelu(proj(_C_U + d_model, d_model), approximate=True)
    mu = jnp.mean(gv, axis=-1, keepdims=True)
    dv = gv - mu
    var = jnp.mean(dv * dv, axis=-1, keepdims=True)
    vn_ref[...] = dv * lax.rsqrt(var + EPS) * lng_ref[...] + lnb_ref[...]
    sga_ref[...] = _sigmoid(proj(_C_U + 2 * d_model, d_model))
    sgb_ref[...] = _sigmoid(proj(_C_U + 3 * d_model, d_model))


def _inproj(x, g, wa, wb, tables, lng, lnb):
    T, D = x.shape
    tm = TOKEN_TILE
    assert T % tm == 0
    kvw = N_KV_HEADS * HEAD_DIM
    qiw = N_IDX_HEADS * IDX_DIM
    row = lambda w: pl.BlockSpec((tm, w), lambda i: (i, 0))
    out_shape = (
        jax.ShapeDtypeStruct((T, N_HEADS * HEAD_DIM), BF16),
        jax.ShapeDtypeStruct((T, kvw), F32),
        jax.ShapeDtypeStruct((T, kvw), BF16),
        jax.ShapeDtypeStruct((T, kvw), F32),
        jax.ShapeDtypeStruct((T // tm, kvw, tm), BF16),
        jax.ShapeDtypeStruct((T, qiw), BF16),
        jax.ShapeDtypeStruct((T, IDX_DIM), F32),
        jax.ShapeDtypeStruct((T, LANES), BF16),
        jax.ShapeDtypeStruct((T, LANES), BF16),
        jax.ShapeDtypeStruct((T, LANES), F32),
        jax.ShapeDtypeStruct((16, T), F32),
        jax.ShapeDtypeStruct((T, D), F32),
        jax.ShapeDtypeStruct((T, D), F32),
        jax.ShapeDtypeStruct((T, D), F32),
        jax.ShapeDtypeStruct((T, D), F32),
    )
    out_specs = (
        row(N_HEADS * HEAD_DIM), row(kvw), row(kvw), row(kvw),
        pl.BlockSpec((1, kvw, tm), lambda i: (i, 0, 0)),
        row(qiw), row(IDX_DIM), row(LANES), row(LANES), row(LANES),
        pl.BlockSpec((16, tm), lambda i: (0, i)),
        row(D), row(D), row(D), row(D),
    )
    in_specs = [row(D), _const_spec((1, D)), _const_spec(wa.shape), _const_spec(wb.shape),
                row(LANES), row(LANES), row(LANES), row(LANES),
                _const_spec((1, D)), _const_spec((1, D))]
    return pl.pallas_call(
        functools.partial(_inproj_kernel, d_model=D),
        grid=(T // tm,),
        in_specs=in_specs, out_specs=out_specs, out_shape=out_shape,
        compiler_params=pltpu.CompilerParams(dimension_semantics=("arbitrary",),
                                             vmem_limit_bytes=VMEM_LIMIT),
        name="inproj",
    )(x, g, wa, wb, *tables, lng, lnb)


def _attn_prompt_kernel(q_ref, qi_ref, wit_ref, kze_ref, kzo_ref, k_ref, vt_ref, o_ref,
                        keys_ref, acc_ref, m_ref, l_ref, *, k_top):
    tq, kb_sz = Q_TILE, KEY_TILE
    qb = pl.program_id(1)
    nkb = (qb + 1) * (tq // kb_sz)
    q_chunk = (qb * tq + lax.broadcasted_iota(jnp.int32, (1, tq), 1)) // CHUNK
    row_iota = lax.broadcasted_iota(jnp.int32, (kb_sz, 1), 0)

    def key_slice(kb):
        return pl.ds(pl.multiple_of(kb * kb_sz, kb_sz), kb_sz)

    def score_body(kb, carry):
        ks = key_slice(kb)
        kze, kzo = kze_ref[ks, :], kzo_ref[ks, :]
        score = jnp.zeros((kb_sz, tq), F32)
        for p in range(N_IDX_HEADS // 2):
            qp = qi_ref[:, p * LANES:(p + 1) * LANES]
            de = lax.dot_general(kze, qp, NT_DIMS, preferred_element_type=F32)
            do = lax.dot_general(kzo, qp, NT_DIMS, preferred_element_type=F32)
            score = score + wit_ref[2 * p:2 * p + 1, :] * jnp.maximum(de, 0.0)
            score = score + wit_ref[2 * p + 1:2 * p + 2, :] * jnp.maximum(do, 0.0)
        k_chunk = (kb * kb_sz + row_iota) // CHUNK
        keys_ref[ks, :] = jnp.where(k_chunk <= q_chunk, _order_key(score), INT_MIN)
        return carry

    lax.fori_loop(0, nkb, score_body, 0)

    def bisect_body(_, carry):
        lo, hi = carry
        mid = _midpoint(lo, hi)

        def count_body(kb, c):
            ge = (keys_ref[key_slice(kb), :] >= mid).astype(jnp.int32)
            return c + jnp.sum(ge.reshape(kb_sz // 8, 8, tq), axis=0)

        c = lax.fori_loop(0, nkb, count_body, jnp.zeros((8, tq), jnp.int32))
        ok = jnp.sum(c, axis=0, keepdims=True) >= k_top
        return jnp.where(ok, mid, lo), jnp.where(ok, hi, mid)

    thr, _ = lax.fori_loop(
        0, 32, bisect_body,
        (jnp.full((1, tq), INT_MIN + 1, jnp.int32), jnp.full((1, tq), INT_MAX, jnp.int32)))

    m_ref[...] = jnp.full(m_ref.shape, MASK_BIAS, F32)
    l_ref[...] = jnp.zeros(l_ref.shape, F32)
    acc_ref[...] = jnp.zeros(acc_ref.shape, F32)
    c_exp = (HEAD_DIM ** -0.5) * LOG2E

    def attn_body(kb, carry):
        ks = key_slice(kb)
        bias = jnp.where(keys_ref[ks, :] >= thr, 0.0, MASK_BIAS)
        for n in range(N_KV_HEADS):
            k_n = k_ref[ks, n * HEAD_DIM:(n + 1) * HEAD_DIM]
            vt_n = vt_ref[kb, n * HEAD_DIM:(n + 1) * HEAD_DIM, :]
            for g in range(KV_GROUP):
                h = n * KV_GROUP + g
                q_h = q_ref[:, h * HEAD_DIM:(h + 1) * HEAD_DIM]
                s = lax.dot_general(k_n, q_h, NT_DIMS, preferred_element_type=F32) + bias
                m_old = m_ref[h:h + 1, :]
                m_new = jnp.maximum(m_old, jnp.max(s, axis=0, keepdims=True))
                p = jnp.exp2((s - m_new) * c_exp)
                alpha = jnp.exp2((m_old - m_new) * c_exp)
                m_ref[h:h + 1, :] = m_new
                l_ref[h:h + 1, :] = alpha * l_ref[h:h + 1, :] + jnp.sum(p, axis=0, keepdims=True)
                pv = jnp.dot(vt_n, p.astype(BF16), preferred_element_type=F32)
                acc_ref[h] = alpha * acc_ref[h] + pv
        return carry

    lax.fori_loop(0, nkb, attn_body, 0)

    for h in range(N_HEADS):
        o_t = acc_ref[h] / l_ref[h:h + 1, :]
        o_ref[:, h * HEAD_DIM:(h + 1) * HEAD_DIM] = o_t.T


def _attn_prompt(q, qi, wit, kze, kzo, kb, vt, batch, seq, k_top):
    tq = Q_TILE
    assert seq % tq == 0 and Q_TILE % KEY_TILE == 0 and KEY_TILE == TOKEN_TILE
    nq = seq // tq
    kvw = N_KV_HEADS * HEAD_DIM
    qrow = lambda w: pl.BlockSpec((tq, w), lambda b, i: (b * nq + i, 0))
    per_batch = lambda w: pl.BlockSpec((seq, w), lambda b, i: (b, 0), pipeline_mode=pl.Buffered(1))
    return pl.pallas_call(
        functools.partial(_attn_prompt_kernel, k_top=k_top),
        grid=(batch, nq),
        in_specs=[qrow(N_HEADS * HEAD_DIM), qrow(N_IDX_HEADS * IDX_DIM),
                  pl.BlockSpec((16, tq), lambda b, i: (0, b * nq + i)),
                  per_batch(LANES), per_batch(LANES), per_batch(kvw),
                  pl.BlockSpec((seq // KEY_TILE, kvw, KEY_TILE), lambda b, i: (b, 0, 0),
                               pipeline_mode=pl.Buffered(1))],
        out_specs=qrow(N_HEADS * HEAD_DIM),
        out_shape=jax.ShapeDtypeStruct((batch * seq, N_HEADS * HEAD_DIM), F32),
        scratch_shapes=[pltpu.VMEM((seq, tq), jnp.int32),
                        pltpu.VMEM((N_HEADS, HEAD_DIM, tq), F32),
                        pltpu.VMEM((N_HEADS, tq), F32),
                        pltpu.VMEM((N_HEADS, tq), F32)],
        compiler_params=pltpu.CompilerParams(dimension_semantics=("arbitrary", "arbitrary"),
                                             vmem_limit_bytes=VMEM_LIMIT),
        name="attn_prompt",
    )(q, qi, wit, kze, kzo, kb, vt)


def _attn_sample_kernel(q_ref, qi_ref, wi_ref, ck_ref, cv_ref, cki_ref, kn_ref, vn_ref, kin_ref,
                        o_ref, *, k_top, past_len, tn):
    b = pl.program_id(0)
    group = LANES // tn
    ck = ck_ref[0].astype(BF16)
    cv = cv_ref[0].astype(BF16)
    cki = cki_ref[0].astype(BF16)
    kn = kn_ref[...]
    vn = vn_ref[...].astype(BF16)
    kin = kin_ref[...].astype(BF16)

    sp = jnp.zeros((tn, past_len), F32)
    sn = jnp.zeros((tn, LANES), F32)
    for h in range(N_IDX_HEADS):
        qh = qi_ref[0, h]
        w = wi_ref[:, h:h + 1]
        dp = lax.dot_general(qh, cki, NT_DIMS, preferred_element_type=F32)
        dn = lax.dot_general(qh, kin, NT_DIMS, preferred_element_type=F32)
        sp = sp + w * jnp.maximum(dp, 0.0)
        sn = sn + w * jnp.maximum(dn, 0.0)

    j = lax.broadcasted_iota(jnp.int32, (tn, LANES), 1)
    t = lax.broadcasted_iota(jnp.int32, (tn, LANES), 0)
    own = (j // tn) == (b % group)
    vis = ((past_len + j % tn) // CHUNK) <= ((past_len + t) // CHUNK)
    key_p = _order_key(sp)
    key_n = jnp.where(own & vis, _order_key(sn), INT_MIN)

    def bisect_body(_, carry):
        lo, hi = carry
        mid = _midpoint(lo, hi)
        cnt = (jnp.sum((key_p >= mid).astype(jnp.int32), axis=1, keepdims=True)
               + jnp.sum((key_n >= mid).astype(jnp.int32), axis=1, keepdims=True))
        ok = cnt >= k_top
        return jnp.where(ok, mid, lo), jnp.where(ok, hi, mid)

    thr, _ = lax.fori_loop(
        0, 32, bisect_body,
        (jnp.full((tn, 1), INT_MIN + 1, jnp.int32), jnp.full((tn, 1), INT_MAX, jnp.int32)))

    bias_p = jnp.where(key_p >= thr, 0.0, MASK_BIAS)
    bias_n = jnp.where(key_n >= thr, 0.0, MASK_BIAS)
    c_exp = (HEAD_DIM ** -0.5) * LOG2E
    for h in range(N_HEADS):
        n = h // KV_GROUP
        hs = slice(n * HEAD_DIM, (n + 1) * HEAD_DIM)
        q_h = q_ref[:, h * HEAD_DIM:(h + 1) * HEAD_DIM]
        s1 = lax.dot_general(q_h, ck[:, hs], NT_DIMS, preferred_element_type=F32) + bias_p
        s2 = lax.dot_general(q_h, kn[:, hs], NT_DIMS, preferred_element_type=F32) + bias_n
        m = jnp.maximum(jnp.max(s1, axis=1, keepdims=True), jnp.max(s2, axis=1, keepdims=True))
        p1 = jnp.exp2((s1 - m) * c_exp)
        p2 = jnp.exp2((s2 - m) * c_exp)
        l = jnp.sum(p1, axis=1, keepdims=True) + jnp.sum(p2, axis=1, keepdims=True)
        o = (jnp.dot(p1.astype(BF16), cv[:, hs], preferred_element_type=F32)
             + jnp.dot(p2.astype(BF16), vn[:, hs], preferred_element_type=F32))
        o_ref[:, h * HEAD_DIM:(h + 1) * HEAD_DIM] = o / l


def _attn_sample(q, qi4, wi, ck, cv, cki, kb, vf, kif, k_top):
    nb, past_len, kvw = ck.shape
    tn = q.shape[0] // nb
    assert LANES % tn == 0 and tn % 16 == 0
    group = LANES // tn
    row = lambda w: pl.BlockSpec((tn, w), lambda b: (b, 0))
    shared = lambda w: pl.BlockSpec((LANES, w), lambda b: (b // group, 0))
    cache = lambda w: pl.BlockSpec((1, past_len, w), lambda b: (b, 0, 0))
    return pl.pallas_call(
        functools.partial(_attn_sample_kernel, k_top=k_top, past_len=past_len, tn=tn),
        grid=(nb,),
        in_specs=[row(N_HEADS * HEAD_DIM),
                  pl.BlockSpec((1, N_IDX_HEADS, tn, IDX_DIM), lambda b: (b, 0, 0, 0)),
                  row(LANES), cache(kvw), cache(kvw), cache(IDX_DIM),
                  shared(kvw), shared(kvw), shared(IDX_DIM)],
        out_specs=row(N_HEADS * HEAD_DIM),
        out_shape=jax.ShapeDtypeStruct((nb * tn, N_HEADS * HEAD_DIM), F32),
        compiler_params=pltpu.CompilerParams(dimension_semantics=("arbitrary",),
                                             vmem_limit_bytes=VMEM_LIMIT),
        name="attn_sample",
    )(q, qi4, wi, ck, cv, cki, kb, vf, kif)


def _outffn_kernel(x_ref, a_ref, sga_ref, sgb_ref, u_ref, vn_ref, wmix_ref, bmix_ref, wo_ref,
                   g2_ref, wg_ref, wu_ref, wd_ref, gf_ref, xo_ref, *rest, final):
    if final:
        y_ref, z_ref = rest
    else:
        (z_ref,) = rest
    tm = x_ref.shape[0]
    group_dim = wmix_ref.shape[-1]
    for c in range(tm // SGU_CHUNK):
        rs = slice(c * SGU_CHUNK, (c + 1) * SGU_CHUNK)
        for g in range(SGU_GROUPS):
            cs = slice(g * group_dim, (g + 1) * group_dim)
            mixed = jnp.dot(wmix_ref[g], vn_ref[rs, cs].astype(BF16),
                            preferred_element_type=F32) + bmix_ref[:, cs]
            z = sga_ref[rs, cs] * a_ref[rs, cs] + sgb_ref[rs, cs] * (u_ref[rs, cs] * mixed)
            z_ref[rs, cs] = z.astype(BF16)
    x1 = x_ref[...] + jnp.dot(z_ref[...], wo_ref[...], preferred_element_type=F32)
    h2 = _rms_norm(x1, g2_ref[...]).astype(BF16)
    gate = jnp.dot(h2, wg_ref[...], preferred_element_type=F32)
    up = jnp.dot(h2, wu_ref[...], preferred_element_type=F32)
    ff = (gate * _sigmoid(gate) * up).astype(BF16)
    x2 = x1 + jnp.dot(ff, wd_ref[...], preferred_element_type=F32)
    xo_ref[...] = x2
    if final:
        y_ref[...] = _rms_norm(x2, gf_ref[...])


def _outffn(x, a, sga, sgb, u, vn, wmix, bmix, wo, g2, wg, wu, wd, gf, final):
    T, D = x.shape
    tm = TOKEN_TILE
    row = pl.BlockSpec((tm, D), lambda i: (i, 0))
    n_out = 2 if final else 1
    out = pl.pallas_call(
        functools.partial(_outffn_kernel, final=final),
        grid=(T // tm,),
        in_specs=[row] * 6 + [_const_spec(wmix.shape), _const_spec(bmix.shape),
                              _const_spec(wo.shape), _const_spec((1, D)), _const_spec(wg.shape),
                              _const_spec(wu.shape), _const_spec(wd.shape), _const_spec((1, D))],
        out_specs=(row,) * n_out,
        out_shape=(jax.ShapeDtypeStruct((T, D), F32),) * n_out,
        scratch_shapes=[pltpu.VMEM((tm, D), BF16)],
        compiler_params=pltpu.CompilerParams(dimension_semantics=("arbitrary",),
                                             vmem_limit_bytes=VMEM_LIMIT),
        name="outffn",
    )(x, a, sga, sgb, u, vn, wmix, bmix, wo, g2, wg, wu, wd, gf)
    return out if final else (out[0], None)


def _rope_tables(pos, d):
    inv = ROPE_THETA ** (-jnp.arange(0, d, 2, dtype=F32) / d)
    ang = pos.astype(F32)[:, None] * inv[None, :]
    cos, sin = jnp.cos(ang), jnp.sin(ang)
    c = jnp.concatenate([cos, cos], axis=-1)
    s = jnp.concatenate([-sin, sin], axis=-1)
    reps = LANES // d
    return jnp.tile(c, (1, reps)), jnp.tile(s, (1, reps))


def _mix_weights(sgu_w, sgu_b, n, group_dim):
    p = jnp.arange(n)
    mask = (p[None, :] // CHUNK) <= (p[:, None] // CHUNK)
    w = jnp.where(mask[None], sgu_w[:, :n, :n], 0.0)
    reps = SGU_CHUNK // n
    eye = jnp.eye(reps, dtype=w.dtype)
    wbd = jnp.einsum('ab,gij->gaibj', eye, w).reshape(SGU_GROUPS, SGU_CHUNK, SGU_CHUNK)
    b = jnp.tile(sgu_b[:, :n], (1, reps))
    bfull = jnp.repeat(b.T, group_dim, axis=1)
    return wbd.astype(BF16), bfull


def kernel(x_prompt, x_sample, cache_k, cache_v, cache_kidx, norm1_g, w_in, ln_v_g, ln_v_b, sgu_w, sgu_b, w_out, norm2_g, w_gate, w_up, w_down, final_norm_g):
    B, S, D = x_prompt.shape
    NB, TN, _ = x_sample.shape
    depth, _, P = cache_k.shape[:3]
    assert D == N_HEADS * HEAD_DIM and SGU_CHUNK % TN == 0 and S % SGU_CHUNK == 0
    kvw = N_KV_HEADS * HEAD_DIM
    qiw = N_IDX_HEADS * IDX_DIM
    k_top_p = min(TOPK_MAX, S // 4)
    k_top_s = min(TOPK_MAX, (P + TN) // 4)
    group_dim = D // SGU_GROUPS

    pos_p = jnp.tile(jnp.arange(S), B)
    pos_s = jnp.tile(P + jnp.arange(TN), NB)
    tab_p = _rope_tables(pos_p, HEAD_DIM) + _rope_tables(pos_p, IDX_DIM)
    tab_s = _rope_tables(pos_s, HEAD_DIM) + _rope_tables(pos_s, IDX_DIM)

    xp = x_prompt.reshape(B * S, D)
    xs = x_sample.reshape(NB * TN, D)
    row = lambda v: v.reshape(1, -1)
    outs = [[] for _ in range(7)]
    yp = ys = None
    for l in range(depth):
        w = w_in[l]
        o = 0
        seg = []
        for width in (D, kvw, kvw, qiw, IDX_DIM, N_IDX_HEADS, D, D, D, D):
            seg.append(w[:, o:o + width])
            o += width
        wq, wk, wv, wqi, wki, wwi, wu_, wvs, wga, wgb = seg
        wwi_pad = jnp.pad(wwi, ((0, 0), (0, LANES - N_IDX_HEADS)))
        wa = jnp.concatenate([wq, wk, wv, wqi, wki, wki, wwi_pad, wu_, wvs, wga, wgb],
                             axis=1).astype(BF16)
        wb = jnp.concatenate([wv.T, jnp.pad(wwi.T, ((0, 16 - N_IDX_HEADS), (0, 0)))],
                             axis=0).astype(BF16)
        wo, wg, wu, wd = (t[l].astype(BF16) for t in (w_out, w_gate, w_up, w_down))
        g1, g2, lng, lnb = row(norm1_g[l]), row(norm2_g[l]), row(ln_v_g[l]), row(ln_v_b[l])
        gf = row(final_norm_g)
        final = l == depth - 1

        (q, kf, kb, vf, vt, qi, kif, kze, kzo, _, wit, u, vn, sga, sgb) = _inproj(
            xp, g1, wa, wb, tab_p, lng, lnb)
        a = _attn_prompt(q, qi, wit, kze, kzo, kb, vt, B, S, k_top_p)
        wmix, bmix = _mix_weights(sgu_w[l], sgu_b[l], SGU_CHUNK, group_dim)
        xp, yp = _outffn(xp, a, sga, sgb, u, vn, wmix, bmix, wo, g2, wg, wu, wd, gf, final)
        outs[0].append(kf.reshape(B, S, N_KV_HEADS, HEAD_DIM))
        outs[1].append(vf.reshape(B, S, N_KV_HEADS, HEAD_DIM))
        outs[2].append(kif.reshape(B, S, IDX_DIM))

        (q, kf, kb, vf, _, qi, kif, _, _, wi, _, u, vn, sga, sgb) = _inproj(
            xs, g1, wa, wb, tab_s, lng, lnb)
        qi4 = qi.reshape(NB, TN, N_IDX_HEADS, IDX_DIM).transpose(0, 2, 1, 3)
        a = _attn_sample(q, qi4, wi, cache_k[l].reshape(NB, P, kvw), cache_v[l].reshape(NB, P, kvw),
                         cache_kidx[l], kb, vf, kif, k_top_s)
        wmix, bmix = _mix_weights(sgu_w[l], sgu_b[l], TN, group_dim)
        xs, ys = _outffn(xs, a, sga, sgb, u, vn, wmix, bmix, wo, g2, wg, wu, wd, gf, final)
        outs[3].append(kf.reshape(NB, TN, N_KV_HEADS, HEAD_DIM))
        outs[4].append(vf.reshape(NB, TN, N_KV_HEADS, HEAD_DIM))
        outs[5].append(kif.reshape(NB, TN, IDX_DIM))
        outs[6].append(vn.reshape(NB, TN, D))

    return (yp.reshape(B, S, D), ys.reshape(NB, TN, D)) + tuple(jnp.stack(o) for o in outs)
```

```python
import functools
import math

import jax
import jax.numpy as jnp
from jax import lax
from jax.experimental import pallas as pl
from jax.experimental.pallas import tpu as pltpu

CHUNK = 64
N_HEADS = 8
HEAD_DIM = 128
N_KV_HEADS = 2
KV_GROUP = N_HEADS // N_KV_HEADS
N_IDX_HEADS = 8
IDX_DIM = 64
TOPK_MAX = 256
SGU_CHUNK = 128
SGU_GROUPS = 8
ROPE_THETA = 10000.0
EPS = 1e-6

LANES = 128
TOKEN_TILE = 256
Q_TILE = 256
KEY_TILE = 256
VMEM_LIMIT = 56 * 1024 * 1024

INT_MIN = -(2 ** 31)
INT_MAX = 2 ** 31 - 1
MASK_BIAS = -1e30
LOG2E = 1.4426950408889634

F32 = jnp.float32
BF16 = jnp.bfloat16
NT_DIMS = (((1,), (1,)), ((), ()))


def _const_spec(shape):
    nd = len(shape)
    return pl.BlockSpec(shape, lambda *_: (0,) * nd, pipeline_mode=pl.Buffered(1))


def _sigmoid(x):
    return 1.0 / (1.0 + jnp.exp(-x))


def _rms_norm(x, g):
    return x * lax.rsqrt(jnp.mean(x * x, axis=-1, keepdims=True) + EPS) * g


def _order_key(score):
    bits = pltpu.bitcast(score, jnp.int32)
    return bits ^ ((bits >> 31) & INT_MAX)


def _midpoint(lo, hi):
    return (lo >> 1) + (hi >> 1) + (lo & hi & 1)


_C_Q = 0
_C_K = _C_Q + N_HEADS * HEAD_DIM
_C_V = _C_K + N_KV_HEADS * HEAD_DIM
_C_QI = _C_V + N_KV_HEADS * HEAD_DIM
_C_KI = _C_QI + N_IDX_HEADS * IDX_DIM
_C_WI = _C_KI + LANES
_C_U = _C_WI + LANES


def _inproj_kernel(x_ref, g_ref, wa_ref, wb_ref, cosh_ref, sinh_ref, cosi_ref, sini_ref,
                   lng_ref, lnb_ref,
                   q_ref, kf_ref, kb_ref, vf_ref, vt_ref, qi_ref, kif_ref, kze_ref, kzo_ref,
                   wi_ref, wit_ref, u_ref, vn_ref, sga_ref, sgb_ref, *, d_model):
    hb = _rms_norm(x_ref[...], g_ref[...]).astype(BF16)

    def proj(c0, width):
        return jnp.dot(hb, wa_ref[:, c0:c0 + width], preferred_element_type=F32)

    cosh, sinh = cosh_ref[...], sinh_ref[...]
    cosi, sini = cosi_ref[...], sini_ref[...]
    lane = lax.broadcasted_iota(jnp.int32, cosi.shape, 1)
    first_half = (lane % IDX_DIM) < (IDX_DIM // 2)

    def rope_head(x):
        return x * cosh + pltpu.roll(x, HEAD_DIM // 2, 1) * sinh

    def rope_idx(x):
        partner = jnp.where(first_half, pltpu.roll(x, LANES - IDX_DIM // 2, 1),
                            pltpu.roll(x, IDX_DIM // 2, 1))
        return x * cosi + partner * sini

    xq = proj(_C_Q, N_HEADS * HEAD_DIM)
    for h in range(N_HEADS):
        sl = slice(h * HEAD_DIM, (h + 1) * HEAD_DIM)
        q_ref[:, sl] = rope_head(xq[:, sl]).astype(BF16)

    xk = proj(_C_K, N_KV_HEADS * HEAD_DIM)
    for h in range(N_KV_HEADS):
        sl = slice(h * HEAD_DIM, (h + 1) * HEAD_DIM)
        kr = rope_head(xk[:, sl])
        kf_ref[:, sl] = kr
        kb_ref[:, sl] = kr.astype(BF16)

    vf_ref[...] = proj(_C_V, N_KV_HEADS * HEAD_DIM)

    xqi = proj(_C_QI, N_IDX_HEADS * IDX_DIM)
    for p in range(N_IDX_HEADS * IDX_DIM // LANES):
        sl = slice(p * LANES, (p + 1) * LANES)
        qi_ref[:, sl] = (rope_idx(xqi[:, sl]) * (IDX_DIM ** -0.5)).astype(BF16)

    kk = rope_idx(proj(_C_KI, LANES))
    kif_ref[...] = kk[:, :IDX_DIM]
    low = lane < IDX_DIM
    kze_ref[...] = jnp.where(low, kk, 0.0).astype(BF16)
    kzo_ref[...] = jnp.where(low, 0.0, kk).astype(BF16)

    wi_ref[...] = proj(_C_WI, LANES) * (N_IDX_HEADS ** -0.5)

    tb = lax.dot_general(wb_ref[...], hb, NT_DIMS, preferred_element_type=F32)
    kvw = N_KV_HEADS * HEAD_DIM
    vt_ref[0] = tb[:kvw].astype(BF16)
    wit_ref[...] = tb[kvw:] * (N_IDX_HEADS ** -0.5)

    u_ref[...] = jax.nn.gelu(proj(_C_U, d_model), approximate=True)
    gv = jax.nn.gelu(proj(_C_U + d_model, d_model), approximate=True)
    mu = jnp.mean(gv, axis=-1, keepdims=True)
    dv = gv - mu
    var = jnp.mean(dv * dv, axis=-1, keepdims=True)
    vn_ref[...] = dv * lax.rsqrt(var + EPS) * lng_ref[...] + lnb_ref[...]
    sga_ref[...] = _sigmoid(proj(_C_U + 2 * d_model, d_model))
    sgb_ref[...] = _sigmoid(proj(_C_U + 3 * d_model, d_model))


def _inproj(x, g, wa, wb, tables, lng, lnb):
    T, D = x.shape
    tm = TOKEN_TILE
    assert T % tm == 0
    kvw = N_KV_HEADS * HEAD_DIM
    qiw = N_IDX_HEADS * IDX_DIM
    row = lambda w: pl.BlockSpec((tm, w), lambda i: (i, 0))
    out_shape = (
        jax.ShapeDtypeStruct((T, N_HEADS * HEAD_DIM), BF16),
        jax.ShapeDtypeStruct((T, kvw), F32),
        jax.ShapeDtypeStruct((T, kvw), BF16),
        jax.ShapeDtypeStruct((T, kvw), F32),
        jax.ShapeDtypeStruct((T // tm, kvw, tm), BF16),
        jax.ShapeDtypeStruct((T, qiw), BF16),
        jax.ShapeDtypeStruct((T, IDX_DIM), F32),
        jax.ShapeDtypeStruct((T, LANES), BF16),
        jax.ShapeDtypeStruct((T, LANES), BF16),
        jax.ShapeDtypeStruct((T, LANES), F32),
        jax.ShapeDtypeStruct((16, T), F32),
        jax.ShapeDtypeStruct((T, D), F32),
        jax.ShapeDtypeStruct((T, D), F32),
        jax.ShapeDtypeStruct((T, D), F32),
        jax.ShapeDtypeStruct((T, D), F32),
    )
    out_specs = (
        row(N_HEADS * HEAD_DIM), row(kvw), row(kvw), row(kvw),
        pl.BlockSpec((1, kvw, tm), lambda i: (i, 0, 0)),
        row(qiw), row(IDX_DIM), row(LANES), row(LANES), row(LANES),
        pl.BlockSpec((16, tm), lambda i: (0, i)),
        row(D), row(D), row(D), row(D),
    )
    in_specs = [row(D), _const_spec((1, D)), _const_spec(wa.shape), _const_spec(wb.shape),
                row(LANES), row(LANES), row(LANES), row(LANES),
                _const_spec((1, D)), _const_spec((1, D))]
    return pl.pallas_call(
        functools.partial(_inproj_kernel, d_model=D),
        grid=(T // tm,),
        in_specs=in_specs, out_specs=out_specs, out_shape=out_shape,
        compiler_params=pltpu.CompilerParams(dimension_semantics=("arbitrary",),
                                             vmem_limit_bytes=VMEM_LIMIT),
        name="inproj",
    )(x, g, wa, wb, *tables, lng, lnb)


def _attn_prompt_kernel(q_ref, qi_ref, wit_ref, kze_ref, kzo_ref, k_ref, vt_ref, o_ref,
                        keys_ref, hi_ref, lo_ref, acc_ref, m_ref, l_ref, mx_ref, al_ref, s_ref, p_ref,
                        *, k_top):
    tq, kb_sz = Q_TILE, KEY_TILE
    qb = pl.program_id(1)
    nkb = (qb + 1) * (tq // kb_sz)
    q_chunk = (qb * tq + lax.broadcasted_iota(jnp.int32, (1, tq), 1)) // CHUNK
    row_iota = lax.broadcasted_iota(jnp.int32, (kb_sz, 1), 0)

    def key_slice(kb):
        return pl.ds(pl.multiple_of(kb * kb_sz, kb_sz), kb_sz)

    def score_body(kb, carry):
        ks = key_slice(kb)
        kze, kzo = kze_ref[ks, :], kzo_ref[ks, :]
        score = jnp.zeros((kb_sz, tq), F32)
        for p in range(N_IDX_HEADS // 2):
            qp = qi_ref[:, p * LANES:(p + 1) * LANES]
            de = lax.dot_general(kze, qp, NT_DIMS, preferred_element_type=F32)
            do = lax.dot_general(kzo, qp, NT_DIMS, preferred_element_type=F32)
            score = score + wit_ref[2 * p:2 * p + 1, :] * jnp.maximum(de, 0.0)
            score = score + wit_ref[2 * p + 1:2 * p + 2, :] * jnp.maximum(do, 0.0)
        k_chunk = (kb * kb_sz + row_iota) // CHUNK
        key = jnp.where(k_chunk <= q_chunk, _order_key(score), INT_MIN)
        keys_ref[ks, :] = key
        hi_ref[ks, :] = (key >> 16).astype(jnp.int16)
        lo_ref[ks, :] = ((key & 0xFFFF) - 2 ** 15).astype(jnp.int16)
        return carry

    lax.fori_loop(0, nkb, score_body, 0)

    n_acc = 4
    rows16 = 16

    def fold_rows(x, accs):
        accs = list(accs)
        for r in range(kb_sz // rows16):
            accs[r % n_acc] = accs[r % n_acc] + x[r * rows16:(r + 1) * rows16, :]
        return tuple(accs)

    def zero_accs():
        return tuple(jnp.zeros((rows16, tq), jnp.int16) for _ in range(n_acc))

    def total(accs):
        tot = accs[0].astype(jnp.int32)
        for a in accs[1:]:
            tot = tot + a.astype(jnp.int32)
        return jnp.sum(tot, axis=0, keepdims=True)

    def count_ge(plane_ref, t):
        t16 = jnp.broadcast_to(t, (kb_sz, tq)).astype(jnp.int16)

        def body(kb, accs):
            ge = (plane_ref[key_slice(kb), :] >= t16).astype(jnp.int16)
            return fold_rows(ge, accs)

        return total(lax.fori_loop(0, nkb, body, zero_accs()))

    def bisect16(plane_ref, need):
        def body(_, carry):
            lo, hi = carry
            mid = (lo + hi) >> 1
            ok = count_ge(plane_ref, mid) >= need
            return jnp.where(ok, mid, lo), jnp.where(ok, hi, mid)

        lo, _ = lax.fori_loop(0, 16, body, (jnp.full((1, tq), -(2 ** 15), jnp.int32),
                                            jnp.full((1, tq), 2 ** 15, jnp.int32)))
        return lo

    t_hi = bisect16(hi_ref, k_top)

    t_hi16 = jnp.broadcast_to(t_hi, (kb_sz, tq)).astype(jnp.int16)

    def tie_body(kb, accs):
        ks = key_slice(kb)
        hi_part = hi_ref[ks, :]
        lo_ref[ks, :] = jnp.where(hi_part == t_hi16, lo_ref[ks, :], jnp.int16(-(2 ** 15)))
        return fold_rows((hi_part > t_hi16).astype(jnp.int16), accs)

    n_above = total(lax.fori_loop(0, nkb, tie_body, zero_accs()))
    t_lo = bisect16(lo_ref, k_top - n_above)
    thr = jnp.maximum((t_hi << 16) + (t_lo + 2 ** 15), INT_MIN + 1)

    m_ref[...] = jnp.full(m_ref.shape, MASK_BIAS, F32)
    l_ref[...] = jnp.zeros(l_ref.shape, F32)
    acc_ref[...] = jnp.zeros(acc_ref.shape, F32)
    c_exp = (HEAD_DIM ** -0.5) * LOG2E

    def attn_body(kb, carry):
        ks = key_slice(kb)
        bias = jnp.where(keys_ref[ks, :] >= thr, 0.0, MASK_BIAS)
        for h in range(N_HEADS):
            n = h // KV_GROUP
            k_n = k_ref[ks, n * HEAD_DIM:(n + 1) * HEAD_DIM]
            q_h = q_ref[:, h * HEAD_DIM:(h + 1) * HEAD_DIM]
            s = lax.dot_general(k_n, q_h, NT_DIMS, preferred_element_type=F32) + bias
            s_ref[h] = s
            mx_ref[h:h + 1, :] = jnp.max(s, axis=0, keepdims=True)
        m_old = m_ref[...]
        m_new = jnp.maximum(m_old, mx_ref[...])
        alpha = jnp.exp2((m_old - m_new) * c_exp)
        m_ref[...] = m_new
        al_ref[...] = alpha
        for h in range(N_HEADS):
            p = jnp.exp2((s_ref[h] - m_new[h:h + 1, :]) * c_exp)
            mx_ref[h:h + 1, :] = jnp.sum(p, axis=0, keepdims=True)
            p_ref[h] = p.astype(BF16)
        l_ref[...] = alpha * l_ref[...] + mx_ref[...]
        for h in range(N_HEADS):
            n = h // KV_GROUP
            vt_n = vt_ref[kb, n * HEAD_DIM:(n + 1) * HEAD_DIM, :]
            pv = jnp.dot(vt_n, p_ref[h], preferred_element_type=F32)
            acc_ref[h] = al_ref[h:h + 1, :] * acc_ref[h] + pv
        return carry

    lax.fori_loop(0, nkb, attn_body, 0)

    for h in range(N_HEADS):
        o_t = acc_ref[h] / l_ref[h:h + 1, :]
        o_ref[:, h * HEAD_DIM:(h + 1) * HEAD_DIM] = o_t.T


def _attn_prompt(q, qi, wit, kze, kzo, kb, vt, batch, seq, k_top):
    tq = Q_TILE
    assert seq % tq == 0 and Q_TILE % KEY_TILE == 0 and KEY_TILE == TOKEN_TILE
    nq = seq // tq
    kvw = N_KV_HEADS * HEAD_DIM
    qrow = lambda w: pl.BlockSpec((tq, w), lambda b, i: (b * nq + i, 0))
    per_batch = lambda w: pl.BlockSpec((seq, w), lambda b, i: (b, 0), pipeline_mode=pl.Buffered(1))
    return pl.pallas_call(
        functools.partial(_attn_prompt_kernel, k_top=k_top),
        grid=(batch, nq),
        in_specs=[qrow(N_HEADS * HEAD_DIM), qrow(N_IDX_HEADS * IDX_DIM),
                  pl.BlockSpec((16, tq), lambda b, i: (0, b * nq + i)),
                  per_batch(LANES), per_batch(LANES), per_batch(kvw),
                  pl.BlockSpec((seq // KEY_TILE, kvw, KEY_TILE), lambda b, i: (b, 0, 0),
                               pipeline_mode=pl.Buffered(1))],
        out_specs=qrow(N_HEADS * HEAD_DIM),
        out_shape=jax.ShapeDtypeStruct((batch * seq, N_HEADS * HEAD_DIM), F32),
        scratch_shapes=[pltpu.VMEM((seq, tq), jnp.int32),
                        pltpu.VMEM((seq, tq), jnp.int16),
                        pltpu.VMEM((seq, tq), jnp.int16),
                        pltpu.VMEM((N_HEADS, HEAD_DIM, tq), F32),
                        pltpu.VMEM((N_HEADS, tq), F32),
                        pltpu.VMEM((N_HEADS, tq), F32),
                        pltpu.VMEM((N_HEADS, tq), F32),
                        pltpu.VMEM((N_HEADS, tq), F32),
                        pltpu.VMEM((N_HEADS, KEY_TILE, tq), F32),
                        pltpu.VMEM((N_HEADS, KEY_TILE, tq), BF16)],
        compiler_params=pltpu.CompilerParams(dimension_semantics=("arbitrary", "arbitrary"),
                                             vmem_limit_bytes=VMEM_LIMIT),
        name="attn_prompt",
    )(q, qi, wit, kze, kzo, kb, vt)


def _attn_sample_kernel(q_ref, qi_ref, wi_ref, ck_ref, cv_ref, cki_ref, kn_ref, vn_ref, kin_ref,
                        o_ref, *, k_top, past_len, tn):
    b = pl.program_id(0)
    group = LANES // tn
    ck = ck_ref[0].astype(BF16)
    cv = cv_ref[0].astype(BF16)
    cki = cki_ref[0].astype(BF16)
    kn = kn_ref[...]
    vn = vn_ref[...].astype(BF16)
    kin = kin_ref[...].astype(BF16)

    sp = jnp.zeros((tn, past_len), F32)
    sn = jnp.zeros((tn, LANES), F32)
    for h in range(N_IDX_HEADS):
        qh = qi_ref[0, h]
        w = wi_ref[:, h:h + 1]
        dp = lax.dot_general(qh, cki, NT_DIMS, preferred_element_type=F32)
        dn = lax.dot_general(qh, kin, NT_DIMS, preferred_element_type=F32)
        sp = sp + w * jnp.maximum(dp, 0.0)
        sn = sn + w * jnp.maximum(dn, 0.0)

    j = lax.broadcasted_iota(jnp.int32, (tn, LANES), 1)
    t = lax.broadcasted_iota(jnp.int32, (tn, LANES), 0)
    own = (j // tn) == (b % group)
    vis = ((past_len + j % tn) // CHUNK) <= ((past_len + t) // CHUNK)
    key_p = _order_key(sp)
    key_n = jnp.where(own & vis, _order_key(sn), INT_MIN)

    def bisect_body(_, carry):
        lo, hi = carry
        mid = _midpoint(lo, hi)
        cnt = (jnp.sum((key_p >= mid).astype(jnp.int32), axis=1, keepdims=True)
               + jnp.sum((key_n >= mid).astype(jnp.int32), axis=1, keepdims=True))
        ok = cnt >= k_top
        return jnp.where(ok, mid, lo), jnp.where(ok, hi, mid)

    thr, _ = lax.fori_loop(
        0, 32, bisect_body,
        (jnp.full((tn, 1), INT_MIN + 1, jnp.int32), jnp.full((tn, 1), INT_MAX, jnp.int32)))

    bias_p = jnp.where(key_p >= thr, 0.0, MASK_BIAS)
    bias_n = jnp.where(key_n >= thr, 0.0, MASK_BIAS)
    c_exp = (HEAD_DIM ** -0.5) * LOG2E
    for h in range(N_HEADS):
        n = h // KV_GROUP
        hs = slice(n * HEAD_DIM, (n + 1) * HEAD_DIM)
        q_h = q_ref[:, h * HEAD_DIM:(h + 1) * HEAD_DIM]
        s1 = lax.dot_general(q_h, ck[:, hs], NT_DIMS, preferred_element_type=F32) + bias_p
        s2 = lax.dot_general(q_h, kn[:, hs], NT_DIMS, preferred_element_type=F32) + bias_n
        m = jnp.maximum(jnp.max(s1, axis=1, keepdims=True), jnp.max(s2, axis=1, keepdims=True))
        p1 = jnp.exp2((s1 - m) * c_exp)
        p2 = jnp.exp2((s2 - m) * c_exp)
        l = jnp.sum(p1, axis=1, keepdims=True) + jnp.sum(p2, axis=1, keepdims=True)
        o = (jnp.dot(p1.astype(BF16), cv[:, hs], preferred_element_type=F32)
             + jnp.dot(p2.astype(BF16), vn[:, hs], preferred_element_type=F32))
        o_ref[:, h * HEAD_DIM:(h + 1) * HEAD_DIM] = o / l


def _attn_sample(q, qi4, wi, ck, cv, cki, kb, vf, kif, k_top):
    nb, past_len, kvw = ck.shape
    tn = q.shape[0] // nb
    assert LANES % tn == 0 and tn % 16 == 0
    group = LANES // tn
    row = lambda w: pl.BlockSpec((tn, w), lambda b: (b, 0))
    shared = lambda w: pl.BlockSpec((LANES, w), lambda b: (b // group, 0))
    cache = lambda w: pl.BlockSpec((1, past_len, w), lambda b: (b, 0, 0))
    return pl.pallas_call(
        functools.partial(_attn_sample_kernel, k_top=k_top, past_len=past_len, tn=tn),
        grid=(nb,),
        in_specs=[row(N_HEADS * HEAD_DIM),
                  pl.BlockSpec((1, N_IDX_HEADS, tn, IDX_DIM), lambda b: (b, 0, 0, 0)),
                  row(LANES), cache(kvw), cache(kvw), cache(IDX_DIM),
                  shared(kvw), shared(kvw), shared(IDX_DIM)],
        out_specs=row(N_HEADS * HEAD_DIM),
        out_shape=jax.ShapeDtypeStruct((nb * tn, N_HEADS * HEAD_DIM), F32),
        compiler_params=pltpu.CompilerParams(dimension_semantics=("arbitrary",),
                                             vmem_limit_bytes=VMEM_LIMIT),
        name="attn_sample",
    )(q, qi4, wi, ck, cv, cki, kb, vf, kif)


def _outffn_kernel(x_ref, a_ref, sga_ref, sgb_ref, u_ref, vn_ref, wmix_ref, bmix_ref, wo_ref,
                   g2_ref, wg_ref, wu_ref, wd_ref, gf_ref, xo_ref, *rest, final):
    if final:
        y_ref, z_ref = rest
    else:
        (z_ref,) = rest
    tm = x_ref.shape[0]
    group_dim = wmix_ref.shape[-1]
    for c in range(tm // SGU_CHUNK):
        rs = slice(c * SGU_CHUNK, (c + 1) * SGU_CHUNK)
        for g in range(SGU_GROUPS):
            cs = slice(g * group_dim, (g + 1) * group_dim)
            mixed = jnp.dot(wmix_ref[g], vn_ref[rs, cs].astype(BF16),
                            preferred_element_type=F32) + bmix_ref[:, cs]
            z = sga_ref[rs, cs] * a_ref[rs, cs] + sgb_ref[rs, cs] * (u_ref[rs, cs] * mixed)
            z_ref[rs, cs] = z.astype(BF16)
    x1 = x_ref[...] + jnp.dot(z_ref[...], wo_ref[...], preferred_element_type=F32)
    h2 = _rms_norm(x1, g2_ref[...]).astype(BF16)
    gate = jnp.dot(h2, wg_ref[...], preferred_element_type=F32)
    up = jnp.dot(h2, wu_ref[...], preferred_element_type=F32)
    ff = (gate * _sigmoid(gate) * up).astype(BF16)
    x2 = x1 + jnp.dot(ff, wd_ref[...], preferred_element_type=F32)
    xo_ref[...] = x2
    if final:
        y_ref[...] = _rms_norm(x2, gf_ref[...])


def _outffn(x, a, sga, sgb, u, vn, wmix, bmix, wo, g2, wg, wu, wd, gf, final):
    T, D = x.shape
    tm = TOKEN_TILE
    row = pl.BlockSpec((tm, D), lambda i: (i, 0))
    n_out = 2 if final else 1
    out = pl.pallas_call(
        functools.partial(_outffn_kernel, final=final),
        grid=(T // tm,),
        in_specs=[row] * 6 + [_const_spec(wmix.shape), _const_spec(bmix.shape),
                              _const_spec(wo.shape), _const_spec((1, D)), _const_spec(wg.shape),
                              _const_spec(wu.shape), _const_spec(wd.shape), _const_spec((1, D))],
        out_specs=(row,) * n_out,
        out_shape=(jax.ShapeDtypeStruct((T, D), F32),) * n_out,
        scratch_shapes=[pltpu.VMEM((tm, D), BF16)],
        compiler_params=pltpu.CompilerParams(dimension_semantics=("arbitrary",),
                                             vmem_limit_bytes=VMEM_LIMIT),
        name="outffn",
    )(x, a, sga, sgb, u, vn, wmix, bmix, wo, g2, wg, wu, wd, gf)
    return out if final else (out[0], None)


def _rope_tables(pos, d):
    inv = ROPE_THETA ** (-jnp.arange(0, d, 2, dtype=F32) / d)
    ang = pos.astype(F32)[:, None] * inv[None, :]
    cos, sin = jnp.cos(ang), jnp.sin(ang)
    c = jnp.concatenate([cos, cos], axis=-1)
    s = jnp.concatenate([-sin, sin], axis=-1)
    reps = LANES // d
    return jnp.tile(c, (1, reps)), jnp.tile(s, (1, reps))


def _mix_weights(sgu_w, sgu_b, n, group_dim):
    p = jnp.arange(n)
    mask = (p[None, :] // CHUNK) <= (p[:, None] // CHUNK)
    w = jnp.where(mask[None], sgu_w[:, :n, :n], 0.0)
    reps = SGU_CHUNK // n
    eye = jnp.eye(reps, dtype=w.dtype)
    wbd = jnp.einsum('ab,gij->gaibj', eye, w).reshape(SGU_GROUPS, SGU_CHUNK, SGU_CHUNK)
    b = jnp.tile(sgu_b[:, :n], (1, reps))
    bfull = jnp.repeat(b.T, group_dim, axis=1)
    return wbd.astype(BF16), bfull


def kernel(x_prompt, x_sample, cache_k, cache_v, cache_kidx, norm1_g, w_in, ln_v_g, ln_v_b, sgu_w, sgu_b, w_out, norm2_g, w_gate, w_up, w_down, final_norm_g):
    B, S, D = x_prompt.shape
    NB, TN, _ = x_sample.shape
    depth, _, P = cache_k.shape[:3]
    assert D == N_HEADS * HEAD_DIM and SGU_CHUNK % TN == 0 and S % SGU_CHUNK == 0
    kvw = N_KV_HEADS * HEAD_DIM
    qiw = N_IDX_HEADS * IDX_DIM
    k_top_p = min(TOPK_MAX, S // 4)
    k_top_s = min(TOPK_MAX, (P + TN) // 4)
    group_dim = D // SGU_GROUPS

    pos_p = jnp.tile(jnp.arange(S), B)
    pos_s = jnp.tile(P + jnp.arange(TN), NB)
    tab_p = _rope_tables(pos_p, HEAD_DIM) + _rope_tables(pos_p, IDX_DIM)
    tab_s = _rope_tables(pos_s, HEAD_DIM) + _rope_tables(pos_s, IDX_DIM)

    xp = x_prompt.reshape(B * S, D)
    xs = x_sample.reshape(NB * TN, D)
    row = lambda v: v.reshape(1, -1)
    outs = [[] for _ in range(7)]
    yp = ys = None
    for l in range(depth):
        w = w_in[l]
        o = 0
        seg = []
        for width in (D, kvw, kvw, qiw, IDX_DIM, N_IDX_HEADS, D, D, D, D):
            seg.append(w[:, o:o + width])
            o += width
        wq, wk, wv, wqi, wki, wwi, wu_, wvs, wga, wgb = seg
        wwi_pad = jnp.pad(wwi, ((0, 0), (0, LANES - N_IDX_HEADS)))
        wa = jnp.concatenate([wq, wk, wv, wqi, wki, wki, wwi_pad, wu_, wvs, wga, wgb],
                             axis=1).astype(BF16)
        wb = jnp.concatenate([wv.T, jnp.pad(wwi.T, ((0, 16 - N_IDX_HEADS), (0, 0)))],
                             axis=0).astype(BF16)
        wo, wg, wu, wd = (t[l].astype(BF16) for t in (w_out, w_gate, w_up, w_down))
        g1, g2, lng, lnb = row(norm1_g[l]), row(norm2_g[l]), row(ln_v_g[l]), row(ln_v_b[l])
        gf = row(final_norm_g)
        final = l == depth - 1

        (q, kf, kb, vf, vt, qi, kif, kze, kzo, _, wit, u, vn, sga, sgb) = _inproj(
            xp, g1, wa, wb, tab_p, lng, lnb)
        a = _attn_prompt(q, qi, wit, kze, kzo, kb, vt, B, S, k_top_p)
        wmix, bmix = _mix_weights(sgu_w[l], sgu_b[l], SGU_CHUNK, group_dim)
        xp, yp = _outffn(xp, a, sga, sgb, u, vn, wmix, bmix, wo, g2, wg, wu, wd, gf, final)
        outs[0].append(kf.reshape(B, S, N_KV_HEADS, HEAD_DIM))
        outs[1].append(vf.reshape(B, S, N_KV_HEADS, HEAD_DIM))
        outs[2].append(kif.reshape(B, S, IDX_DIM))

        (q, kf, kb, vf, _, qi, kif, _, _, wi, _, u, vn, sga, sgb) = _inproj(
            xs, g1, wa, wb, tab_s, lng, lnb)
        qi4 = qi.reshape(NB, TN, N_IDX_HEADS, IDX_DIM).transpose(0, 2, 1, 3)
        a = _attn_sample(q, qi4, wi, cache_k[l].reshape(NB, P, kvw), cache_v[l].reshape(NB, P, kvw),
                         cache_kidx[l], kb, vf, kif, k_top_s)
        wmix, bmix = _mix_weights(sgu_w[l], sgu_b[l], TN, group_dim)
        xs, ys = _outffn(xs, a, sga, sgb, u, vn, wmix, bmix, wo, g2, wg, wu, wd, gf, final)
        outs[3].append(kf.reshape(NB, TN, N_KV_HEADS, HEAD_DIM))
        outs[4].append(vf.reshape(NB, TN, N_KV_HEADS, HEAD_DIM))
        outs[5].append(kif.reshape(NB, TN, IDX_DIM))
        outs[6].append(vn.reshape(NB, TN, D))

    return (yp.reshape(B, S, D), ys.reshape(NB, TN, D)) + tuple(jnp.stack(o) for o in outs)
```

```python
import functools
import math

import jax
import jax.numpy as jnp
from jax import lax
from jax.experimental import pallas as pl
from jax.experimental.pallas import tpu as pltpu

CHUNK = 64
N_HEADS = 8
HEAD_DIM = 128
N_KV_HEADS = 2
KV_GROUP = N_HEADS // N_KV_HEADS
N_IDX_HEADS = 8
IDX_DIM = 64
TOPK_MAX = 256
SGU_CHUNK = 128
SGU_GROUPS = 8
ROPE_THETA = 10000.0
EPS = 1e-6

LANES = 128
TOKEN_TILE = 256
Q_TILE = 256
KEY_TILE = 256
VMEM_LIMIT = 56 * 1024 * 1024

INT_MIN = -(2 ** 31)
INT_MAX = 2 ** 31 - 1
MASK_BIAS = -1e30
LOG2E = 1.4426950408889634
Q_SCALE = (HEAD_DIM ** -0.5) * LOG2E

F32 = jnp.float32
BF16 = jnp.bfloat16
NT_DIMS = (((1,), (1,)), ((), ()))


def _const_spec(shape):
    nd = len(shape)
    return pl.BlockSpec(shape, lambda *_: (0,) * nd, pipeline_mode=pl.Buffered(1))


def _sigmoid(x):
    return 1.0 / (1.0 + jnp.exp(-x))


def _rms_norm(x, g):
    return x * lax.rsqrt(jnp.mean(x * x, axis=-1, keepdims=True) + EPS) * g


def _order_key(score):
    bits = pltpu.bitcast(score, jnp.int32)
    return bits ^ ((bits >> 31) & INT_MAX)


def _midpoint(lo, hi):
    return (lo >> 1) + (hi >> 1) + (lo & hi & 1)


_C_Q = 0
_C_K = _C_Q + N_HEADS * HEAD_DIM
_C_V = _C_K + N_KV_HEADS * HEAD_DIM
_C_QI = _C_V + N_KV_HEADS * HEAD_DIM
_C_KI = _C_QI + N_IDX_HEADS * IDX_DIM
_C_WI = _C_KI + LANES
_C_U = _C_WI + LANES


def _inproj_kernel(x_ref, g_ref, wa_ref, wb_ref, cosh_ref, sinh_ref, cosi_ref, sini_ref,
                   lng_ref, lnb_ref,
                   q_ref, kf_ref, kb_ref, vf_ref, vt_ref, qi_ref, kif_ref, kze_ref, kzo_ref,
                   wi_ref, wit_ref, u_ref, vn_ref, sga_ref, sgb_ref, *, d_model):
    hb = _rms_norm(x_ref[...], g_ref[...]).astype(BF16)

    def proj(c0, width):
        return jnp.dot(hb, wa_ref[:, c0:c0 + width], preferred_element_type=F32)

    cosh, sinh = cosh_ref[...], sinh_ref[...]
    cosi, sini = cosi_ref[...], sini_ref[...]
    lane = lax.broadcasted_iota(jnp.int32, cosi.shape, 1)
    first_half = (lane % IDX_DIM) < (IDX_DIM // 2)

    def rope_head(x):
        return x * cosh + pltpu.roll(x, HEAD_DIM // 2, 1) * sinh

    def rope_idx(x):
        partner = jnp.where(first_half, pltpu.roll(x, LANES - IDX_DIM // 2, 1),
                            pltpu.roll(x, IDX_DIM // 2, 1))
        return x * cosi + partner * sini

    xq = proj(_C_Q, N_HEADS * HEAD_DIM)
    for h in range(N_HEADS):
        sl = slice(h * HEAD_DIM, (h + 1) * HEAD_DIM)
        q_ref[:, sl] = (rope_head(xq[:, sl]) * Q_SCALE).astype(BF16)

    xk = proj(_C_K, N_KV_HEADS * HEAD_DIM)
    for h in range(N_KV_HEADS):
        sl = slice(h * HEAD_DIM, (h + 1) * HEAD_DIM)
        kr = rope_head(xk[:, sl])
        kf_ref[:, sl] = kr
        kb_ref[:, sl] = kr.astype(BF16)

    vf_ref[...] = proj(_C_V, N_KV_HEADS * HEAD_DIM)

    xqi = proj(_C_QI, N_IDX_HEADS * IDX_DIM)
    for p in range(N_IDX_HEADS * IDX_DIM // LANES):
        sl = slice(p * LANES, (p + 1) * LANES)
        qi_ref[:, sl] = (rope_idx(xqi[:, sl]) * (IDX_DIM ** -0.5)).astype(BF16)

    kk = rope_idx(proj(_C_KI, LANES))
    kif_ref[...] = kk[:, :IDX_DIM]
    low = lane < IDX_DIM
    kze_ref[...] = jnp.where(low, kk, 0.0).astype(BF16)
    kzo_ref[...] = jnp.where(low, 0.0, kk).astype(BF16)

    wi_ref[...] = proj(_C_WI, LANES) * (N_IDX_HEADS ** -0.5)

    tb = lax.dot_general(wb_ref[...], hb, NT_DIMS, preferred_element_type=F32)
    kvw = N_KV_HEADS * HEAD_DIM
    vt_ref[0] = tb[:kvw].astype(BF16)
    wit_ref[...] = tb[kvw:] * (N_IDX_HEADS ** -0.5)

    u_ref[...] = jax.nn.gelu(proj(_C_U, d_model), approximate=True)
    gv = jax.nn.gelu(proj(_C_U + d_model, d_model), approximate=True)
    mu = jnp.mean(gv, axis=-1, keepdims=True)
    dv = gv - mu
    var = jnp.mean(dv * dv, axis=-1, keepdims=True)
    vn_ref[...] = dv * lax.rsqrt(var + EPS) * lng_ref[...] + lnb_ref[...]
    sga_ref[...] = _sigmoid(proj(_C_U + 2 * d_model, d_model))
    sgb_ref[...] = _sigmoid(proj(_C_U + 3 * d_model, d_model))


def _inproj(x, g, wa, wb, tables, lng, lnb):
    T, D = x.shape
    tm = TOKEN_TILE
    assert T % tm == 0
    kvw = N_KV_HEADS * HEAD_DIM
    qiw = N_IDX_HEADS * IDX_DIM
    row = lambda w: pl.BlockSpec((tm, w), lambda i: (i, 0))
    out_shape = (
        jax.ShapeDtypeStruct((T, N_HEADS * HEAD_DIM), BF16),
        jax.ShapeDtypeStruct((T, kvw), F32),
        jax.ShapeDtypeStruct((T, kvw), BF16),
        jax.ShapeDtypeStruct((T, kvw), F32),
        jax.ShapeDtypeStruct((T // tm, kvw, tm), BF16),
        jax.ShapeDtypeStruct((T, qiw), BF16),
        jax.ShapeDtypeStruct((T, IDX_DIM), F32),
        jax.ShapeDtypeStruct((T, LANES), BF16),
        jax.ShapeDtypeStruct((T, LANES), BF16),
        jax.ShapeDtypeStruct((T, LANES), F32),
        jax.ShapeDtypeStruct((16, T), F32),
        jax.ShapeDtypeStruct((T, D), F32),
        jax.ShapeDtypeStruct((T, D), F32),
        jax.ShapeDtypeStruct((T, D), F32),
        jax.ShapeDtypeStruct((T, D), F32),
    )
    out_specs = (
        row(N_HEADS * HEAD_DIM), row(kvw), row(kvw), row(kvw),
        pl.BlockSpec((1, kvw, tm), lambda i: (i, 0, 0)),
        row(qiw), row(IDX_DIM), row(LANES), row(LANES), row(LANES),
        pl.BlockSpec((16, tm), lambda i: (0, i)),
        row(D), row(D), row(D), row(D),
    )
    in_specs = [row(D), _const_spec((1, D)), _const_spec(wa.shape), _const_spec(wb.shape),
                row(LANES), row(LANES), row(LANES), row(LANES),
                _const_spec((1, D)), _const_spec((1, D))]
    return pl.pallas_call(
        functools.partial(_inproj_kernel, d_model=D),
        grid=(T // tm,),
        in_specs=in_specs, out_specs=out_specs, out_shape=out_shape,
        compiler_params=pltpu.CompilerParams(dimension_semantics=("arbitrary",),
                                             vmem_limit_bytes=VMEM_LIMIT),
        name="inproj",
    )(x, g, wa, wb, *tables, lng, lnb)


def _attn_prompt_kernel(q_ref, qi_ref, wit_ref, kze_ref, kzo_ref, k_ref, vt_ref, o_ref,
                        keys_ref, hi_ref, lo_ref, acc_ref, m_ref, l_ref, mx_ref, s_ref, p_ref, bias_ref,
                        *, k_top):
    tq, kb_sz = Q_TILE, KEY_TILE
    qb = pl.program_id(1)
    nkb = (qb + 1) * (tq // kb_sz)
    q_chunk = (qb * tq + lax.broadcasted_iota(jnp.int32, (1, tq), 1)) // CHUNK
    row_iota = lax.broadcasted_iota(jnp.int32, (kb_sz, 1), 0)

    def key_slice(kb):
        return pl.ds(pl.multiple_of(kb * kb_sz, kb_sz), kb_sz)

    n_blocks = keys_ref.shape[0] // kb_sz
    n_pairs = (nkb + 1) // 2

    def clamp(blk):
        return jnp.minimum(blk, n_blocks - 1)

    def idx_dots(blk, slot):
        ks = key_slice(blk)
        kze, kzo = kze_ref[ks, :], kzo_ref[ks, :]
        for p in range(N_IDX_HEADS // 2):
            qp = qi_ref[:, p * LANES:(p + 1) * LANES]
            s_ref[slot, 2 * p] = lax.dot_general(kze, qp, NT_DIMS, preferred_element_type=F32)
            s_ref[slot, 2 * p + 1] = lax.dot_general(kzo, qp, NT_DIMS, preferred_element_type=F32)

    def idx_keys(blk, slot):
        ks = key_slice(blk)
        score = jnp.zeros((kb_sz, tq), F32)
        for j in range(N_IDX_HEADS):
            score = score + wit_ref[j:j + 1, :] * jnp.maximum(s_ref[slot, j], 0.0)
        k_chunk = (blk * kb_sz + row_iota) // CHUNK
        key = jnp.where(k_chunk <= q_chunk, _order_key(score), INT_MIN)
        keys_ref[ks, :] = key
        hi_ref[ks, :] = (key >> 16).astype(jnp.int16)
        lo_ref[ks, :] = ((key & 0xFFFF) - 2 ** 15).astype(jnp.int16)

    def score_body(j, carry):
        b0 = 2 * j
        idx_dots(clamp(b0 + 1), 1)
        idx_keys(b0, 0)
        idx_dots(clamp(b0 + 2), 0)
        idx_keys(clamp(b0 + 1), 1)
        return carry

    idx_dots(0, 0)
    lax.fori_loop(0, n_pairs, score_body, 0)

    n_acc = 4
    rows16 = 16

    def fold_rows(x, accs):
        accs = list(accs)
        for r in range(kb_sz // rows16):
            accs[r % n_acc] = accs[r % n_acc] + x[r * rows16:(r + 1) * rows16, :]
        return tuple(accs)

    def zero_accs():
        return tuple(jnp.zeros((rows16, tq), jnp.int16) for _ in range(n_acc))

    def total(accs):
        tot = accs[0].astype(jnp.int32)
        for a in accs[1:]:
            tot = tot + a.astype(jnp.int32)
        return jnp.sum(tot, axis=0, keepdims=True)

    def count_ge(plane_ref, t):
        t16 = jnp.broadcast_to(t, (kb_sz, tq)).astype(jnp.int16)

        def body(kb, accs):
            ge = (plane_ref[key_slice(kb), :] >= t16).astype(jnp.int16)
            return fold_rows(ge, accs)

        return total(lax.fori_loop(0, nkb, body, zero_accs()))

    def bisect16(plane_ref, need):
        def body(_, carry):
            lo, hi = carry
            mid = (lo + hi) >> 1
            ok = count_ge(plane_ref, mid) >= need
            return jnp.where(ok, mid, lo), jnp.where(ok, hi, mid)

        lo, _ = lax.fori_loop(0, 16, body, (jnp.full((1, tq), -(2 ** 15), jnp.int32),
                                            jnp.full((1, tq), 2 ** 15, jnp.int32)))
        return lo

    t_hi = bisect16(hi_ref, k_top)

    t_hi16 = jnp.broadcast_to(t_hi, (kb_sz, tq)).astype(jnp.int16)

    def tie_body(kb, accs):
        ks = key_slice(kb)
        hi_part = hi_ref[ks, :]
        lo_ref[ks, :] = jnp.where(hi_part == t_hi16, lo_ref[ks, :], jnp.int16(-(2 ** 15)))
        return fold_rows((hi_part > t_hi16).astype(jnp.int16), accs)

    n_above = total(lax.fori_loop(0, nkb, tie_body, zero_accs()))
    t_lo = bisect16(lo_ref, k_top - n_above)
    thr = jnp.maximum((t_hi << 16) + (t_lo + 2 ** 15), INT_MIN + 1)

    m_ref[...] = jnp.full(m_ref.shape, MASK_BIAS, F32)
    l_ref[...] = jnp.zeros(l_ref.shape, F32)
    acc_ref[...] = jnp.zeros(acc_ref.shape, F32)

    rc = 32
    n_rc = kb_sz // rc

    def fold8(x):
        return x.reshape(rc // 8, 8, tq)

    def qk_dots(blk, slot):
        ks = key_slice(blk)
        for h in range(N_HEADS):
            n = h // KV_GROUP
            k_n = k_ref[ks, n * HEAD_DIM:(n + 1) * HEAD_DIM]
            q_h = q_ref[:, h * HEAD_DIM:(h + 1) * HEAD_DIM]
            s_ref[slot, h] = lax.dot_general(k_n, q_h, NT_DIMS, preferred_element_type=F32)

    def softmax_block(blk, slot):
        thr_b = jnp.where(blk < nkb, thr, INT_MAX)
        bias_ref[slot] = jnp.where(keys_ref[key_slice(clamp(blk)), :] >= thr_b, 0.0, MASK_BIAS)
        for h in range(N_HEADS):
            mx = jnp.full((8, tq), MASK_BIAS, F32)
            for r in range(n_rc):
                rows = slice(r * rc, (r + 1) * rc)
                sc = s_ref[slot, h, rows, :] + bias_ref[slot, rows, :]
                s_ref[slot, h, rows, :] = sc
                mx = jnp.maximum(mx, jnp.max(fold8(sc), axis=0))
            mx_ref[h:h + 1, :] = jnp.max(mx, axis=0, keepdims=True)
        m_old = m_ref[...]
        m_new = jnp.maximum(m_old, mx_ref[...])
        alpha = jnp.exp2(m_old - m_new)
        m_ref[...] = m_new
        for h in range(N_HEADS):
            m_h = jnp.broadcast_to(m_new[h:h + 1, :], (rc, tq))
            ps = jnp.zeros((8, tq), F32)
            for r in range(n_rc):
                rows = slice(r * rc, (r + 1) * rc)
                p = jnp.exp2(s_ref[slot, h, rows, :] - m_h)
                ps = ps + jnp.sum(fold8(p), axis=0)
                p_ref[slot, h, rows, :] = p.astype(BF16)
            mx_ref[h:h + 1, :] = jnp.sum(ps, axis=0, keepdims=True)
        l_ref[...] = alpha * l_ref[...] + mx_ref[...]
        return alpha

    def pv_update(blk, slot, alpha):
        for h in range(N_HEADS):
            n = h // KV_GROUP
            vt_n = vt_ref[blk, n * HEAD_DIM:(n + 1) * HEAD_DIM, :]
            pv = jnp.dot(vt_n, p_ref[slot, h], preferred_element_type=F32)
            acc_ref[h] = alpha[h:h + 1, :] * acc_ref[h] + pv

    def attn_body(j, carry):
        b0 = 2 * j
        qk_dots(clamp(b0 + 1), 1)
        pv_update(b0, 0, softmax_block(b0, 0))
        qk_dots(clamp(b0 + 2), 0)
        pv_update(clamp(b0 + 1), 1, softmax_block(b0 + 1, 1))
        return carry

    qk_dots(0, 0)
    lax.fori_loop(0, n_pairs, attn_body, 0)

    for h in range(N_HEADS):
        o_t = acc_ref[h] / l_ref[h:h + 1, :]
        o_ref[:, h * HEAD_DIM:(h + 1) * HEAD_DIM] = o_t.T


def _attn_prompt(q, qi, wit, kze, kzo, kb, vt, batch, seq, k_top):
    tq = Q_TILE
    assert seq % tq == 0 and Q_TILE % KEY_TILE == 0 and KEY_TILE == TOKEN_TILE
    nq = seq // tq
    kvw = N_KV_HEADS * HEAD_DIM
    qrow = lambda w: pl.BlockSpec((tq, w), lambda b, i: (b * nq + i, 0))
    per_batch = lambda w: pl.BlockSpec((seq, w), lambda b, i: (b, 0), pipeline_mode=pl.Buffered(1))
    return pl.pallas_call(
        functools.partial(_attn_prompt_kernel, k_top=k_top),
        grid=(batch, nq),
        in_specs=[qrow(N_HEADS * HEAD_DIM), qrow(N_IDX_HEADS * IDX_DIM),
                  pl.BlockSpec((16, tq), lambda b, i: (0, b * nq + i)),
                  per_batch(LANES), per_batch(LANES), per_batch(kvw),
                  pl.BlockSpec((seq // KEY_TILE, kvw, KEY_TILE), lambda b, i: (b, 0, 0),
                               pipeline_mode=pl.Buffered(1))],
        out_specs=qrow(N_HEADS * HEAD_DIM),
        out_shape=jax.ShapeDtypeStruct((batch * seq, N_HEADS * HEAD_DIM), F32),
        scratch_shapes=[pltpu.VMEM((seq, tq), jnp.int32),
                        pltpu.VMEM((seq, tq), jnp.int16),
                        pltpu.VMEM((seq, tq), jnp.int16),
                        pltpu.VMEM((N_HEADS, HEAD_DIM, tq), F32),
                        pltpu.VMEM((N_HEADS, tq), F32),
                        pltpu.VMEM((N_HEADS, tq), F32),
                        pltpu.VMEM((N_HEADS, tq), F32),
                        pltpu.VMEM((2, N_HEADS, KEY_TILE, tq), F32),
                        pltpu.VMEM((2, N_HEADS, KEY_TILE, tq), BF16),
                        pltpu.VMEM((2, KEY_TILE, tq), F32)],
        compiler_params=pltpu.CompilerParams(dimension_semantics=("arbitrary", "arbitrary"),
                                             vmem_limit_bytes=VMEM_LIMIT),
        name="attn_prompt",
    )(q, qi, wit, kze, kzo, kb, vt)


def _attn_sample_kernel(q_ref, qi_ref, wi_ref, ck_ref, cv_ref, cki_ref, kn_ref, vn_ref, kin_ref,
                        o_ref, *, k_top, past_len, tn):
    b = pl.program_id(0)
    group = LANES // tn
    ck = ck_ref[0].astype(BF16)
    cv = cv_ref[0].astype(BF16)
    cki = cki_ref[0].astype(BF16)
    kn = kn_ref[...]
    vn = vn_ref[...].astype(BF16)
    kin = kin_ref[...].astype(BF16)

    sp = jnp.zeros((tn, past_len), F32)
    sn = jnp.zeros((tn, LANES), F32)
    for h in range(N_IDX_HEADS):
        qh = qi_ref[0, h]
        w = wi_ref[:, h:h + 1]
        dp = lax.dot_general(qh, cki, NT_DIMS, preferred_element_type=F32)
        dn = lax.dot_general(qh, kin, NT_DIMS, preferred_element_type=F32)
        sp = sp + w * jnp.maximum(dp, 0.0)
        sn = sn + w * jnp.maximum(dn, 0.0)

    j = lax.broadcasted_iota(jnp.int32, (tn, LANES), 1)
    t = lax.broadcasted_iota(jnp.int32, (tn, LANES), 0)
    own = (j // tn) == (b % group)
    vis = ((past_len + j % tn) // CHUNK) <= ((past_len + t) // CHUNK)
    key_p = _order_key(sp)
    key_n = jnp.where(own & vis, _order_key(sn), INT_MIN)

    def bisect_body(_, carry):
        lo, hi = carry
        mid = _midpoint(lo, hi)
        cnt = (jnp.sum((key_p >= mid).astype(jnp.int32), axis=1, keepdims=True)
               + jnp.sum((key_n >= mid).astype(jnp.int32), axis=1, keepdims=True))
        ok = cnt >= k_top
        return jnp.where(ok, mid, lo), jnp.where(ok, hi, mid)

    thr, _ = lax.fori_loop(
        0, 32, bisect_body,
        (jnp.full((tn, 1), INT_MIN + 1, jnp.int32), jnp.full((tn, 1), INT_MAX, jnp.int32)))

    bias_p = jnp.where(key_p >= thr, 0.0, MASK_BIAS)
    bias_n = jnp.where(key_n >= thr, 0.0, MASK_BIAS)
    for h in range(N_HEADS):
        n = h // KV_GROUP
        hs = slice(n * HEAD_DIM, (n + 1) * HEAD_DIM)
        q_h = q_ref[:, h * HEAD_DIM:(h + 1) * HEAD_DIM]
        s1 = lax.dot_general(q_h, ck[:, hs], NT_DIMS, preferred_element_type=F32) + bias_p
        s2 = lax.dot_general(q_h, kn[:, hs], NT_DIMS, preferred_element_type=F32) + bias_n
        m = jnp.maximum(jnp.max(s1, axis=1, keepdims=True), jnp.max(s2, axis=1, keepdims=True))
        p1 = jnp.exp2(s1 - m)
        p2 = jnp.exp2(s2 - m)
        l = jnp.sum(p1, axis=1, keepdims=True) + jnp.sum(p2, axis=1, keepdims=True)
        o = (jnp.dot(p1.astype(BF16), cv[:, hs], preferred_element_type=F32)
             + jnp.dot(p2.astype(BF16), vn[:, hs], preferred_element_type=F32))
        o_ref[:, h * HEAD_DIM:(h + 1) * HEAD_DIM] = o / l


def _attn_sample(q, qi4, wi, ck, cv, cki, kb, vf, kif, k_top):
    nb, past_len, kvw = ck.shape
    tn = q.shape[0] // nb
    assert LANES % tn == 0 and tn % 16 == 0
    group = LANES // tn
    row = lambda w: pl.BlockSpec((tn, w), lambda b: (b, 0))
    shared = lambda w: pl.BlockSpec((LANES, w), lambda b: (b // group, 0))
    cache = lambda w: pl.BlockSpec((1, past_len, w), lambda b: (b, 0, 0))
    return pl.pallas_call(
        functools.partial(_attn_sample_kernel, k_top=k_top, past_len=past_len, tn=tn),
        grid=(nb,),
        in_specs=[row(N_HEADS * HEAD_DIM),
                  pl.BlockSpec((1, N_IDX_HEADS, tn, IDX_DIM), lambda b: (b, 0, 0, 0)),
                  row(LANES), cache(kvw), cache(kvw), cache(IDX_DIM),
                  shared(kvw), shared(kvw), shared(IDX_DIM)],
        out_specs=row(N_HEADS * HEAD_DIM),
        out_shape=jax.ShapeDtypeStruct((nb * tn, N_HEADS * HEAD_DIM), F32),
        compiler_params=pltpu.CompilerParams(dimension_semantics=("arbitrary",),
                                             vmem_limit_bytes=VMEM_LIMIT),
        name="attn_sample",
    )(q, qi4, wi, ck, cv, cki, kb, vf, kif)


def _outffn_kernel(x_ref, a_ref, sga_ref, sgb_ref, u_ref, vn_ref, wmix_ref, bmix_ref, wo_ref,
                   g2_ref, wg_ref, wu_ref, wd_ref, gf_ref, xo_ref, *rest, final):
    if final:
        y_ref, z_ref = rest
    else:
        (z_ref,) = rest
    tm = x_ref.shape[0]
    group_dim = wmix_ref.shape[-1]
    for c in range(tm // SGU_CHUNK):
        rs = slice(c * SGU_CHUNK, (c + 1) * SGU_CHUNK)
        for g in range(SGU_GROUPS):
            cs = slice(g * group_dim, (g + 1) * group_dim)
            mixed = jnp.dot(wmix_ref[g], vn_ref[rs, cs].astype(BF16),
                            preferred_element_type=F32) + bmix_ref[:, cs]
            z = sga_ref[rs, cs] * a_ref[rs, cs] + sgb_ref[rs, cs] * (u_ref[rs, cs] * mixed)
            z_ref[rs, cs] = z.astype(BF16)
    x1 = x_ref[...] + jnp.dot(z_ref[...], wo_ref[...], preferred_element_type=F32)
    h2 = _rms_norm(x1, g2_ref[...]).astype(BF16)
    gate = jnp.dot(h2, wg_ref[...], preferred_element_type=F32)
    up = jnp.dot(h2, wu_ref[...], preferred_element_type=F32)
    ff = (gate * _sigmoid(gate) * up).astype(BF16)
    x2 = x1 + jnp.dot(ff, wd_ref[...], preferred_element_type=F32)
    xo_ref[...] = x2
    if final:
        y_ref[...] = _rms_norm(x2, gf_ref[...])


def _outffn(x, a, sga, sgb, u, vn, wmix, bmix, wo, g2, wg, wu, wd, gf, final):
    T, D = x.shape
    tm = TOKEN_TILE
    row = pl.BlockSpec((tm, D), lambda i: (i, 0))
    n_out = 2 if final else 1
    out = pl.pallas_call(
        functools.partial(_outffn_kernel, final=final),
        grid=(T // tm,),
        in_specs=[row] * 6 + [_const_spec(wmix.shape), _const_spec(bmix.shape),
                              _const_spec(wo.shape), _const_spec((1, D)), _const_spec(wg.shape),
                              _const_spec(wu.shape), _const_spec(wd.shape), _const_spec((1, D))],
        out_specs=(row,) * n_out,
        out_shape=(jax.ShapeDtypeStruct((T, D), F32),) * n_out,
        scratch_shapes=[pltpu.VMEM((tm, D), BF16)],
        compiler_params=pltpu.CompilerParams(dimension_semantics=("arbitrary",),
                                             vmem_limit_bytes=VMEM_LIMIT),
        name="outffn",
    )(x, a, sga, sgb, u, vn, wmix, bmix, wo, g2, wg, wu, wd, gf)
    return out if final else (out[0], None)


def _rope_tables(pos, d):
    inv = ROPE_THETA ** (-jnp.arange(0, d, 2, dtype=F32) / d)
    ang = pos.astype(F32)[:, None] * inv[None, :]
    cos, sin = jnp.cos(ang), jnp.sin(ang)
    c = jnp.concatenate([cos, cos], axis=-1)
    s = jnp.concatenate([-sin, sin], axis=-1)
    reps = LANES // d
    return jnp.tile(c, (1, reps)), jnp.tile(s, (1, reps))


def _mix_weights(sgu_w, sgu_b, n, group_dim):
    p = jnp.arange(n)
    mask = (p[None, :] // CHUNK) <= (p[:, None] // CHUNK)
    w = jnp.where(mask[None], sgu_w[:, :n, :n], 0.0)
    reps = SGU_CHUNK // n
    eye = jnp.eye(reps, dtype=w.dtype)
    wbd = jnp.einsum('ab,gij->gaibj', eye, w).reshape(SGU_GROUPS, SGU_CHUNK, SGU_CHUNK)
    b = jnp.tile(sgu_b[:, :n], (1, reps))
    bfull = jnp.repeat(b.T, group_dim, axis=1)
    return wbd.astype(BF16), bfull


def kernel(x_prompt, x_sample, cache_k, cache_v, cache_kidx, norm1_g, w_in, ln_v_g, ln_v_b, sgu_w, sgu_b, w_out, norm2_g, w_gate, w_up, w_down, final_norm_g):
    B, S, D = x_prompt.shape
    NB, TN, _ = x_sample.shape
    depth, _, P = cache_k.shape[:3]
    assert D == N_HEADS * HEAD_DIM and SGU_CHUNK % TN == 0 and S % SGU_CHUNK == 0
    kvw = N_KV_HEADS * HEAD_DIM
    qiw = N_IDX_HEADS * IDX_DIM
    k_top_p = min(TOPK_MAX, S // 4)
    k_top_s = min(TOPK_MAX, (P + TN) // 4)
    group_dim = D // SGU_GROUPS

    pos_p = jnp.tile(jnp.arange(S), B)
    pos_s = jnp.tile(P + jnp.arange(TN), NB)
    tab_p = _rope_tables(pos_p, HEAD_DIM) + _rope_tables(pos_p, IDX_DIM)
    tab_s = _rope_tables(pos_s, HEAD_DIM) + _rope_tables(pos_s, IDX_DIM)

    xp = x_prompt.reshape(B * S, D)
    xs = x_sample.reshape(NB * TN, D)
    row = lambda v: v.reshape(1, -1)
    outs = [[] for _ in range(7)]
    yp = ys = None
    for l in range(depth):
        w = w_in[l]
        o = 0
        seg = []
        for width in (D, kvw, kvw, qiw, IDX_DIM, N_IDX_HEADS, D, D, D, D):
            seg.append(w[:, o:o + width])
            o += width
        wq, wk, wv, wqi, wki, wwi, wu_, wvs, wga, wgb = seg
        wwi_pad = jnp.pad(wwi, ((0, 0), (0, LANES - N_IDX_HEADS)))
        wa = jnp.concatenate([wq, wk, wv, wqi, wki, wki, wwi_pad, wu_, wvs, wga, wgb],
                             axis=1).astype(BF16)
        wb = jnp.concatenate([wv.T, jnp.pad(wwi.T, ((0, 16 - N_IDX_HEADS), (0, 0)))],
                             axis=0).astype(BF16)
        wo, wg, wu, wd = (t[l].astype(BF16) for t in (w_out, w_gate, w_up, w_down))
        g1, g2, lng, lnb = row(norm1_g[l]), row(norm2_g[l]), row(ln_v_g[l]), row(ln_v_b[l])
        gf = row(final_norm_g)
        final = l == depth - 1

        (q, kf, kb, vf, vt, qi, kif, kze, kzo, _, wit, u, vn, sga, sgb) = _inproj(
            xp, g1, wa, wb, tab_p, lng, lnb)
        a = _attn_prompt(q, qi, wit, kze, kzo, kb, vt, B, S, k_top_p)
        wmix, bmix = _mix_weights(sgu_w[l], sgu_b[l], SGU_CHUNK, group_dim)
        xp, yp = _outffn(xp, a, sga, sgb, u, vn, wmix, bmix, wo, g2, wg, wu, wd, gf, final)
        outs[0].append(kf.reshape(B, S, N_KV_HEADS, HEAD_DIM))
        outs[1].append(vf.reshape(B, S, N_KV_HEADS, HEAD_DIM))
        outs[2].append(kif.reshape(B, S, IDX_DIM))

        (q, kf, kb, vf, _, qi, kif, _, _, wi, _, u, vn, sga, sgb) = _inproj(
            xs, g1, wa, wb, tab_s, lng, lnb)
        qi4 = qi.reshape(NB, TN, N_IDX_HEADS, IDX_DIM).transpose(0, 2, 1, 3)
        a = _attn_sample(q, qi4, wi, cache_k[l].reshape(NB, P, kvw), cache_v[l].reshape(NB, P, kvw),
                         cache_kidx[l], kb, vf, kif, k_top_s)
        wmix, bmix = _mix_weights(sgu_w[l], sgu_b[l], TN, group_dim)
        xs, ys = _outffn(xs, a, sga, sgb, u, vn, wmix, bmix, wo, g2, wg, wu, wd, gf, final)
        outs[3].append(kf.reshape(NB, TN, N_KV_HEADS, HEAD_DIM))
        outs[4].append(vf.reshape(NB, TN, N_KV_HEADS, HEAD_DIM))
        outs[5].append(kif.reshape(NB, TN, IDX_DIM))
        outs[6].append(vn.reshape(NB, TN, D))

    return (yp.reshape(B, S, D), ys.reshape(NB, TN, D)) + tuple(jnp.stack(o) for o in outs)
```

```python
import functools
import math

import jax
import jax.numpy as jnp
from jax import lax
from jax.experimental import pallas as pl
from jax.experimental.pallas import tpu as pltpu

CHUNK = 64
N_HEADS = 8
HEAD_DIM = 128
N_KV_HEADS = 2
KV_GROUP = N_HEADS // N_KV_HEADS
N_IDX_HEADS = 8
IDX_DIM = 64
TOPK_MAX = 256
SGU_CHUNK = 128
SGU_GROUPS = 8
ROPE_THETA = 10000.0
EPS = 1e-6

LANES = 128
TOKEN_TILE = 256
Q_TILE = 256
KEY_TILE = 256
VMEM_LIMIT = 56 * 1024 * 1024

INT_MIN = -(2 ** 31)
INT_MAX = 2 ** 31 - 1
MASK_BIAS = -1e30
LOG2E = 1.4426950408889634
Q_SCALE = (HEAD_DIM ** -0.5) * LOG2E

F32 = jnp.float32
BF16 = jnp.bfloat16
NT_DIMS = (((1,), (1,)), ((), ()))


def _const_spec(shape):
    nd = len(shape)
    return pl.BlockSpec(shape, lambda *_: (0,) * nd, pipeline_mode=pl.Buffered(1))


def _sigmoid(x):
    return 1.0 / (1.0 + jnp.exp(-x))


def _rms_norm(x, g):
    return x * lax.rsqrt(jnp.mean(x * x, axis=-1, keepdims=True) + EPS) * g


def _order_key(score):
    bits = pltpu.bitcast(score, jnp.int32)
    return bits ^ ((bits >> 31) & INT_MAX)


def _midpoint(lo, hi):
    return (lo >> 1) + (hi >> 1) + (lo & hi & 1)


_C_Q = 0
_C_K = _C_Q + N_HEADS * HEAD_DIM
_C_V = _C_K + N_KV_HEADS * HEAD_DIM
_C_QI = _C_V + N_KV_HEADS * HEAD_DIM
_C_KI = _C_QI + N_IDX_HEADS * IDX_DIM
_C_WI = _C_KI + LANES
_C_U = _C_WI + LANES


def _inproj_kernel(*refs, d_model, n_alias):
    (x_ref, g_ref, wa_ref, wb_ref, cosh_ref, sinh_ref, cosi_ref, sini_ref,
     lng_ref, lnb_ref) = refs[:10]
    (q_ref, kf_ref, kb_ref, vf_ref, vb_ref, vt_ref, qi_ref, kif_ref, kze_ref, kzo_ref,
     wi_ref, wit_ref, u_ref, vn_ref, sga_ref, sgb_ref) = refs[10 + n_alias:]
    tm = x_ref.shape[0]
    hb = _rms_norm(x_ref[...], g_ref[0]).astype(BF16)

    def proj(c0, width):
        return jnp.dot(hb, wa_ref[0, :, c0:c0 + width], preferred_element_type=F32)

    def head_rows(n):
        return pl.ds(n, tm, stride=N_KV_HEADS)

    cosh, sinh = cosh_ref[...], sinh_ref[...]
    cosi, sini = cosi_ref[...], sini_ref[...]
    lane = lax.broadcasted_iota(jnp.int32, cosi.shape, 1)
    first_half = (lane % IDX_DIM) < (IDX_DIM // 2)

    def rope_head(x):
        return x * cosh + pltpu.roll(x, HEAD_DIM // 2, 1) * sinh

    def rope_idx(x):
        partner = jnp.where(first_half, pltpu.roll(x, LANES - IDX_DIM // 2, 1),
                            pltpu.roll(x, IDX_DIM // 2, 1))
        return x * cosi + partner * sini

    xq = proj(_C_Q, N_HEADS * HEAD_DIM)
    for h in range(N_HEADS):
        sl = slice(h * HEAD_DIM, (h + 1) * HEAD_DIM)
        q_ref[:, sl] = (rope_head(xq[:, sl]) * Q_SCALE).astype(BF16)

    xk = proj(_C_K, N_KV_HEADS * HEAD_DIM)
    for h in range(N_KV_HEADS):
        sl = slice(h * HEAD_DIM, (h + 1) * HEAD_DIM)
        kr = rope_head(xk[:, sl])
        kf_ref[0, head_rows(h), :] = kr
        kb_ref[:, sl] = kr.astype(BF16)

    xv = proj(_C_V, N_KV_HEADS * HEAD_DIM)
    vb_ref[...] = xv.astype(BF16)
    for h in range(N_KV_HEADS):
        vf_ref[0, head_rows(h), :] = xv[:, h * HEAD_DIM:(h + 1) * HEAD_DIM]

    xqi = proj(_C_QI, N_IDX_HEADS * IDX_DIM)
    for p in range(N_IDX_HEADS * IDX_DIM // LANES):
        sl = slice(p * LANES, (p + 1) * LANES)
        qi_ref[:, sl] = (rope_idx(xqi[:, sl]) * (IDX_DIM ** -0.5)).astype(BF16)

    kk = rope_idx(proj(_C_KI, LANES))
    kif_ref[0] = kk[:, :IDX_DIM]
    low = lane < IDX_DIM
    kze_ref[...] = jnp.where(low, kk, 0.0).astype(BF16)
    kzo_ref[...] = jnp.where(low, 0.0, kk).astype(BF16)

    wi_ref[...] = proj(_C_WI, LANES) * (N_IDX_HEADS ** -0.5)

    tb = lax.dot_general(wb_ref[0], hb, NT_DIMS, preferred_element_type=F32)
    kvw = N_KV_HEADS * HEAD_DIM
    vt_ref[0] = tb[:kvw].astype(BF16)
    wit_ref[...] = tb[kvw:] * (N_IDX_HEADS ** -0.5)

    u_ref[...] = jax.nn.gelu(proj(_C_U, d_model), approximate=True)
    gv = jax.nn.gelu(proj(_C_U + d_model, d_model), approximate=True)
    mu = jnp.mean(gv, axis=-1, keepdims=True)
    dv = gv - mu
    var = jnp.mean(dv * dv, axis=-1, keepdims=True)
    vn_ref[...] = dv * lax.rsqrt(var + EPS) * lng_ref[0] + lnb_ref[0]
    sga_ref[...] = _sigmoid(proj(_C_U + 2 * d_model, d_model))
    sgb_ref[...] = _sigmoid(proj(_C_U + 3 * d_model, d_model))


def _layer_spec(shape, layer):
    nd = len(shape)
    return pl.BlockSpec((1,) + tuple(shape[1:]), lambda *_: (layer,) + (0,) * (nd - 1),
                        pipeline_mode=pl.Buffered(1))


def _inproj(x, layer, g, wa, wb, tables, lng, lnb, stacked):
    T, D = x.shape
    depth = wa.shape[0]
    tm = TOKEN_TILE
    assert T % tm == 0
    kvw = N_KV_HEADS * HEAD_DIM
    qiw = N_IDX_HEADS * IDX_DIM
    row = lambda w: pl.BlockSpec((tm, w), lambda i: (i, 0))
    kv_rows = N_KV_HEADS * T
    out_shape = (
        jax.ShapeDtypeStruct((T, N_HEADS * HEAD_DIM), BF16),
        jax.ShapeDtypeStruct((depth, kv_rows, HEAD_DIM), F32),
        jax.ShapeDtypeStruct((T, kvw), BF16),
        jax.ShapeDtypeStruct((depth, kv_rows, HEAD_DIM), F32),
        jax.ShapeDtypeStruct((T, kvw), BF16),
        jax.ShapeDtypeStruct((T // tm, kvw, tm), BF16),
        jax.ShapeDtypeStruct((T, qiw), BF16),
        jax.ShapeDtypeStruct((depth, T, IDX_DIM), F32),
        jax.ShapeDtypeStruct((T, LANES), BF16),
        jax.ShapeDtypeStruct((T, LANES), BF16),
        jax.ShapeDtypeStruct((T, LANES), F32),
        jax.ShapeDtypeStruct((16, T), F32),
        jax.ShapeDtypeStruct((T, D), F32),
        jax.ShapeDtypeStruct((T, D), F32),
        jax.ShapeDtypeStruct((T, D), F32),
        jax.ShapeDtypeStruct((T, D), F32),
    )
    kv_spec = pl.BlockSpec((1, N_KV_HEADS * tm, HEAD_DIM), lambda i: (layer, i, 0))
    out_specs = (
        row(N_HEADS * HEAD_DIM), kv_spec, row(kvw), kv_spec, row(kvw),
        pl.BlockSpec((1, kvw, tm), lambda i: (i, 0, 0)),
        row(qiw), pl.BlockSpec((1, tm, IDX_DIM), lambda i: (layer, i, 0)),
        row(LANES), row(LANES), row(LANES),
        pl.BlockSpec((16, tm), lambda i: (0, i)),
        row(D), row(D), row(D), row(D),
    )
    in_specs = [row(D), _layer_spec(g.shape, layer), _layer_spec(wa.shape, layer),
                _layer_spec(wb.shape, layer),
                row(LANES), row(LANES), row(LANES), row(LANES),
                _layer_spec(lng.shape, layer), _layer_spec(lnb.shape, layer)]
    args = [x, g, wa, wb, *tables, lng, lnb]
    aliases = {}
    if stacked is not None:
        for j, out_idx in enumerate((1, 3, 7)):
            aliases[len(args)] = out_idx
            in_specs.append(pl.BlockSpec(memory_space=pl.ANY))
            args.append(stacked[j])
    return pl.pallas_call(
        functools.partial(_inproj_kernel, d_model=D, n_alias=len(aliases)),
        grid=(T // tm,),
        in_specs=in_specs, out_specs=out_specs, out_shape=out_shape,
        input_output_aliases=aliases,
        compiler_params=pltpu.CompilerParams(dimension_semantics=("arbitrary",),
                                             vmem_limit_bytes=VMEM_LIMIT),
        name="inproj",
    )(*args)


def _attn_prompt_kernel(q_ref, qi_ref, wit_ref, kze_ref, kzo_ref, k_ref, vt_ref, o_ref,
                        keys_ref, hi_ref, lo_ref, acc_ref, m_ref, l_ref, mx_ref, s_ref, p_ref, bias_ref,
                        *, k_top):
    tq, kb_sz = Q_TILE, KEY_TILE
    qb = pl.program_id(1)
    nkb = (qb + 1) * (tq // kb_sz)
    q_chunk = (qb * tq + lax.broadcasted_iota(jnp.int32, (1, tq), 1)) // CHUNK
    row_iota = lax.broadcasted_iota(jnp.int32, (kb_sz, 1), 0)

    def key_slice(kb):
        return pl.ds(pl.multiple_of(kb * kb_sz, kb_sz), kb_sz)

    n_blocks = keys_ref.shape[0] // kb_sz
    n_pairs = (nkb + 1) // 2

    def clamp(blk):
        return jnp.minimum(blk, n_blocks - 1)

    def idx_dots(blk, slot):
        ks = key_slice(blk)
        kze, kzo = kze_ref[ks, :], kzo_ref[ks, :]
        for p in range(N_IDX_HEADS // 2):
            qp = qi_ref[:, p * LANES:(p + 1) * LANES]
            s_ref[slot, 2 * p] = lax.dot_general(kze, qp, NT_DIMS, preferred_element_type=F32)
            s_ref[slot, 2 * p + 1] = lax.dot_general(kzo, qp, NT_DIMS, preferred_element_type=F32)

    def idx_keys(blk, slot):
        ks = key_slice(blk)
        score = jnp.zeros((kb_sz, tq), F32)
        for j in range(N_IDX_HEADS):
            score = score + wit_ref[j:j + 1, :] * jnp.maximum(s_ref[slot, j], 0.0)
        k_chunk = (blk * kb_sz + row_iota) // CHUNK
        key = jnp.where(k_chunk <= q_chunk, _order_key(score), INT_MIN)
        keys_ref[ks, :] = key
        hi_ref[ks, :] = (key >> 16).astype(jnp.int16)
        lo_ref[ks, :] = ((key & 0xFFFF) - 2 ** 15).astype(jnp.int16)

    def score_body(j, carry):
        b0 = 2 * j
        idx_dots(clamp(b0 + 1), 1)
        idx_keys(b0, 0)
        idx_dots(clamp(b0 + 2), 0)
        idx_keys(clamp(b0 + 1), 1)
        return carry

    idx_dots(0, 0)
    lax.fori_loop(0, n_pairs, score_body, 0)

    n_acc = 4
    rows16 = 16

    def fold_rows(x, accs):
        accs = list(accs)
        for r in range(kb_sz // rows16):
            accs[r % n_acc] = accs[r % n_acc] + x[r * rows16:(r + 1) * rows16, :]
        return tuple(accs)

    def zero_accs():
        return tuple(jnp.zeros((rows16, tq), jnp.int16) for _ in range(n_acc))

    def total(accs):
        tot = accs[0].astype(jnp.int32)
        for a in accs[1:]:
            tot = tot + a.astype(jnp.int32)
        return jnp.sum(tot, axis=0, keepdims=True)

    def count_ge(plane_ref, t):
        t16 = jnp.broadcast_to(t, (kb_sz, tq)).astype(jnp.int16)

        def body(kb, accs):
            ge = (plane_ref[key_slice(kb), :] >= t16).astype(jnp.int16)
            return fold_rows(ge, accs)

        return total(lax.fori_loop(0, nkb, body, zero_accs()))

    def bisect16(plane_ref, need):
        def body(_, carry):
            lo, hi = carry
            mid = (lo + hi) >> 1
            ok = count_ge(plane_ref, mid) >= need
            return jnp.where(ok, mid, lo), jnp.where(ok, hi, mid)

        lo, _ = lax.fori_loop(0, 16, body, (jnp.full((1, tq), -(2 ** 15), jnp.int32),
                                            jnp.full((1, tq), 2 ** 15, jnp.int32)))
        return lo

    t_hi = bisect16(hi_ref, k_top)

    t_hi16 = jnp.broadcast_to(t_hi, (kb_sz, tq)).astype(jnp.int16)

    def tie_body(kb, accs):
        ks = key_slice(kb)
        hi_part = hi_ref[ks, :]
        lo_ref[ks, :] = jnp.where(hi_part == t_hi16, lo_ref[ks, :], jnp.int16(-(2 ** 15)))
        return fold_rows((hi_part > t_hi16).astype(jnp.int16), accs)

    n_above = total(lax.fori_loop(0, nkb, tie_body, zero_accs()))
    t_lo = bisect16(lo_ref, k_top - n_above)
    thr = jnp.maximum((t_hi << 16) + (t_lo + 2 ** 15), INT_MIN + 1)

    m_ref[...] = jnp.full(m_ref.shape, MASK_BIAS, F32)
    l_ref[...] = jnp.zeros(l_ref.shape, F32)
    acc_ref[...] = jnp.zeros(acc_ref.shape, F32)

    rc = 32
    n_rc = kb_sz // rc

    def fold8(x):
        return x.reshape(rc // 8, 8, tq)

    def qk_dots(blk, slot):
        ks = key_slice(blk)
        for h in range(N_HEADS):
            n = h // KV_GROUP
            k_n = k_ref[ks, n * HEAD_DIM:(n + 1) * HEAD_DIM]
            q_h = q_ref[:, h * HEAD_DIM:(h + 1) * HEAD_DIM]
            s_ref[slot, h] = lax.dot_general(k_n, q_h, NT_DIMS, preferred_element_type=F32)

    def softmax_block(blk, slot):
        thr_b = jnp.where(blk < nkb, thr, INT_MAX)
        bias_ref[slot] = jnp.where(keys_ref[key_slice(clamp(blk)), :] >= thr_b, 0.0, MASK_BIAS)
        for h in range(N_HEADS):
            mx = jnp.full((8, tq), MASK_BIAS, F32)
            for r in range(n_rc):
                rows = slice(r * rc, (r + 1) * rc)
                sc = s_ref[slot, h, rows, :] + bias_ref[slot, rows, :]
                s_ref[slot, h, rows, :] = sc
                mx = jnp.maximum(mx, jnp.max(fold8(sc), axis=0))
            mx_ref[h:h + 1, :] = jnp.max(mx, axis=0, keepdims=True)
        m_old = m_ref[...]
        m_new = jnp.maximum(m_old, mx_ref[...])
        alpha = jnp.exp2(m_old - m_new)
        m_ref[...] = m_new
        for h in range(N_HEADS):
            m_h = jnp.broadcast_to(m_new[h:h + 1, :], (rc, tq))
            ps = jnp.zeros((8, tq), F32)
            for r in range(n_rc):
                rows = slice(r * rc, (r + 1) * rc)
                p = jnp.exp2(s_ref[slot, h, rows, :] - m_h)
                ps = ps + jnp.sum(fold8(p), axis=0)
                p_ref[slot, h, rows, :] = p.astype(BF16)
            mx_ref[h:h + 1, :] = jnp.sum(ps, axis=0, keepdims=True)
        l_ref[...] = alpha * l_ref[...] + mx_ref[...]
        return alpha

    def pv_update(blk, slot, alpha):
        for h in range(N_HEADS):
            n = h // KV_GROUP
            vt_n = vt_ref[blk, n * HEAD_DIM:(n + 1) * HEAD_DIM, :]
            pv = jnp.dot(vt_n, p_ref[slot, h], preferred_element_type=F32)
            acc_ref[h] = alpha[h:h + 1, :] * acc_ref[h] + pv

    def attn_body(j, carry):
        b0 = 2 * j
        qk_dots(clamp(b0 + 1), 1)
        pv_update(b0, 0, softmax_block(b0, 0))
        qk_dots(clamp(b0 + 2), 0)
        pv_update(clamp(b0 + 1), 1, softmax_block(b0 + 1, 1))
        return carry

    qk_dots(0, 0)
    lax.fori_loop(0, n_pairs, attn_body, 0)

    for h in range(N_HEADS):
        o_t = acc_ref[h] / l_ref[h:h + 1, :]
        o_ref[:, h * HEAD_DIM:(h + 1) * HEAD_DIM] = o_t.T


def _attn_prompt(q, qi, wit, kze, kzo, kb, vt, batch, seq, k_top):
    tq = Q_TILE
    assert seq % tq == 0 and Q_TILE % KEY_TILE == 0 and KEY_TILE == TOKEN_TILE
    nq = seq // tq
    kvw = N_KV_HEADS * HEAD_DIM
    qrow = lambda w: pl.BlockSpec((tq, w), lambda b, i: (b * nq + i, 0))
    per_batch = lambda w: pl.BlockSpec((seq, w), lambda b, i: (b, 0), pipeline_mode=pl.Buffered(1))
    return pl.pallas_call(
        functools.partial(_attn_prompt_kernel, k_top=k_top),
        grid=(batch, nq),
        in_specs=[qrow(N_HEADS * HEAD_DIM), qrow(N_IDX_HEADS * IDX_DIM),
                  pl.BlockSpec((16, tq), lambda b, i: (0, b * nq + i)),
                  per_batch(LANES), per_batch(LANES), per_batch(kvw),
                  pl.BlockSpec((seq // KEY_TILE, kvw, KEY_TILE), lambda b, i: (b, 0, 0),
                               pipeline_mode=pl.Buffered(1))],
        out_specs=qrow(N_HEADS * HEAD_DIM),
        out_shape=jax.ShapeDtypeStruct((batch * seq, N_HEADS * HEAD_DIM), F32),
        scratch_shapes=[pltpu.VMEM((seq, tq), jnp.int32),
                        pltpu.VMEM((seq, tq), jnp.int16),
                        pltpu.VMEM((seq, tq), jnp.int16),
                        pltpu.VMEM((N_HEADS, HEAD_DIM, tq), F32),
                        pltpu.VMEM((N_HEADS, tq), F32),
                        pltpu.VMEM((N_HEADS, tq), F32),
                        pltpu.VMEM((N_HEADS, tq), F32),
                        pltpu.VMEM((2, N_HEADS, KEY_TILE, tq), F32),
                        pltpu.VMEM((2, N_HEADS, KEY_TILE, tq), BF16),
                        pltpu.VMEM((2, KEY_TILE, tq), F32)],
        compiler_params=pltpu.CompilerParams(dimension_semantics=("arbitrary", "arbitrary"),
                                             vmem_limit_bytes=VMEM_LIMIT),
        name="attn_prompt",
    )(q, qi, wit, kze, kzo, kb, vt)


def _attn_sample_kernel(q_ref, qi_ref, wi_ref, ck_ref, cv_ref, cki_ref, kn_ref, vn_ref, kin_ref,
                        o_ref, *, k_top, past_len, tn):
    b = pl.program_id(0)
    group = LANES // tn
    ck = [ck_ref[0, 0, pl.ds(n, past_len, stride=N_KV_HEADS), :].astype(BF16)
          for n in range(N_KV_HEADS)]
    cv = [cv_ref[0, 0, pl.ds(n, past_len, stride=N_KV_HEADS), :].astype(BF16)
          for n in range(N_KV_HEADS)]
    cki = cki_ref[0, 0].astype(BF16)
    kn = kn_ref[...]
    vn = vn_ref[...]
    kin = kin_ref[0].astype(BF16)

    sp = jnp.zeros((tn, past_len), F32)
    sn = jnp.zeros((tn, LANES), F32)
    for h in range(N_IDX_HEADS):
        qh = qi_ref[0, h]
        w = wi_ref[:, h:h + 1]
        dp = lax.dot_general(qh, cki, NT_DIMS, preferred_element_type=F32)
        dn = lax.dot_general(qh, kin, NT_DIMS, preferred_element_type=F32)
        sp = sp + w * jnp.maximum(dp, 0.0)
        sn = sn + w * jnp.maximum(dn, 0.0)

    j = lax.broadcasted_iota(jnp.int32, (tn, LANES), 1)
    t = lax.broadcasted_iota(jnp.int32, (tn, LANES), 0)
    own = (j // tn) == (b % group)
    vis = ((past_len + j % tn) // CHUNK) <= ((past_len + t) // CHUNK)
    key_p = _order_key(sp)
    key_n = jnp.where(own & vis, _order_key(sn), INT_MIN)

    def bisect_body(_, carry):
        lo, hi = carry
        mid = _midpoint(lo, hi)
        cnt = (jnp.sum((key_p >= mid).astype(jnp.int32), axis=1, keepdims=True)
               + jnp.sum((key_n >= mid).astype(jnp.int32), axis=1, keepdims=True))
        ok = cnt >= k_top
        return jnp.where(ok, mid, lo), jnp.where(ok, hi, mid)

    thr, _ = lax.fori_loop(
        0, 32, bisect_body,
        (jnp.full((tn, 1), INT_MIN + 1, jnp.int32), jnp.full((tn, 1), INT_MAX, jnp.int32)))

    bias_p = jnp.where(key_p >= thr, 0.0, MASK_BIAS)
    bias_n = jnp.where(key_n >= thr, 0.0, MASK_BIAS)
    for h in range(N_HEADS):
        n = h // KV_GROUP
        hs = slice(n * HEAD_DIM, (n + 1) * HEAD_DIM)
        q_h = q_ref[:, h * HEAD_DIM:(h + 1) * HEAD_DIM]
        s1 = lax.dot_general(q_h, ck[n], NT_DIMS, preferred_element_type=F32) + bias_p
        s2 = lax.dot_general(q_h, kn[:, hs], NT_DIMS, preferred_element_type=F32) + bias_n
        m = jnp.maximum(jnp.max(s1, axis=1, keepdims=True), jnp.max(s2, axis=1, keepdims=True))
        p1 = jnp.exp2(s1 - m)
        p2 = jnp.exp2(s2 - m)
        l = jnp.sum(p1, axis=1, keepdims=True) + jnp.sum(p2, axis=1, keepdims=True)
        o = (jnp.dot(p1.astype(BF16), cv[n], preferred_element_type=F32)
             + jnp.dot(p2.astype(BF16), vn[:, hs], preferred_element_type=F32))
        o_ref[:, h * HEAD_DIM:(h + 1) * HEAD_DIM] = o / l


def _attn_sample(q, qi4, wi, ck, cv, cki, kb, vb, kif, layer, k_top):
    nb = ck.shape[1]
    past_len = cki.shape[2]
    kvw = N_KV_HEADS * HEAD_DIM
    tn = q.shape[0] // nb
    assert LANES % tn == 0 and tn % 16 == 0
    group = LANES // tn
    row = lambda w: pl.BlockSpec((tn, w), lambda b: (b, 0))
    shared = lambda w: pl.BlockSpec((LANES, w), lambda b: (b // group, 0))
    cache = lambda rows, w: pl.BlockSpec((1, 1, rows, w), lambda b: (layer, b, 0, 0))
    return pl.pallas_call(
        functools.partial(_attn_sample_kernel, k_top=k_top, past_len=past_len, tn=tn),
        grid=(nb,),
        in_specs=[row(N_HEADS * HEAD_DIM),
                  pl.BlockSpec((1, N_IDX_HEADS, tn, IDX_DIM), lambda b: (b, 0, 0, 0)),
                  row(LANES), cache(N_KV_HEADS * past_len, HEAD_DIM),
                  cache(N_KV_HEADS * past_len, HEAD_DIM), cache(past_len, IDX_DIM),
                  shared(kvw), shared(kvw),
                  pl.BlockSpec((1, LANES, IDX_DIM), lambda b: (layer, b // group, 0))],
        out_specs=row(N_HEADS * HEAD_DIM),
        out_shape=jax.ShapeDtypeStruct((nb * tn, N_HEADS * HEAD_DIM), F32),
        compiler_params=pltpu.CompilerParams(dimension_semantics=("arbitrary",),
                                             vmem_limit_bytes=VMEM_LIMIT),
        name="attn_sample",
    )(q, qi4, wi, ck, cv, cki, kb, vb, kif)


def _outffn_kernel(x_ref, a_ref, sga_ref, sgb_ref, u_ref, vn_ref, wmix_ref, bmix_ref, wo_ref,
                   g2_ref, wg_ref, wu_ref, wd_ref, gf_ref, xo_ref, *rest, final):
    if final:
        y_ref, z_ref = rest
    else:
        (z_ref,) = rest
    tm = x_ref.shape[0]
    group_dim = wmix_ref.shape[-1]
    for c in range(tm // SGU_CHUNK):
        rs = slice(c * SGU_CHUNK, (c + 1) * SGU_CHUNK)
        for g in range(SGU_GROUPS):
            cs = slice(g * group_dim, (g + 1) * group_dim)
            mixed = jnp.dot(wmix_ref[0, g], vn_ref[rs, cs].astype(BF16),
                            preferred_element_type=F32) + bmix_ref[0, :, cs]
            z = sga_ref[rs, cs] * a_ref[rs, cs] + sgb_ref[rs, cs] * (u_ref[rs, cs] * mixed)
            z_ref[rs, cs] = z.astype(BF16)
    x1 = x_ref[...] + jnp.dot(z_ref[...], wo_ref[0], preferred_element_type=F32)
    h2 = _rms_norm(x1, g2_ref[0]).astype(BF16)
    gate = jnp.dot(h2, wg_ref[0], preferred_element_type=F32)
    up = jnp.dot(h2, wu_ref[0], preferred_element_type=F32)
    ff = (gate * _sigmoid(gate) * up).astype(BF16)
    x2 = x1 + jnp.dot(ff, wd_ref[0], preferred_element_type=F32)
    xo_ref[...] = x2
    if final:
        y_ref[...] = _rms_norm(x2, gf_ref[...])


def _outffn(x, a, sga, sgb, u, vn, wmix, bmix, layer, wo, g2, wg, wu, wd, gf, final):
    T, D = x.shape
    tm = TOKEN_TILE
    row = pl.BlockSpec((tm, D), lambda i: (i, 0))
    n_out = 2 if final else 1
    out = pl.pallas_call(
        functools.partial(_outffn_kernel, final=final),
        grid=(T // tm,),
        in_specs=[row] * 6 + [_layer_spec(wmix.shape, layer), _layer_spec(bmix.shape, layer),
                              _layer_spec(wo.shape, layer), _layer_spec(g2.shape, layer),
                              _layer_spec(wg.shape, layer), _layer_spec(wu.shape, layer),
                              _layer_spec(wd.shape, layer), _const_spec((1, D))],
        out_specs=(row,) * n_out,
        out_shape=(jax.ShapeDtypeStruct((T, D), F32),) * n_out,
        scratch_shapes=[pltpu.VMEM((tm, D), BF16)],
        compiler_params=pltpu.CompilerParams(dimension_semantics=("arbitrary",),
                                             vmem_limit_bytes=VMEM_LIMIT),
        name="outffn",
    )(x, a, sga, sgb, u, vn, wmix, bmix, wo, g2, wg, wu, wd, gf)
    return out if final else (out[0], None)


def _rope_tables(pos, d):
    inv = ROPE_THETA ** (-jnp.arange(0, d, 2, dtype=F32) / d)
    ang = pos.astype(F32)[:, None] * inv[None, :]
    cos, sin = jnp.cos(ang), jnp.sin(ang)
    c = jnp.concatenate([cos, cos], axis=-1)
    s = jnp.concatenate([-sin, sin], axis=-1)
    reps = LANES // d
    return jnp.tile(c, (1, reps)), jnp.tile(s, (1, reps))


def _mix_weights(sgu_w, sgu_b, n, group_dim):
    depth = sgu_w.shape[0]
    p = jnp.arange(n)
    mask = (p[None, :] // CHUNK) <= (p[:, None] // CHUNK)
    w = jnp.where(mask[None, None], sgu_w[:, :, :n, :n], 0.0)
    reps = SGU_CHUNK // n
    eye = jnp.eye(reps, dtype=w.dtype)
    wbd = jnp.einsum('ab,lgij->lgaibj', eye, w).reshape(depth, SGU_GROUPS, SGU_CHUNK, SGU_CHUNK)
    b = jnp.tile(sgu_b[:, :, :n], (1, 1, reps))
    bfull = jnp.repeat(jnp.swapaxes(b, 1, 2), group_dim, axis=2)
    return wbd.astype(BF16), bfull


def _in_weights(w_in, d_model):
    kvw = N_KV_HEADS * HEAD_DIM
    w = w_in.astype(BF16)
    c_ki = d_model + 2 * kvw + N_IDX_HEADS * IDX_DIM
    c_wi = c_ki + IDX_DIM
    c_u = c_wi + N_IDX_HEADS
    wki, wwi = w[:, :, c_ki:c_wi], w[:, :, c_wi:c_u]
    wwi_pad = jnp.pad(wwi, ((0, 0), (0, 0), (0, LANES - N_IDX_HEADS)))
    wa = jnp.concatenate([w[:, :, :c_ki], wki, wki, wwi_pad, w[:, :, c_u:]], axis=2)
    wv_t = jnp.swapaxes(w[:, :, d_model + kvw:d_model + 2 * kvw], 1, 2)
    wwi_t = jnp.pad(jnp.swapaxes(wwi, 1, 2), ((0, 0), (0, 16 - N_IDX_HEADS), (0, 0)))
    return wa, jnp.concatenate([wv_t, wwi_t], axis=1)


def kernel(x_prompt, x_sample, cache_k, cache_v, cache_kidx, norm1_g, w_in, ln_v_g, ln_v_b, sgu_w, sgu_b, w_out, norm2_g, w_gate, w_up, w_down, final_norm_g):
    B, S, D = x_prompt.shape
    NB, TN, _ = x_sample.shape
    depth, _, P = cache_k.shape[:3]
    assert D == N_HEADS * HEAD_DIM and SGU_CHUNK % TN == 0 and S % SGU_CHUNK == 0
    k_top_p = min(TOPK_MAX, S // 4)
    k_top_s = min(TOPK_MAX, (P + TN) // 4)
    group_dim = D // SGU_GROUPS

    pos_p = jnp.tile(jnp.arange(S), B)
    pos_s = jnp.tile(P + jnp.arange(TN), NB)
    tab_p = _rope_tables(pos_p, HEAD_DIM) + _rope_tables(pos_p, IDX_DIM)
    tab_s = _rope_tables(pos_s, HEAD_DIM) + _rope_tables(pos_s, IDX_DIM)

    wa, wb = _in_weights(w_in, D)
    wo, wg, wu, wd = (t.astype(BF16) for t in (w_out, w_gate, w_up, w_down))
    vec = lambda v: v.reshape(depth, 1, -1)
    g1, g2, lng, lnb = vec(norm1_g), vec(norm2_g), vec(ln_v_g), vec(ln_v_b)
    gf = final_norm_g.reshape(1, -1)
    wmix_p, bmix_p = _mix_weights(sgu_w, sgu_b, SGU_CHUNK, group_dim)
    wmix_s, bmix_s = _mix_weights(sgu_w, sgu_b, TN, group_dim)
    ck = cache_k.reshape(depth, NB, P * N_KV_HEADS, HEAD_DIM)
    cv = cache_v.reshape(depth, NB, P * N_KV_HEADS, HEAD_DIM)

    xp = x_prompt.reshape(B * S, D)
    xs = x_sample.reshape(NB * TN, D)
    new_p = new_s = None
    sgu_v = []
    yp = ys = None
    for l in range(depth):
        final = l == depth - 1

        (q, kf, kb, vf, _, vt, qi, kif, kze, kzo, _, wit, u, vn, sga, sgb) = _inproj(
            xp, l, g1, wa, wb, tab_p, lng, lnb, new_p)
        new_p = (kf, vf, kif)
        a = _attn_prompt(q, qi, wit, kze, kzo, kb, vt, B, S, k_top_p)
        xp, yp = _outffn(xp, a, sga, sgb, u, vn, wmix_p, bmix_p, l, wo, g2, wg, wu, wd, gf, final)

        (q, kf, kb, vf, vb, _, qi, kif, _, _, wi, _, u, vn, sga, sgb) = _inproj(
            xs, l, g1, wa, wb, tab_s, lng, lnb, new_s)
        new_s = (kf, vf, kif)
        qi4 = qi.reshape(NB, TN, N_IDX_HEADS, IDX_DIM).transpose(0, 2, 1, 3)
        a = _attn_sample(q, qi4, wi, ck, cv, cache_kidx, kb, vb, kif, l, k_top_s)
        xs, ys = _outffn(xs, a, sga, sgb, u, vn, wmix_s, bmix_s, l, wo, g2, wg, wu, wd, gf, final)
        sgu_v.append(vn.reshape(NB, TN, D))

    kv_p = (depth, B, S, N_KV_HEADS, HEAD_DIM)
    kv_s = (depth, NB, TN, N_KV_HEADS, HEAD_DIM)
    return (yp.reshape(B, S, D), ys.reshape(NB, TN, D),
            new_p[0].reshape(kv_p), new_p[1].reshape(kv_p), new_p[2].reshape(depth, B, S, IDX_DIM),
            new_s[0].reshape(kv_s), new_s[1].reshape(kv_s), new_s[2].reshape(depth, NB, TN, IDX_DIM),
            jnp.stack(sgu_v))
```

```python
import functools
import math

import jax
import jax.numpy as jnp
from jax import lax
from jax.experimental import pallas as pl
from jax.experimental.pallas import tpu as pltpu

CHUNK = 64
N_HEADS = 8
HEAD_DIM = 128
N_KV_HEADS = 2
KV_GROUP = N_HEADS // N_KV_HEADS
N_IDX_HEADS = 8
IDX_DIM = 64
TOPK_MAX = 256
SGU_CHUNK = 128
SGU_GROUPS = 8
ROPE_THETA = 10000.0
EPS = 1e-6

LANES = 128
TOKEN_TILE = 256
Q_TILE = 256
KEY_TILE = 256
VT_ROWS = HEAD_DIM + 16
WINDOW_BITS = 9
VMEM_LIMIT = 56 * 1024 * 1024

INT_MIN = -(2 ** 31)
INT_MAX = 2 ** 31 - 1
MASK_BIAS = -1e30
LOG2E = 1.4426950408889634
Q_SCALE = (HEAD_DIM ** -0.5) * LOG2E

F32 = jnp.float32
BF16 = jnp.bfloat16
NT_DIMS = (((1,), (1,)), ((), ()))


def _const_spec(shape):
    nd = len(shape)
    return pl.BlockSpec(shape, lambda *_: (0,) * nd, pipeline_mode=pl.Buffered(1))


def _sigmoid(x):
    return 1.0 / (1.0 + jnp.exp(-x))


def _rms_norm(x, g):
    return x * lax.rsqrt(jnp.mean(x * x, axis=-1, keepdims=True) + EPS) * g


def _order_key(score):
    bits = pltpu.bitcast(score, jnp.int32)
    return bits ^ ((bits >> 31) & INT_MAX)


def _midpoint(lo, hi):
    return (lo >> 1) + (hi >> 1) + (lo & hi & 1)


_C_Q = 0
_C_K = _C_Q + N_HEADS * HEAD_DIM
_C_V = _C_K + N_KV_HEADS * HEAD_DIM
_C_QI = _C_V + N_KV_HEADS * HEAD_DIM
_C_KI = _C_QI + N_IDX_HEADS * IDX_DIM
_C_WI = _C_KI + LANES
_C_U = _C_WI + LANES


def _inproj_kernel(*refs, d_model, n_alias):
    (x_ref, g_ref, wa_ref, wb_ref, cosh_ref, sinh_ref, cosi_ref, sini_ref,
     lng_ref, lnb_ref) = refs[:10]
    (q_ref, kf_ref, kb_ref, vf_ref, vb_ref, vt_ref, qi_ref, kif_ref, kze_ref, kzo_ref,
     wi_ref, wit_ref, u_ref, vn_ref, sga_ref, sgb_ref) = refs[10 + n_alias:]
    tm = x_ref.shape[0]
    hb = _rms_norm(x_ref[...], g_ref[0]).astype(BF16)

    def proj(c0, width):
        return jnp.dot(hb, wa_ref[0, :, c0:c0 + width], preferred_element_type=F32)

    def head_rows(n):
        return pl.ds(n, tm, stride=N_KV_HEADS)

    cosh, sinh = cosh_ref[...], sinh_ref[...]
    cosi, sini = cosi_ref[...], sini_ref[...]
    lane = lax.broadcasted_iota(jnp.int32, cosi.shape, 1)
    first_half = (lane % IDX_DIM) < (IDX_DIM // 2)

    def rope_head(x):
        return x * cosh + pltpu.roll(x, HEAD_DIM // 2, 1) * sinh

    def rope_idx(x):
        partner = jnp.where(first_half, pltpu.roll(x, LANES - IDX_DIM // 2, 1),
                            pltpu.roll(x, IDX_DIM // 2, 1))
        return x * cosi + partner * sini

    xq = proj(_C_Q, N_HEADS * HEAD_DIM)
    for h in range(N_HEADS):
        sl = slice(h * HEAD_DIM, (h + 1) * HEAD_DIM)
        q_ref[:, sl] = (rope_head(xq[:, sl]) * Q_SCALE).astype(BF16)

    xk = proj(_C_K, N_KV_HEADS * HEAD_DIM)
    for h in range(N_KV_HEADS):
        sl = slice(h * HEAD_DIM, (h + 1) * HEAD_DIM)
        kr = rope_head(xk[:, sl])
        kf_ref[0, head_rows(h), :] = kr
        kb_ref[:, sl] = kr.astype(BF16)

    xv = proj(_C_V, N_KV_HEADS * HEAD_DIM)
    vb_ref[...] = xv.astype(BF16)
    for h in range(N_KV_HEADS):
        vf_ref[0, head_rows(h), :] = xv[:, h * HEAD_DIM:(h + 1) * HEAD_DIM]

    xqi = proj(_C_QI, N_IDX_HEADS * IDX_DIM)
    for p in range(N_IDX_HEADS * IDX_DIM // LANES):
        sl = slice(p * LANES, (p + 1) * LANES)
        qi_ref[:, sl] = (rope_idx(xqi[:, sl]) * (IDX_DIM ** -0.5)).astype(BF16)

    kk = rope_idx(proj(_C_KI, LANES))
    kif_ref[0] = kk[:, :IDX_DIM]
    low = lane < IDX_DIM
    kze_ref[...] = jnp.where(low, kk, 0.0).astype(BF16)
    kzo_ref[...] = jnp.where(low, 0.0, kk).astype(BF16)

    wi_ref[...] = proj(_C_WI, LANES) * (N_IDX_HEADS ** -0.5)

    tb = lax.dot_general(wb_ref[0], hb, NT_DIMS, preferred_element_type=F32)
    kvw = N_KV_HEADS * HEAD_DIM
    ones_rows = (lax.broadcasted_iota(jnp.int32, (VT_ROWS - HEAD_DIM, tm), 0) == 0).astype(BF16)
    for n in range(N_KV_HEADS):
        vt_ref[0, n * VT_ROWS:n * VT_ROWS + HEAD_DIM, :] = (
            tb[n * HEAD_DIM:(n + 1) * HEAD_DIM].astype(BF16))
        vt_ref[0, n * VT_ROWS + HEAD_DIM:(n + 1) * VT_ROWS, :] = ones_rows
    wit_ref[...] = tb[kvw:] * (N_IDX_HEADS ** -0.5)

    u_ref[...] = jax.nn.gelu(proj(_C_U, d_model), approximate=True)
    gv = jax.nn.gelu(proj(_C_U + d_model, d_model), approximate=True)
    mu = jnp.mean(gv, axis=-1, keepdims=True)
    dv = gv - mu
    var = jnp.mean(dv * dv, axis=-1, keepdims=True)
    vn_ref[...] = dv * lax.rsqrt(var + EPS) * lng_ref[0] + lnb_ref[0]
    sga_ref[...] = _sigmoid(proj(_C_U + 2 * d_model, d_model))
    sgb_ref[...] = _sigmoid(proj(_C_U + 3 * d_model, d_model))


def _layer_spec(shape, layer):
    nd = len(shape)
    return pl.BlockSpec((1,) + tuple(shape[1:]), lambda *_: (layer,) + (0,) * (nd - 1),
                        pipeline_mode=pl.Buffered(1))


def _inproj(x, layer, g, wa, wb, tables, lng, lnb, stacked):
    T, D = x.shape
    depth = wa.shape[0]
    tm = TOKEN_TILE
    assert T % tm == 0
    kvw = N_KV_HEADS * HEAD_DIM
    qiw = N_IDX_HEADS * IDX_DIM
    row = lambda w: pl.BlockSpec((tm, w), lambda i: (i, 0))
    kv_rows = N_KV_HEADS * T
    out_shape = (
        jax.ShapeDtypeStruct((T, N_HEADS * HEAD_DIM), BF16),
        jax.ShapeDtypeStruct((depth, kv_rows, HEAD_DIM), F32),
        jax.ShapeDtypeStruct((T, kvw), BF16),
        jax.ShapeDtypeStruct((depth, kv_rows, HEAD_DIM), F32),
        jax.ShapeDtypeStruct((T, kvw), BF16),
        jax.ShapeDtypeStruct((T // tm, N_KV_HEADS * VT_ROWS, tm), BF16),
        jax.ShapeDtypeStruct((T, qiw), BF16),
        jax.ShapeDtypeStruct((depth, T, IDX_DIM), F32),
        jax.ShapeDtypeStruct((T, LANES), BF16),
        jax.ShapeDtypeStruct((T, LANES), BF16),
        jax.ShapeDtypeStruct((T, LANES), F32),
        jax.ShapeDtypeStruct((16, T), F32),
        jax.ShapeDtypeStruct((T, D), F32),
        jax.ShapeDtypeStruct((T, D), F32),
        jax.ShapeDtypeStruct((T, D), F32),
        jax.ShapeDtypeStruct((T, D), F32),
    )
    kv_spec = pl.BlockSpec((1, N_KV_HEADS * tm, HEAD_DIM), lambda i: (layer, i, 0))
    out_specs = (
        row(N_HEADS * HEAD_DIM), kv_spec, row(kvw), kv_spec, row(kvw),
        pl.BlockSpec((1, N_KV_HEADS * VT_ROWS, tm), lambda i: (i, 0, 0)),
        row(qiw), pl.BlockSpec((1, tm, IDX_DIM), lambda i: (layer, i, 0)),
        row(LANES), row(LANES), row(LANES),
        pl.BlockSpec((16, tm), lambda i: (0, i)),
        row(D), row(D), row(D), row(D),
    )
    in_specs = [row(D), _layer_spec(g.shape, layer), _layer_spec(wa.shape, layer),
                _layer_spec(wb.shape, layer),
                row(LANES), row(LANES), row(LANES), row(LANES),
                _layer_spec(lng.shape, layer), _layer_spec(lnb.shape, layer)]
    args = [x, g, wa, wb, *tables, lng, lnb]
    aliases = {}
    if stacked is not None:
        for j, out_idx in enumerate((1, 3, 7)):
            aliases[len(args)] = out_idx
            in_specs.append(pl.BlockSpec(memory_space=pl.ANY))
            args.append(stacked[j])
    return pl.pallas_call(
        functools.partial(_inproj_kernel, d_model=D, n_alias=len(aliases)),
        grid=(T // tm,),
        in_specs=in_specs, out_specs=out_specs, out_shape=out_shape,
        input_output_aliases=aliases,
        compiler_params=pltpu.CompilerParams(dimension_semantics=("arbitrary",),
                                             vmem_limit_bytes=VMEM_LIMIT),
        name="inproj",
    )(*args)


def _attn_prompt_kernel(q_ref, qi_ref, wit_ref, kze_ref, kzo_ref, k_ref, vt_ref, o_ref,
                        keys_ref, hi_ref, lo_ref, kmax_ref, acc_ref, m_ref, mx_ref, s_ref, p_ref, bias_ref,
                        *, k_top):
    tq, kb_sz = Q_TILE, KEY_TILE
    qb = pl.program_id(1)
    nkb = (qb + 1) * (tq // kb_sz)
    q_chunk = (qb * tq + lax.broadcasted_iota(jnp.int32, (1, tq), 1)) // CHUNK
    row_iota = lax.broadcasted_iota(jnp.int32, (kb_sz, 1), 0)

    def key_slice(kb):
        return pl.ds(pl.multiple_of(kb * kb_sz, kb_sz), kb_sz)

    n_blocks = keys_ref.shape[0] // kb_sz
    n_pairs = (nkb + 1) // 2

    def clamp(blk):
        return jnp.minimum(blk, n_blocks - 1)

    def idx_dots(blk, slot):
        ks = key_slice(blk)
        kze, kzo = kze_ref[ks, :], kzo_ref[ks, :]
        for p in range(N_IDX_HEADS // 2):
            qp = qi_ref[:, p * LANES:(p + 1) * LANES]
            s_ref[slot, 2 * p] = lax.dot_general(kze, qp, NT_DIMS, preferred_element_type=F32)
            s_ref[slot, 2 * p + 1] = lax.dot_general(kzo, qp, NT_DIMS, preferred_element_type=F32)

    def idx_keys(blk, slot):
        base = pl.multiple_of(blk * kb_sz, kb_sz)
        kmax = kmax_ref[...]
        for c in range(kb_sz // CHUNK):
            rows = slice(c * CHUNK, (c + 1) * CHUNK)
            score = jnp.zeros((CHUNK, tq), F32)
            for j in range(N_IDX_HEADS):
                score = score + wit_ref[j:j + 1, :] * jnp.maximum(s_ref[slot, j, rows, :], 0.0)
            visible = (blk * (kb_sz // CHUNK) + c) <= q_chunk
            key = jnp.where(visible, _order_key(score), INT_MIN)
            ks = pl.ds(base + c * CHUNK, CHUNK)
            keys_ref[ks, :] = key
            hi_ref[ks, :] = (key >> 16).astype(jnp.int16)
            lo_ref[ks, :] = ((key & 0xFFFF) - 2 ** 15).astype(jnp.int16)
            kmax = jnp.maximum(kmax, jnp.max(key.reshape(CHUNK // 8, 8, tq), axis=0))
        kmax_ref[...] = kmax

    def score_body(j, carry):
        b0 = 2 * j
        idx_dots(clamp(b0 + 1), 1)
        idx_keys(b0, 0)
        idx_dots(clamp(b0 + 2), 0)
        idx_keys(clamp(b0 + 1), 1)
        return carry

    kmax_ref[...] = jnp.full(kmax_ref.shape, INT_MIN, jnp.int32)
    idx_dots(0, 0)
    lax.fori_loop(0, n_pairs, score_body, 0)

    n_acc = 4
    rows16 = 16

    def fold_rows(x, accs):
        accs = list(accs)
        for r in range(kb_sz // rows16):
            accs[r % n_acc] = accs[r % n_acc] + x[r * rows16:(r + 1) * rows16, :]
        return tuple(accs)

    def zero_accs():
        return tuple(jnp.zeros((rows16, tq), jnp.int16) for _ in range(n_acc))

    def total(accs):
        tot = accs[0].astype(jnp.int32)
        for a in accs[1:]:
            tot = tot + a.astype(jnp.int32)
        return jnp.sum(tot, axis=0, keepdims=True)

    def count_ge(plane_ref, t):
        t16 = jnp.broadcast_to(t, (kb_sz, tq)).astype(jnp.int16)

        def body(kb, accs):
            ge = (plane_ref[key_slice(kb), :] >= t16).astype(jnp.int16)
            return fold_rows(ge, accs)

        return total(lax.fori_loop(0, nkb, body, zero_accs()))

    def bisect16(plane_ref, need, lo, hi, steps):
        def body(_, carry):
            lo, hi = carry
            mid = (lo + hi) >> 1
            ok = count_ge(plane_ref, mid) >= need
            return jnp.where(ok, mid, lo), jnp.where(ok, hi, mid)

        return lax.fori_loop(0, steps, body, (lo, hi))[0]

    floor16 = jnp.full((1, tq), -(2 ** 15), jnp.int32)
    top16 = (jnp.max(kmax_ref[...], axis=0, keepdims=True) >> 16) + 1
    lo_try = jnp.maximum(top16 - 2 ** WINDOW_BITS, -(2 ** 15) + 1)
    in_window = count_ge(hi_ref, lo_try) >= k_top
    steps = jnp.where(jnp.min(in_window.astype(jnp.int32)) > 0, WINDOW_BITS, 16)
    t_hi = bisect16(hi_ref, k_top, jnp.where(in_window, lo_try, floor16), top16, steps)

    t_hi16 = jnp.broadcast_to(t_hi, (kb_sz, tq)).astype(jnp.int16)

    def tie_body(kb, accs):
        ks = key_slice(kb)
        hi_part = hi_ref[ks, :]
        lo_ref[ks, :] = jnp.where(hi_part == t_hi16, lo_ref[ks, :], jnp.int16(-(2 ** 15)))
        return fold_rows((hi_part > t_hi16).astype(jnp.int16), accs)

    n_above = total(lax.fori_loop(0, nkb, tie_body, zero_accs()))
    t_lo = bisect16(lo_ref, k_top - n_above, floor16, jnp.full((1, tq), 2 ** 15, jnp.int32), 16)
    thr = jnp.maximum((t_hi << 16) + (t_lo + 2 ** 15), INT_MIN + 1)

    m_ref[...] = jnp.full(m_ref.shape, MASK_BIAS, F32)
    acc_ref[...] = jnp.zeros(acc_ref.shape, F32)

    rc = 32
    n_rc = kb_sz // rc

    def fold8(x):
        return x.reshape(rc // 8, 8, tq)

    def qk_dots(blk, slot):
        ks = key_slice(blk)
        for h in range(N_HEADS):
            n = h // KV_GROUP
            k_n = k_ref[ks, n * HEAD_DIM:(n + 1) * HEAD_DIM]
            q_h = q_ref[:, h * HEAD_DIM:(h + 1) * HEAD_DIM]
            s_ref[slot, h] = lax.dot_general(k_n, q_h, NT_DIMS, preferred_element_type=F32)

    def softmax_block(blk, slot):
        thr_b = jnp.where(blk < nkb, thr, INT_MAX)
        bias_ref[slot] = jnp.where(keys_ref[key_slice(clamp(blk)), :] >= thr_b, 0.0, MASK_BIAS)
        for h in range(N_HEADS):
            mx = jnp.full((8, tq), MASK_BIAS, F32)
            for r in range(n_rc):
                rows = slice(r * rc, (r + 1) * rc)
                sc = s_ref[slot, h, rows, :] + bias_ref[slot, rows, :]
                s_ref[slot, h, rows, :] = sc
                mx = jnp.maximum(mx, jnp.max(fold8(sc), axis=0))
            mx_ref[h:h + 1, :] = jnp.max(mx, axis=0, keepdims=True)
        m_old = m_ref[...]
        m_new = jnp.maximum(m_old, mx_ref[...])
        alpha = jnp.exp2(m_old - m_new)
        m_ref[...] = m_new
        for h in range(N_HEADS):
            m_h = jnp.broadcast_to(m_new[h:h + 1, :], (rc, tq))
            for r in range(n_rc):
                rows = slice(r * rc, (r + 1) * rc)
                p_ref[slot, h, rows, :] = jnp.exp2(s_ref[slot, h, rows, :] - m_h).astype(BF16)
        return alpha

    def pv_update(blk, slot, alpha):
        for h in range(N_HEADS):
            n = h // KV_GROUP
            vt_n = vt_ref[blk, n * VT_ROWS:(n + 1) * VT_ROWS, :]
            pv = jnp.dot(vt_n, p_ref[slot, h], preferred_element_type=F32)
            acc_ref[h] = alpha[h:h + 1, :] * acc_ref[h] + pv

    def attn_body(j, carry):
        b0 = 2 * j
        qk_dots(clamp(b0 + 1), 1)
        pv_update(b0, 0, softmax_block(b0, 0))
        qk_dots(clamp(b0 + 2), 0)
        pv_update(clamp(b0 + 1), 1, softmax_block(b0 + 1, 1))
        return carry

    qk_dots(0, 0)
    lax.fori_loop(0, n_pairs, attn_body, 0)

    for h in range(N_HEADS):
        o_t = acc_ref[h, :HEAD_DIM, :] / acc_ref[h, HEAD_DIM:HEAD_DIM + 1, :]
        o_ref[:, h * HEAD_DIM:(h + 1) * HEAD_DIM] = o_t.T


def _attn_prompt(q, qi, wit, kze, kzo, kb, vt, batch, seq, k_top):
    tq = Q_TILE
    assert seq % tq == 0 and Q_TILE % KEY_TILE == 0 and KEY_TILE == TOKEN_TILE
    nq = seq // tq
    kvw = N_KV_HEADS * HEAD_DIM
    qrow = lambda w: pl.BlockSpec((tq, w), lambda b, i: (b * nq + i, 0))
    per_batch = lambda w: pl.BlockSpec((seq, w), lambda b, i: (b, 0), pipeline_mode=pl.Buffered(1))
    return pl.pallas_call(
        functools.partial(_attn_prompt_kernel, k_top=k_top),
        grid=(batch, nq),
        in_specs=[qrow(N_HEADS * HEAD_DIM), qrow(N_IDX_HEADS * IDX_DIM),
                  pl.BlockSpec((16, tq), lambda b, i: (0, b * nq + i)),
                  per_batch(LANES), per_batch(LANES), per_batch(kvw),
                  pl.BlockSpec((seq // KEY_TILE, N_KV_HEADS * VT_ROWS, KEY_TILE), lambda b, i: (b, 0, 0),
                               pipeline_mode=pl.Buffered(1))],
        out_specs=qrow(N_HEADS * HEAD_DIM),
        out_shape=jax.ShapeDtypeStruct((batch * seq, N_HEADS * HEAD_DIM), F32),
        scratch_shapes=[pltpu.VMEM((seq, tq), jnp.int32),
                        pltpu.VMEM((seq, tq), jnp.int16),
                        pltpu.VMEM((seq, tq), jnp.int16),
                        pltpu.VMEM((8, tq), jnp.int32),
                        pltpu.VMEM((N_HEADS, VT_ROWS, tq), F32),
                        pltpu.VMEM((N_HEADS, tq), F32),
                        pltpu.VMEM((N_HEADS, tq), F32),
                        pltpu.VMEM((2, N_HEADS, KEY_TILE, tq), F32),
                        pltpu.VMEM((2, N_HEADS, KEY_TILE, tq), BF16),
                        pltpu.VMEM((2, KEY_TILE, tq), F32)],
        compiler_params=pltpu.CompilerParams(dimension_semantics=("arbitrary", "arbitrary"),
                                             vmem_limit_bytes=VMEM_LIMIT),
        name="attn_prompt",
    )(q, qi, wit, kze, kzo, kb, vt)


def _attn_sample_kernel(q_ref, qi_ref, wi_ref, ck_ref, cv_ref, cki_ref, kn_ref, vn_ref, kin_ref,
                        o_ref, *, k_top, past_len, tn):
    b = pl.program_id(0)
    group = LANES // tn
    ck = [ck_ref[0, 0, pl.ds(n, past_len, stride=N_KV_HEADS), :].astype(BF16)
          for n in range(N_KV_HEADS)]
    cv = [cv_ref[0, 0, pl.ds(n, past_len, stride=N_KV_HEADS), :].astype(BF16)
          for n in range(N_KV_HEADS)]
    cki = cki_ref[0, 0].astype(BF16)
    kn = kn_ref[...]
    vn = vn_ref[...]
    kin = kin_ref[0].astype(BF16)

    sp = jnp.zeros((tn, past_len), F32)
    sn = jnp.zeros((tn, LANES), F32)
    for h in range(N_IDX_HEADS):
        qh = qi_ref[0, h]
        w = wi_ref[:, h:h + 1]
        dp = lax.dot_general(qh, cki, NT_DIMS, preferred_element_type=F32)
        dn = lax.dot_general(qh, kin, NT_DIMS, preferred_element_type=F32)
        sp = sp + w * jnp.maximum(dp, 0.0)
        sn = sn + w * jnp.maximum(dn, 0.0)

    j = lax.broadcasted_iota(jnp.int32, (tn, LANES), 1)
    t = lax.broadcasted_iota(jnp.int32, (tn, LANES), 0)
    own = (j // tn) == (b % group)
    vis = ((past_len + j % tn) // CHUNK) <= ((past_len + t) // CHUNK)
    key_p = _order_key(sp)
    key_n = jnp.where(own & vis, _order_key(sn), INT_MIN)

    def bisect_body(_, carry):
        lo, hi = carry
        mid = _midpoint(lo, hi)
        cnt = (jnp.sum((key_p >= mid).astype(jnp.int32), axis=1, keepdims=True)
               + jnp.sum((key_n >= mid).astype(jnp.int32), axis=1, keepdims=True))
        ok = cnt >= k_top
        return jnp.where(ok, mid, lo), jnp.where(ok, hi, mid)

    thr, _ = lax.fori_loop(
        0, 32, bisect_body,
        (jnp.full((tn, 1), INT_MIN + 1, jnp.int32), jnp.full((tn, 1), INT_MAX, jnp.int32)))

    bias_p = jnp.where(key_p >= thr, 0.0, MASK_BIAS)
    bias_n = jnp.where(key_n >= thr, 0.0, MASK_BIAS)
    for h in range(N_HEADS):
        n = h // KV_GROUP
        hs = slice(n * HEAD_DIM, (n + 1) * HEAD_DIM)
        q_h = q_ref[:, h * HEAD_DIM:(h + 1) * HEAD_DIM]
        s1 = lax.dot_general(q_h, ck[n], NT_DIMS, preferred_element_type=F32) + bias_p
        s2 = lax.dot_general(q_h, kn[:, hs], NT_DIMS, preferred_element_type=F32) + bias_n
        m = jnp.maximum(jnp.max(s1, axis=1, keepdims=True), jnp.max(s2, axis=1, keepdims=True))
        p1 = jnp.exp2(s1 - m)
        p2 = jnp.exp2(s2 - m)
        l = jnp.sum(p1, axis=1, keepdims=True) + jnp.sum(p2, axis=1, keepdims=True)
        o = (jnp.dot(p1.astype(BF16), cv[n], preferred_element_type=F32)
             + jnp.dot(p2.astype(BF16), vn[:, hs], preferred_element_type=F32))
        o_ref[:, h * HEAD_DIM:(h + 1) * HEAD_DIM] = o / l


def _attn_sample(q, qi4, wi, ck, cv, cki, kb, vb, kif, layer, k_top):
    nb = ck.shape[1]
    past_len = cki.shape[2]
    kvw = N_KV_HEADS * HEAD_DIM
    tn = q.shape[0] // nb
    assert LANES % tn == 0 and tn % 16 == 0
    group = LANES // tn
    row = lambda w: pl.BlockSpec((tn, w), lambda b: (b, 0))
    shared = lambda w: pl.BlockSpec((LANES, w), lambda b: (b // group, 0))
    cache = lambda rows, w: pl.BlockSpec((1, 1, rows, w), lambda b: (layer, b, 0, 0))
    return pl.pallas_call(
        functools.partial(_attn_sample_kernel, k_top=k_top, past_len=past_len, tn=tn),
        grid=(nb,),
        in_specs=[row(N_HEADS * HEAD_DIM),
                  pl.BlockSpec((1, N_IDX_HEADS, tn, IDX_DIM), lambda b: (b, 0, 0, 0)),
                  row(LANES), cache(N_KV_HEADS * past_len, HEAD_DIM),
                  cache(N_KV_HEADS * past_len, HEAD_DIM), cache(past_len, IDX_DIM),
                  shared(kvw), shared(kvw),
                  pl.BlockSpec((1, LANES, IDX_DIM), lambda b: (layer, b // group, 0))],
        out_specs=row(N_HEADS * HEAD_DIM),
        out_shape=jax.ShapeDtypeStruct((nb * tn, N_HEADS * HEAD_DIM), F32),
        compiler_params=pltpu.CompilerParams(dimension_semantics=("arbitrary",),
                                             vmem_limit_bytes=VMEM_LIMIT),
        name="attn_sample",
    )(q, qi4, wi, ck, cv, cki, kb, vb, kif)


def _outffn_kernel(x_ref, a_ref, sga_ref, sgb_ref, u_ref, vn_ref, wmix_ref, bmix_ref, wo_ref,
                   g2_ref, wg_ref, wu_ref, wd_ref, gf_ref, xo_ref, *rest, final):
    if final:
        y_ref, z_ref = rest
    else:
        (z_ref,) = rest
    tm = x_ref.shape[0]
    group_dim = wmix_ref.shape[-1]
    for c in range(tm // SGU_CHUNK):
        rs = slice(c * SGU_CHUNK, (c + 1) * SGU_CHUNK)
        for g in range(SGU_GROUPS):
            cs = slice(g * group_dim, (g + 1) * group_dim)
            mixed = jnp.dot(wmix_ref[0, g], vn_ref[rs, cs].astype(BF16),
                            preferred_element_type=F32) + bmix_ref[0, :, cs]
            z = sga_ref[rs, cs] * a_ref[rs, cs] + sgb_ref[rs, cs] * (u_ref[rs, cs] * mixed)
            z_ref[rs, cs] = z.astype(BF16)
    x1 = x_ref[...] + jnp.dot(z_ref[...], wo_ref[0], preferred_element_type=F32)
    h2 = _rms_norm(x1, g2_ref[0]).astype(BF16)
    gate = jnp.dot(h2, wg_ref[0], preferred_element_type=F32)
    up = jnp.dot(h2, wu_ref[0], preferred_element_type=F32)
    ff = (gate * _sigmoid(gate) * up).astype(BF16)
    x2 = x1 + jnp.dot(ff, wd_ref[0], preferred_element_type=F32)
    xo_ref[...] = x2
    if final:
        y_ref[...] = _rms_norm(x2, gf_ref[...])


def _outffn(x, a, sga, sgb, u, vn, wmix, bmix, layer, wo, g2, wg, wu, wd, gf, final):
    T, D = x.shape
    tm = TOKEN_TILE
    row = pl.BlockSpec((tm, D), lambda i: (i, 0))
    n_out = 2 if final else 1
    out = pl.pallas_call(
        functools.partial(_outffn_kernel, final=final),
        grid=(T // tm,),
        in_specs=[row] * 6 + [_layer_spec(wmix.shape, layer), _layer_spec(bmix.shape, layer),
                              _layer_spec(wo.shape, layer), _layer_spec(g2.shape, layer),
                              _layer_spec(wg.shape, layer), _layer_spec(wu.shape, layer),
                              _layer_spec(wd.shape, layer), _const_spec((1, D))],
        out_specs=(row,) * n_out,
        out_shape=(jax.ShapeDtypeStruct((T, D), F32),) * n_out,
        scratch_shapes=[pltpu.VMEM((tm, D), BF16)],
        compiler_params=pltpu.CompilerParams(dimension_semantics=("arbitrary",),
                                             vmem_limit_bytes=VMEM_LIMIT),
        name="outffn",
    )(x, a, sga, sgb, u, vn, wmix, bmix, wo, g2, wg, wu, wd, gf)
    return out if final else (out[0], None)


def _rope_tables(pos, d):
    inv = ROPE_THETA ** (-jnp.arange(0, d, 2, dtype=F32) / d)
    ang = pos.astype(F32)[:, None] * inv[None, :]
    cos, sin = jnp.cos(ang), jnp.sin(ang)
    c = jnp.concatenate([cos, cos], axis=-1)
    s = jnp.concatenate([-sin, sin], axis=-1)
    reps = LANES // d
    return jnp.tile(c, (1, reps)), jnp.tile(s, (1, reps))


def _mix_weights(sgu_w, sgu_b, n, group_dim):
    depth = sgu_w.shape[0]
    p = jnp.arange(n)
    mask = (p[None, :] // CHUNK) <= (p[:, None] // CHUNK)
    w = jnp.where(mask[None, None], sgu_w[:, :, :n, :n], 0.0)
    reps = SGU_CHUNK // n
    eye = jnp.eye(reps, dtype=w.dtype)
    wbd = jnp.einsum('ab,lgij->lgaibj', eye, w).reshape(depth, SGU_GROUPS, SGU_CHUNK, SGU_CHUNK)
    b = jnp.tile(sgu_b[:, :, :n], (1, 1, reps))
    bfull = jnp.repeat(jnp.swapaxes(b, 1, 2), group_dim, axis=2)
    return wbd.astype(BF16), bfull


def _in_weights(w_in, d_model):
    kvw = N_KV_HEADS * HEAD_DIM
    w = w_in.astype(BF16)
    c_ki = d_model + 2 * kvw + N_IDX_HEADS * IDX_DIM
    c_wi = c_ki + IDX_DIM
    c_u = c_wi + N_IDX_HEADS
    wki, wwi = w[:, :, c_ki:c_wi], w[:, :, c_wi:c_u]
    wwi_pad = jnp.pad(wwi, ((0, 0), (0, 0), (0, LANES - N_IDX_HEADS)))
    wa = jnp.concatenate([w[:, :, :c_ki], wki, wki, wwi_pad, w[:, :, c_u:]], axis=2)
    wv_t = jnp.swapaxes(w[:, :, d_model + kvw:d_model + 2 * kvw], 1, 2)
    wwi_t = jnp.pad(jnp.swapaxes(wwi, 1, 2), ((0, 0), (0, 16 - N_IDX_HEADS), (0, 0)))
    return wa, jnp.concatenate([wv_t, wwi_t], axis=1)


def kernel(x_prompt, x_sample, cache_k, cache_v, cache_kidx, norm1_g, w_in, ln_v_g, ln_v_b, sgu_w, sgu_b, w_out, norm2_g, w_gate, w_up, w_down, final_norm_g):
    B, S, D = x_prompt.shape
    NB, TN, _ = x_sample.shape
    depth, _, P = cache_k.shape[:3]
    assert D == N_HEADS * HEAD_DIM and SGU_CHUNK % TN == 0 and S % SGU_CHUNK == 0
    k_top_p = min(TOPK_MAX, S // 4)
    k_top_s = min(TOPK_MAX, (P + TN) // 4)
    group_dim = D // SGU_GROUPS

    pos_p = jnp.tile(jnp.arange(S), B)
    pos_s = jnp.tile(P + jnp.arange(TN), NB)
    tab_p = _rope_tables(pos_p, HEAD_DIM) + _rope_tables(pos_p, IDX_DIM)
    tab_s = _rope_tables(pos_s, HEAD_DIM) + _rope_tables(pos_s, IDX_DIM)

    wa, wb = _in_weights(w_in, D)
    wo, wg, wu, wd = (t.astype(BF16) for t in (w_out, w_gate, w_up, w_down))
    vec = lambda v: v.reshape(depth, 1, -1)
    g1, g2, lng, lnb = vec(norm1_g), vec(norm2_g), vec(ln_v_g), vec(ln_v_b)
    gf = final_norm_g.reshape(1, -1)
    wmix_p, bmix_p = _mix_weights(sgu_w, sgu_b, SGU_CHUNK, group_dim)
    wmix_s, bmix_s = _mix_weights(sgu_w, sgu_b, TN, group_dim)
    ck = cache_k.reshape(depth, NB, P * N_KV_HEADS, HEAD_DIM)
    cv = cache_v.reshape(depth, NB, P * N_KV_HEADS, HEAD_DIM)

    xp = x_prompt.reshape(B * S, D)
    xs = x_sample.reshape(NB * TN, D)
    new_p = new_s = None
    sgu_v = []
    yp = ys = None
    for l in range(depth):
        final = l == depth - 1

        (q, kf, kb, vf, _, vt, qi, kif, kze, kzo, _, wit, u, vn, sga, sgb) = _inproj(
            xp, l, g1, wa, wb, tab_p, lng, lnb, new_p)
        new_p = (kf, vf, kif)
        a = _attn_prompt(q, qi, wit, kze, kzo, kb, vt, B, S, k_top_p)
        xp, yp = _outffn(xp, a, sga, sgb, u, vn, wmix_p, bmix_p, l, wo, g2, wg, wu, wd, gf, final)

        (q, kf, kb, vf, vb, _, qi, kif, _, _, wi, _, u, vn, sga, sgb) = _inproj(
            xs, l, g1, wa, wb, tab_s, lng, lnb, new_s)
        new_s = (kf, vf, kif)
        qi4 = qi.reshape(NB, TN, N_IDX_HEADS, IDX_DIM).transpose(0, 2, 1, 3)
        a = _attn_sample(q, qi4, wi, ck, cv, cache_kidx, kb, vb, kif, l, k_top_s)
        xs, ys = _outffn(xs, a, sga, sgb, u, vn, wmix_s, bmix_s, l, wo, g2, wg, wu, wd, gf, final)
        sgu_v.append(vn.reshape(NB, TN, D))

    kv_p = (depth, B, S, N_KV_HEADS, HEAD_DIM)
    kv_s = (depth, NB, TN, N_KV_HEADS, HEAD_DIM)
    return (yp.reshape(B, S, D), ys.reshape(NB, TN, D),
            new_p[0].reshape(kv_p), new_p[1].reshape(kv_p), new_p[2].reshape(depth, B, S, IDX_DIM),
            new_s[0].reshape(kv_s), new_s[1].reshape(kv_s), new_s[2].reshape(depth, NB, TN, IDX_DIM),
            jnp.stack(sgu_v))
```

```python
import functools
import math

import jax
import jax.numpy as jnp
from jax import lax
from jax.experimental import pallas as pl
from jax.experimental.pallas import tpu as pltpu

CHUNK = 64
N_HEADS = 8
HEAD_DIM = 128
N_KV_HEADS = 2
KV_GROUP = N_HEADS // N_KV_HEADS
N_IDX_HEADS = 8
IDX_DIM = 64
TOPK_MAX = 256
SGU_CHUNK = 128
SGU_GROUPS = 8
ROPE_THETA = 10000.0
EPS = 1e-6

LANES = 128
TOKEN_TILE = 256
Q_TILE = 256
KEY_TILE = 256
VT_ROWS = HEAD_DIM + 16
WINDOW_BITS = 9
VMEM_LIMIT = 56 * 1024 * 1024

INT_MIN = -(2 ** 31)
INT_MAX = 2 ** 31 - 1
MASK_BIAS = -1e30
LOG2E = 1.4426950408889634
Q_SCALE = (HEAD_DIM ** -0.5) * LOG2E

F32 = jnp.float32
BF16 = jnp.bfloat16
NT_DIMS = (((1,), (1,)), ((), ()))


def _const_spec(shape):
    nd = len(shape)
    return pl.BlockSpec(shape, lambda *_: (0,) * nd, pipeline_mode=pl.Buffered(1))


def _sigmoid(x):
    return 1.0 / (1.0 + jnp.exp(-x))


def _rms_norm(x, g):
    return x * lax.rsqrt(jnp.mean(x * x, axis=-1, keepdims=True) + EPS) * g


def _order_key(score):
    bits = pltpu.bitcast(score, jnp.int32)
    return bits ^ ((bits >> 31) & INT_MAX)


def _midpoint(lo, hi):
    return (lo >> 1) + (hi >> 1) + (lo & hi & 1)


_C_Q = 0
_C_K = _C_Q + N_HEADS * HEAD_DIM
_C_V = _C_K + N_KV_HEADS * HEAD_DIM
_C_QI = _C_V + N_KV_HEADS * HEAD_DIM
_C_KI = _C_QI + N_IDX_HEADS * IDX_DIM
_C_WI = _C_KI + LANES
_C_U = _C_WI + LANES


def _inproj_kernel(*refs, d_model, n_alias):
    (x_ref, g_ref, wa_ref, wb_ref, cosh_ref, sinh_ref, cosi_ref, sini_ref,
     lng_ref, lnb_ref) = refs[:10]
    (q_ref, kf_ref, kb_ref, vf_ref, vb_ref, vt_ref, qi_ref, kif_ref, kze_ref, kzo_ref,
     wi_ref, wit_ref, u_ref, vn_ref, sga_ref, sgb_ref) = refs[10 + n_alias:]
    tm = x_ref.shape[0]
    hb = _rms_norm(x_ref[...], g_ref[0]).astype(BF16)

    def proj(c0, width):
        return jnp.dot(hb, wa_ref[0, :, c0:c0 + width], preferred_element_type=F32)

    def head_rows(n):
        return pl.ds(n, tm, stride=N_KV_HEADS)

    cosh, sinh = cosh_ref[...], sinh_ref[...]
    cosi, sini = cosi_ref[...], sini_ref[...]
    lane = lax.broadcasted_iota(jnp.int32, cosi.shape, 1)
    first_half = (lane % IDX_DIM) < (IDX_DIM // 2)

    def rope_head(x):
        return x * cosh + pltpu.roll(x, HEAD_DIM // 2, 1) * sinh

    def rope_idx(x):
        partner = jnp.where(first_half, pltpu.roll(x, LANES - IDX_DIM // 2, 1),
                            pltpu.roll(x, IDX_DIM // 2, 1))
        return x * cosi + partner * sini

    xq = proj(_C_Q, N_HEADS * HEAD_DIM)
    for h in range(N_HEADS):
        sl = slice(h * HEAD_DIM, (h + 1) * HEAD_DIM)
        q_ref[:, sl] = (rope_head(xq[:, sl]) * Q_SCALE).astype(BF16)

    xk = proj(_C_K, N_KV_HEADS * HEAD_DIM)
    for h in range(N_KV_HEADS):
        sl = slice(h * HEAD_DIM, (h + 1) * HEAD_DIM)
        kr = rope_head(xk[:, sl])
        kf_ref[0, head_rows(h), :] = kr
        kb_ref[:, sl] = kr.astype(BF16)

    xv = proj(_C_V, N_KV_HEADS * HEAD_DIM)
    vb_ref[...] = xv.astype(BF16)
    for h in range(N_KV_HEADS):
        vf_ref[0, head_rows(h), :] = xv[:, h * HEAD_DIM:(h + 1) * HEAD_DIM]

    xqi = proj(_C_QI, N_IDX_HEADS * IDX_DIM)
    for p in range(N_IDX_HEADS * IDX_DIM // LANES):
        sl = slice(p * LANES, (p + 1) * LANES)
        qi_ref[:, sl] = (rope_idx(xqi[:, sl]) * (IDX_DIM ** -0.5)).astype(BF16)

    kk = rope_idx(proj(_C_KI, LANES))
    kif_ref[0] = kk[:, :IDX_DIM]
    low = lane < IDX_DIM
    kze_ref[...] = jnp.where(low, kk, 0.0).astype(BF16)
    kzo_ref[...] = jnp.where(low, 0.0, kk).astype(BF16)

    wi_ref[...] = proj(_C_WI, LANES) * (N_IDX_HEADS ** -0.5)

    tb = lax.dot_general(wb_ref[0], hb, NT_DIMS, preferred_element_type=F32)
    kvw = N_KV_HEADS * HEAD_DIM
    ones_rows = (lax.broadcasted_iota(jnp.int32, (VT_ROWS - HEAD_DIM, tm), 0) == 0).astype(BF16)
    for n in range(N_KV_HEADS):
        vt_ref[0, n * VT_ROWS:n * VT_ROWS + HEAD_DIM, :] = (
            tb[n * HEAD_DIM:(n + 1) * HEAD_DIM].astype(BF16))
        vt_ref[0, n * VT_ROWS + HEAD_DIM:(n + 1) * VT_ROWS, :] = ones_rows
    wit_ref[...] = tb[kvw:] * (N_IDX_HEADS ** -0.5)

    u_ref[...] = jax.nn.gelu(proj(_C_U, d_model), approximate=True)
    gv = jax.nn.gelu(proj(_C_U + d_model, d_model), approximate=True)
    mu = jnp.mean(gv, axis=-1, keepdims=True)
    dv = gv - mu
    var = jnp.mean(dv * dv, axis=-1, keepdims=True)
    vn_ref[...] = dv * lax.rsqrt(var + EPS) * lng_ref[0] + lnb_ref[0]
    sga_ref[...] = _sigmoid(proj(_C_U + 2 * d_model, d_model))
    sgb_ref[...] = _sigmoid(proj(_C_U + 3 * d_model, d_model))


def _layer_spec(shape, layer):
    nd = len(shape)
    return pl.BlockSpec((1,) + tuple(shape[1:]), lambda *_: (layer,) + (0,) * (nd - 1),
                        pipeline_mode=pl.Buffered(1))


def _inproj(x, layer, g, wa, wb, tables, lng, lnb, stacked):
    T, D = x.shape
    depth = wa.shape[0]
    tm = TOKEN_TILE
    assert T % tm == 0
    kvw = N_KV_HEADS * HEAD_DIM
    qiw = N_IDX_HEADS * IDX_DIM
    row = lambda w: pl.BlockSpec((tm, w), lambda i: (i, 0))
    kv_rows = N_KV_HEADS * T
    out_shape = (
        jax.ShapeDtypeStruct((T, N_HEADS * HEAD_DIM), BF16),
        jax.ShapeDtypeStruct((depth, kv_rows, HEAD_DIM), F32),
        jax.ShapeDtypeStruct((T, kvw), BF16),
        jax.ShapeDtypeStruct((depth, kv_rows, HEAD_DIM), F32),
        jax.ShapeDtypeStruct((T, kvw), BF16),
        jax.ShapeDtypeStruct((T // tm, N_KV_HEADS * VT_ROWS, tm), BF16),
        jax.ShapeDtypeStruct((T, qiw), BF16),
        jax.ShapeDtypeStruct((depth, T, IDX_DIM), F32),
        jax.ShapeDtypeStruct((T, LANES), BF16),
        jax.ShapeDtypeStruct((T, LANES), BF16),
        jax.ShapeDtypeStruct((T, LANES), F32),
        jax.ShapeDtypeStruct((16, T), F32),
        jax.ShapeDtypeStruct((T, D), F32),
        jax.ShapeDtypeStruct((T, D), F32),
        jax.ShapeDtypeStruct((T, D), F32),
        jax.ShapeDtypeStruct((T, D), F32),
    )
    kv_spec = pl.BlockSpec((1, N_KV_HEADS * tm, HEAD_DIM), lambda i: (layer, i, 0))
    out_specs = (
        row(N_HEADS * HEAD_DIM), kv_spec, row(kvw), kv_spec, row(kvw),
        pl.BlockSpec((1, N_KV_HEADS * VT_ROWS, tm), lambda i: (i, 0, 0)),
        row(qiw), pl.BlockSpec((1, tm, IDX_DIM), lambda i: (layer, i, 0)),
        row(LANES), row(LANES), row(LANES),
        pl.BlockSpec((16, tm), lambda i: (0, i)),
        row(D), row(D), row(D), row(D),
    )
    in_specs = [row(D), _layer_spec(g.shape, layer), _layer_spec(wa.shape, layer),
                _layer_spec(wb.shape, layer),
                row(LANES), row(LANES), row(LANES), row(LANES),
                _layer_spec(lng.shape, layer), _layer_spec(lnb.shape, layer)]
    args = [x, g, wa, wb, *tables, lng, lnb]
    aliases = {}
    if stacked is not None:
        for j, out_idx in enumerate((1, 3, 7)):
            aliases[len(args)] = out_idx
            in_specs.append(pl.BlockSpec(memory_space=pl.ANY))
            args.append(stacked[j])
    return pl.pallas_call(
        functools.partial(_inproj_kernel, d_model=D, n_alias=len(aliases)),
        grid=(T // tm,),
        in_specs=in_specs, out_specs=out_specs, out_shape=out_shape,
        input_output_aliases=aliases,
        compiler_params=pltpu.CompilerParams(dimension_semantics=("arbitrary",),
                                             vmem_limit_bytes=VMEM_LIMIT),
        name="inproj",
    )(*args)


def _attn_prompt_kernel(q_ref, qi_ref, wit_ref, kze_ref, kzo_ref, k_ref, vt_ref, o_ref,
                        keys_ref, hi_ref, lo_ref, kmax_ref, acc_ref, m_ref, mx_ref, s_ref, p_ref, bias_ref,
                        *, k_top):
    tq, kb_sz = Q_TILE, KEY_TILE
    qb = pl.program_id(1)
    nkb = (qb + 1) * (tq // kb_sz)
    q_chunk = (qb * tq + lax.broadcasted_iota(jnp.int32, (1, tq), 1)) // CHUNK
    row_iota = lax.broadcasted_iota(jnp.int32, (kb_sz, 1), 0)

    def key_slice(kb):
        return pl.ds(pl.multiple_of(kb * kb_sz, kb_sz), kb_sz)

    n_blocks = keys_ref.shape[0] // kb_sz
    n_pairs = (nkb + 1) // 2

    def clamp(blk):
        return jnp.minimum(blk, n_blocks - 1)

    def idx_dots(blk, slot):
        ks = key_slice(blk)
        kze, kzo = kze_ref[ks, :], kzo_ref[ks, :]
        for p in range(N_IDX_HEADS // 2):
            qp = qi_ref[:, p * LANES:(p + 1) * LANES]
            s_ref[slot, 2 * p] = lax.dot_general(kze, qp, NT_DIMS, preferred_element_type=F32)
            s_ref[slot, 2 * p + 1] = lax.dot_general(kzo, qp, NT_DIMS, preferred_element_type=F32)

    def idx_keys(blk, slot):
        base = pl.multiple_of(blk * kb_sz, kb_sz)
        kmax = kmax_ref[...]
        for c in range(kb_sz // CHUNK):
            rows = slice(c * CHUNK, (c + 1) * CHUNK)
            score = jnp.zeros((CHUNK, tq), F32)
            for j in range(N_IDX_HEADS):
                score = score + wit_ref[j:j + 1, :] * jnp.maximum(s_ref[slot, j, rows, :], 0.0)
            visible = (blk * (kb_sz // CHUNK) + c) <= q_chunk
            key = jnp.where(visible, _order_key(score), INT_MIN)
            ks = pl.ds(base + c * CHUNK, CHUNK)
            keys_ref[ks, :] = key
            hi_ref[ks, :] = (key >> 16).astype(jnp.int16)
            lo_ref[ks, :] = ((key & 0xFFFF) - 2 ** 15).astype(jnp.int16)
            kmax = jnp.maximum(kmax, jnp.max(key.reshape(CHUNK // 8, 8, tq), axis=0))
        kmax_ref[...] = kmax

    def score_body(j, carry):
        b0 = 2 * j
        idx_dots(clamp(b0 + 1), 1)
        idx_keys(b0, 0)
        idx_dots(clamp(b0 + 2), 0)
        idx_keys(clamp(b0 + 1), 1)
        return carry

    kmax_ref[...] = jnp.full(kmax_ref.shape, INT_MIN, jnp.int32)
    idx_dots(0, 0)
    lax.fori_loop(0, n_pairs, score_body, 0)

    n_acc = 4
    rows16 = 16

    def fold_rows(x, accs):
        accs = list(accs)
        for r in range(kb_sz // rows16):
            accs[r % n_acc] = accs[r % n_acc] + x[r * rows16:(r + 1) * rows16, :]
        return tuple(accs)

    def zero_accs():
        return tuple(jnp.zeros((rows16, tq), jnp.int16) for _ in range(n_acc))

    def total(accs):
        tot = accs[0].astype(jnp.int32)
        for a in accs[1:]:
            tot = tot + a.astype(jnp.int32)
        return jnp.sum(tot, axis=0, keepdims=True)

    def count_ge(plane_ref, t):
        t16 = jnp.broadcast_to(t, (kb_sz, tq)).astype(jnp.int16)

        def body(kb, accs):
            ge = (plane_ref[key_slice(kb), :] >= t16).astype(jnp.int16)
            return fold_rows(ge, accs)

        return total(lax.fori_loop(0, nkb, body, zero_accs()))

    def bisect16(plane_ref, need, lo, hi, steps):
        def body(_, carry):
            lo, hi = carry
            mid = (lo + hi) >> 1
            ok = count_ge(plane_ref, mid) >= need
            return jnp.where(ok, mid, lo), jnp.where(ok, hi, mid)

        return lax.fori_loop(0, steps, body, (lo, hi))[0]

    floor16 = jnp.full((1, tq), -(2 ** 15), jnp.int32)
    top16 = (jnp.max(kmax_ref[...], axis=0, keepdims=True) >> 16) + 1
    lo_try = jnp.maximum(top16 - 2 ** WINDOW_BITS, -(2 ** 15) + 1)
    in_window = count_ge(hi_ref, lo_try) >= k_top
    steps = jnp.where(jnp.min(in_window.astype(jnp.int32)) > 0, WINDOW_BITS, 16)
    t_hi = bisect16(hi_ref, k_top, jnp.where(in_window, lo_try, floor16), top16, steps)

    t_hi16 = jnp.broadcast_to(t_hi, (kb_sz, tq)).astype(jnp.int16)

    def tie_body(kb, accs):
        ks = key_slice(kb)
        hi_part = hi_ref[ks, :]
        lo_ref[ks, :] = jnp.where(hi_part == t_hi16, lo_ref[ks, :], jnp.int16(-(2 ** 15)))
        return fold_rows((hi_part > t_hi16).astype(jnp.int16), accs)

    n_above = total(lax.fori_loop(0, nkb, tie_body, zero_accs()))
    t_lo = bisect16(lo_ref, k_top - n_above, floor16, jnp.full((1, tq), 2 ** 15, jnp.int32), 16)
    thr = jnp.maximum((t_hi << 16) + (t_lo + 2 ** 15), INT_MIN + 1)

    m_ref[...] = jnp.full(m_ref.shape, MASK_BIAS, F32)
    acc_ref[...] = jnp.zeros(acc_ref.shape, F32)

    rc = 32
    n_rc = kb_sz // rc

    def fold8(x):
        return x.reshape(rc // 8, 8, tq)

    def qk_dots(blk, slot):
        ks = key_slice(blk)
        for h in range(N_HEADS):
            n = h // KV_GROUP
            k_n = k_ref[ks, n * HEAD_DIM:(n + 1) * HEAD_DIM]
            q_h = q_ref[:, h * HEAD_DIM:(h + 1) * HEAD_DIM]
            s_ref[slot, h] = lax.dot_general(k_n, q_h, NT_DIMS, preferred_element_type=F32)

    def softmax_block(blk, slot):
        thr_b = jnp.where(blk < nkb, thr, INT_MAX)
        bias_ref[slot] = jnp.where(keys_ref[key_slice(clamp(blk)), :] >= thr_b, 0.0, MASK_BIAS)
        for h in range(N_HEADS):
            mx = jnp.full((8, tq), MASK_BIAS, F32)
            for r in range(n_rc):
                rows = slice(r * rc, (r + 1) * rc)
                sc = s_ref[slot, h, rows, :] + bias_ref[slot, rows, :]
                s_ref[slot, h, rows, :] = sc
                mx = jnp.maximum(mx, jnp.max(fold8(sc), axis=0))
            mx_ref[h:h + 1, :] = jnp.max(mx, axis=0, keepdims=True)
        m_old = m_ref[...]
        m_new = jnp.maximum(m_old, mx_ref[...])
        alpha = jnp.exp2(m_old - m_new)
        m_ref[...] = m_new
        for h in range(N_HEADS):
            m_h = jnp.broadcast_to(m_new[h:h + 1, :], (rc, tq))
            for r in range(n_rc):
                rows = slice(r * rc, (r + 1) * rc)
                p_ref[slot, h, rows, :] = jnp.exp2(s_ref[slot, h, rows, :] - m_h).astype(BF16)
        return alpha

    def pv_update(blk, slot, alpha):
        for h in range(N_HEADS):
            n = h // KV_GROUP
            vt_n = vt_ref[blk, n * VT_ROWS:(n + 1) * VT_ROWS, :]
            pv = jnp.dot(vt_n, p_ref[slot, h], preferred_element_type=F32)
            acc_ref[h] = alpha[h:h + 1, :] * acc_ref[h] + pv

    def attn_body(j, carry):
        b0 = 2 * j
        qk_dots(clamp(b0 + 1), 1)
        pv_update(b0, 0, softmax_block(b0, 0))
        qk_dots(clamp(b0 + 2), 0)
        pv_update(clamp(b0 + 1), 1, softmax_block(b0 + 1, 1))
        return carry

    qk_dots(0, 0)
    lax.fori_loop(0, n_pairs, attn_body, 0)

    for h in range(N_HEADS):
        o_t = acc_ref[h, :HEAD_DIM, :] / acc_ref[h, HEAD_DIM:HEAD_DIM + 1, :]
        o_ref[:, h * HEAD_DIM:(h + 1) * HEAD_DIM] = o_t.T


def _attn_prompt(q, qi, wit, kze, kzo, kb, vt, batch, seq, k_top):
    tq = Q_TILE
    assert seq % tq == 0 and Q_TILE % KEY_TILE == 0 and KEY_TILE == TOKEN_TILE
    nq = seq // tq
    kvw = N_KV_HEADS * HEAD_DIM
    qrow = lambda w: pl.BlockSpec((tq, w), lambda b, i: (b * nq + i, 0))
    per_batch = lambda w: pl.BlockSpec((seq, w), lambda b, i: (b, 0), pipeline_mode=pl.Buffered(1))
    return pl.pallas_call(
        functools.partial(_attn_prompt_kernel, k_top=k_top),
        grid=(batch, nq),
        in_specs=[qrow(N_HEADS * HEAD_DIM), qrow(N_IDX_HEADS * IDX_DIM),
                  pl.BlockSpec((16, tq), lambda b, i: (0, b * nq + i)),
                  per_batch(LANES), per_batch(LANES), per_batch(kvw),
                  pl.BlockSpec((seq // KEY_TILE, N_KV_HEADS * VT_ROWS, KEY_TILE), lambda b, i: (b, 0, 0),
                               pipeline_mode=pl.Buffered(1))],
        out_specs=qrow(N_HEADS * HEAD_DIM),
        out_shape=jax.ShapeDtypeStruct((batch * seq, N_HEADS * HEAD_DIM), F32),
        scratch_shapes=[pltpu.VMEM((seq, tq), jnp.int32),
                        pltpu.VMEM((seq, tq), jnp.int16),
                        pltpu.VMEM((seq, tq), jnp.int16),
                        pltpu.VMEM((8, tq), jnp.int32),
                        pltpu.VMEM((N_HEADS, VT_ROWS, tq), F32),
                        pltpu.VMEM((N_HEADS, tq), F32),
                        pltpu.VMEM((N_HEADS, tq), F32),
                        pltpu.VMEM((2, N_HEADS, KEY_TILE, tq), F32),
                        pltpu.VMEM((2, N_HEADS, KEY_TILE, tq), BF16),
                        pltpu.VMEM((2, KEY_TILE, tq), F32)],
        compiler_params=pltpu.CompilerParams(dimension_semantics=("arbitrary", "arbitrary"),
                                             vmem_limit_bytes=VMEM_LIMIT),
        name="attn_prompt",
    )(q, qi, wit, kze, kzo, kb, vt)


def _attn_sample_kernel(q_ref, qi_ref, wi_ref, ck_ref, cv_ref, cki_ref, kn_ref, vn_ref, kin_ref,
                        o_ref, keyp_ref, keyn_ref, keyt_ref, biasp_ref, *, k_top, past_len, tn):
    group = LANES // tn
    rows = group * tn
    n_cols = past_len // LANES
    kn = kn_ref[...]
    vn = vn_ref[...]
    kin = kin_ref[0].astype(BF16)

    lane = lax.broadcasted_iota(jnp.int32, (tn, LANES), 1)
    pos = lax.broadcasted_iota(jnp.int32, (tn, LANES), 0)
    vis = ((past_len + lane % tn) // CHUNK) <= ((past_len + pos) // CHUNK)
    for j in range(group):
        qj = qi_ref[j].reshape(N_IDX_HEADS * tn, IDX_DIM)
        dp = lax.dot_general(qj, cki_ref[0, j].astype(BF16), NT_DIMS, preferred_element_type=F32)
        dn = lax.dot_general(qj, kin, NT_DIMS, preferred_element_type=F32)
        sp = jnp.zeros((tn, past_len), F32)
        sn = jnp.zeros((tn, LANES), F32)
        for h in range(N_IDX_HEADS):
            w = wi_ref[j * tn:(j + 1) * tn, h:h + 1]
            sp = sp + w * jnp.maximum(dp[h * tn:(h + 1) * tn, :], 0.0)
            sn = sn + w * jnp.maximum(dn[h * tn:(h + 1) * tn, :], 0.0)
        keyp_ref[j * tn:(j + 1) * tn, :] = _order_key(sp)
        own = (lane // tn) == j
        keyn_ref[j * tn:(j + 1) * tn, :] = jnp.where(own & vis, _order_key(sn), INT_MIN)

    for c in range(n_cols):
        keyt_ref[c * LANES:(c + 1) * LANES, :] = keyp_ref[:, c * LANES:(c + 1) * LANES].T
    keyt_ref[past_len:past_len + LANES, :] = keyn_ref[...].T
    n_keys = past_len + LANES

    def bisect_body(_, carry):
        lo, hi = carry
        mid = _midpoint(lo, hi)
        ge = (keyt_ref[...] >= mid).astype(jnp.int32)
        cnt = jnp.sum(jnp.sum(ge.reshape(n_keys // 8, 8, rows), axis=0), axis=0, keepdims=True)
        ok = cnt >= k_top
        return jnp.where(ok, mid, lo), jnp.where(ok, hi, mid)

    thr, _ = lax.fori_loop(
        0, 32, bisect_body,
        (jnp.full((1, rows), INT_MIN + 1, jnp.int32), jnp.full((1, rows), INT_MAX, jnp.int32)))
    thr_col = jnp.broadcast_to(thr, (rows, rows)).T

    bias_n = jnp.where(keyn_ref[...] >= thr_col, 0.0, MASK_BIAS)
    for c in range(n_cols):
        cols = slice(c * LANES, (c + 1) * LANES)
        biasp_ref[:, cols] = jnp.where(keyp_ref[:, cols] >= thr_col, 0.0, MASK_BIAS)

    for j in range(group):
        bias_pj = biasp_ref[j * tn:(j + 1) * tn, :]
        bias_nj = bias_n[j * tn:(j + 1) * tn, :]
        for n in range(N_KV_HEADS):
            hs = slice(n * HEAD_DIM, (n + 1) * HEAD_DIM)
            k_n = ck_ref[0, j, pl.ds(n, past_len, stride=N_KV_HEADS), :].astype(BF16)
            v_n = cv_ref[0, j, pl.ds(n, past_len, stride=N_KV_HEADS), :].astype(BF16)
            qg = jnp.concatenate(
                [q_ref[j * tn:(j + 1) * tn, (n * KV_GROUP + g) * HEAD_DIM:(n * KV_GROUP + g + 1) * HEAD_DIM]
                 for g in range(KV_GROUP)], axis=0)
            s1 = lax.dot_general(qg, k_n, NT_DIMS, preferred_element_type=F32)
            s2 = lax.dot_general(qg, kn[:, hs], NT_DIMS, preferred_element_type=F32)
            s1 = (s1.reshape(KV_GROUP, tn, past_len) + bias_pj[None]).reshape(KV_GROUP * tn, past_len)
            s2 = (s2.reshape(KV_GROUP, tn, LANES) + bias_nj[None]).reshape(KV_GROUP * tn, LANES)
            m = jnp.maximum(jnp.max(s1, axis=1, keepdims=True), jnp.max(s2, axis=1, keepdims=True))
            p1 = jnp.exp2(s1 - m)
            p2 = jnp.exp2(s2 - m)
            l = jnp.sum(p1, axis=1, keepdims=True) + jnp.sum(p2, axis=1, keepdims=True)
            o = (jnp.dot(p1.astype(BF16), v_n, preferred_element_type=F32)
                 + jnp.dot(p2.astype(BF16), vn[:, hs], preferred_element_type=F32)) / l
            for g in range(KV_GROUP):
                h = n * KV_GROUP + g
                o_ref[j * tn:(j + 1) * tn, h * HEAD_DIM:(h + 1) * HEAD_DIM] = o[g * tn:(g + 1) * tn, :]


def _attn_sample(q, qi4, wi, ck, cv, cki, kb, vb, kif, layer, k_top):
    nb = ck.shape[1]
    past_len = cki.shape[2]
    kvw = N_KV_HEADS * HEAD_DIM
    tn = q.shape[0] // nb
    assert LANES % tn == 0 and tn % 16 == 0 and past_len % LANES == 0
    group = LANES // tn
    assert nb % group == 0
    row = lambda w: pl.BlockSpec((LANES, w), lambda i: (i, 0))
    cache = lambda rows, w: pl.BlockSpec((1, group, rows, w), lambda i: (layer, i, 0, 0))
    return pl.pallas_call(
        functools.partial(_attn_sample_kernel, k_top=k_top, past_len=past_len, tn=tn),
        grid=(nb // group,),
        in_specs=[row(N_HEADS * HEAD_DIM),
                  pl.BlockSpec((group, N_IDX_HEADS, tn, IDX_DIM), lambda i: (i, 0, 0, 0)),
                  row(LANES), cache(N_KV_HEADS * past_len, HEAD_DIM),
                  cache(N_KV_HEADS * past_len, HEAD_DIM), cache(past_len, IDX_DIM),
                  row(kvw), row(kvw),
                  pl.BlockSpec((1, LANES, IDX_DIM), lambda i: (layer, i, 0))],
        out_specs=row(N_HEADS * HEAD_DIM),
        out_shape=jax.ShapeDtypeStruct((nb * tn, N_HEADS * HEAD_DIM), F32),
        scratch_shapes=[pltpu.VMEM((LANES, past_len), jnp.int32),
                        pltpu.VMEM((LANES, LANES), jnp.int32),
                        pltpu.VMEM((past_len + LANES, LANES), jnp.int32),
                        pltpu.VMEM((LANES, past_len), F32)],
        compiler_params=pltpu.CompilerParams(dimension_semantics=("arbitrary",),
                                             vmem_limit_bytes=VMEM_LIMIT),
        name="attn_sample",
    )(q, qi4, wi, ck, cv, cki, kb, vb, kif)


def _outffn_kernel(x_ref, a_ref, sga_ref, sgb_ref, u_ref, vn_ref, wmix_ref, bmix_ref, wo_ref,
                   g2_ref, wg_ref, wu_ref, wd_ref, gf_ref, xo_ref, *rest, final):
    if final:
        y_ref, z_ref = rest
    else:
        (z_ref,) = rest
    tm = x_ref.shape[0]
    group_dim = wmix_ref.shape[-1]
    for c in range(tm // SGU_CHUNK):
        rs = slice(c * SGU_CHUNK, (c + 1) * SGU_CHUNK)
        for g in range(SGU_GROUPS):
            cs = slice(g * group_dim, (g + 1) * group_dim)
            mixed = jnp.dot(wmix_ref[0, g], vn_ref[rs, cs].astype(BF16),
                            preferred_element_type=F32) + bmix_ref[0, :, cs]
            z = sga_ref[rs, cs] * a_ref[rs, cs] + sgb_ref[rs, cs] * (u_ref[rs, cs] * mixed)
            z_ref[rs, cs] = z.astype(BF16)
    x1 = x_ref[...] + jnp.dot(z_ref[...], wo_ref[0], preferred_element_type=F32)
    h2 = _rms_norm(x1, g2_ref[0]).astype(BF16)
    gate = jnp.dot(h2, wg_ref[0], preferred_element_type=F32)
    up = jnp.dot(h2, wu_ref[0], preferred_element_type=F32)
    ff = (gate * _sigmoid(gate) * up).astype(BF16)
    x2 = x1 + jnp.dot(ff, wd_ref[0], preferred_element_type=F32)
    xo_ref[...] = x2
    if final:
        y_ref[...] = _rms_norm(x2, gf_ref[...])


def _outffn(x, a, sga, sgb, u, vn, wmix, bmix, layer, wo, g2, wg, wu, wd, gf, final):
    T, D = x.shape
    tm = TOKEN_TILE
    row = pl.BlockSpec((tm, D), lambda i: (i, 0))
    n_out = 2 if final else 1
    out = pl.pallas_call(
        functools.partial(_outffn_kernel, final=final),
        grid=(T // tm,),
        in_specs=[row] * 6 + [_layer_spec(wmix.shape, layer), _layer_spec(bmix.shape, layer),
                              _layer_spec(wo.shape, layer), _layer_spec(g2.shape, layer),
                              _layer_spec(wg.shape, layer), _layer_spec(wu.shape, layer),
                              _layer_spec(wd.shape, layer), _const_spec((1, D))],
        out_specs=(row,) * n_out,
        out_shape=(jax.ShapeDtypeStruct((T, D), F32),) * n_out,
        scratch_shapes=[pltpu.VMEM((tm, D), BF16)],
        compiler_params=pltpu.CompilerParams(dimension_semantics=("arbitrary",),
                                             vmem_limit_bytes=VMEM_LIMIT),
        name="outffn",
    )(x, a, sga, sgb, u, vn, wmix, bmix, wo, g2, wg, wu, wd, gf)
    return out if final else (out[0], None)


def _rope_tables(pos, d):
    inv = ROPE_THETA ** (-jnp.arange(0, d, 2, dtype=F32) / d)
    ang = pos.astype(F32)[:, None] * inv[None, :]
    cos, sin = jnp.cos(ang), jnp.sin(ang)
    c = jnp.concatenate([cos, cos], axis=-1)
    s = jnp.concatenate([-sin, sin], axis=-1)
    reps = LANES // d
    return jnp.tile(c, (1, reps)), jnp.tile(s, (1, reps))


def _mix_weights(sgu_w, sgu_b, n, group_dim):
    depth = sgu_w.shape[0]
    p = jnp.arange(n)
    mask = (p[None, :] // CHUNK) <= (p[:, None] // CHUNK)
    w = jnp.where(mask[None, None], sgu_w[:, :, :n, :n], 0.0)
    reps = SGU_CHUNK // n
    eye = jnp.eye(reps, dtype=w.dtype)
    wbd = jnp.einsum('ab,lgij->lgaibj', eye, w).reshape(depth, SGU_GROUPS, SGU_CHUNK, SGU_CHUNK)
    b = jnp.tile(sgu_b[:, :, :n], (1, 1, reps))
    bfull = jnp.repeat(jnp.swapaxes(b, 1, 2), group_dim, axis=2)
    return wbd.astype(BF16), bfull


def _in_weights(w_in, d_model):
    kvw = N_KV_HEADS * HEAD_DIM
    w = w_in.astype(BF16)
    c_ki = d_model + 2 * kvw + N_IDX_HEADS * IDX_DIM
    c_wi = c_ki + IDX_DIM
    c_u = c_wi + N_IDX_HEADS
    wki, wwi = w[:, :, c_ki:c_wi], w[:, :, c_wi:c_u]
    wwi_pad = jnp.pad(wwi, ((0, 0), (0, 0), (0, LANES - N_IDX_HEADS)))
    wa = jnp.concatenate([w[:, :, :c_ki], wki, wki, wwi_pad, w[:, :, c_u:]], axis=2)
    wv_t = jnp.swapaxes(w[:, :, d_model + kvw:d_model + 2 * kvw], 1, 2)
    wwi_t = jnp.pad(jnp.swapaxes(wwi, 1, 2), ((0, 0), (0, 16 - N_IDX_HEADS), (0, 0)))
    return wa, jnp.concatenate([wv_t, wwi_t], axis=1)


def kernel(x_prompt, x_sample, cache_k, cache_v, cache_kidx, norm1_g, w_in, ln_v_g, ln_v_b, sgu_w, sgu_b, w_out, norm2_g, w_gate, w_up, w_down, final_norm_g):
    B, S, D = x_prompt.shape
    NB, TN, _ = x_sample.shape
    depth, _, P = cache_k.shape[:3]
    assert D == N_HEADS * HEAD_DIM and SGU_CHUNK % TN == 0 and S % SGU_CHUNK == 0
    k_top_p = min(TOPK_MAX, S // 4)
    k_top_s = min(TOPK_MAX, (P + TN) // 4)
    group_dim = D // SGU_GROUPS

    pos_p = jnp.tile(jnp.arange(S), B)
    pos_s = jnp.tile(P + jnp.arange(TN), NB)
    tab_p = _rope_tables(pos_p, HEAD_DIM) + _rope_tables(pos_p, IDX_DIM)
    tab_s = _rope_tables(pos_s, HEAD_DIM) + _rope_tables(pos_s, IDX_DIM)

    wa, wb = _in_weights(w_in, D)
    wo, wg, wu, wd = (t.astype(BF16) for t in (w_out, w_gate, w_up, w_down))
    vec = lambda v: v.reshape(depth, 1, -1)
    g1, g2, lng, lnb = vec(norm1_g), vec(norm2_g), vec(ln_v_g), vec(ln_v_b)
    gf = final_norm_g.reshape(1, -1)
    wmix_p, bmix_p = _mix_weights(sgu_w, sgu_b, SGU_CHUNK, group_dim)
    wmix_s, bmix_s = _mix_weights(sgu_w, sgu_b, TN, group_dim)
    ck = cache_k.reshape(depth, NB, P * N_KV_HEADS, HEAD_DIM)
    cv = cache_v.reshape(depth, NB, P * N_KV_HEADS, HEAD_DIM)

    xp = x_prompt.reshape(B * S, D)
    xs = x_sample.reshape(NB * TN, D)
    new_p = new_s = None
    sgu_v = []
    yp = ys = None
    for l in range(depth):
        final = l == depth - 1

        (q, kf, kb, vf, _, vt, qi, kif, kze, kzo, _, wit, u, vn, sga, sgb) = _inproj(
            xp, l, g1, wa, wb, tab_p, lng, lnb, new_p)
        new_p = (kf, vf, kif)
        a = _attn_prompt(q, qi, wit, kze, kzo, kb, vt, B, S, k_top_p)
        xp, yp = _outffn(xp, a, sga, sgb, u, vn, wmix_p, bmix_p, l, wo, g2, wg, wu, wd, gf, final)

        (q, kf, kb, vf, vb, _, qi, kif, _, _, wi, _, u, vn, sga, sgb) = _inproj(
            xs, l, g1, wa, wb, tab_s, lng, lnb, new_s)
        new_s = (kf, vf, kif)
        qi4 = qi.reshape(NB, TN, N_IDX_HEADS, IDX_DIM).transpose(0, 2, 1, 3)
        a = _attn_sample(q, qi4, wi, ck, cv, cache_kidx, kb, vb, kif, l, k_top_s)
        xs, ys = _outffn(xs, a, sga, sgb, u, vn, wmix_s, bmix_s, l, wo, g2, wg, wu, wd, gf, final)
        sgu_v.append(vn.reshape(NB, TN, D))

    kv_p = (depth, B, S, N_KV_HEADS, HEAD_DIM)
    kv_s = (depth, NB, TN, N_KV_HEADS, HEAD_DIM)
    return (yp.reshape(B, S, D), ys.reshape(NB, TN, D),
            new_p[0].reshape(kv_p), new_p[1].reshape(kv_p), new_p[2].reshape(depth, B, S, IDX_DIM),
            new_s[0].reshape(kv_s), new_s[1].reshape(kv_s), new_s[2].reshape(depth, NB, TN, IDX_DIM),
            jnp.stack(sgu_v))
```

```python
import functools
import math

import jax
import jax.numpy as jnp
from jax import lax
from jax.experimental import pallas as pl
from jax.experimental.pallas import tpu as pltpu

CHUNK = 64
N_HEADS = 8
HEAD_DIM = 128
N_KV_HEADS = 2
KV_GROUP = N_HEADS // N_KV_HEADS
N_IDX_HEADS = 8
IDX_DIM = 64
TOPK_MAX = 256
SGU_CHUNK = 128
SGU_GROUPS = 8
ROPE_THETA = 10000.0
EPS = 1e-6

LANES = 128
TOKEN_TILE = 256
Q_TILE = 256
KEY_TILE = 256
VT_ROWS = HEAD_DIM + 16
WINDOW_BITS = 9
VMEM_LIMIT = 56 * 1024 * 1024

INT_MIN = -(2 ** 31)
INT_MAX = 2 ** 31 - 1
MASK_BIAS = -1e30
LOG2E = 1.4426950408889634
Q_SCALE = (HEAD_DIM ** -0.5) * LOG2E

F32 = jnp.float32
BF16 = jnp.bfloat16
NT_DIMS = (((1,), (1,)), ((), ()))


def _const_spec(shape):
    nd = len(shape)
    return pl.BlockSpec(shape, lambda *_: (0,) * nd, pipeline_mode=pl.Buffered(1))


def _sigmoid(x):
    return 1.0 / (1.0 + jnp.exp(-x))


def _rms_norm(x, g):
    return x * lax.rsqrt(jnp.mean(x * x, axis=-1, keepdims=True) + EPS) * g


def _order_key(score):
    bits = pltpu.bitcast(score, jnp.int32)
    return bits ^ ((bits >> 31) & INT_MAX)


def _midpoint(lo, hi):
    return (lo >> 1) + (hi >> 1) + (lo & hi & 1)


_C_Q = 0
_C_K = _C_Q + N_HEADS * HEAD_DIM
_C_V = _C_K + N_KV_HEADS * HEAD_DIM
_C_QI = _C_V + N_KV_HEADS * HEAD_DIM
_C_KI = _C_QI + N_IDX_HEADS * IDX_DIM
_C_WI = _C_KI + LANES
_C_U = _C_WI + LANES


def _inproj_kernel(*refs, d_model, n_alias):
    (x_ref, g_ref, wa_ref, wb_ref, cosh_ref, sinh_ref, cosi_ref, sini_ref,
     lng_ref, lnb_ref) = refs[:10]
    (q_ref, kf_ref, kb_ref, vf_ref, vb_ref, vt_ref, qi_ref, kif_ref, kze_ref, kzo_ref,
     wi_ref, wit_ref, u_ref, vn_ref, sga_ref, sgb_ref) = refs[10 + n_alias:]
    tm = x_ref.shape[0]
    hb = _rms_norm(x_ref[...], g_ref[0]).astype(BF16)

    def proj(c0, width):
        return jnp.dot(hb, wa_ref[0, :, c0:c0 + width], preferred_element_type=F32)

    def head_rows(n):
        return pl.ds(n, tm, stride=N_KV_HEADS)

    cosh, sinh = cosh_ref[...], sinh_ref[...]
    cosi, sini = cosi_ref[...], sini_ref[...]
    lane = lax.broadcasted_iota(jnp.int32, cosi.shape, 1)
    first_half = (lane % IDX_DIM) < (IDX_DIM // 2)

    def rope_head(x):
        return x * cosh + pltpu.roll(x, HEAD_DIM // 2, 1) * sinh

    def rope_idx(x):
        partner = jnp.where(first_half, pltpu.roll(x, LANES - IDX_DIM // 2, 1),
                            pltpu.roll(x, IDX_DIM // 2, 1))
        return x * cosi + partner * sini

    xq = proj(_C_Q, N_HEADS * HEAD_DIM)
    for h in range(N_HEADS):
        sl = slice(h * HEAD_DIM, (h + 1) * HEAD_DIM)
        q_ref[:, sl] = (rope_head(xq[:, sl]) * Q_SCALE).astype(BF16)

    xk = proj(_C_K, N_KV_HEADS * HEAD_DIM)
    for h in range(N_KV_HEADS):
        sl = slice(h * HEAD_DIM, (h + 1) * HEAD_DIM)
        kr = rope_head(xk[:, sl])
        kf_ref[0, head_rows(h), :] = kr
        kb_ref[:, sl] = kr.astype(BF16)

    xv = proj(_C_V, N_KV_HEADS * HEAD_DIM)
    vb_ref[...] = xv.astype(BF16)
    for h in range(N_KV_HEADS):
        vf_ref[0, head_rows(h), :] = xv[:, h * HEAD_DIM:(h + 1) * HEAD_DIM]

    xqi = proj(_C_QI, N_IDX_HEADS * IDX_DIM)
    for p in range(N_IDX_HEADS * IDX_DIM // LANES):
        sl = slice(p * LANES, (p + 1) * LANES)
        qi_ref[:, sl] = (rope_idx(xqi[:, sl]) * (IDX_DIM ** -0.5)).astype(BF16)

    kk = rope_idx(proj(_C_KI, LANES))
    kif_ref[0] = kk[:, :IDX_DIM]
    low = lane < IDX_DIM
    kze_ref[...] = jnp.where(low, kk, 0.0).astype(BF16)
    kzo_ref[...] = jnp.where(low, 0.0, kk).astype(BF16)

    wi_ref[...] = proj(_C_WI, LANES) * (N_IDX_HEADS ** -0.5)

    tb = lax.dot_general(wb_ref[0], hb, NT_DIMS, preferred_element_type=F32)
    kvw = N_KV_HEADS * HEAD_DIM
    ones_rows = (lax.broadcasted_iota(jnp.int32, (VT_ROWS - HEAD_DIM, tm), 0) == 0).astype(BF16)
    for n in range(N_KV_HEADS):
        vt_ref[0, n * VT_ROWS:n * VT_ROWS + HEAD_DIM, :] = (
            tb[n * HEAD_DIM:(n + 1) * HEAD_DIM].astype(BF16))
        vt_ref[0, n * VT_ROWS + HEAD_DIM:(n + 1) * VT_ROWS, :] = ones_rows
    wit_ref[...] = tb[kvw:] * (N_IDX_HEADS ** -0.5)

    u_ref[...] = jax.nn.gelu(proj(_C_U, d_model), approximate=True)
    gv = jax.nn.gelu(proj(_C_U + d_model, d_model), approximate=True)
    mu = jnp.mean(gv, axis=-1, keepdims=True)
    dv = gv - mu
    var = jnp.mean(dv * dv, axis=-1, keepdims=True)
    vn_ref[...] = dv * lax.rsqrt(var + EPS) * lng_ref[0] + lnb_ref[0]
    sga_ref[...] = _sigmoid(proj(_C_U + 2 * d_model, d_model))
    sgb_ref[...] = _sigmoid(proj(_C_U + 3 * d_model, d_model))


def _layer_spec(shape, layer):
    nd = len(shape)
    return pl.BlockSpec((1,) + tuple(shape[1:]), lambda *_: (layer,) + (0,) * (nd - 1),
                        pipeline_mode=pl.Buffered(1))


def _inproj(x, layer, g, wa, wb, tables, lng, lnb, stacked):
    T, D = x.shape
    depth = wa.shape[0]
    tm = TOKEN_TILE
    assert T % tm == 0
    kvw = N_KV_HEADS * HEAD_DIM
    qiw = N_IDX_HEADS * IDX_DIM
    row = lambda w: pl.BlockSpec((tm, w), lambda i: (i, 0))
    kv_rows = N_KV_HEADS * T
    out_shape = (
        jax.ShapeDtypeStruct((T, N_HEADS * HEAD_DIM), BF16),
        jax.ShapeDtypeStruct((depth, kv_rows, HEAD_DIM), F32),
        jax.ShapeDtypeStruct((T, kvw), BF16),
        jax.ShapeDtypeStruct((depth, kv_rows, HEAD_DIM), F32),
        jax.ShapeDtypeStruct((T, kvw), BF16),
        jax.ShapeDtypeStruct((T // tm, N_KV_HEADS * VT_ROWS, tm), BF16),
        jax.ShapeDtypeStruct((T, qiw), BF16),
        jax.ShapeDtypeStruct((depth, T, IDX_DIM), F32),
        jax.ShapeDtypeStruct((T, LANES), BF16),
        jax.ShapeDtypeStruct((T, LANES), BF16),
        jax.ShapeDtypeStruct((T, LANES), F32),
        jax.ShapeDtypeStruct((16, T), F32),
        jax.ShapeDtypeStruct((T, D), F32),
        jax.ShapeDtypeStruct((T, D), F32),
        jax.ShapeDtypeStruct((T, D), F32),
        jax.ShapeDtypeStruct((T, D), F32),
    )
    kv_spec = pl.BlockSpec((1, N_KV_HEADS * tm, HEAD_DIM), lambda i: (layer, i, 0))
    out_specs = (
        row(N_HEADS * HEAD_DIM), kv_spec, row(kvw), kv_spec, row(kvw),
        pl.BlockSpec((1, N_KV_HEADS * VT_ROWS, tm), lambda i: (i, 0, 0)),
        row(qiw), pl.BlockSpec((1, tm, IDX_DIM), lambda i: (layer, i, 0)),
        row(LANES), row(LANES), row(LANES),
        pl.BlockSpec((16, tm), lambda i: (0, i)),
        row(D), row(D), row(D), row(D),
    )
    n_tab = tables[0].shape[0] // tm
    tab = pl.BlockSpec((tm, LANES), lambda i: (i % n_tab, 0))
    in_specs = [row(D), _layer_spec(g.shape, layer), _layer_spec(wa.shape, layer),
                _layer_spec(wb.shape, layer),
                tab, tab, tab, tab,
                _layer_spec(lng.shape, layer), _layer_spec(lnb.shape, layer)]
    args = [x, g, wa, wb, *tables, lng, lnb]
    aliases = {}
    for j, out_idx in enumerate((1, 3, 7)):
        aliases[len(args)] = out_idx
        in_specs.append(pl.BlockSpec(memory_space=pl.ANY))
        args.append(stacked[j])
    return pl.pallas_call(
        functools.partial(_inproj_kernel, d_model=D, n_alias=len(aliases)),
        grid=(T // tm,),
        in_specs=in_specs, out_specs=out_specs, out_shape=out_shape,
        input_output_aliases=aliases,
        compiler_params=pltpu.CompilerParams(dimension_semantics=("arbitrary",),
                                             vmem_limit_bytes=VMEM_LIMIT),
        name="inproj",
    )(*args)


def _attn_prompt_kernel(q_ref, qi_ref, wit_ref, kze_ref, kzo_ref, k_ref, vt_ref, o_ref,
                        keys_ref, hi_ref, lo_ref, kmax_ref, acc_ref, m_ref, mx_ref, s_ref, p_ref, bias_ref,
                        *, k_top):
    tq, kb_sz = Q_TILE, KEY_TILE
    qb = pl.program_id(1)
    nkb = (qb + 1) * (tq // kb_sz)
    q_chunk = (qb * tq + lax.broadcasted_iota(jnp.int32, (1, tq), 1)) // CHUNK
    row_iota = lax.broadcasted_iota(jnp.int32, (kb_sz, 1), 0)

    def key_slice(kb):
        return pl.ds(pl.multiple_of(kb * kb_sz, kb_sz), kb_sz)

    n_blocks = keys_ref.shape[0] // kb_sz
    n_pairs = (nkb + 1) // 2

    def clamp(blk):
        return jnp.minimum(blk, n_blocks - 1)

    def idx_dots(blk, slot):
        ks = key_slice(blk)
        kze, kzo = kze_ref[ks, :], kzo_ref[ks, :]
        for p in range(N_IDX_HEADS // 2):
            qp = qi_ref[:, p * LANES:(p + 1) * LANES]
            s_ref[slot, 2 * p] = lax.dot_general(kze, qp, NT_DIMS, preferred_element_type=F32)
            s_ref[slot, 2 * p + 1] = lax.dot_general(kzo, qp, NT_DIMS, preferred_element_type=F32)

    def idx_keys(blk, slot):
        base = pl.multiple_of(blk * kb_sz, kb_sz)
        kmax = kmax_ref[...]
        for c in range(kb_sz // CHUNK):
            rows = slice(c * CHUNK, (c + 1) * CHUNK)
            score = jnp.zeros((CHUNK, tq), F32)
            for j in range(N_IDX_HEADS):
                score = score + wit_ref[j:j + 1, :] * jnp.maximum(s_ref[slot, j, rows, :], 0.0)
            visible = (blk * (kb_sz // CHUNK) + c) <= q_chunk
            key = jnp.where(visible, _order_key(score), INT_MIN)
            ks = pl.ds(base + c * CHUNK, CHUNK)
            keys_ref[ks, :] = key
            hi_ref[ks, :] = (key >> 16).astype(jnp.int16)
            lo_ref[ks, :] = ((key & 0xFFFF) - 2 ** 15).astype(jnp.int16)
            kmax = jnp.maximum(kmax, jnp.max(key.reshape(CHUNK // 8, 8, tq), axis=0))
        kmax_ref[...] = kmax

    def score_body(j, carry):
        b0 = 2 * j
        idx_dots(clamp(b0 + 1), 1)
        idx_keys(b0, 0)
        idx_dots(clamp(b0 + 2), 0)
        idx_keys(clamp(b0 + 1), 1)
        return carry

    kmax_ref[...] = jnp.full(kmax_ref.shape, INT_MIN, jnp.int32)
    idx_dots(0, 0)
    lax.fori_loop(0, n_pairs, score_body, 0)

    n_acc = 4
    rows16 = 16

    def fold_rows(x, accs):
        accs = list(accs)
        for r in range(kb_sz // rows16):
            accs[r % n_acc] = accs[r % n_acc] + x[r * rows16:(r + 1) * rows16, :]
        return tuple(accs)

    def zero_accs():
        return tuple(jnp.zeros((rows16, tq), jnp.int16) for _ in range(n_acc))

    def total(accs):
        tot = accs[0].astype(jnp.int32)
        for a in accs[1:]:
            tot = tot + a.astype(jnp.int32)
        return jnp.sum(tot, axis=0, keepdims=True)

    def count_ge(plane_ref, t):
        t16 = jnp.broadcast_to(t, (kb_sz, tq)).astype(jnp.int16)

        def body(kb, accs):
            ge = (plane_ref[key_slice(kb), :] >= t16).astype(jnp.int16)
            return fold_rows(ge, accs)

        return total(lax.fori_loop(0, nkb, body, zero_accs()))

    def bisect16(plane_ref, need, lo, hi, steps):
        def body(_, carry):
            lo, hi = carry
            mid = (lo + hi) >> 1
            ok = count_ge(plane_ref, mid) >= need
            return jnp.where(ok, mid, lo), jnp.where(ok, hi, mid)

        return lax.fori_loop(0, steps, body, (lo, hi))[0]

    floor16 = jnp.full((1, tq), -(2 ** 15), jnp.int32)
    top16 = (jnp.max(kmax_ref[...], axis=0, keepdims=True) >> 16) + 1
    lo_try = jnp.maximum(top16 - 2 ** WINDOW_BITS, -(2 ** 15) + 1)
    in_window = count_ge(hi_ref, lo_try) >= k_top
    steps = jnp.where(jnp.min(in_window.astype(jnp.int32)) > 0, WINDOW_BITS, 16)
    t_hi = bisect16(hi_ref, k_top, jnp.where(in_window, lo_try, floor16), top16, steps)

    t_hi16 = jnp.broadcast_to(t_hi, (kb_sz, tq)).astype(jnp.int16)

    def tie_body(kb, accs):
        ks = key_slice(kb)
        hi_part = hi_ref[ks, :]
        lo_ref[ks, :] = jnp.where(hi_part == t_hi16, lo_ref[ks, :], jnp.int16(-(2 ** 15)))
        return fold_rows((hi_part > t_hi16).astype(jnp.int16), accs)

    n_above = total(lax.fori_loop(0, nkb, tie_body, zero_accs()))
    t_lo = bisect16(lo_ref, k_top - n_above, floor16, jnp.full((1, tq), 2 ** 15, jnp.int32), 16)
    thr = jnp.maximum((t_hi << 16) + (t_lo + 2 ** 15), INT_MIN + 1)

    m_ref[...] = jnp.full(m_ref.shape, MASK_BIAS, F32)
    acc_ref[...] = jnp.zeros(acc_ref.shape, F32)

    rc = 32
    n_rc = kb_sz // rc

    def fold8(x):
        return x.reshape(rc // 8, 8, tq)

    def qk_dots(blk, slot):
        ks = key_slice(blk)
        for h in range(N_HEADS):
            n = h // KV_GROUP
            k_n = k_ref[ks, n * HEAD_DIM:(n + 1) * HEAD_DIM]
            q_h = q_ref[:, h * HEAD_DIM:(h + 1) * HEAD_DIM]
            s_ref[slot, h] = lax.dot_general(k_n, q_h, NT_DIMS, preferred_element_type=F32)

    def softmax_block(blk, slot):
        thr_b = jnp.where(blk < nkb, thr, INT_MAX)
        bias_ref[slot] = jnp.where(keys_ref[key_slice(clamp(blk)), :] >= thr_b, 0.0, MASK_BIAS)
        for h in range(N_HEADS):
            mx = jnp.full((8, tq), MASK_BIAS, F32)
            for r in range(n_rc):
                rows = slice(r * rc, (r + 1) * rc)
                sc = s_ref[slot, h, rows, :] + bias_ref[slot, rows, :]
                s_ref[slot, h, rows, :] = sc
                mx = jnp.maximum(mx, jnp.max(fold8(sc), axis=0))
            mx_ref[h:h + 1, :] = jnp.max(mx, axis=0, keepdims=True)
        m_old = m_ref[...]
        m_new = jnp.maximum(m_old, mx_ref[...])
        alpha = jnp.exp2(m_old - m_new)
        m_ref[...] = m_new
        for h in range(N_HEADS):
            m_h = jnp.broadcast_to(m_new[h:h + 1, :], (rc, tq))
            for r in range(n_rc):
                rows = slice(r * rc, (r + 1) * rc)
                p_ref[slot, h, rows, :] = jnp.exp2(s_ref[slot, h, rows, :] - m_h).astype(BF16)
        return alpha

    def pv_update(blk, slot, alpha):
        for h in range(N_HEADS):
            n = h // KV_GROUP
            vt_n = vt_ref[blk, n * VT_ROWS:(n + 1) * VT_ROWS, :]
            pv = jnp.dot(vt_n, p_ref[slot, h], preferred_element_type=F32)
            acc_ref[h] = alpha[h:h + 1, :] * acc_ref[h] + pv

    def attn_body(j, carry):
        b0 = 2 * j
        qk_dots(clamp(b0 + 1), 1)
        pv_update(b0, 0, softmax_block(b0, 0))
        qk_dots(clamp(b0 + 2), 0)
        pv_update(clamp(b0 + 1), 1, softmax_block(b0 + 1, 1))
        return carry

    qk_dots(0, 0)
    lax.fori_loop(0, n_pairs, attn_body, 0)

    for h in range(N_HEADS):
        o_t = acc_ref[h, :HEAD_DIM, :] / acc_ref[h, HEAD_DIM:HEAD_DIM + 1, :]
        o_ref[:, h * HEAD_DIM:(h + 1) * HEAD_DIM] = o_t.T


def _attn_prompt(q, qi, wit, kze, kzo, kb, vt, batch, seq, k_top):
    tq = Q_TILE
    assert seq % tq == 0 and Q_TILE % KEY_TILE == 0 and KEY_TILE == TOKEN_TILE
    nq = seq // tq
    kvw = N_KV_HEADS * HEAD_DIM
    qrow = lambda w: pl.BlockSpec((tq, w), lambda b, i: (b * nq + i, 0))
    per_batch = lambda w: pl.BlockSpec((seq, w), lambda b, i: (b, 0), pipeline_mode=pl.Buffered(1))
    return pl.pallas_call(
        functools.partial(_attn_prompt_kernel, k_top=k_top),
        grid=(batch, nq),
        in_specs=[qrow(N_HEADS * HEAD_DIM), qrow(N_IDX_HEADS * IDX_DIM),
                  pl.BlockSpec((16, tq), lambda b, i: (0, b * nq + i)),
                  per_batch(LANES), per_batch(LANES), per_batch(kvw),
                  pl.BlockSpec((seq // KEY_TILE, N_KV_HEADS * VT_ROWS, KEY_TILE), lambda b, i: (b, 0, 0),
                               pipeline_mode=pl.Buffered(1))],
        out_specs=qrow(N_HEADS * HEAD_DIM),
        out_shape=jax.ShapeDtypeStruct((batch * seq, N_HEADS * HEAD_DIM), F32),
        scratch_shapes=[pltpu.VMEM((seq, tq), jnp.int32),
                        pltpu.VMEM((seq, tq), jnp.int16),
                        pltpu.VMEM((seq, tq), jnp.int16),
                        pltpu.VMEM((8, tq), jnp.int32),
                        pltpu.VMEM((N_HEADS, VT_ROWS, tq), F32),
                        pltpu.VMEM((N_HEADS, tq), F32),
                        pltpu.VMEM((N_HEADS, tq), F32),
                        pltpu.VMEM((2, N_HEADS, KEY_TILE, tq), F32),
                        pltpu.VMEM((2, N_HEADS, KEY_TILE, tq), BF16),
                        pltpu.VMEM((2, KEY_TILE, tq), F32)],
        compiler_params=pltpu.CompilerParams(dimension_semantics=("arbitrary", "arbitrary"),
                                             vmem_limit_bytes=VMEM_LIMIT),
        name="attn_prompt",
    )(q, qi, wit, kze, kzo, kb, vt)


def _attn_sample_kernel(q_ref, qi_ref, wi_ref, ck_ref, cv_ref, cki_ref, kn_ref, vn_ref, kin_ref,
                        o_ref, keyp_ref, keyn_ref, keyt_ref, biasp_ref, *, k_top, past_len, tn):
    group = LANES // tn
    rows = group * tn
    n_cols = past_len // LANES
    kn = kn_ref[...]
    vn = vn_ref[...]
    kin = kin_ref[0].astype(BF16)

    lane = lax.broadcasted_iota(jnp.int32, (tn, LANES), 1)
    pos = lax.broadcasted_iota(jnp.int32, (tn, LANES), 0)
    vis = ((past_len + lane % tn) // CHUNK) <= ((past_len + pos) // CHUNK)
    for j in range(group):
        qj = qi_ref[j].reshape(N_IDX_HEADS * tn, IDX_DIM)
        dp = lax.dot_general(qj, cki_ref[0, j].astype(BF16), NT_DIMS, preferred_element_type=F32)
        dn = lax.dot_general(qj, kin, NT_DIMS, preferred_element_type=F32)
        sp = jnp.zeros((tn, past_len), F32)
        sn = jnp.zeros((tn, LANES), F32)
        for h in range(N_IDX_HEADS):
            w = wi_ref[j * tn:(j + 1) * tn, h:h + 1]
            sp = sp + w * jnp.maximum(dp[h * tn:(h + 1) * tn, :], 0.0)
            sn = sn + w * jnp.maximum(dn[h * tn:(h + 1) * tn, :], 0.0)
        keyp_ref[j * tn:(j + 1) * tn, :] = _order_key(sp)
        own = (lane // tn) == j
        keyn_ref[j * tn:(j + 1) * tn, :] = jnp.where(own & vis, _order_key(sn), INT_MIN)

    for c in range(n_cols):
        keyt_ref[c * LANES:(c + 1) * LANES, :] = keyp_ref[:, c * LANES:(c + 1) * LANES].T
    keyt_ref[past_len:past_len + LANES, :] = keyn_ref[...].T
    n_keys = past_len + LANES

    def bisect_body(_, carry):
        lo, hi = carry
        mid = _midpoint(lo, hi)
        ge = (keyt_ref[...] >= mid).astype(jnp.int32)
        cnt = jnp.sum(jnp.sum(ge.reshape(n_keys // 8, 8, rows), axis=0), axis=0, keepdims=True)
        ok = cnt >= k_top
        return jnp.where(ok, mid, lo), jnp.where(ok, hi, mid)

    thr, _ = lax.fori_loop(
        0, 32, bisect_body,
        (jnp.full((1, rows), INT_MIN + 1, jnp.int32), jnp.full((1, rows), INT_MAX, jnp.int32)))
    thr_col = jnp.broadcast_to(thr, (rows, rows)).T

    bias_n = jnp.where(keyn_ref[...] >= thr_col, 0.0, MASK_BIAS)
    for c in range(n_cols):
        cols = slice(c * LANES, (c + 1) * LANES)
        biasp_ref[:, cols] = jnp.where(keyp_ref[:, cols] >= thr_col, 0.0, MASK_BIAS)

    for j in range(group):
        bias_pj = biasp_ref[j * tn:(j + 1) * tn, :]
        bias_nj = bias_n[j * tn:(j + 1) * tn, :]
        for n in range(N_KV_HEADS):
            hs = slice(n * HEAD_DIM, (n + 1) * HEAD_DIM)
            k_n = ck_ref[0, j, pl.ds(n, past_len, stride=N_KV_HEADS), :].astype(BF16)
            v_n = cv_ref[0, j, pl.ds(n, past_len, stride=N_KV_HEADS), :].astype(BF16)
            qg = jnp.concatenate(
                [q_ref[j * tn:(j + 1) * tn, (n * KV_GROUP + g) * HEAD_DIM:(n * KV_GROUP + g + 1) * HEAD_DIM]
                 for g in range(KV_GROUP)], axis=0)
            s1 = lax.dot_general(qg, k_n, NT_DIMS, preferred_element_type=F32)
            s2 = lax.dot_general(qg, kn[:, hs], NT_DIMS, preferred_element_type=F32)
            s1 = (s1.reshape(KV_GROUP, tn, past_len) + bias_pj[None]).reshape(KV_GROUP * tn, past_len)
            s2 = (s2.reshape(KV_GROUP, tn, LANES) + bias_nj[None]).reshape(KV_GROUP * tn, LANES)
            m = jnp.maximum(jnp.max(s1, axis=1, keepdims=True), jnp.max(s2, axis=1, keepdims=True))
            p1 = jnp.exp2(s1 - m)
            p2 = jnp.exp2(s2 - m)
            l = jnp.sum(p1, axis=1, keepdims=True) + jnp.sum(p2, axis=1, keepdims=True)
            o = (jnp.dot(p1.astype(BF16), v_n, preferred_element_type=F32)
                 + jnp.dot(p2.astype(BF16), vn[:, hs], preferred_element_type=F32)) / l
            for g in range(KV_GROUP):
                h = n * KV_GROUP + g
                o_ref[j * tn:(j + 1) * tn, h * HEAD_DIM:(h + 1) * HEAD_DIM] = o[g * tn:(g + 1) * tn, :]


def _attn_sample(q, qi4, wi, ck, cv, cki, kb, vb, kif, layer, k_top):
    nb = ck.shape[1]
    past_len = cki.shape[2]
    kvw = N_KV_HEADS * HEAD_DIM
    tn = q.shape[0] // nb
    assert LANES % tn == 0 and tn % 16 == 0 and past_len % LANES == 0
    group = LANES // tn
    assert nb % group == 0
    row = lambda w: pl.BlockSpec((LANES, w), lambda i: (i, 0))
    cache = lambda rows, w: pl.BlockSpec((1, group, rows, w), lambda i: (layer, i, 0, 0))
    return pl.pallas_call(
        functools.partial(_attn_sample_kernel, k_top=k_top, past_len=past_len, tn=tn),
        grid=(nb // group,),
        in_specs=[row(N_HEADS * HEAD_DIM),
                  pl.BlockSpec((group, N_IDX_HEADS, tn, IDX_DIM), lambda i: (i, 0, 0, 0)),
                  row(LANES), cache(N_KV_HEADS * past_len, HEAD_DIM),
                  cache(N_KV_HEADS * past_len, HEAD_DIM), cache(past_len, IDX_DIM),
                  row(kvw), row(kvw),
                  pl.BlockSpec((1, LANES, IDX_DIM), lambda i: (layer, i, 0))],
        out_specs=row(N_HEADS * HEAD_DIM),
        out_shape=jax.ShapeDtypeStruct((nb * tn, N_HEADS * HEAD_DIM), F32),
        scratch_shapes=[pltpu.VMEM((LANES, past_len), jnp.int32),
                        pltpu.VMEM((LANES, LANES), jnp.int32),
                        pltpu.VMEM((past_len + LANES, LANES), jnp.int32),
                        pltpu.VMEM((LANES, past_len), F32)],
        compiler_params=pltpu.CompilerParams(dimension_semantics=("arbitrary",),
                                             vmem_limit_bytes=VMEM_LIMIT),
        name="attn_sample",
    )(q, qi4, wi, ck, cv, cki, kb, vb, kif)


def _outffn_kernel(x_ref, a_ref, sga_ref, sgb_ref, u_ref, vn_ref, wmix_ref, bmix_ref, wo_ref,
                   g2_ref, wg_ref, wu_ref, wd_ref, gf_ref, xo_ref, *rest, final):
    if final:
        y_ref, z_ref = rest
    else:
        (z_ref,) = rest
    tm = x_ref.shape[0]
    group_dim = wmix_ref.shape[-1]
    for c in range(tm // SGU_CHUNK):
        rs = slice(c * SGU_CHUNK, (c + 1) * SGU_CHUNK)
        for g in range(SGU_GROUPS):
            cs = slice(g * group_dim, (g + 1) * group_dim)
            mixed = jnp.dot(wmix_ref[0, g], vn_ref[rs, cs].astype(BF16),
                            preferred_element_type=F32) + bmix_ref[0, :, cs]
            z = sga_ref[rs, cs] * a_ref[rs, cs] + sgb_ref[rs, cs] * (u_ref[rs, cs] * mixed)
            z_ref[rs, cs] = z.astype(BF16)
    x1 = x_ref[...] + jnp.dot(z_ref[...], wo_ref[0], preferred_element_type=F32)
    h2 = _rms_norm(x1, g2_ref[0]).astype(BF16)
    gate = jnp.dot(h2, wg_ref[0], preferred_element_type=F32)
    up = jnp.dot(h2, wu_ref[0], preferred_element_type=F32)
    ff = (gate * _sigmoid(gate) * up).astype(BF16)
    x2 = x1 + jnp.dot(ff, wd_ref[0], preferred_element_type=F32)
    xo_ref[...] = x2
    if final:
        y_ref[...] = _rms_norm(x2, gf_ref[...])


def _outffn(x, a, sga, sgb, u, vn, wmix, bmix, layer, wo, g2, wg, wu, wd, gf, final):
    T, D = x.shape
    tm = TOKEN_TILE
    row = pl.BlockSpec((tm, D), lambda i: (i, 0))
    n_out = 2 if final else 1
    out = pl.pallas_call(
        functools.partial(_outffn_kernel, final=final),
        grid=(T // tm,),
        in_specs=[row] * 6 + [_layer_spec(wmix.shape, layer), _layer_spec(bmix.shape, layer),
                              _layer_spec(wo.shape, layer), _layer_spec(g2.shape, layer),
                              _layer_spec(wg.shape, layer), _layer_spec(wu.shape, layer),
                              _layer_spec(wd.shape, layer), _const_spec((1, D))],
        out_specs=(row,) * n_out,
        out_shape=(jax.ShapeDtypeStruct((T, D), F32),) * n_out,
        scratch_shapes=[pltpu.VMEM((tm, D), BF16)],
        compiler_params=pltpu.CompilerParams(dimension_semantics=("arbitrary",),
                                             vmem_limit_bytes=VMEM_LIMIT),
        name="outffn",
    )(x, a, sga, sgb, u, vn, wmix, bmix, wo, g2, wg, wu, wd, gf)
    return out if final else (out[0], None)


def _rope_tables(pos, d):
    inv = ROPE_THETA ** (-jnp.arange(0, d, 2, dtype=F32) / d)
    ang = pos.astype(F32)[:, None] * inv[None, :]
    cos, sin = jnp.cos(ang), jnp.sin(ang)
    c = jnp.concatenate([cos, cos], axis=-1)
    s = jnp.concatenate([-sin, sin], axis=-1)
    reps = LANES // d
    return jnp.tile(c, (1, reps)), jnp.tile(s, (1, reps))


def _mix_weights(sgu_w, sgu_b, n, group_dim):
    depth = sgu_w.shape[0]
    p = jnp.arange(n)
    mask = (p[None, :] // CHUNK) <= (p[:, None] // CHUNK)
    w = jnp.where(mask[None, None], sgu_w[:, :, :n, :n], 0.0)
    reps = SGU_CHUNK // n
    eye = jnp.eye(reps, dtype=w.dtype)
    wbd = jnp.einsum('ab,lgij->lgaibj', eye, w).reshape(depth, SGU_GROUPS, SGU_CHUNK, SGU_CHUNK)
    b = jnp.tile(sgu_b[:, :, :n], (1, 1, reps))
    bfull = jnp.repeat(jnp.swapaxes(b, 1, 2), group_dim, axis=2)
    return wbd.astype(BF16), bfull


def _in_weights(w_in, d_model):
    kvw = N_KV_HEADS * HEAD_DIM
    w = w_in
    c_ki = d_model + 2 * kvw + N_IDX_HEADS * IDX_DIM
    c_wi = c_ki + IDX_DIM
    c_u = c_wi + N_IDX_HEADS
    wki, wwi = w[:, :, c_ki:c_wi], w[:, :, c_wi:c_u]
    wwi_pad = jnp.pad(wwi, ((0, 0), (0, 0), (0, LANES - N_IDX_HEADS)))
    wa = jnp.concatenate([w[:, :, :c_ki], wki, wki, wwi_pad, w[:, :, c_u:]], axis=2)
    wv_t = jnp.swapaxes(w[:, :, d_model + kvw:d_model + 2 * kvw], 1, 2)
    wwi_t = jnp.pad(jnp.swapaxes(wwi, 1, 2), ((0, 0), (0, 16 - N_IDX_HEADS), (0, 0)))
    return wa.astype(BF16), jnp.concatenate([wv_t, wwi_t], axis=1).astype(BF16)


def kernel(x_prompt, x_sample, cache_k, cache_v, cache_kidx, norm1_g, w_in, ln_v_g, ln_v_b, sgu_w, sgu_b, w_out, norm2_g, w_gate, w_up, w_down, final_norm_g):
    B, S, D = x_prompt.shape
    NB, TN, _ = x_sample.shape
    depth, _, P = cache_k.shape[:3]
    assert D == N_HEADS * HEAD_DIM and SGU_CHUNK % TN == 0 and S % SGU_CHUNK == 0
    k_top_p = min(TOPK_MAX, S // 4)
    k_top_s = min(TOPK_MAX, (P + TN) // 4)
    group_dim = D // SGU_GROUPS

    assert S % TOKEN_TILE == 0 and TOKEN_TILE % TN == 0
    pos_p = jnp.arange(S)
    pos_s = jnp.tile(P + jnp.arange(TN), TOKEN_TILE // TN)
    tab_p = _rope_tables(pos_p, HEAD_DIM) + _rope_tables(pos_p, IDX_DIM)
    tab_s = _rope_tables(pos_s, HEAD_DIM) + _rope_tables(pos_s, IDX_DIM)

    wa, wb = _in_weights(w_in, D)
    wo, wg, wu, wd = (t.astype(BF16) for t in (w_out, w_gate, w_up, w_down))
    vec = lambda v: v.reshape(depth, 1, -1)
    g1, g2, lng, lnb = vec(norm1_g), vec(norm2_g), vec(ln_v_g), vec(ln_v_b)
    gf = final_norm_g.reshape(1, -1)
    wmix_p, bmix_p = _mix_weights(sgu_w, sgu_b, SGU_CHUNK, group_dim)
    wmix_s, bmix_s = _mix_weights(sgu_w, sgu_b, TN, group_dim)
    ck = cache_k.reshape(depth, NB, P * N_KV_HEADS, HEAD_DIM)
    cv = cache_v.reshape(depth, NB, P * N_KV_HEADS, HEAD_DIM)

    xp = x_prompt.reshape(B * S, D)
    xs = x_sample.reshape(NB * TN, D)
    def new_buffers(tokens):
        kv = (depth, N_KV_HEADS * tokens, HEAD_DIM)
        return jnp.zeros(kv, F32), jnp.zeros(kv, F32), jnp.zeros((depth, tokens, IDX_DIM), F32)

    new_p, new_s = new_buffers(B * S), new_buffers(NB * TN)
    sgu_v = []
    yp = ys = None
    for l in range(depth):
        final = l == depth - 1

        (q, kf, kb, vf, _, vt, qi, kif, kze, kzo, _, wit, u, vn, sga, sgb) = _inproj(
            xp, l, g1, wa, wb, tab_p, lng, lnb, new_p)
        new_p = (kf, vf, kif)
        a = _attn_prompt(q, qi, wit, kze, kzo, kb, vt, B, S, k_top_p)
        xp, yp = _outffn(xp, a, sga, sgb, u, vn, wmix_p, bmix_p, l, wo, g2, wg, wu, wd, gf, final)

        (q, kf, kb, vf, vb, _, qi, kif, _, _, wi, _, u, vn, sga, sgb) = _inproj(
            xs, l, g1, wa, wb, tab_s, lng, lnb, new_s)
        new_s = (kf, vf, kif)
        qi4 = qi.reshape(NB, TN, N_IDX_HEADS, IDX_DIM).transpose(0, 2, 1, 3)
        a = _attn_sample(q, qi4, wi, ck, cv, cache_kidx, kb, vb, kif, l, k_top_s)
        xs, ys = _outffn(xs, a, sga, sgb, u, vn, wmix_s, bmix_s, l, wo, g2, wg, wu, wd, gf, final)
        sgu_v.append(vn.reshape(NB, TN, D))

    kv_p = (depth, B, S, N_KV_HEADS, HEAD_DIM)
    kv_s = (depth, NB, TN, N_KV_HEADS, HEAD_DIM)
    return (yp.reshape(B, S, D), ys.reshape(NB, TN, D),
            new_p[0].reshape(kv_p), new_p[1].reshape(kv_p), new_p[2].reshape(depth, B, S, IDX_DIM),
            new_s[0].reshape(kv_s), new_s[1].reshape(kv_s), new_s[2].reshape(depth, NB, TN, IDX_DIM),
            jnp.stack(sgu_v))
```

```python
import functools
import math

import jax
import jax.numpy as jnp
from jax import lax
from jax.experimental import pallas as pl
from jax.experimental.pallas import tpu as pltpu

CHUNK = 64
N_HEADS = 8
HEAD_DIM = 128
N_KV_HEADS = 2
KV_GROUP = N_HEADS // N_KV_HEADS
N_IDX_HEADS = 8
IDX_DIM = 64
TOPK_MAX = 256
SGU_CHUNK = 128
SGU_GROUPS = 8
ROPE_THETA = 10000.0
EPS = 1e-6

LANES = 128
TOKEN_TILE = 256
Q_TILE = 256
KEY_TILE = 256
VT_ROWS = HEAD_DIM + 16
FOLD = 8
WINDOW_BITS = 9
VMEM_LIMIT = 56 * 1024 * 1024

INT_MIN = -(2 ** 31)
INT_MAX = 2 ** 31 - 1
MASK_BIAS = -1e30
LOG2E = 1.4426950408889634
Q_SCALE = (HEAD_DIM ** -0.5) * LOG2E

F32 = jnp.float32
BF16 = jnp.bfloat16
NT_DIMS = (((1,), (1,)), ((), ()))


def _const_spec(shape):
    nd = len(shape)
    return pl.BlockSpec(shape, lambda *_: (0,) * nd, pipeline_mode=pl.Buffered(1))


def _sigmoid(x):
    return 1.0 / (1.0 + jnp.exp(-x))


def _rms_norm(x, g):
    return x * lax.rsqrt(jnp.mean(x * x, axis=-1, keepdims=True) + EPS) * g


def _order_key(score):
    bits = pltpu.bitcast(score, jnp.int32)
    return bits ^ ((bits >> 31) & INT_MAX)


def _midpoint(lo, hi):
    return (lo >> 1) + (hi >> 1) + (lo & hi & 1)


_C_Q = 0
_C_K = _C_Q + N_HEADS * HEAD_DIM
_C_V = _C_K + N_KV_HEADS * HEAD_DIM
_C_QI = _C_V + N_KV_HEADS * HEAD_DIM
_C_KI = _C_QI + N_IDX_HEADS * IDX_DIM
_C_WI = _C_KI + LANES
_C_U = _C_WI + LANES


def _inproj_kernel(*refs, d_model, n_alias):
    (x_ref, g_ref, wa_ref, wb_ref, cosh_ref, sinh_ref, cosi_ref, sini_ref,
     lng_ref, lnb_ref) = refs[:10]
    (q_ref, kf_ref, kb_ref, vf_ref, vb_ref, vt_ref, qi_ref, kif_ref, kze_ref, kzo_ref,
     wi_ref, wit_ref, u_ref, vn_ref, sga_ref, sgb_ref) = refs[10 + n_alias:]
    tm = x_ref.shape[0]
    hb = _rms_norm(x_ref[...], g_ref[0]).astype(BF16)

    def proj(c0, width):
        return jnp.dot(hb, wa_ref[0, :, c0:c0 + width], preferred_element_type=F32)

    def head_rows(n):
        return pl.ds(n, tm, stride=N_KV_HEADS)

    cosh, sinh = cosh_ref[...], sinh_ref[...]
    cosi, sini = cosi_ref[...], sini_ref[...]
    lane = lax.broadcasted_iota(jnp.int32, cosi.shape, 1)
    first_half = (lane % IDX_DIM) < (IDX_DIM // 2)

    def rope_head(x):
        return x * cosh + pltpu.roll(x, HEAD_DIM // 2, 1) * sinh

    def rope_idx(x):
        partner = jnp.where(first_half, pltpu.roll(x, LANES - IDX_DIM // 2, 1),
                            pltpu.roll(x, IDX_DIM // 2, 1))
        return x * cosi + partner * sini

    xq = proj(_C_Q, N_HEADS * HEAD_DIM)
    for h in range(N_HEADS):
        sl = slice(h * HEAD_DIM, (h + 1) * HEAD_DIM)
        q_ref[:, sl] = (rope_head(xq[:, sl]) * Q_SCALE).astype(BF16)

    xk = proj(_C_K, N_KV_HEADS * HEAD_DIM)
    for h in range(N_KV_HEADS):
        sl = slice(h * HEAD_DIM, (h + 1) * HEAD_DIM)
        kr = rope_head(xk[:, sl])
        kf_ref[0, head_rows(h), :] = kr
        kb_ref[:, sl] = kr.astype(BF16)

    xv = proj(_C_V, N_KV_HEADS * HEAD_DIM)
    vb_ref[...] = xv.astype(BF16)
    for h in range(N_KV_HEADS):
        vf_ref[0, head_rows(h), :] = xv[:, h * HEAD_DIM:(h + 1) * HEAD_DIM]

    xqi = proj(_C_QI, N_IDX_HEADS * IDX_DIM)
    for p in range(N_IDX_HEADS * IDX_DIM // LANES):
        sl = slice(p * LANES, (p + 1) * LANES)
        qi_ref[:, sl] = (rope_idx(xqi[:, sl]) * (IDX_DIM ** -0.5)).astype(BF16)

    kk = rope_idx(proj(_C_KI, LANES))
    kif_ref[0] = kk[:, :IDX_DIM]
    low = lane < IDX_DIM
    kze_ref[...] = jnp.where(low, kk, 0.0).astype(BF16)
    kzo_ref[...] = jnp.where(low, 0.0, kk).astype(BF16)

    wi_ref[...] = proj(_C_WI, LANES) * (N_IDX_HEADS ** -0.5)

    tb = lax.dot_general(wb_ref[0], hb, NT_DIMS, preferred_element_type=F32)
    kvw = N_KV_HEADS * HEAD_DIM
    ones_rows = (lax.broadcasted_iota(jnp.int32, (VT_ROWS - HEAD_DIM, tm), 0) == 0).astype(BF16)
    for n in range(N_KV_HEADS):
        vt_ref[0, n * VT_ROWS:n * VT_ROWS + HEAD_DIM, :] = (
            tb[n * HEAD_DIM:(n + 1) * HEAD_DIM].astype(BF16))
        vt_ref[0, n * VT_ROWS + HEAD_DIM:(n + 1) * VT_ROWS, :] = ones_rows
    wit_ref[...] = tb[kvw:] * (N_IDX_HEADS ** -0.5)

    u_ref[...] = jax.nn.gelu(proj(_C_U, d_model), approximate=True)
    gv = jax.nn.gelu(proj(_C_U + d_model, d_model), approximate=True)
    mu = jnp.mean(gv, axis=-1, keepdims=True)
    dv = gv - mu
    var = jnp.mean(dv * dv, axis=-1, keepdims=True)
    vn_ref[...] = dv * lax.rsqrt(var + EPS) * lng_ref[0] + lnb_ref[0]
    sga_ref[...] = _sigmoid(proj(_C_U + 2 * d_model, d_model))
    sgb_ref[...] = _sigmoid(proj(_C_U + 3 * d_model, d_model))


def _layer_spec(shape, layer):
    nd = len(shape)
    return pl.BlockSpec((1,) + tuple(shape[1:]), lambda *_: (layer,) + (0,) * (nd - 1),
                        pipeline_mode=pl.Buffered(1))


def _inproj(x, layer, g, wa, wb, tables, lng, lnb, stacked):
    T, D = x.shape
    depth = wa.shape[0]
    tm = TOKEN_TILE
    assert T % tm == 0
    kvw = N_KV_HEADS * HEAD_DIM
    qiw = N_IDX_HEADS * IDX_DIM
    row = lambda w: pl.BlockSpec((tm, w), lambda i: (i, 0))
    kv_rows = N_KV_HEADS * T
    out_shape = (
        jax.ShapeDtypeStruct((T, N_HEADS * HEAD_DIM), BF16),
        jax.ShapeDtypeStruct((depth, kv_rows, HEAD_DIM), F32),
        jax.ShapeDtypeStruct((T, kvw), BF16),
        jax.ShapeDtypeStruct((depth, kv_rows, HEAD_DIM), F32),
        jax.ShapeDtypeStruct((T, kvw), BF16),
        jax.ShapeDtypeStruct((T // tm, N_KV_HEADS * VT_ROWS, tm), BF16),
        jax.ShapeDtypeStruct((T, qiw), BF16),
        jax.ShapeDtypeStruct((depth, T, IDX_DIM), F32),
        jax.ShapeDtypeStruct((T, LANES), BF16),
        jax.ShapeDtypeStruct((T, LANES), BF16),
        jax.ShapeDtypeStruct((T, LANES), F32),
        jax.ShapeDtypeStruct((16, T), F32),
        jax.ShapeDtypeStruct((T, D), F32),
        jax.ShapeDtypeStruct((T, D), F32),
        jax.ShapeDtypeStruct((T, D), F32),
        jax.ShapeDtypeStruct((T, D), F32),
    )
    kv_spec = pl.BlockSpec((1, N_KV_HEADS * tm, HEAD_DIM), lambda i: (layer, i, 0))
    out_specs = (
        row(N_HEADS * HEAD_DIM), kv_spec, row(kvw), kv_spec, row(kvw),
        pl.BlockSpec((1, N_KV_HEADS * VT_ROWS, tm), lambda i: (i, 0, 0)),
        row(qiw), pl.BlockSpec((1, tm, IDX_DIM), lambda i: (layer, i, 0)),
        row(LANES), row(LANES), row(LANES),
        pl.BlockSpec((16, tm), lambda i: (0, i)),
        row(D), row(D), row(D), row(D),
    )
    n_tab = tables[0].shape[0] // tm
    tab = pl.BlockSpec((tm, LANES), lambda i: (i % n_tab, 0))
    in_specs = [row(D), _layer_spec(g.shape, layer), _layer_spec(wa.shape, layer),
                _layer_spec(wb.shape, layer),
                tab, tab, tab, tab,
                _layer_spec(lng.shape, layer), _layer_spec(lnb.shape, layer)]
    args = [x, g, wa, wb, *tables, lng, lnb]
    aliases = {}
    for j, out_idx in enumerate((1, 3, 7)):
        aliases[len(args)] = out_idx
        in_specs.append(pl.BlockSpec(memory_space=pl.ANY))
        args.append(stacked[j])
    return pl.pallas_call(
        functools.partial(_inproj_kernel, d_model=D, n_alias=len(aliases)),
        grid=(T // tm,),
        in_specs=in_specs, out_specs=out_specs, out_shape=out_shape,
        input_output_aliases=aliases,
        compiler_params=pltpu.CompilerParams(dimension_semantics=("arbitrary",),
                                             vmem_limit_bytes=VMEM_LIMIT),
        name="inproj",
    )(*args)


def _attn_prompt_kernel(q_ref, qi_ref, wit_ref, kze_ref, kzo_ref, k_ref, vt_ref, o_ref,
                        keys_ref, hi_ref, lo_ref, f1_ref, f2_ref, kmax_ref, acc_ref, m_ref, mx_ref, s_ref, p_ref,
                        bias_ref, *, k_top):
    tq, kb_sz = Q_TILE, KEY_TILE
    qb = pl.program_id(1)
    nkb = (qb + 1) * (tq // kb_sz)
    q_chunk = (qb * tq + lax.broadcasted_iota(jnp.int32, (1, tq), 1)) // CHUNK
    row_iota = lax.broadcasted_iota(jnp.int32, (kb_sz, 1), 0)

    def key_slice(kb):
        return pl.ds(pl.multiple_of(kb * kb_sz, kb_sz), kb_sz)

    n_blocks = keys_ref.shape[0] // kb_sz
    n_pairs = (nkb + 1) // 2

    def clamp(blk):
        return jnp.minimum(blk, n_blocks - 1)

    def idx_dots(blk, slot):
        ks = key_slice(blk)
        kze, kzo = kze_ref[ks, :], kzo_ref[ks, :]
        for p in range(N_IDX_HEADS // 2):
            qp = qi_ref[:, p * LANES:(p + 1) * LANES]
            s_ref[slot, 2 * p] = lax.dot_general(kze, qp, NT_DIMS, preferred_element_type=F32)
            s_ref[slot, 2 * p + 1] = lax.dot_general(kzo, qp, NT_DIMS, preferred_element_type=F32)

    def idx_keys(blk, slot):
        base = pl.multiple_of(blk * kb_sz, kb_sz)
        kmax = kmax_ref[...]
        for c in range(kb_sz // CHUNK):
            rows = slice(c * CHUNK, (c + 1) * CHUNK)
            score = jnp.zeros((CHUNK, tq), F32)
            for j in range(N_IDX_HEADS):
                score = score + wit_ref[j:j + 1, :] * jnp.maximum(s_ref[slot, j, rows, :], 0.0)
            visible = (blk * (kb_sz // CHUNK) + c) <= q_chunk
            key = jnp.where(visible, _order_key(score), INT_MIN)
            ks = pl.ds(base + c * CHUNK, CHUNK)
            keys_ref[ks, :] = key
            hi_ref[ks, :] = (key >> 16).astype(jnp.int16)
            lo_ref[ks, :] = ((key & 0xFFFF) - 2 ** 15).astype(jnp.int16)
            kmax = jnp.maximum(kmax, jnp.max(key.reshape(CHUNK // 8, 8, tq), axis=0))
        kmax_ref[...] = kmax

    def score_body(j, carry):
        b0 = 2 * j
        idx_dots(clamp(b0 + 1), 1)
        idx_keys(b0, 0)
        idx_dots(clamp(b0 + 2), 0)
        idx_keys(clamp(b0 + 1), 1)
        return carry

    kmax_ref[...] = jnp.full(kmax_ref.shape, INT_MIN, jnp.int32)
    idx_dots(0, 0)
    lax.fori_loop(0, n_pairs, score_body, 0)

    n_acc = 4
    rows16 = 16

    def fold_rows(x, accs):
        accs = list(accs)
        for r in range(kb_sz // rows16):
            accs[r % n_acc] = accs[r % n_acc] + x[r * rows16:(r + 1) * rows16, :]
        return tuple(accs)

    def zero_accs():
        return tuple(jnp.zeros((rows16, tq), jnp.int16) for _ in range(n_acc))

    def total(accs):
        tot = accs[0].astype(jnp.int32)
        for a in accs[1:]:
            tot = tot + a.astype(jnp.int32)
        return jnp.sum(tot, axis=0, keepdims=True)

    def count_ge(plane_ref, t):
        t16 = jnp.broadcast_to(t, (kb_sz, tq)).astype(jnp.int16)

        def body(kb, accs):
            ge = (plane_ref[key_slice(kb), :] >= t16).astype(jnp.int16)
            return fold_rows(ge, accs)

        return total(lax.fori_loop(0, nkb, body, zero_accs()))

    def bisect16(count, need, lo, hi, steps):
        def body(_, carry):
            lo, hi = carry
            mid = (lo + hi) >> 1
            ok = count(mid) >= need
            return jnp.where(ok, mid, lo), jnp.where(ok, hi, mid)

        return lax.fori_loop(0, steps, body, (lo, hi))[0]

    floor16 = jnp.full((1, tq), -(2 ** 15), jnp.int32)
    ceil16 = jnp.full((1, tq), 2 ** 15, jnp.int32)
    count_hi = functools.partial(count_ge, hi_ref)
    count_lo = functools.partial(count_ge, lo_ref)
    top16 = (jnp.max(kmax_ref[...], axis=0, keepdims=True) >> 16) + 1
    lo_try = jnp.maximum(top16 - 2 ** WINDOW_BITS, -(2 ** 15) + 1)
    in_window = count_hi(lo_try) >= k_top
    steps = jnp.where(jnp.min(in_window.astype(jnp.int32)) > 0, WINDOW_BITS, 16)
    t_hi = bisect16(count_hi, k_top, jnp.where(in_window, lo_try, floor16), top16, steps)

    t_hi16 = jnp.broadcast_to(t_hi, (kb_sz, tq)).astype(jnp.int16)
    sentinel = jnp.int16(-(2 ** 15))
    n_slots = kb_sz // rows16 // FOLD

    def max_min(a, b):
        a_ge = a >= b
        return jnp.where(a_ge, a, b), jnp.where(a_ge, b, a)

    def tie_body(kb, accs):
        ks = key_slice(kb)
        hi_part = hi_ref[ks, :]
        e = jnp.where(hi_part == t_hi16, lo_ref[ks, :], sentinel)
        lo_ref[ks, :] = e
        g = [e[r * rows16:(r + 1) * rows16, :] for r in range(kb_sz // rows16)]
        n = len(g) // 2
        pairs = [max_min(g[i], g[i + n]) for i in range(n)]
        first, second = [p[0] for p in pairs], [p[1] for p in pairs]
        while n > n_slots:
            n //= 2
            pairs = [max_min(first[i], first[i + n]) for i in range(n)]
            second = [max_min(pairs[i][1], max_min(second[i], second[i + n])[0])[0] for i in range(n)]
            first = [p[0] for p in pairs]
        fs = pl.ds(pl.multiple_of(kb * (n_slots * rows16), n_slots * rows16), n_slots * rows16)
        f1_ref[fs, :] = jnp.concatenate(first, axis=0)
        f2_ref[fs, :] = jnp.concatenate(second, axis=0)
        return fold_rows((hi_part > t_hi16).astype(jnp.int16), accs)

    n_above = total(lax.fori_loop(0, nkb, tie_body, zero_accs()))
    need_lo = k_top - n_above

    def count_folded(t):
        t16 = jnp.broadcast_to(t, (n_slots * rows16, tq)).astype(jnp.int16)

        def body(kb, accs):
            fs = pl.ds(pl.multiple_of(kb * (n_slots * rows16), n_slots * rows16), n_slots * rows16)
            accs = list(accs)
            for i, plane in enumerate((f1_ref, f2_ref)):
                ge = (plane[fs, :] >= t16).astype(jnp.int16)
                for r in range(n_slots):
                    a = (i * n_slots + r) % n_acc
                    accs[a] = accs[a] + ge[r * rows16:(r + 1) * rows16, :]
            return tuple(accs)

        return total(lax.fori_loop(0, nkb, body, zero_accs()))

    t_lo = bisect16(count_folded, need_lo, floor16, ceil16, 16)
    t_next = jnp.minimum(t_lo + 1, 2 ** 15 - 1)
    too_small = (count_lo(t_next) >= need_lo) & (t_lo < 2 ** 15 - 1)
    t_lo = lax.cond(jnp.max(too_small.astype(jnp.int32)) > 0,
                    lambda: bisect16(count_lo, need_lo, floor16, ceil16, 16),
                    lambda: t_lo)
    thr = jnp.maximum((t_hi << 16) + (t_lo + 2 ** 15), INT_MIN + 1)

    m_ref[...] = jnp.full(m_ref.shape, MASK_BIAS, F32)
    acc_ref[...] = jnp.zeros(acc_ref.shape, F32)

    rc = 32
    n_rc = kb_sz // rc

    def fold8(x):
        return x.reshape(rc // 8, 8, tq)

    def qk_dots(blk, slot):
        ks = key_slice(blk)
        for h in range(N_HEADS):
            n = h // KV_GROUP
            k_n = k_ref[ks, n * HEAD_DIM:(n + 1) * HEAD_DIM]
            q_h = q_ref[:, h * HEAD_DIM:(h + 1) * HEAD_DIM]
            s_ref[slot, h] = lax.dot_general(k_n, q_h, NT_DIMS, preferred_element_type=F32)

    def softmax_block(blk, slot):
        thr_b = jnp.where(blk < nkb, thr, INT_MAX)
        bias_ref[slot] = jnp.where(keys_ref[key_slice(clamp(blk)), :] >= thr_b, 0.0, MASK_BIAS)
        for h in range(N_HEADS):
            mx = jnp.full((8, tq), MASK_BIAS, F32)
            for r in range(n_rc):
                rows = slice(r * rc, (r + 1) * rc)
                sc = s_ref[slot, h, rows, :] + bias_ref[slot, rows, :]
                s_ref[slot, h, rows, :] = sc
                mx = jnp.maximum(mx, jnp.max(fold8(sc), axis=0))
            mx_ref[h:h + 1, :] = jnp.max(mx, axis=0, keepdims=True)
        m_old = m_ref[...]
        m_new = jnp.maximum(m_old, mx_ref[...])
        alpha = jnp.exp2(m_old - m_new)
        m_ref[...] = m_new
        for h in range(N_HEADS):
            m_h = jnp.broadcast_to(m_new[h:h + 1, :], (rc, tq))
            for r in range(n_rc):
                rows = slice(r * rc, (r + 1) * rc)
                p_ref[slot, h, rows, :] = jnp.exp2(s_ref[slot, h, rows, :] - m_h).astype(BF16)
        return alpha

    def pv_update(blk, slot, alpha):
        for h in range(N_HEADS):
            n = h // KV_GROUP
            vt_n = vt_ref[blk, n * VT_ROWS:(n + 1) * VT_ROWS, :]
            pv = jnp.dot(vt_n, p_ref[slot, h], preferred_element_type=F32)
            acc_ref[h] = alpha[h:h + 1, :] * acc_ref[h] + pv

    def attn_body(j, carry):
        b0 = 2 * j
        qk_dots(clamp(b0 + 1), 1)
        pv_update(b0, 0, softmax_block(b0, 0))
        qk_dots(clamp(b0 + 2), 0)
        pv_update(clamp(b0 + 1), 1, softmax_block(b0 + 1, 1))
        return carry

    qk_dots(0, 0)
    lax.fori_loop(0, n_pairs, attn_body, 0)

    for h in range(N_HEADS):
        o_t = acc_ref[h, :HEAD_DIM, :] / acc_ref[h, HEAD_DIM:HEAD_DIM + 1, :]
        o_ref[:, h * HEAD_DIM:(h + 1) * HEAD_DIM] = o_t.T


def _attn_prompt(q, qi, wit, kze, kzo, kb, vt, batch, seq, k_top):
    tq = Q_TILE
    assert seq % tq == 0 and Q_TILE % KEY_TILE == 0 and KEY_TILE == TOKEN_TILE
    nq = seq // tq
    kvw = N_KV_HEADS * HEAD_DIM
    qrow = lambda w: pl.BlockSpec((tq, w), lambda b, i: (b * nq + i, 0))
    per_batch = lambda w: pl.BlockSpec((seq, w), lambda b, i: (b, 0), pipeline_mode=pl.Buffered(1))
    return pl.pallas_call(
        functools.partial(_attn_prompt_kernel, k_top=k_top),
        grid=(batch, nq),
        in_specs=[qrow(N_HEADS * HEAD_DIM), qrow(N_IDX_HEADS * IDX_DIM),
                  pl.BlockSpec((16, tq), lambda b, i: (0, b * nq + i)),
                  per_batch(LANES), per_batch(LANES), per_batch(kvw),
                  pl.BlockSpec((seq // KEY_TILE, N_KV_HEADS * VT_ROWS, KEY_TILE), lambda b, i: (b, 0, 0),
                               pipeline_mode=pl.Buffered(1))],
        out_specs=qrow(N_HEADS * HEAD_DIM),
        out_shape=jax.ShapeDtypeStruct((batch * seq, N_HEADS * HEAD_DIM), F32),
        scratch_shapes=[pltpu.VMEM((seq, tq), jnp.int32),
                        pltpu.VMEM((seq, tq), jnp.int16),
                        pltpu.VMEM((seq, tq), jnp.int16),
                        pltpu.VMEM((seq // FOLD, tq), jnp.int16),
                        pltpu.VMEM((seq // FOLD, tq), jnp.int16),
                        pltpu.VMEM((8, tq), jnp.int32),
                        pltpu.VMEM((N_HEADS, VT_ROWS, tq), F32),
                        pltpu.VMEM((N_HEADS, tq), F32),
                        pltpu.VMEM((N_HEADS, tq), F32),
                        pltpu.VMEM((2, N_HEADS, KEY_TILE, tq), F32),
                        pltpu.VMEM((2, N_HEADS, KEY_TILE, tq), BF16),
                        pltpu.VMEM((2, KEY_TILE, tq), F32)],
        compiler_params=pltpu.CompilerParams(dimension_semantics=("arbitrary", "arbitrary"),
                                             vmem_limit_bytes=VMEM_LIMIT),
        name="attn_prompt",
    )(q, qi, wit, kze, kzo, kb, vt)


def _attn_sample_kernel(q_ref, qi_ref, wi_ref, ck_ref, cv_ref, cki_ref, kn_ref, vn_ref, kin_ref,
                        o_ref, keyp_ref, keyn_ref, keyt_ref, biasp_ref, *, k_top, past_len, tn):
    group = LANES // tn
    rows = group * tn
    n_cols = past_len // LANES
    kn = kn_ref[...]
    vn = vn_ref[...]
    kin = kin_ref[0].astype(BF16)

    lane = lax.broadcasted_iota(jnp.int32, (tn, LANES), 1)
    pos = lax.broadcasted_iota(jnp.int32, (tn, LANES), 0)
    vis = ((past_len + lane % tn) // CHUNK) <= ((past_len + pos) // CHUNK)
    for j in range(group):
        qj = qi_ref[j].reshape(N_IDX_HEADS * tn, IDX_DIM)
        dp = lax.dot_general(qj, cki_ref[0, j].astype(BF16), NT_DIMS, preferred_element_type=F32)
        dn = lax.dot_general(qj, kin, NT_DIMS, preferred_element_type=F32)
        sp = jnp.zeros((tn, past_len), F32)
        sn = jnp.zeros((tn, LANES), F32)
        for h in range(N_IDX_HEADS):
            w = wi_ref[j * tn:(j + 1) * tn, h:h + 1]
            sp = sp + w * jnp.maximum(dp[h * tn:(h + 1) * tn, :], 0.0)
            sn = sn + w * jnp.maximum(dn[h * tn:(h + 1) * tn, :], 0.0)
        keyp_ref[j * tn:(j + 1) * tn, :] = _order_key(sp)
        own = (lane // tn) == j
        keyn_ref[j * tn:(j + 1) * tn, :] = jnp.where(own & vis, _order_key(sn), INT_MIN)

    for c in range(n_cols):
        keyt_ref[c * LANES:(c + 1) * LANES, :] = keyp_ref[:, c * LANES:(c + 1) * LANES].T
    keyt_ref[past_len:past_len + LANES, :] = keyn_ref[...].T
    n_keys = past_len + LANES

    def bisect_body(_, carry):
        lo, hi = carry
        mid = _midpoint(lo, hi)
        ge = (keyt_ref[...] >= mid).astype(jnp.int32)
        cnt = jnp.sum(jnp.sum(ge.reshape(n_keys // 8, 8, rows), axis=0), axis=0, keepdims=True)
        ok = cnt >= k_top
        return jnp.where(ok, mid, lo), jnp.where(ok, hi, mid)

    thr, _ = lax.fori_loop(
        0, 32, bisect_body,
        (jnp.full((1, rows), INT_MIN + 1, jnp.int32), jnp.full((1, rows), INT_MAX, jnp.int32)))
    thr_col = jnp.broadcast_to(thr, (rows, rows)).T

    bias_n = jnp.where(keyn_ref[...] >= thr_col, 0.0, MASK_BIAS)
    for c in range(n_cols):
        cols = slice(c * LANES, (c + 1) * LANES)
        biasp_ref[:, cols] = jnp.where(keyp_ref[:, cols] >= thr_col, 0.0, MASK_BIAS)

    for j in range(group):
        bias_pj = biasp_ref[j * tn:(j + 1) * tn, :]
        bias_nj = bias_n[j * tn:(j + 1) * tn, :]
        for n in range(N_KV_HEADS):
            hs = slice(n * HEAD_DIM, (n + 1) * HEAD_DIM)
            k_n = ck_ref[0, j, pl.ds(n, past_len, stride=N_KV_HEADS), :].astype(BF16)
            v_n = cv_ref[0, j, pl.ds(n, past_len, stride=N_KV_HEADS), :].astype(BF16)
            qg = jnp.concatenate(
                [q_ref[j * tn:(j + 1) * tn, (n * KV_GROUP + g) * HEAD_DIM:(n * KV_GROUP + g + 1) * HEAD_DIM]
                 for g in range(KV_GROUP)], axis=0)
            s1 = lax.dot_general(qg, k_n, NT_DIMS, preferred_element_type=F32)
            s2 = lax.dot_general(qg, kn[:, hs], NT_DIMS, preferred_element_type=F32)
            s1 = (s1.reshape(KV_GROUP, tn, past_len) + bias_pj[None]).reshape(KV_GROUP * tn, past_len)
            s2 = (s2.reshape(KV_GROUP, tn, LANES) + bias_nj[None]).reshape(KV_GROUP * tn, LANES)
            m = jnp.maximum(jnp.max(s1, axis=1, keepdims=True), jnp.max(s2, axis=1, keepdims=True))
            p1 = jnp.exp2(s1 - m)
            p2 = jnp.exp2(s2 - m)
            l = jnp.sum(p1, axis=1, keepdims=True) + jnp.sum(p2, axis=1, keepdims=True)
            o = (jnp.dot(p1.astype(BF16), v_n, preferred_element_type=F32)
                 + jnp.dot(p2.astype(BF16), vn[:, hs], preferred_element_type=F32)) / l
            for g in range(KV_GROUP):
                h = n * KV_GROUP + g
                o_ref[j * tn:(j + 1) * tn, h * HEAD_DIM:(h + 1) * HEAD_DIM] = o[g * tn:(g + 1) * tn, :]


def _attn_sample(q, qi4, wi, ck, cv, cki, kb, vb, kif, layer, k_top):
    nb = ck.shape[1]
    past_len = cki.shape[2]
    kvw = N_KV_HEADS * HEAD_DIM
    tn = q.shape[0] // nb
    assert LANES % tn == 0 and tn % 16 == 0 and past_len % LANES == 0
    group = LANES // tn
    assert nb % group == 0
    row = lambda w: pl.BlockSpec((LANES, w), lambda i: (i, 0))
    cache = lambda rows, w: pl.BlockSpec((1, group, rows, w), lambda i: (layer, i, 0, 0))
    return pl.pallas_call(
        functools.partial(_attn_sample_kernel, k_top=k_top, past_len=past_len, tn=tn),
        grid=(nb // group,),
        in_specs=[row(N_HEADS * HEAD_DIM),
                  pl.BlockSpec((group, N_IDX_HEADS, tn, IDX_DIM), lambda i: (i, 0, 0, 0)),
                  row(LANES), cache(N_KV_HEADS * past_len, HEAD_DIM),
                  cache(N_KV_HEADS * past_len, HEAD_DIM), cache(past_len, IDX_DIM),
                  row(kvw), row(kvw),
                  pl.BlockSpec((1, LANES, IDX_DIM), lambda i: (layer, i, 0))],
        out_specs=row(N_HEADS * HEAD_DIM),
        out_shape=jax.ShapeDtypeStruct((nb * tn, N_HEADS * HEAD_DIM), F32),
        scratch_shapes=[pltpu.VMEM((LANES, past_len), jnp.int32),
                        pltpu.VMEM((LANES, LANES), jnp.int32),
                        pltpu.VMEM((past_len + LANES, LANES), jnp.int32),
                        pltpu.VMEM((LANES, past_len), F32)],
        compiler_params=pltpu.CompilerParams(dimension_semantics=("arbitrary",),
                                             vmem_limit_bytes=VMEM_LIMIT),
        name="attn_sample",
    )(q, qi4, wi, ck, cv, cki, kb, vb, kif)


def _outffn_kernel(x_ref, a_ref, sga_ref, sgb_ref, u_ref, vn_ref, wmix_ref, bmix_ref, wo_ref,
                   g2_ref, wg_ref, wu_ref, wd_ref, gf_ref, xo_ref, *rest, final):
    if final:
        y_ref, z_ref = rest
    else:
        (z_ref,) = rest
    tm = x_ref.shape[0]
    group_dim = wmix_ref.shape[-1]
    for c in range(tm // SGU_CHUNK):
        rs = slice(c * SGU_CHUNK, (c + 1) * SGU_CHUNK)
        for g in range(SGU_GROUPS):
            cs = slice(g * group_dim, (g + 1) * group_dim)
            mixed = jnp.dot(wmix_ref[0, g], vn_ref[rs, cs].astype(BF16),
                            preferred_element_type=F32) + bmix_ref[0, :, cs]
            z = sga_ref[rs, cs] * a_ref[rs, cs] + sgb_ref[rs, cs] * (u_ref[rs, cs] * mixed)
            z_ref[rs, cs] = z.astype(BF16)
    x1 = x_ref[...] + jnp.dot(z_ref[...], wo_ref[0], preferred_element_type=F32)
    h2 = _rms_norm(x1, g2_ref[0]).astype(BF16)
    gate = jnp.dot(h2, wg_ref[0], preferred_element_type=F32)
    up = jnp.dot(h2, wu_ref[0], preferred_element_type=F32)
    ff = (gate * _sigmoid(gate) * up).astype(BF16)
    x2 = x1 + jnp.dot(ff, wd_ref[0], preferred_element_type=F32)
    xo_ref[...] = x2
    if final:
        y_ref[...] = _rms_norm(x2, gf_ref[...])


def _outffn(x, a, sga, sgb, u, vn, wmix, bmix, layer, wo, g2, wg, wu, wd, gf, final):
    T, D = x.shape
    tm = TOKEN_TILE
    row = pl.BlockSpec((tm, D), lambda i: (i, 0))
    n_out = 2 if final else 1
    out = pl.pallas_call(
        functools.partial(_outffn_kernel, final=final),
        grid=(T // tm,),
        in_specs=[row] * 6 + [_layer_spec(wmix.shape, layer), _layer_spec(bmix.shape, layer),
                              _layer_spec(wo.shape, layer), _layer_spec(g2.shape, layer),
                              _layer_spec(wg.shape, layer), _layer_spec(wu.shape, layer),
                              _layer_spec(wd.shape, layer), _const_spec((1, D))],
        out_specs=(row,) * n_out,
        out_shape=(jax.ShapeDtypeStruct((T, D), F32),) * n_out,
        scratch_shapes=[pltpu.VMEM((tm, D), BF16)],
        compiler_params=pltpu.CompilerParams(dimension_semantics=("arbitrary",),
                                             vmem_limit_bytes=VMEM_LIMIT),
        name="outffn",
    )(x, a, sga, sgb, u, vn, wmix, bmix, wo, g2, wg, wu, wd, gf)
    return out if final else (out[0], None)


def _rope_tables(pos, d):
    inv = ROPE_THETA ** (-jnp.arange(0, d, 2, dtype=F32) / d)
    ang = pos.astype(F32)[:, None] * inv[None, :]
    cos, sin = jnp.cos(ang), jnp.sin(ang)
    c = jnp.concatenate([cos, cos], axis=-1)
    s = jnp.concatenate([-sin, sin], axis=-1)
    reps = LANES // d
    return jnp.tile(c, (1, reps)), jnp.tile(s, (1, reps))


def _mix_weights(sgu_w, sgu_b, n, group_dim):
    depth = sgu_w.shape[0]
    p = jnp.arange(n)
    mask = (p[None, :] // CHUNK) <= (p[:, None] // CHUNK)
    w = jnp.where(mask[None, None], sgu_w[:, :, :n, :n], 0.0)
    reps = SGU_CHUNK // n
    eye = jnp.eye(reps, dtype=w.dtype)
    wbd = jnp.einsum('ab,lgij->lgaibj', eye, w).reshape(depth, SGU_GROUPS, SGU_CHUNK, SGU_CHUNK)
    b = jnp.tile(sgu_b[:, :, :n], (1, 1, reps))
    bfull = jnp.repeat(jnp.swapaxes(b, 1, 2), group_dim, axis=2)
    return wbd.astype(BF16), bfull


def _in_weights(w_in, d_model):
    kvw = N_KV_HEADS * HEAD_DIM
    w = w_in
    c_ki = d_model + 2 * kvw + N_IDX_HEADS * IDX_DIM
    c_wi = c_ki + IDX_DIM
    c_u = c_wi + N_IDX_HEADS
    wki, wwi = w[:, :, c_ki:c_wi], w[:, :, c_wi:c_u]
    wwi_pad = jnp.pad(wwi, ((0, 0), (0, 0), (0, LANES - N_IDX_HEADS)))
    wa = jnp.concatenate([w[:, :, :c_ki], wki, wki, wwi_pad, w[:, :, c_u:]], axis=2)
    wv_t = jnp.swapaxes(w[:, :, d_model + kvw:d_model + 2 * kvw], 1, 2)
    wwi_t = jnp.pad(jnp.swapaxes(wwi, 1, 2), ((0, 0), (0, 16 - N_IDX_HEADS), (0, 0)))
    return wa.astype(BF16), jnp.concatenate([wv_t, wwi_t], axis=1).astype(BF16)


def kernel(x_prompt, x_sample, cache_k, cache_v, cache_kidx, norm1_g, w_in, ln_v_g, ln_v_b, sgu_w, sgu_b, w_out, norm2_g, w_gate, w_up, w_down, final_norm_g):
    B, S, D = x_prompt.shape
    NB, TN, _ = x_sample.shape
    depth, _, P = cache_k.shape[:3]
    assert D == N_HEADS * HEAD_DIM and SGU_CHUNK % TN == 0 and S % SGU_CHUNK == 0
    k_top_p = min(TOPK_MAX, S // 4)
    k_top_s = min(TOPK_MAX, (P + TN) // 4)
    group_dim = D // SGU_GROUPS

    assert S % TOKEN_TILE == 0 and TOKEN_TILE % TN == 0
    pos_p = jnp.arange(S)
    pos_s = jnp.tile(P + jnp.arange(TN), TOKEN_TILE // TN)
    tab_p = _rope_tables(pos_p, HEAD_DIM) + _rope_tables(pos_p, IDX_DIM)
    tab_s = _rope_tables(pos_s, HEAD_DIM) + _rope_tables(pos_s, IDX_DIM)

    wa, wb = _in_weights(w_in, D)
    wo, wg, wu, wd = (t.astype(BF16) for t in (w_out, w_gate, w_up, w_down))
    vec = lambda v: v.reshape(depth, 1, -1)
    g1, g2, lng, lnb = vec(norm1_g), vec(norm2_g), vec(ln_v_g), vec(ln_v_b)
    gf = final_norm_g.reshape(1, -1)
    wmix_p, bmix_p = _mix_weights(sgu_w, sgu_b, SGU_CHUNK, group_dim)
    wmix_s, bmix_s = _mix_weights(sgu_w, sgu_b, TN, group_dim)
    ck = cache_k.reshape(depth, NB, P * N_KV_HEADS, HEAD_DIM)
    cv = cache_v.reshape(depth, NB, P * N_KV_HEADS, HEAD_DIM)

    xp = x_prompt.reshape(B * S, D)
    xs = x_sample.reshape(NB * TN, D)
    def new_buffers(tokens):
        kv = (depth, N_KV_HEADS * tokens, HEAD_DIM)
        return jnp.zeros(kv, F32), jnp.zeros(kv, F32), jnp.zeros((depth, tokens, IDX_DIM), F32)

    new_p, new_s = new_buffers(B * S), new_buffers(NB * TN)
    sgu_v = []
    yp = ys = None
    for l in range(depth):
        final = l == depth - 1

        (q, kf, kb, vf, _, vt, qi, kif, kze, kzo, _, wit, u, vn, sga, sgb) = _inproj(
            xp, l, g1, wa, wb, tab_p, lng, lnb, new_p)
        new_p = (kf, vf, kif)
        a = _attn_prompt(q, qi, wit, kze, kzo, kb, vt, B, S, k_top_p)
        xp, yp = _outffn(xp, a, sga, sgb, u, vn, wmix_p, bmix_p, l, wo, g2, wg, wu, wd, gf, final)

        (q, kf, kb, vf, vb, _, qi, kif, _, _, wi, _, u, vn, sga, sgb) = _inproj(
            xs, l, g1, wa, wb, tab_s, lng, lnb, new_s)
        new_s = (kf, vf, kif)
        qi4 = qi.reshape(NB, TN, N_IDX_HEADS, IDX_DIM).transpose(0, 2, 1, 3)
        a = _attn_sample(q, qi4, wi, ck, cv, cache_kidx, kb, vb, kif, l, k_top_s)
        xs, ys = _outffn(xs, a, sga, sgb, u, vn, wmix_s, bmix_s, l, wo, g2, wg, wu, wd, gf, final)
        sgu_v.append(vn.reshape(NB, TN, D))

    kv_p = (depth, B, S, N_KV_HEADS, HEAD_DIM)
    kv_s = (depth, NB, TN, N_KV_HEADS, HEAD_DIM)
    return (yp.reshape(B, S, D), ys.reshape(NB, TN, D),
            new_p[0].reshape(kv_p), new_p[1].reshape(kv_p), new_p[2].reshape(depth, B, S, IDX_DIM),
            new_s[0].reshape(kv_s), new_s[1].reshape(kv_s), new_s[2].reshape(depth, NB, TN, IDX_DIM),
            jnp.stack(sgu_v))
```

```python
import functools
import math

import jax
import jax.numpy as jnp
from jax import lax
from jax.experimental import pallas as pl
from jax.experimental.pallas import tpu as pltpu

CHUNK = 64
N_HEADS = 8
HEAD_DIM = 128
N_KV_HEADS = 2
KV_GROUP = N_HEADS // N_KV_HEADS
N_IDX_HEADS = 8
IDX_DIM = 64
TOPK_MAX = 256
SGU_CHUNK = 128
SGU_GROUPS = 8
ROPE_THETA = 10000.0
EPS = 1e-6

LANES = 128
TOKEN_TILE = 256
Q_TILE = 256
KEY_TILE = 256
VT_ROWS = HEAD_DIM + 16
FOLD = 8
WINDOW_BITS = 9
VMEM_LIMIT = 56 * 1024 * 1024

INT_MIN = -(2 ** 31)
INT_MAX = 2 ** 31 - 1
MASK_BIAS = -1e30
LOG2E = 1.4426950408889634
Q_SCALE = (HEAD_DIM ** -0.5) * LOG2E

F32 = jnp.float32
BF16 = jnp.bfloat16
NT_DIMS = (((1,), (1,)), ((), ()))


def _const_spec(shape):
    nd = len(shape)
    return pl.BlockSpec(shape, lambda *_: (0,) * nd, pipeline_mode=pl.Buffered(1))


def _sigmoid(x):
    return 1.0 / (1.0 + jnp.exp(-x))


def _rms_norm(x, g):
    return x * lax.rsqrt(jnp.mean(x * x, axis=-1, keepdims=True) + EPS) * g


def _order_key(score):
    bits = pltpu.bitcast(score, jnp.int32)
    return bits ^ ((bits >> 31) & INT_MAX)


def _midpoint(lo, hi):
    return (lo >> 1) + (hi >> 1) + (lo & hi & 1)


_C_Q = 0
_C_K = _C_Q + N_HEADS * HEAD_DIM
_C_V = _C_K + N_KV_HEADS * HEAD_DIM
_C_QI = _C_V + N_KV_HEADS * HEAD_DIM
_C_KI = _C_QI + N_IDX_HEADS * IDX_DIM
_C_WI = _C_KI + LANES
_C_U = _C_WI + LANES


def _inproj_kernel(*refs, d_model, n_alias):
    (x_ref, g_ref, wa_ref, wb_ref, cosh_ref, sinh_ref, cosi_ref, sini_ref,
     lng_ref, lnb_ref) = refs[:10]
    (q_ref, kf_ref, kb_ref, vf_ref, vb_ref, vt_ref, qi_ref, kif_ref, kze_ref, kzo_ref,
     wi_ref, wit_ref, u_ref, vn_ref, sga_ref, sgb_ref) = refs[10 + n_alias:]
    tm = x_ref.shape[0]
    hb = _rms_norm(x_ref[...], g_ref[0]).astype(BF16)

    def proj(c0, width):
        return jnp.dot(hb, wa_ref[0, :, c0:c0 + width], preferred_element_type=F32)

    def head_rows(n):
        return pl.ds(n, tm, stride=N_KV_HEADS)

    cosh, sinh = cosh_ref[...], sinh_ref[...]
    cosi, sini = cosi_ref[...], sini_ref[...]
    lane = lax.broadcasted_iota(jnp.int32, cosi.shape, 1)
    first_half = (lane % IDX_DIM) < (IDX_DIM // 2)

    def rope_head(x):
        return x * cosh + pltpu.roll(x, HEAD_DIM // 2, 1) * sinh

    def rope_idx(x):
        partner = jnp.where(first_half, pltpu.roll(x, LANES - IDX_DIM // 2, 1),
                            pltpu.roll(x, IDX_DIM // 2, 1))
        return x * cosi + partner * sini

    xq = proj(_C_Q, N_HEADS * HEAD_DIM)
    for h in range(N_HEADS):
        sl = slice(h * HEAD_DIM, (h + 1) * HEAD_DIM)
        q_ref[:, sl] = (rope_head(xq[:, sl]) * Q_SCALE).astype(BF16)

    xk = proj(_C_K, N_KV_HEADS * HEAD_DIM)
    for h in range(N_KV_HEADS):
        sl = slice(h * HEAD_DIM, (h + 1) * HEAD_DIM)
        kr = rope_head(xk[:, sl])
        kf_ref[0, head_rows(h), :] = kr
        kb_ref[:, sl] = kr.astype(BF16)

    xv = proj(_C_V, N_KV_HEADS * HEAD_DIM)
    vb_ref[...] = xv.astype(BF16)
    for h in range(N_KV_HEADS):
        vf_ref[0, head_rows(h), :] = xv[:, h * HEAD_DIM:(h + 1) * HEAD_DIM]

    xqi = proj(_C_QI, N_IDX_HEADS * IDX_DIM)
    for p in range(N_IDX_HEADS * IDX_DIM // LANES):
        sl = slice(p * LANES, (p + 1) * LANES)
        qi_ref[:, sl] = (rope_idx(xqi[:, sl]) * (IDX_DIM ** -0.5)).astype(BF16)

    kk = rope_idx(proj(_C_KI, LANES))
    kif_ref[0] = kk[:, :IDX_DIM]
    low = lane < IDX_DIM
    kze_ref[...] = jnp.where(low, kk, 0.0).astype(BF16)
    kzo_ref[...] = jnp.where(low, 0.0, kk).astype(BF16)

    wi_ref[...] = proj(_C_WI, LANES) * (N_IDX_HEADS ** -0.5)

    tb = lax.dot_general(wb_ref[0], hb, NT_DIMS, preferred_element_type=F32)
    kvw = N_KV_HEADS * HEAD_DIM
    ones_rows = (lax.broadcasted_iota(jnp.int32, (VT_ROWS - HEAD_DIM, tm), 0) == 0).astype(BF16)
    for n in range(N_KV_HEADS):
        vt_ref[0, n * VT_ROWS:n * VT_ROWS + HEAD_DIM, :] = (
            tb[n * HEAD_DIM:(n + 1) * HEAD_DIM].astype(BF16))
        vt_ref[0, n * VT_ROWS + HEAD_DIM:(n + 1) * VT_ROWS, :] = ones_rows
    wit_ref[...] = tb[kvw:] * (N_IDX_HEADS ** -0.5)

    u_ref[...] = jax.nn.gelu(proj(_C_U, d_model), approximate=True)
    gv = jax.nn.gelu(proj(_C_U + d_model, d_model), approximate=True)
    mu = jnp.mean(gv, axis=-1, keepdims=True)
    dv = gv - mu
    var = jnp.mean(dv * dv, axis=-1, keepdims=True)
    vn_ref[...] = dv * lax.rsqrt(var + EPS) * lng_ref[0] + lnb_ref[0]
    sga_ref[...] = _sigmoid(proj(_C_U + 2 * d_model, d_model))
    sgb_ref[...] = _sigmoid(proj(_C_U + 3 * d_model, d_model))


def _layer_spec(shape, layer):
    nd = len(shape)
    return pl.BlockSpec((1,) + tuple(shape[1:]), lambda *_: (layer,) + (0,) * (nd - 1),
                        pipeline_mode=pl.Buffered(1))


def _inproj(x, layer, g, wa, wb, tables, lng, lnb, stacked):
    T, D = x.shape
    depth = wa.shape[0]
    tm = TOKEN_TILE
    assert T % tm == 0
    kvw = N_KV_HEADS * HEAD_DIM
    qiw = N_IDX_HEADS * IDX_DIM
    row = lambda w: pl.BlockSpec((tm, w), lambda i: (i, 0))
    kv_rows = N_KV_HEADS * T
    out_shape = (
        jax.ShapeDtypeStruct((T, N_HEADS * HEAD_DIM), BF16),
        jax.ShapeDtypeStruct((depth, kv_rows, HEAD_DIM), F32),
        jax.ShapeDtypeStruct((T, kvw), BF16),
        jax.ShapeDtypeStruct((depth, kv_rows, HEAD_DIM), F32),
        jax.ShapeDtypeStruct((T, kvw), BF16),
        jax.ShapeDtypeStruct((T // tm, N_KV_HEADS * VT_ROWS, tm), BF16),
        jax.ShapeDtypeStruct((T, qiw), BF16),
        jax.ShapeDtypeStruct((depth, T, IDX_DIM), F32),
        jax.ShapeDtypeStruct((T, LANES), BF16),
        jax.ShapeDtypeStruct((T, LANES), BF16),
        jax.ShapeDtypeStruct((T, LANES), F32),
        jax.ShapeDtypeStruct((16, T), F32),
        jax.ShapeDtypeStruct((T, D), F32),
        jax.ShapeDtypeStruct((T, D), F32),
        jax.ShapeDtypeStruct((T, D), F32),
        jax.ShapeDtypeStruct((T, D), F32),
    )
    kv_spec = pl.BlockSpec((1, N_KV_HEADS * tm, HEAD_DIM), lambda i: (layer, i, 0))
    out_specs = (
        row(N_HEADS * HEAD_DIM), kv_spec, row(kvw), kv_spec, row(kvw),
        pl.BlockSpec((1, N_KV_HEADS * VT_ROWS, tm), lambda i: (i, 0, 0)),
        row(qiw), pl.BlockSpec((1, tm, IDX_DIM), lambda i: (layer, i, 0)),
        row(LANES), row(LANES), row(LANES),
        pl.BlockSpec((16, tm), lambda i: (0, i)),
        row(D), row(D), row(D), row(D),
    )
    n_tab = tables[0].shape[0] // tm
    tab = pl.BlockSpec((tm, LANES), lambda i: (i % n_tab, 0))
    in_specs = [row(D), _layer_spec(g.shape, layer), _layer_spec(wa.shape, layer),
                _layer_spec(wb.shape, layer),
                tab, tab, tab, tab,
                _layer_spec(lng.shape, layer), _layer_spec(lnb.shape, layer)]
    args = [x, g, wa, wb, *tables, lng, lnb]
    aliases = {}
    for j, out_idx in enumerate((1, 3, 7)):
        aliases[len(args)] = out_idx
        in_specs.append(pl.BlockSpec(memory_space=pl.ANY))
        args.append(stacked[j])
    return pl.pallas_call(
        functools.partial(_inproj_kernel, d_model=D, n_alias=len(aliases)),
        grid=(T // tm,),
        in_specs=in_specs, out_specs=out_specs, out_shape=out_shape,
        input_output_aliases=aliases,
        compiler_params=pltpu.CompilerParams(dimension_semantics=("arbitrary",),
                                             vmem_limit_bytes=VMEM_LIMIT),
        name="inproj",
    )(*args)


def _attn_prompt_kernel(q_ref, qi_ref, wit_ref, kze_ref, kzo_ref, k_ref, vt_ref, o_ref,
                        keys_ref, hi_ref, lo_ref, f1_ref, f2_ref, kmax_ref, acc_ref, m_ref, mx_ref, s_ref, p_ref,
                        bias_ref, *, k_top):
    tq, kb_sz = Q_TILE, KEY_TILE
    qb = pl.program_id(1)
    nkb = (qb + 1) * (tq // kb_sz)
    q_chunk = (qb * tq + lax.broadcasted_iota(jnp.int32, (1, tq), 1)) // CHUNK
    row_iota = lax.broadcasted_iota(jnp.int32, (kb_sz, 1), 0)

    def key_slice(kb):
        return pl.ds(pl.multiple_of(kb * kb_sz, kb_sz), kb_sz)

    n_blocks = keys_ref.shape[0] // kb_sz
    n_pairs = (nkb + 1) // 2

    def clamp(blk):
        return jnp.minimum(blk, n_blocks - 1)

    def idx_dots(blk, slot):
        ks = key_slice(blk)
        kze, kzo = kze_ref[ks, :], kzo_ref[ks, :]
        for p in range(N_IDX_HEADS // 2):
            qp = qi_ref[:, p * LANES:(p + 1) * LANES]
            s_ref[slot, 2 * p] = lax.dot_general(kze, qp, NT_DIMS, preferred_element_type=F32)
            s_ref[slot, 2 * p + 1] = lax.dot_general(kzo, qp, NT_DIMS, preferred_element_type=F32)

    def idx_keys(blk, slot):
        base = pl.multiple_of(blk * kb_sz, kb_sz)
        kmax = kmax_ref[...]
        for c in range(kb_sz // CHUNK):
            rows = slice(c * CHUNK, (c + 1) * CHUNK)
            score = jnp.zeros((CHUNK, tq), F32)
            for j in range(N_IDX_HEADS):
                score = score + wit_ref[j:j + 1, :] * jnp.maximum(s_ref[slot, j, rows, :], 0.0)
            visible = (blk * (kb_sz // CHUNK) + c) <= q_chunk
            key = jnp.where(visible, _order_key(score), INT_MIN)
            ks = pl.ds(base + c * CHUNK, CHUNK)
            keys_ref[ks, :] = key
            hi_ref[ks, :] = (key >> 16).astype(jnp.int16)
            lo_ref[ks, :] = ((key & 0xFFFF) - 2 ** 15).astype(jnp.int16)
            kmax = jnp.maximum(kmax, jnp.max(key.reshape(CHUNK // 8, 8, tq), axis=0))
        kmax_ref[...] = kmax

    def score_body(j, carry):
        b0 = 2 * j
        idx_dots(clamp(b0 + 1), 1)
        idx_keys(b0, 0)
        idx_dots(clamp(b0 + 2), 0)
        idx_keys(clamp(b0 + 1), 1)
        return carry

    kmax_ref[...] = jnp.full(kmax_ref.shape, INT_MIN, jnp.int32)
    idx_dots(0, 0)
    lax.fori_loop(0, n_pairs, score_body, 0)

    n_acc = 4
    rows16 = 16

    def fold_rows(x, accs):
        accs = list(accs)
        for r in range(kb_sz // rows16):
            accs[r % n_acc] = accs[r % n_acc] + x[r * rows16:(r + 1) * rows16, :]
        return tuple(accs)

    def zero_accs():
        return tuple(jnp.zeros((rows16, tq), jnp.int16) for _ in range(n_acc))

    def total(accs):
        tot = accs[0].astype(jnp.int32)
        for a in accs[1:]:
            tot = tot + a.astype(jnp.int32)
        return jnp.sum(tot, axis=0, keepdims=True)

    def count_ge(plane_ref, t):
        t16 = jnp.broadcast_to(t, (kb_sz, tq)).astype(jnp.int16)

        def body(kb, accs):
            ge = (plane_ref[key_slice(kb), :] >= t16).astype(jnp.int16)
            return fold_rows(ge, accs)

        return total(lax.fori_loop(0, nkb, body, zero_accs()))

    def bisect16(count, need, lo, hi, steps):
        def body(_, carry):
            lo, hi = carry
            mid = (lo + hi) >> 1
            ok = count(mid) >= need
            return jnp.where(ok, mid, lo), jnp.where(ok, hi, mid)

        return lax.fori_loop(0, steps, body, (lo, hi))[0]

    floor16 = jnp.full((1, tq), -(2 ** 15), jnp.int32)
    ceil16 = jnp.full((1, tq), 2 ** 15, jnp.int32)
    count_hi = functools.partial(count_ge, hi_ref)
    count_lo = functools.partial(count_ge, lo_ref)
    top16 = (jnp.max(kmax_ref[...], axis=0, keepdims=True) >> 16) + 1
    lo_try = jnp.maximum(top16 - 2 ** WINDOW_BITS, -(2 ** 15) + 1)
    in_window = count_hi(lo_try) >= k_top
    steps = jnp.where(jnp.min(in_window.astype(jnp.int32)) > 0, WINDOW_BITS, 16)
    t_hi = bisect16(count_hi, k_top, jnp.where(in_window, lo_try, floor16), top16, steps)

    t_hi16 = jnp.broadcast_to(t_hi, (kb_sz, tq)).astype(jnp.int16)
    sentinel = jnp.int16(-(2 ** 15))
    n_slots = kb_sz // rows16 // FOLD

    def max_min(a, b):
        a_ge = a >= b
        return jnp.where(a_ge, a, b), jnp.where(a_ge, b, a)

    def tie_body(kb, accs):
        ks = key_slice(kb)
        hi_part = hi_ref[ks, :]
        e = jnp.where(hi_part == t_hi16, lo_ref[ks, :], sentinel)
        lo_ref[ks, :] = e
        g = [e[r * rows16:(r + 1) * rows16, :] for r in range(kb_sz // rows16)]
        n = len(g) // 2
        pairs = [max_min(g[i], g[i + n]) for i in range(n)]
        first, second = [p[0] for p in pairs], [p[1] for p in pairs]
        while n > n_slots:
            n //= 2
            pairs = [max_min(first[i], first[i + n]) for i in range(n)]
            second = [max_min(pairs[i][1], max_min(second[i], second[i + n])[0])[0] for i in range(n)]
            first = [p[0] for p in pairs]
        fs = pl.ds(pl.multiple_of(kb * (n_slots * rows16), n_slots * rows16), n_slots * rows16)
        f1_ref[fs, :] = jnp.concatenate(first, axis=0)
        f2_ref[fs, :] = jnp.concatenate(second, axis=0)
        return fold_rows((hi_part > t_hi16).astype(jnp.int16), accs)

    n_above = total(lax.fori_loop(0, nkb, tie_body, zero_accs()))
    need_lo = k_top - n_above

    def count_folded(t):
        t16 = jnp.broadcast_to(t, (n_slots * rows16, tq)).astype(jnp.int16)

        def body(kb, accs):
            fs = pl.ds(pl.multiple_of(kb * (n_slots * rows16), n_slots * rows16), n_slots * rows16)
            accs = list(accs)
            for i, plane in enumerate((f1_ref, f2_ref)):
                ge = (plane[fs, :] >= t16).astype(jnp.int16)
                for r in range(n_slots):
                    a = (i * n_slots + r) % n_acc
                    accs[a] = accs[a] + ge[r * rows16:(r + 1) * rows16, :]
            return tuple(accs)

        return total(lax.fori_loop(0, nkb, body, zero_accs()))

    t_lo = bisect16(count_folded, need_lo, floor16, ceil16, 16)
    t_next = jnp.minimum(t_lo + 1, 2 ** 15 - 1)
    too_small = (count_lo(t_next) >= need_lo) & (t_lo < 2 ** 15 - 1)
    t_lo = lax.cond(jnp.max(too_small.astype(jnp.int32)) > 0,
                    lambda: bisect16(count_lo, need_lo, floor16, ceil16, 16),
                    lambda: t_lo)
    thr = jnp.maximum((t_hi << 16) + (t_lo + 2 ** 15), INT_MIN + 1)

    def count_keys(pred):
        def body(kb, c):
            hit = pred(keys_ref[key_slice(kb), :], kb * kb_sz + row_iota).astype(jnp.int32)
            return c + jnp.sum(hit.reshape(kb_sz // 8, 8, tq), axis=0)

        c = lax.fori_loop(0, nkb, body, jnp.zeros((8, tq), jnp.int32))
        return jnp.sum(c, axis=0, keepdims=True)

    n_selected = count_keys(lambda key, row: key >= thr)

    @pl.when(jnp.max(n_selected) > k_top)
    def _():
        need_eq = k_top - count_keys(lambda key, row: key > thr)

        def body(_, carry):
            lo, hi = carry
            mid = (lo + hi) >> 1
            ok = count_keys(lambda key, row: (key == thr) & (row < mid)) >= need_eq
            return jnp.where(ok, lo, mid), jnp.where(ok, mid, hi)

        n_rows = keys_ref.shape[0]
        _, cut = lax.fori_loop(0, n_rows.bit_length(), body,
                               (jnp.zeros((1, tq), jnp.int32), jnp.full((1, tq), n_rows, jnp.int32)))

        def drop(kb, carry):
            ks = key_slice(kb)
            key = keys_ref[ks, :]
            surplus = (key == thr) & (kb * kb_sz + row_iota >= cut)
            keys_ref[ks, :] = jnp.where(surplus, key - 1, key)
            return carry

        lax.fori_loop(0, nkb, drop, 0)

    m_ref[...] = jnp.full(m_ref.shape, MASK_BIAS, F32)
    acc_ref[...] = jnp.zeros(acc_ref.shape, F32)

    rc = 32
    n_rc = kb_sz // rc

    def fold8(x):
        return x.reshape(rc // 8, 8, tq)

    def qk_dots(blk, slot):
        ks = key_slice(blk)
        for h in range(N_HEADS):
            n = h // KV_GROUP
            k_n = k_ref[ks, n * HEAD_DIM:(n + 1) * HEAD_DIM]
            q_h = q_ref[:, h * HEAD_DIM:(h + 1) * HEAD_DIM]
            s_ref[slot, h] = lax.dot_general(k_n, q_h, NT_DIMS, preferred_element_type=F32)

    def softmax_block(blk, slot):
        thr_b = jnp.where(blk < nkb, thr, INT_MAX)
        bias_ref[slot] = jnp.where(keys_ref[key_slice(clamp(blk)), :] >= thr_b, 0.0, MASK_BIAS)
        for h in range(N_HEADS):
            mx = jnp.full((8, tq), MASK_BIAS, F32)
            for r in range(n_rc):
                rows = slice(r * rc, (r + 1) * rc)
                sc = s_ref[slot, h, rows, :] + bias_ref[slot, rows, :]
                s_ref[slot, h, rows, :] = sc
                mx = jnp.maximum(mx, jnp.max(fold8(sc), axis=0))
            mx_ref[h:h + 1, :] = jnp.max(mx, axis=0, keepdims=True)
        m_old = m_ref[...]
        m_new = jnp.maximum(m_old, mx_ref[...])
        alpha = jnp.exp2(m_old - m_new)
        m_ref[...] = m_new
        for h in range(N_HEADS):
            m_h = jnp.broadcast_to(m_new[h:h + 1, :], (rc, tq))
            for r in range(n_rc):
                rows = slice(r * rc, (r + 1) * rc)
                p_ref[slot, h, rows, :] = jnp.exp2(s_ref[slot, h, rows, :] - m_h).astype(BF16)
        return alpha

    def pv_update(blk, slot, alpha):
        for h in range(N_HEADS):
            n = h // KV_GROUP
            vt_n = vt_ref[blk, n * VT_ROWS:(n + 1) * VT_ROWS, :]
            pv = jnp.dot(vt_n, p_ref[slot, h], preferred_element_type=F32)
            acc_ref[h] = alpha[h:h + 1, :] * acc_ref[h] + pv

    def attn_body(j, carry):
        b0 = 2 * j
        qk_dots(clamp(b0 + 1), 1)
        pv_update(b0, 0, softmax_block(b0, 0))
        qk_dots(clamp(b0 + 2), 0)
        pv_update(clamp(b0 + 1), 1, softmax_block(b0 + 1, 1))
        return carry

    qk_dots(0, 0)
    lax.fori_loop(0, n_pairs, attn_body, 0)

    for h in range(N_HEADS):
        o_t = acc_ref[h, :HEAD_DIM, :] / acc_ref[h, HEAD_DIM:HEAD_DIM + 1, :]
        o_ref[:, h * HEAD_DIM:(h + 1) * HEAD_DIM] = o_t.T


def _attn_prompt(q, qi, wit, kze, kzo, kb, vt, batch, seq, k_top):
    tq = Q_TILE
    assert seq % tq == 0 and Q_TILE % KEY_TILE == 0 and KEY_TILE == TOKEN_TILE
    nq = seq // tq
    kvw = N_KV_HEADS * HEAD_DIM
    qrow = lambda w: pl.BlockSpec((tq, w), lambda b, i: (b * nq + i, 0))
    per_batch = lambda w: pl.BlockSpec((seq, w), lambda b, i: (b, 0), pipeline_mode=pl.Buffered(1))
    return pl.pallas_call(
        functools.partial(_attn_prompt_kernel, k_top=k_top),
        grid=(batch, nq),
        in_specs=[qrow(N_HEADS * HEAD_DIM), qrow(N_IDX_HEADS * IDX_DIM),
                  pl.BlockSpec((16, tq), lambda b, i: (0, b * nq + i)),
                  per_batch(LANES), per_batch(LANES), per_batch(kvw),
                  pl.BlockSpec((seq // KEY_TILE, N_KV_HEADS * VT_ROWS, KEY_TILE), lambda b, i: (b, 0, 0),
                               pipeline_mode=pl.Buffered(1))],
        out_specs=qrow(N_HEADS * HEAD_DIM),
        out_shape=jax.ShapeDtypeStruct((batch * seq, N_HEADS * HEAD_DIM), F32),
        scratch_shapes=[pltpu.VMEM((seq, tq), jnp.int32),
                        pltpu.VMEM((seq, tq), jnp.int16),
                        pltpu.VMEM((seq, tq), jnp.int16),
                        pltpu.VMEM((seq // FOLD, tq), jnp.int16),
                        pltpu.VMEM((seq // FOLD, tq), jnp.int16),
                        pltpu.VMEM((8, tq), jnp.int32),
                        pltpu.VMEM((N_HEADS, VT_ROWS, tq), F32),
                        pltpu.VMEM((N_HEADS, tq), F32),
                        pltpu.VMEM((N_HEADS, tq), F32),
                        pltpu.VMEM((2, N_HEADS, KEY_TILE, tq), F32),
                        pltpu.VMEM((2, N_HEADS, KEY_TILE, tq), BF16),
                        pltpu.VMEM((2, KEY_TILE, tq), F32)],
        compiler_params=pltpu.CompilerParams(dimension_semantics=("arbitrary", "arbitrary"),
                                             vmem_limit_bytes=VMEM_LIMIT),
        name="attn_prompt",
    )(q, qi, wit, kze, kzo, kb, vt)


def _attn_sample_kernel(q_ref, qi_ref, wi_ref, ck_ref, cv_ref, cki_ref, kn_ref, vn_ref, kin_ref,
                        o_ref, keyp_ref, keyn_ref, keyt_ref, biasp_ref, *, k_top, past_len, tn):
    group = LANES // tn
    rows = group * tn
    n_cols = past_len // LANES
    kn = kn_ref[...]
    vn = vn_ref[...]
    kin = kin_ref[0].astype(BF16)

    lane = lax.broadcasted_iota(jnp.int32, (tn, LANES), 1)
    pos = lax.broadcasted_iota(jnp.int32, (tn, LANES), 0)
    vis = ((past_len + lane % tn) // CHUNK) <= ((past_len + pos) // CHUNK)
    for j in range(group):
        qj = qi_ref[j].reshape(N_IDX_HEADS * tn, IDX_DIM)
        dp = lax.dot_general(qj, cki_ref[0, j].astype(BF16), NT_DIMS, preferred_element_type=F32)
        dn = lax.dot_general(qj, kin, NT_DIMS, preferred_element_type=F32)
        sp = jnp.zeros((tn, past_len), F32)
        sn = jnp.zeros((tn, LANES), F32)
        for h in range(N_IDX_HEADS):
            w = wi_ref[j * tn:(j + 1) * tn, h:h + 1]
            sp = sp + w * jnp.maximum(dp[h * tn:(h + 1) * tn, :], 0.0)
            sn = sn + w * jnp.maximum(dn[h * tn:(h + 1) * tn, :], 0.0)
        keyp_ref[j * tn:(j + 1) * tn, :] = _order_key(sp)
        own = (lane // tn) == j
        keyn_ref[j * tn:(j + 1) * tn, :] = jnp.where(own & vis, _order_key(sn), INT_MIN)

    for c in range(n_cols):
        keyt_ref[c * LANES:(c + 1) * LANES, :] = keyp_ref[:, c * LANES:(c + 1) * LANES].T
    keyt_ref[past_len:past_len + LANES, :] = keyn_ref[...].T
    n_keys = past_len + LANES

    def bisect_body(_, carry):
        lo, hi = carry
        mid = _midpoint(lo, hi)
        ge = (keyt_ref[...] >= mid).astype(jnp.int32)
        cnt = jnp.sum(jnp.sum(ge.reshape(n_keys // 8, 8, rows), axis=0), axis=0, keepdims=True)
        ok = cnt >= k_top
        return jnp.where(ok, mid, lo), jnp.where(ok, hi, mid)

    thr, _ = lax.fori_loop(
        0, 32, bisect_body,
        (jnp.full((1, rows), INT_MIN + 1, jnp.int32), jnp.full((1, rows), INT_MAX, jnp.int32)))
    thr_col = jnp.broadcast_to(thr, (rows, rows)).T

    key_row = lax.broadcasted_iota(jnp.int32, (n_keys, 1), 0)

    def count_keys(pred):
        hit = pred(keyt_ref[...], key_row).astype(jnp.int32)
        return jnp.sum(jnp.sum(hit.reshape(n_keys // 8, 8, rows), axis=0), axis=0, keepdims=True)

    n_selected = count_keys(lambda key, row: key >= thr)

    @pl.when(jnp.max(n_selected) > k_top)
    def _():
        need_eq = k_top - count_keys(lambda key, row: key > thr)

        def body(_, carry):
            lo, hi = carry
            mid = (lo + hi) >> 1
            ok = count_keys(lambda key, row: (key == thr) & (row < mid)) >= need_eq
            return jnp.where(ok, lo, mid), jnp.where(ok, mid, hi)

        _, cut = lax.fori_loop(0, n_keys.bit_length(), body,
                               (jnp.zeros((1, rows), jnp.int32), jnp.full((1, rows), n_keys, jnp.int32)))
        cut_col = jnp.broadcast_to(cut, (rows, rows)).T
        lane_idx = lax.broadcasted_iota(jnp.int32, (rows, LANES), 1)
        for c in range(n_cols + 1):
            ref, cols = (keyp_ref, slice(c * LANES, (c + 1) * LANES)) if c < n_cols else (keyn_ref, slice(None))
            key = ref[:, cols]
            surplus = (key == thr_col) & (c * LANES + lane_idx >= cut_col)
            ref[:, cols] = jnp.where(surplus, key - 1, key)

    bias_n = jnp.where(keyn_ref[...] >= thr_col, 0.0, MASK_BIAS)
    for c in range(n_cols):
        cols = slice(c * LANES, (c + 1) * LANES)
        biasp_ref[:, cols] = jnp.where(keyp_ref[:, cols] >= thr_col, 0.0, MASK_BIAS)

    for j in range(group):
        bias_pj = biasp_ref[j * tn:(j + 1) * tn, :]
        bias_nj = bias_n[j * tn:(j + 1) * tn, :]
        for n in range(N_KV_HEADS):
            hs = slice(n * HEAD_DIM, (n + 1) * HEAD_DIM)
            k_n = ck_ref[0, j, pl.ds(n, past_len, stride=N_KV_HEADS), :].astype(BF16)
            v_n = cv_ref[0, j, pl.ds(n, past_len, stride=N_KV_HEADS), :].astype(BF16)
            qg = jnp.concatenate(
                [q_ref[j * tn:(j + 1) * tn, (n * KV_GROUP + g) * HEAD_DIM:(n * KV_GROUP + g + 1) * HEAD_DIM]
                 for g in range(KV_GROUP)], axis=0)
            s1 = lax.dot_general(qg, k_n, NT_DIMS, preferred_element_type=F32)
            s2 = lax.dot_general(qg, kn[:, hs], NT_DIMS, preferred_element_type=F32)
            s1 = (s1.reshape(KV_GROUP, tn, past_len) + bias_pj[None]).reshape(KV_GROUP * tn, past_len)
            s2 = (s2.reshape(KV_GROUP, tn, LANES) + bias_nj[None]).reshape(KV_GROUP * tn, LANES)
            m = jnp.maximum(jnp.max(s1, axis=1, keepdims=True), jnp.max(s2, axis=1, keepdims=True))
            p1 = jnp.exp2(s1 - m)
            p2 = jnp.exp2(s2 - m)
            l = jnp.sum(p1, axis=1, keepdims=True) + jnp.sum(p2, axis=1, keepdims=True)
            o = (jnp.dot(p1.astype(BF16), v_n, preferred_element_type=F32)
                 + jnp.dot(p2.astype(BF16), vn[:, hs], preferred_element_type=F32)) / l
            for g in range(KV_GROUP):
                h = n * KV_GROUP + g
                o_ref[j * tn:(j + 1) * tn, h * HEAD_DIM:(h + 1) * HEAD_DIM] = o[g * tn:(g + 1) * tn, :]


def _attn_sample(q, qi4, wi, ck, cv, cki, kb, vb, kif, layer, k_top):
    nb = ck.shape[1]
    past_len = cki.shape[2]
    kvw = N_KV_HEADS * HEAD_DIM
    tn = q.shape[0] // nb
    assert LANES % tn == 0 and tn % 16 == 0 and past_len % LANES == 0
    group = LANES // tn
    assert nb % group == 0
    row = lambda w: pl.BlockSpec((LANES, w), lambda i: (i, 0))
    cache = lambda rows, w: pl.BlockSpec((1, group, rows, w), lambda i: (layer, i, 0, 0))
    return pl.pallas_call(
        functools.partial(_attn_sample_kernel, k_top=k_top, past_len=past_len, tn=tn),
        grid=(nb // group,),
        in_specs=[row(N_HEADS * HEAD_DIM),
                  pl.BlockSpec((group, N_IDX_HEADS, tn, IDX_DIM), lambda i: (i, 0, 0, 0)),
                  row(LANES), cache(N_KV_HEADS * past_len, HEAD_DIM),
                  cache(N_KV_HEADS * past_len, HEAD_DIM), cache(past_len, IDX_DIM),
                  row(kvw), row(kvw),
                  pl.BlockSpec((1, LANES, IDX_DIM), lambda i: (layer, i, 0))],
        out_specs=row(N_HEADS * HEAD_DIM),
        out_shape=jax.ShapeDtypeStruct((nb * tn, N_HEADS * HEAD_DIM), F32),
        scratch_shapes=[pltpu.VMEM((LANES, past_len), jnp.int32),
                        pltpu.VMEM((LANES, LANES), jnp.int32),
                        pltpu.VMEM((past_len + LANES, LANES), jnp.int32),
                        pltpu.VMEM((LANES, past_len), F32)],
        compiler_params=pltpu.CompilerParams(dimension_semantics=("arbitrary",),
                                             vmem_limit_bytes=VMEM_LIMIT),
        name="attn_sample",
    )(q, qi4, wi, ck, cv, cki, kb, vb, kif)


def _outffn_kernel(x_ref, a_ref, sga_ref, sgb_ref, u_ref, vn_ref, wmix_ref, bmix_ref, wo_ref,
                   g2_ref, wg_ref, wu_ref, wd_ref, gf_ref, xo_ref, *rest, final):
    if final:
        y_ref, z_ref = rest
    else:
        (z_ref,) = rest
    tm = x_ref.shape[0]
    group_dim = wmix_ref.shape[-1]
    for c in range(tm // SGU_CHUNK):
        rs = slice(c * SGU_CHUNK, (c + 1) * SGU_CHUNK)
        for g in range(SGU_GROUPS):
            cs = slice(g * group_dim, (g + 1) * group_dim)
            mixed = jnp.dot(wmix_ref[0, g], vn_ref[rs, cs].astype(BF16),
                            preferred_element_type=F32) + bmix_ref[0, :, cs]
            z = sga_ref[rs, cs] * a_ref[rs, cs] + sgb_ref[rs, cs] * (u_ref[rs, cs] * mixed)
            z_ref[rs, cs] = z.astype(BF16)
    x1 = x_ref[...] + jnp.dot(z_ref[...], wo_ref[0], preferred_element_type=F32)
    h2 = _rms_norm(x1, g2_ref[0]).astype(BF16)
    gate = jnp.dot(h2, wg_ref[0], preferred_element_type=F32)
    up = jnp.dot(h2, wu_ref[0], preferred_element_type=F32)
    ff = (gate * _sigmoid(gate) * up).astype(BF16)
    x2 = x1 + jnp.dot(ff, wd_ref[0], preferred_element_type=F32)
    xo_ref[...] = x2
    if final:
        y_ref[...] = _rms_norm(x2, gf_ref[...])


def _outffn(x, a, sga, sgb, u, vn, wmix, bmix, layer, wo, g2, wg, wu, wd, gf, final):
    T, D = x.shape
    tm = TOKEN_TILE
    row = pl.BlockSpec((tm, D), lambda i: (i, 0))
    n_out = 2 if final else 1
    out = pl.pallas_call(
        functools.partial(_outffn_kernel, final=final),
        grid=(T // tm,),
        in_specs=[row] * 6 + [_layer_spec(wmix.shape, layer), _layer_spec(bmix.shape, layer),
                              _layer_spec(wo.shape, layer), _layer_spec(g2.shape, layer),
                              _layer_spec(wg.shape, layer), _layer_spec(wu.shape, layer),
                              _layer_spec(wd.shape, layer), _const_spec((1, D))],
        out_specs=(row,) * n_out,
        out_shape=(jax.ShapeDtypeStruct((T, D), F32),) * n_out,
        scratch_shapes=[pltpu.VMEM((tm, D), BF16)],
        compiler_params=pltpu.CompilerParams(dimension_semantics=("arbitrary",),
                                             vmem_limit_bytes=VMEM_LIMIT),
        name="outffn",
    )(x, a, sga, sgb, u, vn, wmix, bmix, wo, g2, wg, wu, wd, gf)
    return out if final else (out[0], None)


def _rope_tables(pos, d):
    inv = ROPE_THETA ** (-jnp.arange(0, d, 2, dtype=F32) / d)
    ang = pos.astype(F32)[:, None] * inv[None, :]
    cos, sin = jnp.cos(ang), jnp.sin(ang)
    c = jnp.concatenate([cos, cos], axis=-1)
    s = jnp.concatenate([-sin, sin], axis=-1)
    reps = LANES // d
    return jnp.tile(c, (1, reps)), jnp.tile(s, (1, reps))


def _mix_weights(sgu_w, sgu_b, n, group_dim):
    depth = sgu_w.shape[0]
    p = jnp.arange(n)
    mask = (p[None, :] // CHUNK) <= (p[:, None] // CHUNK)
    w = jnp.where(mask[None, None], sgu_w[:, :, :n, :n], 0.0)
    reps = SGU_CHUNK // n
    eye = jnp.eye(reps, dtype=w.dtype)
    wbd = jnp.einsum('ab,lgij->lgaibj', eye, w).reshape(depth, SGU_GROUPS, SGU_CHUNK, SGU_CHUNK)
    b = jnp.tile(sgu_b[:, :, :n], (1, 1, reps))
    bfull = jnp.repeat(jnp.swapaxes(b, 1, 2), group_dim, axis=2)
    return wbd.astype(BF16), bfull


def _in_weights(w_in, d_model):
    kvw = N_KV_HEADS * HEAD_DIM
    w = w_in
    c_ki = d_model + 2 * kvw + N_IDX_HEADS * IDX_DIM
    c_wi = c_ki + IDX_DIM
    c_u = c_wi + N_IDX_HEADS
    wki, wwi = w[:, :, c_ki:c_wi], w[:, :, c_wi:c_u]
    wwi_pad = jnp.pad(wwi, ((0, 0), (0, 0), (0, LANES - N_IDX_HEADS)))
    wa = jnp.concatenate([w[:, :, :c_ki], wki, wki, wwi_pad, w[:, :, c_u:]], axis=2)
    wv_t = jnp.swapaxes(w[:, :, d_model + kvw:d_model + 2 * kvw], 1, 2)
    wwi_t = jnp.pad(jnp.swapaxes(wwi, 1, 2), ((0, 0), (0, 16 - N_IDX_HEADS), (0, 0)))
    return wa.astype(BF16), jnp.concatenate([wv_t, wwi_t], axis=1).astype(BF16)


def kernel(x_prompt, x_sample, cache_k, cache_v, cache_kidx, norm1_g, w_in, ln_v_g, ln_v_b, sgu_w, sgu_b, w_out, norm2_g, w_gate, w_up, w_down, final_norm_g):
    B, S, D = x_prompt.shape
    NB, TN, _ = x_sample.shape
    depth, _, P = cache_k.shape[:3]
    assert D == N_HEADS * HEAD_DIM and SGU_CHUNK % TN == 0 and S % SGU_CHUNK == 0
    k_top_p = min(TOPK_MAX, S // 4)
    k_top_s = min(TOPK_MAX, (P + TN) // 4)
    group_dim = D // SGU_GROUPS

    assert S % TOKEN_TILE == 0 and TOKEN_TILE % TN == 0
    pos_p = jnp.arange(S)
    pos_s = jnp.tile(P + jnp.arange(TN), TOKEN_TILE // TN)
    tab_p = _rope_tables(pos_p, HEAD_DIM) + _rope_tables(pos_p, IDX_DIM)
    tab_s = _rope_tables(pos_s, HEAD_DIM) + _rope_tables(pos_s, IDX_DIM)

    wa, wb = _in_weights(w_in, D)
    wo, wg, wu, wd = (t.astype(BF16) for t in (w_out, w_gate, w_up, w_down))
    vec = lambda v: v.reshape(depth, 1, -1)
    g1, g2, lng, lnb = vec(norm1_g), vec(norm2_g), vec(ln_v_g), vec(ln_v_b)
    gf = final_norm_g.reshape(1, -1)
    wmix_p, bmix_p = _mix_weights(sgu_w, sgu_b, SGU_CHUNK, group_dim)
    wmix_s, bmix_s = _mix_weights(sgu_w, sgu_b, TN, group_dim)
    ck = cache_k.reshape(depth, NB, P * N_KV_HEADS, HEAD_DIM)
    cv = cache_v.reshape(depth, NB, P * N_KV_HEADS, HEAD_DIM)

    xp = x_prompt.reshape(B * S, D)
    xs = x_sample.reshape(NB * TN, D)
    def new_buffers(tokens):
        kv = (depth, N_KV_HEADS * tokens, HEAD_DIM)
        return jnp.zeros(kv, F32), jnp.zeros(kv, F32), jnp.zeros((depth, tokens, IDX_DIM), F32)

    new_p, new_s = new_buffers(B * S), new_buffers(NB * TN)
    sgu_v = []
    yp = ys = None
    for l in range(depth):
        final = l == depth - 1

        (q, kf, kb, vf, _, vt, qi, kif, kze, kzo, _, wit, u, vn, sga, sgb) = _inproj(
            xp, l, g1, wa, wb, tab_p, lng, lnb, new_p)
        new_p = (kf, vf, kif)
        a = _attn_prompt(q, qi, wit, kze, kzo, kb, vt, B, S, k_top_p)
        xp, yp = _outffn(xp, a, sga, sgb, u, vn, wmix_p, bmix_p, l, wo, g2, wg, wu, wd, gf, final)

        (q, kf, kb, vf, vb, _, qi, kif, _, _, wi, _, u, vn, sga, sgb) = _inproj(
            xs, l, g1, wa, wb, tab_s, lng, lnb, new_s)
        new_s = (kf, vf, kif)
        qi4 = qi.reshape(NB, TN, N_IDX_HEADS, IDX_DIM).transpose(0, 2, 1, 3)
        a = _attn_sample(q, qi4, wi, ck, cv, cache_kidx, kb, vb, kif, l, k_top_s)
        xs, ys = _outffn(xs, a, sga, sgb, u, vn, wmix_s, bmix_s, l, wo, g2, wg, wu, wd, gf, final)
        sgu_v.append(vn.reshape(NB, TN, D))

    kv_p = (depth, B, S, N_KV_HEADS, HEAD_DIM)
    kv_s = (depth, NB, TN, N_KV_HEADS, HEAD_DIM)
    return (yp.reshape(B, S, D), ys.reshape(NB, TN, D),
            new_p[0].reshape(kv_p), new_p[1].reshape(kv_p), new_p[2].reshape(depth, B, S, IDX_DIM),
            new_s[0].reshape(kv_s), new_s[1].reshape(kv_s), new_s[2].reshape(depth, NB, TN, IDX_DIM),
            jnp.stack(sgu_v))
```

```python
import functools
import math

import jax
import jax.numpy as jnp
from jax import lax
from jax.experimental import pallas as pl
from jax.experimental.pallas import tpu as pltpu

CHUNK = 64
N_HEADS = 8
HEAD_DIM = 128
N_KV_HEADS = 2
KV_GROUP = N_HEADS // N_KV_HEADS
N_IDX_HEADS = 8
IDX_DIM = 64
TOPK_MAX = 256
SGU_CHUNK = 128
SGU_GROUPS = 8
ROPE_THETA = 10000.0
EPS = 1e-6

LANES = 128
TOKEN_TILE = 256
Q_TILE = 256
KEY_TILE = 256
VT_ROWS = HEAD_DIM + 16
FOLD = 8
WINDOW_BITS = 9
VMEM_LIMIT = 56 * 1024 * 1024

INT_MIN = -(2 ** 31)
INT_MAX = 2 ** 31 - 1
MASK_BIAS = -1e30
LOG2E = 1.4426950408889634
Q_SCALE = (HEAD_DIM ** -0.5) * LOG2E

F32 = jnp.float32
BF16 = jnp.bfloat16
NT_DIMS = (((1,), (1,)), ((), ()))


def _const_spec(shape):
    nd = len(shape)
    return pl.BlockSpec(shape, lambda *_: (0,) * nd, pipeline_mode=pl.Buffered(1))


def _sigmoid(x):
    return 1.0 / (1.0 + jnp.exp(-x))


def _rms_norm(x, g):
    return x * lax.rsqrt(jnp.mean(x * x, axis=-1, keepdims=True) + EPS) * g


def _order_key(score):
    bits = pltpu.bitcast(score, jnp.int32)
    return bits ^ ((bits >> 31) & INT_MAX)


def _midpoint(lo, hi):
    return (lo >> 1) + (hi >> 1) + (lo & hi & 1)


_C_Q = 0
_C_K = _C_Q + N_HEADS * HEAD_DIM
_C_V = _C_K + N_KV_HEADS * HEAD_DIM
_C_QI = _C_V + N_KV_HEADS * HEAD_DIM
_C_KI = _C_QI + N_IDX_HEADS * IDX_DIM
_C_WI = _C_KI + LANES
_C_U = _C_WI + LANES


def _inproj_kernel(*refs, d_model, n_alias):
    (x_ref, g_ref, wa_ref, wb_ref, cosh_ref, sinh_ref, cosi_ref, sini_ref,
     lng_ref, lnb_ref) = refs[:10]
    (q_ref, kf_ref, kb_ref, vf_ref, vb_ref, vt_ref, qi_ref, kif_ref, kze_ref, kzo_ref,
     wi_ref, wit_ref, u_ref, vn_ref, sga_ref, sgb_ref) = refs[10 + n_alias:]
    tm = x_ref.shape[0]
    hb = _rms_norm(x_ref[...], g_ref[0]).astype(BF16)

    def proj(c0, width):
        return jnp.dot(hb, wa_ref[0, :, c0:c0 + width], preferred_element_type=F32)

    def head_rows(n):
        return pl.ds(n, tm, stride=N_KV_HEADS)

    cosh, sinh = cosh_ref[...], sinh_ref[...]
    cosi, sini = cosi_ref[...], sini_ref[...]
    lane = lax.broadcasted_iota(jnp.int32, cosi.shape, 1)
    first_half = (lane % IDX_DIM) < (IDX_DIM // 2)

    def rope_head(x):
        return x * cosh + pltpu.roll(x, HEAD_DIM // 2, 1) * sinh

    def rope_idx(x):
        partner = jnp.where(first_half, pltpu.roll(x, LANES - IDX_DIM // 2, 1),
                            pltpu.roll(x, IDX_DIM // 2, 1))
        return x * cosi + partner * sini

    xq = proj(_C_Q, N_HEADS * HEAD_DIM)
    for h in range(N_HEADS):
        sl = slice(h * HEAD_DIM, (h + 1) * HEAD_DIM)
        q_ref[:, sl] = (rope_head(xq[:, sl]) * Q_SCALE).astype(BF16)

    xk = proj(_C_K, N_KV_HEADS * HEAD_DIM)
    for h in range(N_KV_HEADS):
        sl = slice(h * HEAD_DIM, (h + 1) * HEAD_DIM)
        kr = rope_head(xk[:, sl])
        kf_ref[0, head_rows(h), :] = kr
        kb_ref[:, sl] = kr.astype(BF16)

    xv = proj(_C_V, N_KV_HEADS * HEAD_DIM)
    vb_ref[...] = xv.astype(BF16)
    for h in range(N_KV_HEADS):
        vf_ref[0, head_rows(h), :] = xv[:, h * HEAD_DIM:(h + 1) * HEAD_DIM]

    xqi = proj(_C_QI, N_IDX_HEADS * IDX_DIM)
    for p in range(N_IDX_HEADS * IDX_DIM // LANES):
        sl = slice(p * LANES, (p + 1) * LANES)
        qi_ref[:, sl] = (rope_idx(xqi[:, sl]) * (IDX_DIM ** -0.5)).astype(BF16)

    kk = rope_idx(proj(_C_KI, LANES))
    kif_ref[0] = kk[:, :IDX_DIM]
    low = lane < IDX_DIM
    kze_ref[...] = jnp.where(low, kk, 0.0).astype(BF16)
    kzo_ref[...] = jnp.where(low, 0.0, kk).astype(BF16)

    wi_ref[...] = proj(_C_WI, LANES) * (N_IDX_HEADS ** -0.5)

    tb = lax.dot_general(wb_ref[0], hb, NT_DIMS, preferred_element_type=F32)
    kvw = N_KV_HEADS * HEAD_DIM
    ones_rows = (lax.broadcasted_iota(jnp.int32, (VT_ROWS - HEAD_DIM, tm), 0) == 0).astype(BF16)
    for n in range(N_KV_HEADS):
        vt_ref[0, n * VT_ROWS:n * VT_ROWS + HEAD_DIM, :] = (
            tb[n * HEAD_DIM:(n + 1) * HEAD_DIM].astype(BF16))
        vt_ref[0, n * VT_ROWS + HEAD_DIM:(n + 1) * VT_ROWS, :] = ones_rows
    wit_ref[...] = tb[kvw:] * (N_IDX_HEADS ** -0.5)

    u_ref[...] = jax.nn.gelu(proj(_C_U, d_model), approximate=True)
    gv = jax.nn.gelu(proj(_C_U + d_model, d_model), approximate=True)
    mu = jnp.mean(gv, axis=-1, keepdims=True)
    dv = gv - mu
    var = jnp.mean(dv * dv, axis=-1, keepdims=True)
    vn_ref[...] = dv * lax.rsqrt(var + EPS) * lng_ref[0] + lnb_ref[0]
    sga_ref[...] = _sigmoid(proj(_C_U + 2 * d_model, d_model))
    sgb_ref[...] = _sigmoid(proj(_C_U + 3 * d_model, d_model))


def _layer_spec(shape, layer):
    nd = len(shape)
    return pl.BlockSpec((1,) + tuple(shape[1:]), lambda *_: (layer,) + (0,) * (nd - 1),
                        pipeline_mode=pl.Buffered(1))


def _inproj(x, layer, g, wa, wb, tables, lng, lnb, stacked):
    T, D = x.shape
    depth = wa.shape[0]
    tm = TOKEN_TILE
    assert T % tm == 0
    kvw = N_KV_HEADS * HEAD_DIM
    qiw = N_IDX_HEADS * IDX_DIM
    row = lambda w: pl.BlockSpec((tm, w), lambda i: (i, 0))
    kv_rows = N_KV_HEADS * T
    out_shape = (
        jax.ShapeDtypeStruct((T, N_HEADS * HEAD_DIM), BF16),
        jax.ShapeDtypeStruct((depth, kv_rows, HEAD_DIM), F32),
        jax.ShapeDtypeStruct((T, kvw), BF16),
        jax.ShapeDtypeStruct((depth, kv_rows, HEAD_DIM), F32),
        jax.ShapeDtypeStruct((T, kvw), BF16),
        jax.ShapeDtypeStruct((T // tm, N_KV_HEADS * VT_ROWS, tm), BF16),
        jax.ShapeDtypeStruct((T, qiw), BF16),
        jax.ShapeDtypeStruct((depth, T, IDX_DIM), F32),
        jax.ShapeDtypeStruct((T, LANES), BF16),
        jax.ShapeDtypeStruct((T, LANES), BF16),
        jax.ShapeDtypeStruct((T, LANES), F32),
        jax.ShapeDtypeStruct((16, T), F32),
        jax.ShapeDtypeStruct((T, D), F32),
        jax.ShapeDtypeStruct((T, D), F32),
        jax.ShapeDtypeStruct((T, D), F32),
        jax.ShapeDtypeStruct((T, D), F32),
    )
    kv_spec = pl.BlockSpec((1, N_KV_HEADS * tm, HEAD_DIM), lambda i: (layer, i, 0))
    out_specs = (
        row(N_HEADS * HEAD_DIM), kv_spec, row(kvw), kv_spec, row(kvw),
        pl.BlockSpec((1, N_KV_HEADS * VT_ROWS, tm), lambda i: (i, 0, 0)),
        row(qiw), pl.BlockSpec((1, tm, IDX_DIM), lambda i: (layer, i, 0)),
        row(LANES), row(LANES), row(LANES),
        pl.BlockSpec((16, tm), lambda i: (0, i)),
        row(D), row(D), row(D), row(D),
    )
    n_tab = tables[0].shape[0] // tm
    tab = pl.BlockSpec((tm, LANES), lambda i: (i % n_tab, 0))
    in_specs = [row(D), _layer_spec(g.shape, layer), _layer_spec(wa.shape, layer),
                _layer_spec(wb.shape, layer),
                tab, tab, tab, tab,
                _layer_spec(lng.shape, layer), _layer_spec(lnb.shape, layer)]
    args = [x, g, wa, wb, *tables, lng, lnb]
    aliases = {}
    for j, out_idx in enumerate((1, 3, 7)):
        aliases[len(args)] = out_idx
        in_specs.append(pl.BlockSpec(memory_space=pl.ANY))
        args.append(stacked[j])
    return pl.pallas_call(
        functools.partial(_inproj_kernel, d_model=D, n_alias=len(aliases)),
        grid=(T // tm,),
        in_specs=in_specs, out_specs=out_specs, out_shape=out_shape,
        input_output_aliases=aliases,
        compiler_params=pltpu.CompilerParams(dimension_semantics=("arbitrary",),
                                             vmem_limit_bytes=VMEM_LIMIT),
        name="inproj",
    )(*args)


def _attn_prompt_kernel(q_ref, qi_ref, wit_ref, kze_ref, kzo_ref, k_ref, vt_ref, o_ref,
                        keys_ref, hi_ref, lo_ref, f1_ref, f2_ref, kmax_ref, acc_ref, m_ref, mx_ref, s_ref, p_ref,
                        bias_ref, *, k_top):
    tq, kb_sz = Q_TILE, KEY_TILE
    qb = pl.program_id(1)
    nkb = (qb + 1) * (tq // kb_sz)
    q_chunk = (qb * tq + lax.broadcasted_iota(jnp.int32, (1, tq), 1)) // CHUNK
    row_iota = lax.broadcasted_iota(jnp.int32, (kb_sz, 1), 0)

    def key_slice(kb):
        return pl.ds(pl.multiple_of(kb * kb_sz, kb_sz), kb_sz)

    n_blocks = keys_ref.shape[0] // kb_sz
    n_pairs = (nkb + 1) // 2

    def clamp(blk):
        return jnp.minimum(blk, n_blocks - 1)

    def idx_dots(blk, slot):
        ks = key_slice(blk)
        kze, kzo = kze_ref[ks, :], kzo_ref[ks, :]
        for p in range(N_IDX_HEADS // 2):
            qp = qi_ref[:, p * LANES:(p + 1) * LANES]
            s_ref[slot, 2 * p] = lax.dot_general(kze, qp, NT_DIMS, preferred_element_type=F32)
            s_ref[slot, 2 * p + 1] = lax.dot_general(kzo, qp, NT_DIMS, preferred_element_type=F32)

    def idx_keys(blk, slot):
        base = pl.multiple_of(blk * kb_sz, kb_sz)
        kmax = kmax_ref[...]
        for c in range(kb_sz // CHUNK):
            rows = slice(c * CHUNK, (c + 1) * CHUNK)
            score = jnp.zeros((CHUNK, tq), F32)
            for j in range(N_IDX_HEADS):
                score = score + wit_ref[j:j + 1, :] * jnp.maximum(s_ref[slot, j, rows, :], 0.0)
            visible = (blk * (kb_sz // CHUNK) + c) <= q_chunk
            key = jnp.where(visible, _order_key(score), INT_MIN)
            ks = pl.ds(base + c * CHUNK, CHUNK)
            keys_ref[ks, :] = key
            hi_ref[ks, :] = (key >> 16).astype(jnp.int16)
            lo_ref[ks, :] = ((key & 0xFFFF) - 2 ** 15).astype(jnp.int16)
            kmax = jnp.maximum(kmax, jnp.max(key.reshape(CHUNK // 8, 8, tq), axis=0))
        kmax_ref[...] = kmax

    def score_body(j, carry):
        b0 = 2 * j
        idx_dots(clamp(b0 + 1), 1)
        idx_keys(b0, 0)
        idx_dots(clamp(b0 + 2), 0)
        idx_keys(clamp(b0 + 1), 1)
        return carry

    kmax_ref[...] = jnp.full(kmax_ref.shape, INT_MIN, jnp.int32)
    idx_dots(0, 0)
    lax.fori_loop(0, n_pairs, score_body, 0)

    n_acc = 4
    rows16 = 16

    def fold_rows(x, accs):
        accs = list(accs)
        for r in range(kb_sz // rows16):
            accs[r % n_acc] = accs[r % n_acc] + x[r * rows16:(r + 1) * rows16, :]
        return tuple(accs)

    def zero_accs():
        return tuple(jnp.zeros((rows16, tq), jnp.int16) for _ in range(n_acc))

    def total(accs):
        tot = accs[0].astype(jnp.int32)
        for a in accs[1:]:
            tot = tot + a.astype(jnp.int32)
        return jnp.sum(tot, axis=0, keepdims=True)

    def count_ge(plane_ref, t):
        t16 = jnp.broadcast_to(t, (kb_sz, tq)).astype(jnp.int16)

        def body(kb, accs):
            ge = (plane_ref[key_slice(kb), :] >= t16).astype(jnp.int16)
            return fold_rows(ge, accs)

        return total(lax.fori_loop(0, nkb, body, zero_accs()))

    def bisect16(count, need, lo, hi, count_lo, steps):
        def body(_, carry):
            lo, hi, n_lo = carry
            mid = (lo + hi) >> 1
            n_mid = count(mid)
            ok = n_mid >= need
            return jnp.where(ok, mid, lo), jnp.where(ok, hi, mid), jnp.where(ok, n_mid, n_lo)

        lo, _, n_lo = lax.fori_loop(0, steps, body, (lo, hi, count_lo))
        return lo, n_lo

    floor16 = jnp.full((1, tq), -(2 ** 15), jnp.int32)
    ceil16 = jnp.full((1, tq), 2 ** 15, jnp.int32)
    every = jnp.full((1, tq), keys_ref.shape[0], jnp.int32)
    count_hi = functools.partial(count_ge, hi_ref)
    top16 = (jnp.max(kmax_ref[...], axis=0, keepdims=True) >> 16) + 1
    lo_try = jnp.maximum(top16 - 2 ** WINDOW_BITS, -(2 ** 15) + 1)
    n_try = count_hi(lo_try)
    in_window = n_try >= k_top
    steps = jnp.where(jnp.min(in_window.astype(jnp.int32)) > 0, WINDOW_BITS, 16)
    t_hi, n_hi = bisect16(count_hi, k_top, jnp.where(in_window, lo_try, floor16), top16,
                          jnp.where(in_window, n_try, every), steps)

    t_hi16 = jnp.broadcast_to(t_hi, (kb_sz, tq)).astype(jnp.int16)
    sentinel = jnp.int16(-(2 ** 15))
    n_slots = kb_sz // rows16 // FOLD

    def max_min(a, b):
        a_ge = a >= b
        return jnp.where(a_ge, a, b), jnp.where(a_ge, b, a)

    def tie_body(kb, accs):
        ks = key_slice(kb)
        hi_part = hi_ref[ks, :]
        e = jnp.where(hi_part == t_hi16, lo_ref[ks, :], sentinel)
        lo_ref[ks, :] = e
        g = [e[r * rows16:(r + 1) * rows16, :] for r in range(kb_sz // rows16)]
        n = len(g) // 2
        pairs = [max_min(g[i], g[i + n]) for i in range(n)]
        first, second = [p[0] for p in pairs], [p[1] for p in pairs]
        while n > n_slots:
            n //= 2
            pairs = [max_min(first[i], first[i + n]) for i in range(n)]
            second = [max_min(pairs[i][1], max_min(second[i], second[i + n])[0])[0] for i in range(n)]
            first = [p[0] for p in pairs]
        fs = pl.ds(pl.multiple_of(kb * (n_slots * rows16), n_slots * rows16), n_slots * rows16)
        f1_ref[fs, :] = jnp.concatenate(first, axis=0)
        f2_ref[fs, :] = jnp.concatenate(second, axis=0)
        return fold_rows((hi_part > t_hi16).astype(jnp.int16), accs)

    n_above = total(lax.fori_loop(0, nkb, tie_body, zero_accs()))
    need_lo = k_top - n_above

    def count_folded(t):
        t16 = jnp.broadcast_to(t, (n_slots * rows16, tq)).astype(jnp.int16)

        def body(kb, accs):
            fs = pl.ds(pl.multiple_of(kb * (n_slots * rows16), n_slots * rows16), n_slots * rows16)
            accs = list(accs)
            for i, plane in enumerate((f1_ref, f2_ref)):
                ge = (plane[fs, :] >= t16).astype(jnp.int16)
                for r in range(n_slots):
                    a = (i * n_slots + r) % n_acc
                    accs[a] = accs[a] + ge[r * rows16:(r + 1) * rows16, :]
            return tuple(accs)

        return total(lax.fori_loop(0, nkb, body, zero_accs()))

    def count_lo_pair(t):
        t_next = jnp.minimum(t + 1, 2 ** 15 - 1)
        ta = jnp.broadcast_to(t, (kb_sz, tq)).astype(jnp.int16)
        tb = jnp.broadcast_to(t_next, (kb_sz, tq)).astype(jnp.int16)

        def body(kb, accs):
            plane = lo_ref[key_slice(kb), :]
            return (fold_rows((plane >= ta).astype(jnp.int16), accs[0]),
                    fold_rows((plane >= tb).astype(jnp.int16), accs[1]))

        a, b = lax.fori_loop(0, nkb, body, (zero_accs(), zero_accs()))
        return total(a), jnp.where(t < 2 ** 15 - 1, total(b), 0)

    t_lo, _ = bisect16(count_folded, need_lo, floor16, ceil16, every, 16)
    n_at, n_next = count_lo_pair(t_lo)

    def unfolded_search():
        t, _ = bisect16(functools.partial(count_ge, lo_ref), need_lo, floor16, ceil16, every, 16)
        return (t,) + count_lo_pair(t)

    t_lo, n_at, n_next = lax.cond(jnp.max((n_next >= need_lo).astype(jnp.int32)) > 0,
                                  unfolded_search, lambda: (t_lo, n_at, n_next))
    thr = jnp.maximum((t_hi << 16) + (t_lo + 2 ** 15), INT_MIN + 1)

    n_selected = jnp.where(t_hi > -(2 ** 15),
                           jnp.where(t_lo > -(2 ** 15), n_above + n_at, n_hi), 0)
    need_eq = (need_lo - n_next).astype(F32)
    col_iota = lax.broadcasted_iota(jnp.int32, (1, kb_sz), 1)

    for c in range(tq // LANES):
        cs = slice(c * LANES, (c + 1) * LANES)

        @pl.when(jnp.max(n_selected[:, cs]) > k_top)
        def _(cs=cs):
            tri = jnp.where(col_iota <= row_iota, 1.0, 0.0).astype(BF16)
            thr_c, need_c = thr[:, cs], need_eq[:, cs]

            unroll = 4

            def drop(j, seen):
                blocks = [unroll * j + i for i in range(unroll)]
                keys = [keys_ref[key_slice(clamp(b)), cs] for b in blocks]
                equal = [(key == thr_c) & (b < nkb) for b, key in zip(blocks, keys)]
                local = [jnp.dot(tri, jnp.where(e, 1.0, 0.0).astype(BF16),
                                 preferred_element_type=F32) for e in equal]
                for b, key, e, rank in zip(blocks, keys, equal, local):
                    rank = rank + seen
                    keys_ref[key_slice(clamp(b)), cs] = jnp.where(e & (rank > need_c), key - 1, key)
                    seen = rank[kb_sz - 1:kb_sz, :]
                return seen

            lax.fori_loop(0, (nkb + unroll - 1) // unroll, drop, jnp.zeros((1, LANES), F32))

    m_ref[...] = jnp.full(m_ref.shape, MASK_BIAS, F32)
    acc_ref[...] = jnp.zeros(acc_ref.shape, F32)

    rc = 32
    n_rc = kb_sz // rc

    def fold8(x):
        return x.reshape(rc // 8, 8, tq)

    def qk_dots(blk, slot):
        ks = key_slice(blk)
        for h in range(N_HEADS):
            n = h // KV_GROUP
            k_n = k_ref[ks, n * HEAD_DIM:(n + 1) * HEAD_DIM]
            q_h = q_ref[:, h * HEAD_DIM:(h + 1) * HEAD_DIM]
            s_ref[slot, h] = lax.dot_general(k_n, q_h, NT_DIMS, preferred_element_type=F32)

    def softmax_block(blk, slot):
        thr_b = jnp.where(blk < nkb, thr, INT_MAX)
        bias_ref[slot] = jnp.where(keys_ref[key_slice(clamp(blk)), :] >= thr_b, 0.0, MASK_BIAS)
        for h in range(N_HEADS):
            mx = jnp.full((8, tq), MASK_BIAS, F32)
            for r in range(n_rc):
                rows = slice(r * rc, (r + 1) * rc)
                sc = s_ref[slot, h, rows, :] + bias_ref[slot, rows, :]
                s_ref[slot, h, rows, :] = sc
                mx = jnp.maximum(mx, jnp.max(fold8(sc), axis=0))
            mx_ref[h:h + 1, :] = jnp.max(mx, axis=0, keepdims=True)
        m_old = m_ref[...]
        m_new = jnp.maximum(m_old, mx_ref[...])
        alpha = jnp.exp2(m_old - m_new)
        m_ref[...] = m_new
        for h in range(N_HEADS):
            m_h = jnp.broadcast_to(m_new[h:h + 1, :], (rc, tq))
            for r in range(n_rc):
                rows = slice(r * rc, (r + 1) * rc)
                p_ref[slot, h, rows, :] = jnp.exp2(s_ref[slot, h, rows, :] - m_h).astype(BF16)
        return alpha

    def pv_update(blk, slot, alpha):
        for h in range(N_HEADS):
            n = h // KV_GROUP
            vt_n = vt_ref[blk, n * VT_ROWS:(n + 1) * VT_ROWS, :]
            pv = jnp.dot(vt_n, p_ref[slot, h], preferred_element_type=F32)
            acc_ref[h] = alpha[h:h + 1, :] * acc_ref[h] + pv

    def attn_body(j, carry):
        b0 = 2 * j
        qk_dots(clamp(b0 + 1), 1)
        pv_update(b0, 0, softmax_block(b0, 0))
        qk_dots(clamp(b0 + 2), 0)
        pv_update(clamp(b0 + 1), 1, softmax_block(b0 + 1, 1))
        return carry

    qk_dots(0, 0)
    lax.fori_loop(0, n_pairs, attn_body, 0)

    for h in range(N_HEADS):
        o_t = acc_ref[h, :HEAD_DIM, :] / acc_ref[h, HEAD_DIM:HEAD_DIM + 1, :]
        o_ref[:, h * HEAD_DIM:(h + 1) * HEAD_DIM] = o_t.T


def _attn_prompt(q, qi, wit, kze, kzo, kb, vt, batch, seq, k_top):
    tq = Q_TILE
    assert seq % tq == 0 and Q_TILE % KEY_TILE == 0 and KEY_TILE == TOKEN_TILE
    assert (seq // KEY_TILE) % 4 == 0
    nq = seq // tq
    kvw = N_KV_HEADS * HEAD_DIM
    qrow = lambda w: pl.BlockSpec((tq, w), lambda b, i: (b * nq + i, 0))
    per_batch = lambda w: pl.BlockSpec((seq, w), lambda b, i: (b, 0), pipeline_mode=pl.Buffered(1))
    return pl.pallas_call(
        functools.partial(_attn_prompt_kernel, k_top=k_top),
        grid=(batch, nq),
        in_specs=[qrow(N_HEADS * HEAD_DIM), qrow(N_IDX_HEADS * IDX_DIM),
                  pl.BlockSpec((16, tq), lambda b, i: (0, b * nq + i)),
                  per_batch(LANES), per_batch(LANES), per_batch(kvw),
                  pl.BlockSpec((seq // KEY_TILE, N_KV_HEADS * VT_ROWS, KEY_TILE), lambda b, i: (b, 0, 0),
                               pipeline_mode=pl.Buffered(1))],
        out_specs=qrow(N_HEADS * HEAD_DIM),
        out_shape=jax.ShapeDtypeStruct((batch * seq, N_HEADS * HEAD_DIM), F32),
        scratch_shapes=[pltpu.VMEM((seq, tq), jnp.int32),
                        pltpu.VMEM((seq, tq), jnp.int16),
                        pltpu.VMEM((seq, tq), jnp.int16),
                        pltpu.VMEM((seq // FOLD, tq), jnp.int16),
                        pltpu.VMEM((seq // FOLD, tq), jnp.int16),
                        pltpu.VMEM((8, tq), jnp.int32),
                        pltpu.VMEM((N_HEADS, VT_ROWS, tq), F32),
                        pltpu.VMEM((N_HEADS, tq), F32),
                        pltpu.VMEM((N_HEADS, tq), F32),
                        pltpu.VMEM((2, N_HEADS, KEY_TILE, tq), F32),
                        pltpu.VMEM((2, N_HEADS, KEY_TILE, tq), BF16),
                        pltpu.VMEM((2, KEY_TILE, tq), F32)],
        compiler_params=pltpu.CompilerParams(dimension_semantics=("arbitrary", "arbitrary"),
                                             vmem_limit_bytes=VMEM_LIMIT),
        name="attn_prompt",
    )(q, qi, wit, kze, kzo, kb, vt)


def _attn_sample_kernel(q_ref, qi_ref, wi_ref, ck_ref, cv_ref, cki_ref, kn_ref, vn_ref, kin_ref,
                        o_ref, keyp_ref, keyn_ref, keyt_ref, biasp_ref, *, k_top, past_len, tn):
    group = LANES // tn
    rows = group * tn
    n_cols = past_len // LANES
    kn = kn_ref[...]
    vn = vn_ref[...]
    kin = kin_ref[0].astype(BF16)

    lane = lax.broadcasted_iota(jnp.int32, (tn, LANES), 1)
    pos = lax.broadcasted_iota(jnp.int32, (tn, LANES), 0)
    vis = ((past_len + lane % tn) // CHUNK) <= ((past_len + pos) // CHUNK)
    for j in range(group):
        qj = qi_ref[j].reshape(N_IDX_HEADS * tn, IDX_DIM)
        dp = lax.dot_general(qj, cki_ref[0, j].astype(BF16), NT_DIMS, preferred_element_type=F32)
        dn = lax.dot_general(qj, kin, NT_DIMS, preferred_element_type=F32)
        sp = jnp.zeros((tn, past_len), F32)
        sn = jnp.zeros((tn, LANES), F32)
        for h in range(N_IDX_HEADS):
            w = wi_ref[j * tn:(j + 1) * tn, h:h + 1]
            sp = sp + w * jnp.maximum(dp[h * tn:(h + 1) * tn, :], 0.0)
            sn = sn + w * jnp.maximum(dn[h * tn:(h + 1) * tn, :], 0.0)
        keyp_ref[j * tn:(j + 1) * tn, :] = _order_key(sp)
        own = (lane // tn) == j
        keyn_ref[j * tn:(j + 1) * tn, :] = jnp.where(own & vis, _order_key(sn), INT_MIN)

    for c in range(n_cols):
        keyt_ref[c * LANES:(c + 1) * LANES, :] = keyp_ref[:, c * LANES:(c + 1) * LANES].T
    keyt_ref[past_len:past_len + LANES, :] = keyn_ref[...].T
    n_keys = past_len + LANES

    def bisect_body(_, carry):
        lo, hi = carry
        mid = _midpoint(lo, hi)
        ge = (keyt_ref[...] >= mid).astype(jnp.int32)
        cnt = jnp.sum(jnp.sum(ge.reshape(n_keys // 8, 8, rows), axis=0), axis=0, keepdims=True)
        ok = cnt >= k_top
        return jnp.where(ok, mid, lo), jnp.where(ok, hi, mid)

    thr, _ = lax.fori_loop(
        0, 32, bisect_body,
        (jnp.full((1, rows), INT_MIN + 1, jnp.int32), jnp.full((1, rows), INT_MAX, jnp.int32)))
    thr_col = jnp.broadcast_to(thr, (rows, rows)).T

    key_row = lax.broadcasted_iota(jnp.int32, (n_keys, 1), 0)

    def count_keys(pred):
        hit = pred(keyt_ref[...], key_row).astype(jnp.int32)
        return jnp.sum(jnp.sum(hit.reshape(n_keys // 8, 8, rows), axis=0), axis=0, keepdims=True)

    n_selected = count_keys(lambda key, row: key >= thr)

    @pl.when(jnp.max(n_selected) > k_top)
    def _():
        need_eq = k_top - count_keys(lambda key, row: key > thr)

        def body(_, carry):
            lo, hi = carry
            mid = (lo + hi) >> 1
            ok = count_keys(lambda key, row: (key == thr) & (row < mid)) >= need_eq
            return jnp.where(ok, lo, mid), jnp.where(ok, mid, hi)

        _, cut = lax.fori_loop(0, n_keys.bit_length(), body,
                               (jnp.zeros((1, rows), jnp.int32), jnp.full((1, rows), n_keys, jnp.int32)))
        cut_col = jnp.broadcast_to(cut, (rows, rows)).T
        lane_idx = lax.broadcasted_iota(jnp.int32, (rows, LANES), 1)
        for c in range(n_cols + 1):
            ref, cols = (keyp_ref, slice(c * LANES, (c + 1) * LANES)) if c < n_cols else (keyn_ref, slice(None))
            key = ref[:, cols]
            surplus = (key == thr_col) & (c * LANES + lane_idx >= cut_col)
            ref[:, cols] = jnp.where(surplus, key - 1, key)

    bias_n = jnp.where(keyn_ref[...] >= thr_col, 0.0, MASK_BIAS)
    for c in range(n_cols):
        cols = slice(c * LANES, (c + 1) * LANES)
        biasp_ref[:, cols] = jnp.where(keyp_ref[:, cols] >= thr_col, 0.0, MASK_BIAS)

    for j in range(group):
        bias_pj = biasp_ref[j * tn:(j + 1) * tn, :]
        bias_nj = bias_n[j * tn:(j + 1) * tn, :]
        for n in range(N_KV_HEADS):
            hs = slice(n * HEAD_DIM, (n + 1) * HEAD_DIM)
            k_n = ck_ref[0, j, pl.ds(n, past_len, stride=N_KV_HEADS), :].astype(BF16)
            v_n = cv_ref[0, j, pl.ds(n, past_len, stride=N_KV_HEADS), :].astype(BF16)
            qg = jnp.concatenate(
                [q_ref[j * tn:(j + 1) * tn, (n * KV_GROUP + g) * HEAD_DIM:(n * KV_GROUP + g + 1) * HEAD_DIM]
                 for g in range(KV_GROUP)], axis=0)
            s1 = lax.dot_general(qg, k_n, NT_DIMS, preferred_element_type=F32)
            s2 = lax.dot_general(qg, kn[:, hs], NT_DIMS, preferred_element_type=F32)
            s1 = (s1.reshape(KV_GROUP, tn, past_len) + bias_pj[None]).reshape(KV_GROUP * tn, past_len)
            s2 = (s2.reshape(KV_GROUP, tn, LANES) + bias_nj[None]).reshape(KV_GROUP * tn, LANES)
            m = jnp.maximum(jnp.max(s1, axis=1, keepdims=True), jnp.max(s2, axis=1, keepdims=True))
            p1 = jnp.exp2(s1 - m)
            p2 = jnp.exp2(s2 - m)
            l = jnp.sum(p1, axis=1, keepdims=True) + jnp.sum(p2, axis=1, keepdims=True)
            o = (jnp.dot(p1.astype(BF16), v_n, preferred_element_type=F32)
                 + jnp.dot(p2.astype(BF16), vn[:, hs], preferred_element_type=F32)) / l
            for g in range(KV_GROUP):
                h = n * KV_GROUP + g
                o_ref[j * tn:(j + 1) * tn, h * HEAD_DIM:(h + 1) * HEAD_DIM] = o[g * tn:(g + 1) * tn, :]


def _attn_sample(q, qi4, wi, ck, cv, cki, kb, vb, kif, layer, k_top):
    nb = ck.shape[1]
    past_len = cki.shape[2]
    kvw = N_KV_HEADS * HEAD_DIM
    tn = q.shape[0] // nb
    assert LANES % tn == 0 and tn % 16 == 0 and past_len % LANES == 0
    group = LANES // tn
    assert nb % group == 0
    row = lambda w: pl.BlockSpec((LANES, w), lambda i: (i, 0))
    cache = lambda rows, w: pl.BlockSpec((1, group, rows, w), lambda i: (layer, i, 0, 0))
    return pl.pallas_call(
        functools.partial(_attn_sample_kernel, k_top=k_top, past_len=past_len, tn=tn),
        grid=(nb // group,),
        in_specs=[row(N_HEADS * HEAD_DIM),
                  pl.BlockSpec((group, N_IDX_HEADS, tn, IDX_DIM), lambda i: (i, 0, 0, 0)),
                  row(LANES), cache(N_KV_HEADS * past_len, HEAD_DIM),
                  cache(N_KV_HEADS * past_len, HEAD_DIM), cache(past_len, IDX_DIM),
                  row(kvw), row(kvw),
                  pl.BlockSpec((1, LANES, IDX_DIM), lambda i: (layer, i, 0))],
        out_specs=row(N_HEADS * HEAD_DIM),
        out_shape=jax.ShapeDtypeStruct((nb * tn, N_HEADS * HEAD_DIM), F32),
        scratch_shapes=[pltpu.VMEM((LANES, past_len), jnp.int32),
                        pltpu.VMEM((LANES, LANES), jnp.int32),
                        pltpu.VMEM((past_len + LANES, LANES), jnp.int32),
                        pltpu.VMEM((LANES, past_len), F32)],
        compiler_params=pltpu.CompilerParams(dimension_semantics=("arbitrary",),
                                             vmem_limit_bytes=VMEM_LIMIT),
        name="attn_sample",
    )(q, qi4, wi, ck, cv, cki, kb, vb, kif)


def _outffn_kernel(x_ref, a_ref, sga_ref, sgb_ref, u_ref, vn_ref, wmix_ref, bmix_ref, wo_ref,
                   g2_ref, wg_ref, wu_ref, wd_ref, gf_ref, xo_ref, *rest, final):
    if final:
        y_ref, z_ref = rest
    else:
        (z_ref,) = rest
    tm = x_ref.shape[0]
    group_dim = wmix_ref.shape[-1]
    for c in range(tm // SGU_CHUNK):
        rs = slice(c * SGU_CHUNK, (c + 1) * SGU_CHUNK)
        for g in range(SGU_GROUPS):
            cs = slice(g * group_dim, (g + 1) * group_dim)
            mixed = jnp.dot(wmix_ref[0, g], vn_ref[rs, cs].astype(BF16),
                            preferred_element_type=F32) + bmix_ref[0, :, cs]
            z = sga_ref[rs, cs] * a_ref[rs, cs] + sgb_ref[rs, cs] * (u_ref[rs, cs] * mixed)
            z_ref[rs, cs] = z.astype(BF16)
    x1 = x_ref[...] + jnp.dot(z_ref[...], wo_ref[0], preferred_element_type=F32)
    h2 = _rms_norm(x1, g2_ref[0]).astype(BF16)
    gate = jnp.dot(h2, wg_ref[0], preferred_element_type=F32)
    up = jnp.dot(h2, wu_ref[0], preferred_element_type=F32)
    ff = (gate * _sigmoid(gate) * up).astype(BF16)
    x2 = x1 + jnp.dot(ff, wd_ref[0], preferred_element_type=F32)
    xo_ref[...] = x2
    if final:
        y_ref[...] = _rms_norm(x2, gf_ref[...])


def _outffn(x, a, sga, sgb, u, vn, wmix, bmix, layer, wo, g2, wg, wu, wd, gf, final):
    T, D = x.shape
    tm = TOKEN_TILE
    row = pl.BlockSpec((tm, D), lambda i: (i, 0))
    n_out = 2 if final else 1
    out = pl.pallas_call(
        functools.partial(_outffn_kernel, final=final),
        grid=(T // tm,),
        in_specs=[row] * 6 + [_layer_spec(wmix.shape, layer), _layer_spec(bmix.shape, layer),
                              _layer_spec(wo.shape, layer), _layer_spec(g2.shape, layer),
                              _layer_spec(wg.shape, layer), _layer_spec(wu.shape, layer),
                              _layer_spec(wd.shape, layer), _const_spec((1, D))],
        out_specs=(row,) * n_out,
        out_shape=(jax.ShapeDtypeStruct((T, D), F32),) * n_out,
        scratch_shapes=[pltpu.VMEM((tm, D), BF16)],
        compiler_params=pltpu.CompilerParams(dimension_semantics=("arbitrary",),
                                             vmem_limit_bytes=VMEM_LIMIT),
        name="outffn",
    )(x, a, sga, sgb, u, vn, wmix, bmix, wo, g2, wg, wu, wd, gf)
    return out if final else (out[0], None)


def _rope_tables(pos, d):
    inv = ROPE_THETA ** (-jnp.arange(0, d, 2, dtype=F32) / d)
    ang = pos.astype(F32)[:, None] * inv[None, :]
    cos, sin = jnp.cos(ang), jnp.sin(ang)
    c = jnp.concatenate([cos, cos], axis=-1)
    s = jnp.concatenate([-sin, sin], axis=-1)
    reps = LANES // d
    return jnp.tile(c, (1, reps)), jnp.tile(s, (1, reps))


def _mix_weights(sgu_w, sgu_b, n, group_dim):
    depth = sgu_w.shape[0]
    p = jnp.arange(n)
    mask = (p[None, :] // CHUNK) <= (p[:, None] // CHUNK)
    w = jnp.where(mask[None, None], sgu_w[:, :, :n, :n], 0.0)
    reps = SGU_CHUNK // n
    eye = jnp.eye(reps, dtype=w.dtype)
    wbd = jnp.einsum('ab,lgij->lgaibj', eye, w).reshape(depth, SGU_GROUPS, SGU_CHUNK, SGU_CHUNK)
    b = jnp.tile(sgu_b[:, :, :n], (1, 1, reps))
    bfull = jnp.repeat(jnp.swapaxes(b, 1, 2), group_dim, axis=2)
    return wbd.astype(BF16), bfull


def _in_weights(w_in, d_model):
    kvw = N_KV_HEADS * HEAD_DIM
    w = w_in
    c_ki = d_model + 2 * kvw + N_IDX_HEADS * IDX_DIM
    c_wi = c_ki + IDX_DIM
    c_u = c_wi + N_IDX_HEADS
    wki, wwi = w[:, :, c_ki:c_wi], w[:, :, c_wi:c_u]
    wwi_pad = jnp.pad(wwi, ((0, 0), (0, 0), (0, LANES - N_IDX_HEADS)))
    wa = jnp.concatenate([w[:, :, :c_ki], wki, wki, wwi_pad, w[:, :, c_u:]], axis=2)
    wv_t = jnp.swapaxes(w[:, :, d_model + kvw:d_model + 2 * kvw], 1, 2)
    wwi_t = jnp.pad(jnp.swapaxes(wwi, 1, 2), ((0, 0), (0, 16 - N_IDX_HEADS), (0, 0)))
    return wa.astype(BF16), jnp.concatenate([wv_t, wwi_t], axis=1).astype(BF16)


def kernel(x_prompt, x_sample, cache_k, cache_v, cache_kidx, norm1_g, w_in, ln_v_g, ln_v_b, sgu_w, sgu_b, w_out, norm2_g, w_gate, w_up, w_down, final_norm_g):
    B, S, D = x_prompt.shape
    NB, TN, _ = x_sample.shape
    depth, _, P = cache_k.shape[:3]
    assert D == N_HEADS * HEAD_DIM and SGU_CHUNK % TN == 0 and S % SGU_CHUNK == 0
    k_top_p = min(TOPK_MAX, S // 4)
    k_top_s = min(TOPK_MAX, (P + TN) // 4)
    group_dim = D // SGU_GROUPS

    assert S % TOKEN_TILE == 0 and TOKEN_TILE % TN == 0
    pos_p = jnp.arange(S)
    pos_s = jnp.tile(P + jnp.arange(TN), TOKEN_TILE // TN)
    tab_p = _rope_tables(pos_p, HEAD_DIM) + _rope_tables(pos_p, IDX_DIM)
    tab_s = _rope_tables(pos_s, HEAD_DIM) + _rope_tables(pos_s, IDX_DIM)

    wa, wb = _in_weights(w_in, D)
    wo, wg, wu, wd = (t.astype(BF16) for t in (w_out, w_gate, w_up, w_down))
    vec = lambda v: v.reshape(depth, 1, -1)
    g1, g2, lng, lnb = vec(norm1_g), vec(norm2_g), vec(ln_v_g), vec(ln_v_b)
    gf = final_norm_g.reshape(1, -1)
    wmix_p, bmix_p = _mix_weights(sgu_w, sgu_b, SGU_CHUNK, group_dim)
    wmix_s, bmix_s = _mix_weights(sgu_w, sgu_b, TN, group_dim)
    ck = cache_k.reshape(depth, NB, P * N_KV_HEADS, HEAD_DIM)
    cv = cache_v.reshape(depth, NB, P * N_KV_HEADS, HEAD_DIM)

    xp = x_prompt.reshape(B * S, D)
    xs = x_sample.reshape(NB * TN, D)
    def new_buffers(tokens):
        kv = (depth, N_KV_HEADS * tokens, HEAD_DIM)
        return jnp.zeros(kv, F32), jnp.zeros(kv, F32), jnp.zeros((depth, tokens, IDX_DIM), F32)

    new_p, new_s = new_buffers(B * S), new_buffers(NB * TN)
    sgu_v = []
    yp = ys = None
    for l in range(depth):
        final = l == depth - 1

        (q, kf, kb, vf, _, vt, qi, kif, kze, kzo, _, wit, u, vn, sga, sgb) = _inproj(
            xp, l, g1, wa, wb, tab_p, lng, lnb, new_p)
        new_p = (kf, vf, kif)
        a = _attn_prompt(q, qi, wit, kze, kzo, kb, vt, B, S, k_top_p)
        xp, yp = _outffn(xp, a, sga, sgb, u, vn, wmix_p, bmix_p, l, wo, g2, wg, wu, wd, gf, final)

        (q, kf, kb, vf, vb, _, qi, kif, _, _, wi, _, u, vn, sga, sgb) = _inproj(
            xs, l, g1, wa, wb, tab_s, lng, lnb, new_s)
        new_s = (kf, vf, kif)
        qi4 = qi.reshape(NB, TN, N_IDX_HEADS, IDX_DIM).transpose(0, 2, 1, 3)
        a = _attn_sample(q, qi4, wi, ck, cv, cache_kidx, kb, vb, kif, l, k_top_s)
        xs, ys = _outffn(xs, a, sga, sgb, u, vn, wmix_s, bmix_s, l, wo, g2, wg, wu, wd, gf, final)
        sgu_v.append(vn.reshape(NB, TN, D))

    kv_p = (depth, B, S, N_KV_HEADS, HEAD_DIM)
    kv_s = (depth, NB, TN, N_KV_HEADS, HEAD_DIM)
    return (yp.reshape(B, S, D), ys.reshape(NB, TN, D),
            new_p[0].reshape(kv_p), new_p[1].reshape(kv_p), new_p[2].reshape(depth, B, S, IDX_DIM),
            new_s[0].reshape(kv_s), new_s[1].reshape(kv_s), new_s[2].reshape(depth, NB, TN, IDX_DIM),
            jnp.stack(sgu_v))
```

```python
import functools
import math

import jax
import jax.numpy as jnp
from jax import lax
from jax.experimental import pallas as pl
from jax.experimental.pallas import tpu as pltpu

CHUNK = 64
N_HEADS = 8
HEAD_DIM = 128
N_KV_HEADS = 2
KV_GROUP = N_HEADS // N_KV_HEADS
N_IDX_HEADS = 8
IDX_DIM = 64
TOPK_MAX = 256
SGU_CHUNK = 128
SGU_GROUPS = 8
ROPE_THETA = 10000.0
EPS = 1e-6

LANES = 128
TOKEN_TILE = 256
Q_TILE = 256
KEY_TILE = 256
VT_ROWS = HEAD_DIM + 16
FOLD = 8
WINDOW_BITS = 9
VMEM_LIMIT = 56 * 1024 * 1024

INT_MIN = -(2 ** 31)
INT_MAX = 2 ** 31 - 1
MASK_BIAS = -1e30
LOG2E = 1.4426950408889634
Q_SCALE = (HEAD_DIM ** -0.5) * LOG2E

F32 = jnp.float32
BF16 = jnp.bfloat16
NT_DIMS = (((1,), (1,)), ((), ()))


def _const_spec(shape):
    nd = len(shape)
    return pl.BlockSpec(shape, lambda *_: (0,) * nd, pipeline_mode=pl.Buffered(1))


def _sigmoid(x):
    return 1.0 / (1.0 + jnp.exp(-x))


def _rms_norm(x, g):
    return x * lax.rsqrt(jnp.mean(x * x, axis=-1, keepdims=True) + EPS) * g


def _order_key(score):
    bits = pltpu.bitcast(score, jnp.int32)
    return bits ^ ((bits >> 31) & INT_MAX)


def _midpoint(lo, hi):
    return (lo >> 1) + (hi >> 1) + (lo & hi & 1)


_C_Q = 0
_C_K = _C_Q + N_HEADS * HEAD_DIM
_C_V = _C_K + N_KV_HEADS * HEAD_DIM
_C_QI = _C_V + N_KV_HEADS * HEAD_DIM
_C_KI = _C_QI + N_IDX_HEADS * IDX_DIM
_C_WI = _C_KI + LANES
_C_U = _C_WI + LANES


def _inproj_kernel(*refs, d_model, n_alias):
    (x_ref, g_ref, wa_ref, wb_ref, cosh_ref, sinh_ref, cosi_ref, sini_ref,
     lng_ref, lnb_ref) = refs[:10]
    (q_ref, kf_ref, kb_ref, vf_ref, vb_ref, vt_ref, qi_ref, kif_ref, kze_ref, kzo_ref,
     wi_ref, wit_ref, u_ref, vn_ref, sga_ref, sgb_ref) = refs[10 + n_alias:]
    tm = x_ref.shape[0]
    hb = _rms_norm(x_ref[...], g_ref[0]).astype(BF16)

    def proj(c0, width):
        return jnp.dot(hb, wa_ref[0, :, c0:c0 + width], preferred_element_type=F32)

    def head_rows(n):
        return pl.ds(n, tm, stride=N_KV_HEADS)

    cosh, sinh = cosh_ref[...], sinh_ref[...]
    cosi, sini = cosi_ref[...], sini_ref[...]
    lane = lax.broadcasted_iota(jnp.int32, cosi.shape, 1)
    first_half = (lane % IDX_DIM) < (IDX_DIM // 2)

    def rope_head(x):
        return x * cosh + pltpu.roll(x, HEAD_DIM // 2, 1) * sinh

    def rope_idx(x):
        partner = jnp.where(first_half, pltpu.roll(x, LANES - IDX_DIM // 2, 1),
                            pltpu.roll(x, IDX_DIM // 2, 1))
        return x * cosi + partner * sini

    xq = proj(_C_Q, N_HEADS * HEAD_DIM)
    for h in range(N_HEADS):
        sl = slice(h * HEAD_DIM, (h + 1) * HEAD_DIM)
        q_ref[:, sl] = (rope_head(xq[:, sl]) * Q_SCALE).astype(BF16)

    xk = proj(_C_K, N_KV_HEADS * HEAD_DIM)
    for h in range(N_KV_HEADS):
        sl = slice(h * HEAD_DIM, (h + 1) * HEAD_DIM)
        kr = rope_head(xk[:, sl])
        kf_ref[0, head_rows(h), :] = kr
        kb_ref[:, sl] = kr.astype(BF16)

    xv = proj(_C_V, N_KV_HEADS * HEAD_DIM)
    vb_ref[...] = xv.astype(BF16)
    for h in range(N_KV_HEADS):
        vf_ref[0, head_rows(h), :] = xv[:, h * HEAD_DIM:(h + 1) * HEAD_DIM]

    xqi = proj(_C_QI, N_IDX_HEADS * IDX_DIM)
    for p in range(N_IDX_HEADS * IDX_DIM // LANES):
        sl = slice(p * LANES, (p + 1) * LANES)
        qi_ref[:, sl] = (rope_idx(xqi[:, sl]) * (IDX_DIM ** -0.5)).astype(BF16)

    kk = rope_idx(proj(_C_KI, LANES))
    kif_ref[0] = kk[:, :IDX_DIM]
    low = lane < IDX_DIM
    kze_ref[...] = jnp.where(low, kk, 0.0).astype(BF16)
    kzo_ref[...] = jnp.where(low, 0.0, kk).astype(BF16)

    wi_ref[...] = proj(_C_WI, LANES) * (N_IDX_HEADS ** -0.5)

    tb = lax.dot_general(wb_ref[0], hb, NT_DIMS, preferred_element_type=F32)
    kvw = N_KV_HEADS * HEAD_DIM
    ones_rows = (lax.broadcasted_iota(jnp.int32, (VT_ROWS - HEAD_DIM, tm), 0) == 0).astype(BF16)
    for n in range(N_KV_HEADS):
        vt_ref[0, n * VT_ROWS:n * VT_ROWS + HEAD_DIM, :] = (
            tb[n * HEAD_DIM:(n + 1) * HEAD_DIM].astype(BF16))
        vt_ref[0, n * VT_ROWS + HEAD_DIM:(n + 1) * VT_ROWS, :] = ones_rows
    wit_ref[...] = tb[kvw:] * (N_IDX_HEADS ** -0.5)

    u_ref[...] = jax.nn.gelu(proj(_C_U, d_model), approximate=True)
    gv = jax.nn.gelu(proj(_C_U + d_model, d_model), approximate=True)
    mu = jnp.mean(gv, axis=-1, keepdims=True)
    dv = gv - mu
    var = jnp.mean(dv * dv, axis=-1, keepdims=True)
    vn_ref[...] = dv * lax.rsqrt(var + EPS) * lng_ref[0] + lnb_ref[0]
    sga_ref[...] = _sigmoid(proj(_C_U + 2 * d_model, d_model))
    sgb_ref[...] = _sigmoid(proj(_C_U + 3 * d_model, d_model))


def _layer_spec(shape, layer):
    nd = len(shape)
    return pl.BlockSpec((1,) + tuple(shape[1:]), lambda *_: (layer,) + (0,) * (nd - 1),
                        pipeline_mode=pl.Buffered(1))


def _inproj(x, layer, g, wa, wb, tables, lng, lnb, stacked):
    T, D = x.shape
    depth = wa.shape[0]
    tm = TOKEN_TILE
    assert T % tm == 0
    kvw = N_KV_HEADS * HEAD_DIM
    qiw = N_IDX_HEADS * IDX_DIM
    row = lambda w: pl.BlockSpec((tm, w), lambda i: (i, 0))
    kv_rows = N_KV_HEADS * T
    out_shape = (
        jax.ShapeDtypeStruct((T, N_HEADS * HEAD_DIM), BF16),
        jax.ShapeDtypeStruct((depth, kv_rows, HEAD_DIM), F32),
        jax.ShapeDtypeStruct((T, kvw), BF16),
        jax.ShapeDtypeStruct((depth, kv_rows, HEAD_DIM), F32),
        jax.ShapeDtypeStruct((T, kvw), BF16),
        jax.ShapeDtypeStruct((T // tm, N_KV_HEADS * VT_ROWS, tm), BF16),
        jax.ShapeDtypeStruct((T, qiw), BF16),
        jax.ShapeDtypeStruct((depth, T, IDX_DIM), F32),
        jax.ShapeDtypeStruct((T, LANES), BF16),
        jax.ShapeDtypeStruct((T, LANES), BF16),
        jax.ShapeDtypeStruct((T, LANES), F32),
        jax.ShapeDtypeStruct((16, T), F32),
        jax.ShapeDtypeStruct((T, D), F32),
        jax.ShapeDtypeStruct((T, D), F32),
        jax.ShapeDtypeStruct((T, D), F32),
        jax.ShapeDtypeStruct((T, D), F32),
    )
    kv_spec = pl.BlockSpec((1, N_KV_HEADS * tm, HEAD_DIM), lambda i: (layer, i, 0))
    out_specs = (
        row(N_HEADS * HEAD_DIM), kv_spec, row(kvw), kv_spec, row(kvw),
        pl.BlockSpec((1, N_KV_HEADS * VT_ROWS, tm), lambda i: (i, 0, 0)),
        row(qiw), pl.BlockSpec((1, tm, IDX_DIM), lambda i: (layer, i, 0)),
        row(LANES), row(LANES), row(LANES),
        pl.BlockSpec((16, tm), lambda i: (0, i)),
        row(D), row(D), row(D), row(D),
    )
    n_tab = tables[0].shape[0] // tm
    tab = pl.BlockSpec((tm, LANES), lambda i: (i % n_tab, 0))
    in_specs = [row(D), _layer_spec(g.shape, layer), _layer_spec(wa.shape, layer),
                _layer_spec(wb.shape, layer),
                tab, tab, tab, tab,
                _layer_spec(lng.shape, layer), _layer_spec(lnb.shape, layer)]
    args = [x, g, wa, wb, *tables, lng, lnb]
    aliases = {}
    for j, out_idx in enumerate((1, 3, 7)):
        aliases[len(args)] = out_idx
        in_specs.append(pl.BlockSpec(memory_space=pl.ANY))
        args.append(stacked[j])
    return pl.pallas_call(
        functools.partial(_inproj_kernel, d_model=D, n_alias=len(aliases)),
        grid=(T // tm,),
        in_specs=in_specs, out_specs=out_specs, out_shape=out_shape,
        input_output_aliases=aliases,
        compiler_params=pltpu.CompilerParams(dimension_semantics=("arbitrary",),
                                             vmem_limit_bytes=VMEM_LIMIT),
        name="inproj",
    )(*args)


def _attn_prompt_kernel(q_ref, qi_ref, wit_ref, kze_ref, kzo_ref, k_ref, vt_ref, o_ref,
                        keys_ref, hi_ref, lo_ref, f1_ref, f2_ref, kmax_ref, acc_ref, m_ref, mx_ref, s_ref, p_ref,
                        bias_ref, *, k_top):
    tq, kb_sz = Q_TILE, KEY_TILE
    qb = pl.program_id(1)
    nkb = (qb + 1) * (tq // kb_sz)
    q_chunk = (qb * tq + lax.broadcasted_iota(jnp.int32, (1, tq), 1)) // CHUNK
    row_iota = lax.broadcasted_iota(jnp.int32, (kb_sz, 1), 0)

    def key_slice(kb):
        return pl.ds(pl.multiple_of(kb * kb_sz, kb_sz), kb_sz)

    n_blocks = keys_ref.shape[0] // kb_sz
    n_full = nkb // 2

    def clamp(blk):
        return jnp.minimum(blk, n_blocks - 1)

    def idx_dots(blk, slot):
        ks = key_slice(blk)
        kze, kzo = kze_ref[ks, :], kzo_ref[ks, :]
        for p in range(N_IDX_HEADS // 2):
            qp = qi_ref[:, p * LANES:(p + 1) * LANES]
            s_ref[slot, 2 * p] = lax.dot_general(kze, qp, NT_DIMS, preferred_element_type=F32)
            s_ref[slot, 2 * p + 1] = lax.dot_general(kzo, qp, NT_DIMS, preferred_element_type=F32)

    def idx_keys(blk, slot):
        base = pl.multiple_of(blk * kb_sz, kb_sz)
        kmax = kmax_ref[...]
        for c in range(kb_sz // CHUNK):
            rows = slice(c * CHUNK, (c + 1) * CHUNK)
            score = jnp.zeros((CHUNK, tq), F32)
            for j in range(N_IDX_HEADS):
                score = score + wit_ref[j:j + 1, :] * jnp.maximum(s_ref[slot, j, rows, :], 0.0)
            visible = (blk * (kb_sz // CHUNK) + c) <= q_chunk
            key = jnp.where(visible, _order_key(score), INT_MIN)
            ks = pl.ds(base + c * CHUNK, CHUNK)
            keys_ref[ks, :] = key
            hi_ref[ks, :] = (key >> 16).astype(jnp.int16)
            lo_ref[ks, :] = (key ^ 0x8000).astype(jnp.int16)
            kmax = jnp.maximum(kmax, jnp.max(key.reshape(CHUNK // 8, 8, tq), axis=0))
        kmax_ref[...] = kmax

    def score_body(j, carry):
        b0 = 2 * j
        idx_dots(b0 + 1, 1)
        idx_keys(b0, 0)
        idx_dots(clamp(b0 + 2), 0)
        idx_keys(b0 + 1, 1)
        return carry

    kmax_ref[...] = jnp.full(kmax_ref.shape, INT_MIN, jnp.int32)
    idx_dots(0, 0)
    lax.fori_loop(0, n_full, score_body, 0)

    @pl.when(nkb % 2 == 1)
    def _():
        idx_keys(nkb - 1, 0)
        keys_ref[key_slice(nkb), :] = jnp.full((kb_sz, tq), INT_MIN, jnp.int32)

    n_acc = 4
    rows16 = 16

    def fold_rows(x, accs):
        accs = list(accs)
        for r in range(kb_sz // rows16):
            accs[r % n_acc] = accs[r % n_acc] + x[r * rows16:(r + 1) * rows16, :]
        return tuple(accs)

    def zero_accs():
        return tuple(jnp.zeros((rows16, tq), jnp.int16) for _ in range(n_acc))

    def total(accs):
        tot = accs[0].astype(jnp.int32)
        for a in accs[1:]:
            tot = tot + a.astype(jnp.int32)
        return jnp.sum(tot, axis=0, keepdims=True)

    def count_ge(plane_ref, t):
        t16 = jnp.broadcast_to(t, (kb_sz, tq)).astype(jnp.int16)

        def body(kb, accs):
            ge = (plane_ref[key_slice(kb), :] >= t16).astype(jnp.int16)
            return fold_rows(ge, accs)

        return total(lax.fori_loop(0, nkb, body, zero_accs()))

    def bisect16(count, need, lo, hi, count_lo, steps):
        def body(_, carry):
            lo, hi, n_lo = carry
            mid = (lo + hi) >> 1
            n_mid = count(mid)
            ok = n_mid >= need
            return jnp.where(ok, mid, lo), jnp.where(ok, hi, mid), jnp.where(ok, n_mid, n_lo)

        lo, _, n_lo = lax.fori_loop(0, steps, body, (lo, hi, count_lo))
        return lo, n_lo

    floor16 = jnp.full((1, tq), -(2 ** 15), jnp.int32)
    ceil16 = jnp.full((1, tq), 2 ** 15, jnp.int32)
    every = jnp.full((1, tq), keys_ref.shape[0], jnp.int32)
    count_hi = functools.partial(count_ge, hi_ref)
    top16 = (jnp.max(kmax_ref[...], axis=0, keepdims=True) >> 16) + 1
    lo_try = jnp.maximum(top16 - 2 ** WINDOW_BITS, -(2 ** 15) + 1)
    n_try = count_hi(lo_try)
    in_window = n_try >= k_top
    steps = jnp.where(jnp.min(in_window.astype(jnp.int32)) > 0, WINDOW_BITS, 16)
    t_hi, n_hi = bisect16(count_hi, k_top, jnp.where(in_window, lo_try, floor16), top16,
                          jnp.where(in_window, n_try, every), steps)

    t_hi16 = jnp.broadcast_to(t_hi, (kb_sz, tq)).astype(jnp.int16)
    sentinel = jnp.int16(-(2 ** 15))
    n_slots = kb_sz // rows16 // FOLD

    def max_min(a, b):
        a_ge = a >= b
        return jnp.where(a_ge, a, b), jnp.where(a_ge, b, a)

    def tie_body(kb, accs):
        ks = key_slice(kb)
        hi_part = hi_ref[ks, :]
        e = jnp.where(hi_part == t_hi16, lo_ref[ks, :], sentinel)
        lo_ref[ks, :] = e
        g = [e[r * rows16:(r + 1) * rows16, :] for r in range(kb_sz // rows16)]
        n = len(g) // 2
        pairs = [max_min(g[i], g[i + n]) for i in range(n)]
        first, second = [p[0] for p in pairs], [p[1] for p in pairs]
        while n > n_slots:
            n //= 2
            pairs = [max_min(first[i], first[i + n]) for i in range(n)]
            second = [max_min(pairs[i][1], max_min(second[i], second[i + n])[0])[0] for i in range(n)]
            first = [p[0] for p in pairs]
        fs = pl.ds(pl.multiple_of(kb * (n_slots * rows16), n_slots * rows16), n_slots * rows16)
        f1_ref[fs, :] = jnp.concatenate(first, axis=0)
        f2_ref[fs, :] = jnp.concatenate(second, axis=0)
        return fold_rows((hi_part > t_hi16).astype(jnp.int16), accs)

    n_above = total(lax.fori_loop(0, nkb, tie_body, zero_accs()))
    need_lo = k_top - n_above

    def count_folded(t):
        t16 = jnp.broadcast_to(t, (n_slots * rows16, tq)).astype(jnp.int16)

        def body(kb, accs):
            fs = pl.ds(pl.multiple_of(kb * (n_slots * rows16), n_slots * rows16), n_slots * rows16)
            accs = list(accs)
            for i, plane in enumerate((f1_ref, f2_ref)):
                ge = (plane[fs, :] >= t16).astype(jnp.int16)
                for r in range(n_slots):
                    a = (i * n_slots + r) % n_acc
                    accs[a] = accs[a] + ge[r * rows16:(r + 1) * rows16, :]
            return tuple(accs)

        return total(lax.fori_loop(0, nkb, body, zero_accs()))

    def count_lo_pair(t):
        t_next = jnp.minimum(t + 1, 2 ** 15 - 1)
        ta = jnp.broadcast_to(t, (kb_sz, tq)).astype(jnp.int16)
        tb = jnp.broadcast_to(t_next, (kb_sz, tq)).astype(jnp.int16)

        def body(kb, accs):
            plane = lo_ref[key_slice(kb), :]
            return (fold_rows((plane >= ta).astype(jnp.int16), accs[0]),
                    fold_rows((plane >= tb).astype(jnp.int16), accs[1]))

        a, b = lax.fori_loop(0, nkb, body, (zero_accs(), zero_accs()))
        return total(a), jnp.where(t < 2 ** 15 - 1, total(b), 0)

    t_lo, _ = bisect16(count_folded, need_lo, floor16, ceil16, every, 16)
    n_at, n_next = count_lo_pair(t_lo)

    def unfolded_search():
        t, _ = bisect16(functools.partial(count_ge, lo_ref), need_lo, floor16, ceil16, every, 16)
        return (t,) + count_lo_pair(t)

    t_lo, n_at, n_next = lax.cond(jnp.max((n_next >= need_lo).astype(jnp.int32)) > 0,
                                  unfolded_search, lambda: (t_lo, n_at, n_next))
    thr = jnp.maximum((t_hi << 16) + (t_lo + 2 ** 15), INT_MIN + 1)

    n_selected = jnp.where(t_hi > -(2 ** 15),
                           jnp.where(t_lo > -(2 ** 15), n_above + n_at, n_hi), 0)
    need_eq = (need_lo - n_next).astype(F32)
    col_iota = lax.broadcasted_iota(jnp.int32, (1, kb_sz), 1)

    for c in range(tq // LANES):
        cs = slice(c * LANES, (c + 1) * LANES)

        @pl.when(jnp.max(n_selected[:, cs]) > k_top)
        def _(cs=cs):
            tri = jnp.where(col_iota <= row_iota, 1.0, 0.0).astype(BF16)
            thr_c, need_c = thr[:, cs], need_eq[:, cs]

            unroll = 4

            def drop(j, seen):
                blocks = [unroll * j + i for i in range(unroll)]
                keys = [keys_ref[key_slice(clamp(b)), cs] for b in blocks]
                equal = [(key == thr_c) & (b < nkb) for b, key in zip(blocks, keys)]
                local = [jnp.dot(tri, jnp.where(e, 1.0, 0.0).astype(BF16),
                                 preferred_element_type=F32) for e in equal]
                for b, key, e, rank in zip(blocks, keys, equal, local):
                    rank = rank + seen
                    keys_ref[key_slice(clamp(b)), cs] = jnp.where(e & (rank > need_c), key - 1, key)
                    seen = rank[kb_sz - 1:kb_sz, :]
                return seen

            lax.fori_loop(0, (nkb + unroll - 1) // unroll, drop, jnp.zeros((1, LANES), F32))

    m_ref[...] = jnp.full(m_ref.shape, MASK_BIAS, F32)
    acc_ref[...] = jnp.zeros(acc_ref.shape, F32)

    rc = 32
    n_rc = kb_sz // rc

    def fold8(x):
        return x.reshape(rc // 8, 8, tq)

    def qk_dots(blk, slot):
        ks = key_slice(blk)
        for h in range(N_HEADS):
            n = h // KV_GROUP
            k_n = k_ref[ks, n * HEAD_DIM:(n + 1) * HEAD_DIM]
            q_h = q_ref[:, h * HEAD_DIM:(h + 1) * HEAD_DIM]
            s_ref[slot, h] = lax.dot_general(k_n, q_h, NT_DIMS, preferred_element_type=F32)

    def softmax_block(blk, slot):
        thr_b = jnp.where(blk < nkb, thr, INT_MAX)
        bias_ref[slot] = jnp.where(keys_ref[key_slice(clamp(blk)), :] >= thr_b, 0.0, MASK_BIAS)
        for h in range(N_HEADS):
            mx = jnp.full((8, tq), MASK_BIAS, F32)
            for r in range(n_rc):
                rows = slice(r * rc, (r + 1) * rc)
                sc = s_ref[slot, h, rows, :] + bias_ref[slot, rows, :]
                s_ref[slot, h, rows, :] = sc
                mx = jnp.maximum(mx, jnp.max(fold8(sc), axis=0))
            mx_ref[h:h + 1, :] = jnp.max(mx, axis=0, keepdims=True)
        m_old = m_ref[...]
        m_new = jnp.maximum(m_old, mx_ref[...])
        alpha = jnp.exp2(m_old - m_new)
        m_ref[...] = m_new
        for h in range(N_HEADS):
            m_h = jnp.broadcast_to(m_new[h:h + 1, :], (rc, tq))
            for r in range(n_rc):
                rows = slice(r * rc, (r + 1) * rc)
                p_ref[slot, h, rows, :] = jnp.exp2(s_ref[slot, h, rows, :] - m_h).astype(BF16)
        return alpha

    def pv_update(blk, slot, alpha):
        for h in range(N_HEADS):
            n = h // KV_GROUP
            vt_n = vt_ref[blk, n * VT_ROWS:(n + 1) * VT_ROWS, :]
            pv = jnp.dot(vt_n, p_ref[slot, h], preferred_element_type=F32)
            acc_ref[h] = alpha[h:h + 1, :] * acc_ref[h] + pv

    def attn_body(j, carry):
        b0 = 2 * j
        qk_dots(clamp(b0 + 1), 1)
        pv_update(b0, 0, softmax_block(b0, 0))
        qk_dots(clamp(b0 + 2), 0)
        pv_update(clamp(b0 + 1), 1, softmax_block(b0 + 1, 1))
        return carry

    qk_dots(0, 0)
    lax.fori_loop(0, (nkb + 1) // 2, attn_body, 0)

    for h in range(N_HEADS):
        o_t = acc_ref[h, :HEAD_DIM, :] / acc_ref[h, HEAD_DIM:HEAD_DIM + 1, :]
        o_ref[:, h * HEAD_DIM:(h + 1) * HEAD_DIM] = o_t.T


def _attn_prompt(q, qi, wit, kze, kzo, kb, vt, batch, seq, k_top):
    tq = Q_TILE
    assert seq % tq == 0 and Q_TILE % KEY_TILE == 0 and KEY_TILE == TOKEN_TILE
    assert (seq // KEY_TILE) % 4 == 0
    nq = seq // tq
    kvw = N_KV_HEADS * HEAD_DIM
    qrow = lambda w: pl.BlockSpec((tq, w), lambda b, i: (b * nq + i, 0))
    per_batch = lambda w: pl.BlockSpec((seq, w), lambda b, i: (b, 0), pipeline_mode=pl.Buffered(1))
    return pl.pallas_call(
        functools.partial(_attn_prompt_kernel, k_top=k_top),
        grid=(batch, nq),
        in_specs=[qrow(N_HEADS * HEAD_DIM), qrow(N_IDX_HEADS * IDX_DIM),
                  pl.BlockSpec((16, tq), lambda b, i: (0, b * nq + i)),
                  per_batch(LANES), per_batch(LANES), per_batch(kvw),
                  pl.BlockSpec((seq // KEY_TILE, N_KV_HEADS * VT_ROWS, KEY_TILE), lambda b, i: (b, 0, 0),
                               pipeline_mode=pl.Buffered(1))],
        out_specs=qrow(N_HEADS * HEAD_DIM),
        out_shape=jax.ShapeDtypeStruct((batch * seq, N_HEADS * HEAD_DIM), F32),
        scratch_shapes=[pltpu.VMEM((seq, tq), jnp.int32),
                        pltpu.VMEM((seq, tq), jnp.int16),
                        pltpu.VMEM((seq, tq), jnp.int16),
                        pltpu.VMEM((seq // FOLD, tq), jnp.int16),
                        pltpu.VMEM((seq // FOLD, tq), jnp.int16),
                        pltpu.VMEM((8, tq), jnp.int32),
                        pltpu.VMEM((N_HEADS, VT_ROWS, tq), F32),
                        pltpu.VMEM((N_HEADS, tq), F32),
                        pltpu.VMEM((N_HEADS, tq), F32),
                        pltpu.VMEM((2, N_HEADS, KEY_TILE, tq), F32),
                        pltpu.VMEM((2, N_HEADS, KEY_TILE, tq), BF16),
                        pltpu.VMEM((2, KEY_TILE, tq), F32)],
        compiler_params=pltpu.CompilerParams(dimension_semantics=("arbitrary", "arbitrary"),
                                             vmem_limit_bytes=VMEM_LIMIT),
        name="attn_prompt",
    )(q, qi, wit, kze, kzo, kb, vt)


def _attn_sample_kernel(q_ref, qi_ref, wi_ref, ck_ref, cv_ref, cki_ref, kn_ref, vn_ref, kin_ref,
                        o_ref, keyp_ref, keyn_ref, keyt_ref, biasp_ref, *, k_top, past_len, tn):
    group = LANES // tn
    rows = group * tn
    n_cols = past_len // LANES
    kn = kn_ref[...]
    vn = vn_ref[...]
    kin = kin_ref[0].astype(BF16)

    lane = lax.broadcasted_iota(jnp.int32, (tn, LANES), 1)
    pos = lax.broadcasted_iota(jnp.int32, (tn, LANES), 0)
    vis = ((past_len + lane % tn) // CHUNK) <= ((past_len + pos) // CHUNK)
    for j in range(group):
        qj = qi_ref[j].reshape(N_IDX_HEADS * tn, IDX_DIM)
        dp = lax.dot_general(qj, cki_ref[0, j].astype(BF16), NT_DIMS, preferred_element_type=F32)
        dn = lax.dot_general(qj, kin, NT_DIMS, preferred_element_type=F32)
        sp = jnp.zeros((tn, past_len), F32)
        sn = jnp.zeros((tn, LANES), F32)
        for h in range(N_IDX_HEADS):
            w = wi_ref[j * tn:(j + 1) * tn, h:h + 1]
            sp = sp + w * jnp.maximum(dp[h * tn:(h + 1) * tn, :], 0.0)
            sn = sn + w * jnp.maximum(dn[h * tn:(h + 1) * tn, :], 0.0)
        keyp_ref[j * tn:(j + 1) * tn, :] = _order_key(sp)
        own = (lane // tn) == j
        keyn_ref[j * tn:(j + 1) * tn, :] = jnp.where(own & vis, _order_key(sn), INT_MIN)

    for c in range(n_cols):
        keyt_ref[c * LANES:(c + 1) * LANES, :] = keyp_ref[:, c * LANES:(c + 1) * LANES].T
    keyt_ref[past_len:past_len + LANES, :] = keyn_ref[...].T
    n_keys = past_len + LANES

    def bisect_body(_, carry):
        lo, hi = carry
        mid = _midpoint(lo, hi)
        ge = (keyt_ref[...] >= mid).astype(jnp.int32)
        cnt = jnp.sum(jnp.sum(ge.reshape(n_keys // 8, 8, rows), axis=0), axis=0, keepdims=True)
        ok = cnt >= k_top
        return jnp.where(ok, mid, lo), jnp.where(ok, hi, mid)

    thr, _ = lax.fori_loop(
        0, 32, bisect_body,
        (jnp.full((1, rows), INT_MIN + 1, jnp.int32), jnp.full((1, rows), INT_MAX, jnp.int32)))
    thr_col = jnp.broadcast_to(thr, (rows, rows)).T

    key_row = lax.broadcasted_iota(jnp.int32, (n_keys, 1), 0)

    def count_keys(pred):
        hit = pred(keyt_ref[...], key_row).astype(jnp.int32)
        return jnp.sum(jnp.sum(hit.reshape(n_keys // 8, 8, rows), axis=0), axis=0, keepdims=True)

    n_selected = count_keys(lambda key, row: key >= thr)

    @pl.when(jnp.max(n_selected) > k_top)
    def _():
        need_eq = k_top - count_keys(lambda key, row: key > thr)

        def body(_, carry):
            lo, hi = carry
            mid = (lo + hi) >> 1
            ok = count_keys(lambda key, row: (key == thr) & (row < mid)) >= need_eq
            return jnp.where(ok, lo, mid), jnp.where(ok, mid, hi)

        _, cut = lax.fori_loop(0, n_keys.bit_length(), body,
                               (jnp.zeros((1, rows), jnp.int32), jnp.full((1, rows), n_keys, jnp.int32)))
        cut_col = jnp.broadcast_to(cut, (rows, rows)).T
        lane_idx = lax.broadcasted_iota(jnp.int32, (rows, LANES), 1)
        for c in range(n_cols + 1):
            ref, cols = (keyp_ref, slice(c * LANES, (c + 1) * LANES)) if c < n_cols else (keyn_ref, slice(None))
            key = ref[:, cols]
            surplus = (key == thr_col) & (c * LANES + lane_idx >= cut_col)
            ref[:, cols] = jnp.where(surplus, key - 1, key)

    bias_n = jnp.where(keyn_ref[...] >= thr_col, 0.0, MASK_BIAS)
    for c in range(n_cols):
        cols = slice(c * LANES, (c + 1) * LANES)
        biasp_ref[:, cols] = jnp.where(keyp_ref[:, cols] >= thr_col, 0.0, MASK_BIAS)

    for j in range(group):
        bias_pj = biasp_ref[j * tn:(j + 1) * tn, :]
        bias_nj = bias_n[j * tn:(j + 1) * tn, :]
        for n in range(N_KV_HEADS):
            hs = slice(n * HEAD_DIM, (n + 1) * HEAD_DIM)
            k_n = ck_ref[0, j, pl.ds(n, past_len, stride=N_KV_HEADS), :].astype(BF16)
            v_n = cv_ref[0, j, pl.ds(n, past_len, stride=N_KV_HEADS), :].astype(BF16)
            qg = jnp.concatenate(
                [q_ref[j * tn:(j + 1) * tn, (n * KV_GROUP + g) * HEAD_DIM:(n * KV_GROUP + g + 1) * HEAD_DIM]
                 for g in range(KV_GROUP)], axis=0)
            s1 = lax.dot_general(qg, k_n, NT_DIMS, preferred_element_type=F32)
            s2 = lax.dot_general(qg, kn[:, hs], NT_DIMS, preferred_element_type=F32)
            s1 = (s1.reshape(KV_GROUP, tn, past_len) + bias_pj[None]).reshape(KV_GROUP * tn, past_len)
            s2 = (s2.reshape(KV_GROUP, tn, LANES) + bias_nj[None]).reshape(KV_GROUP * tn, LANES)
            m = jnp.maximum(jnp.max(s1, axis=1, keepdims=True), jnp.max(s2, axis=1, keepdims=True))
            p1 = jnp.exp2(s1 - m)
            p2 = jnp.exp2(s2 - m)
            l = jnp.sum(p1, axis=1, keepdims=True) + jnp.sum(p2, axis=1, keepdims=True)
            o = (jnp.dot(p1.astype(BF16), v_n, preferred_element_type=F32)
                 + jnp.dot(p2.astype(BF16), vn[:, hs], preferred_element_type=F32)) / l
            for g in range(KV_GROUP):
                h = n * KV_GROUP + g
                o_ref[j * tn:(j + 1) * tn, h * HEAD_DIM:(h + 1) * HEAD_DIM] = o[g * tn:(g + 1) * tn, :]


def _attn_sample(q, qi4, wi, ck, cv, cki, kb, vb, kif, layer, k_top):
    nb = ck.shape[1]
    past_len = cki.shape[2]
    kvw = N_KV_HEADS * HEAD_DIM
    tn = q.shape[0] // nb
    assert LANES % tn == 0 and tn % 16 == 0 and past_len % LANES == 0
    group = LANES // tn
    assert nb % group == 0
    row = lambda w: pl.BlockSpec((LANES, w), lambda i: (i, 0))
    cache = lambda rows, w: pl.BlockSpec((1, group, rows, w), lambda i: (layer, i, 0, 0))
    return pl.pallas_call(
        functools.partial(_attn_sample_kernel, k_top=k_top, past_len=past_len, tn=tn),
        grid=(nb // group,),
        in_specs=[row(N_HEADS * HEAD_DIM),
                  pl.BlockSpec((group, N_IDX_HEADS, tn, IDX_DIM), lambda i: (i, 0, 0, 0)),
                  row(LANES), cache(N_KV_HEADS * past_len, HEAD_DIM),
                  cache(N_KV_HEADS * past_len, HEAD_DIM), cache(past_len, IDX_DIM),
                  row(kvw), row(kvw),
                  pl.BlockSpec((1, LANES, IDX_DIM), lambda i: (layer, i, 0))],
        out_specs=row(N_HEADS * HEAD_DIM),
        out_shape=jax.ShapeDtypeStruct((nb * tn, N_HEADS * HEAD_DIM), F32),
        scratch_shapes=[pltpu.VMEM((LANES, past_len), jnp.int32),
                        pltpu.VMEM((LANES, LANES), jnp.int32),
                        pltpu.VMEM((past_len + LANES, LANES), jnp.int32),
                        pltpu.VMEM((LANES, past_len), F32)],
        compiler_params=pltpu.CompilerParams(dimension_semantics=("arbitrary",),
                                             vmem_limit_bytes=VMEM_LIMIT),
        name="attn_sample",
    )(q, qi4, wi, ck, cv, cki, kb, vb, kif)


def _outffn_kernel(x_ref, a_ref, sga_ref, sgb_ref, u_ref, vn_ref, wmix_ref, bmix_ref, wo_ref,
                   g2_ref, wg_ref, wu_ref, wd_ref, gf_ref, xo_ref, *rest, final):
    if final:
        y_ref, z_ref = rest
    else:
        (z_ref,) = rest
    tm = x_ref.shape[0]
    group_dim = wmix_ref.shape[-1]
    for c in range(tm // SGU_CHUNK):
        rs = slice(c * SGU_CHUNK, (c + 1) * SGU_CHUNK)
        for g in range(SGU_GROUPS):
            cs = slice(g * group_dim, (g + 1) * group_dim)
            mixed = jnp.dot(wmix_ref[0, g], vn_ref[rs, cs].astype(BF16),
                            preferred_element_type=F32) + bmix_ref[0, :, cs]
            z = sga_ref[rs, cs] * a_ref[rs, cs] + sgb_ref[rs, cs] * (u_ref[rs, cs] * mixed)
            z_ref[rs, cs] = z.astype(BF16)
    x1 = x_ref[...] + jnp.dot(z_ref[...], wo_ref[0], preferred_element_type=F32)
    h2 = _rms_norm(x1, g2_ref[0]).astype(BF16)
    gate = jnp.dot(h2, wg_ref[0], preferred_element_type=F32)
    up = jnp.dot(h2, wu_ref[0], preferred_element_type=F32)
    ff = (gate * _sigmoid(gate) * up).astype(BF16)
    x2 = x1 + jnp.dot(ff, wd_ref[0], preferred_element_type=F32)
    xo_ref[...] = x2
    if final:
        y_ref[...] = _rms_norm(x2, gf_ref[...])


def _outffn(x, a, sga, sgb, u, vn, wmix, bmix, layer, wo, g2, wg, wu, wd, gf, final):
    T, D = x.shape
    tm = TOKEN_TILE
    row = pl.BlockSpec((tm, D), lambda i: (i, 0))
    n_out = 2 if final else 1
    out = pl.pallas_call(
        functools.partial(_outffn_kernel, final=final),
        grid=(T // tm,),
        in_specs=[row] * 6 + [_layer_spec(wmix.shape, layer), _layer_spec(bmix.shape, layer),
                              _layer_spec(wo.shape, layer), _layer_spec(g2.shape, layer),
                              _layer_spec(wg.shape, layer), _layer_spec(wu.shape, layer),
                              _layer_spec(wd.shape, layer), _const_spec((1, D))],
        out_specs=(row,) * n_out,
        out_shape=(jax.ShapeDtypeStruct((T, D), F32),) * n_out,
        scratch_shapes=[pltpu.VMEM((tm, D), BF16)],
        compiler_params=pltpu.CompilerParams(dimension_semantics=("arbitrary",),
                                             vmem_limit_bytes=VMEM_LIMIT),
        name="outffn",
    )(x, a, sga, sgb, u, vn, wmix, bmix, wo, g2, wg, wu, wd, gf)
    return out if final else (out[0], None)


def _rope_tables(pos, d):
    inv = ROPE_THETA ** (-jnp.arange(0, d, 2, dtype=F32) / d)
    ang = pos.astype(F32)[:, None] * inv[None, :]
    cos, sin = jnp.cos(ang), jnp.sin(ang)
    c = jnp.concatenate([cos, cos], axis=-1)
    s = jnp.concatenate([-sin, sin], axis=-1)
    reps = LANES // d
    return jnp.tile(c, (1, reps)), jnp.tile(s, (1, reps))


def _mix_weights(sgu_w, sgu_b, n, group_dim):
    depth = sgu_w.shape[0]
    p = jnp.arange(n)
    mask = (p[None, :] // CHUNK) <= (p[:, None] // CHUNK)
    w = jnp.where(mask[None, None], sgu_w[:, :, :n, :n], 0.0)
    reps = SGU_CHUNK // n
    eye = jnp.eye(reps, dtype=w.dtype)
    wbd = jnp.einsum('ab,lgij->lgaibj', eye, w).reshape(depth, SGU_GROUPS, SGU_CHUNK, SGU_CHUNK)
    b = jnp.tile(sgu_b[:, :, :n], (1, 1, reps))
    bfull = jnp.repeat(jnp.swapaxes(b, 1, 2), group_dim, axis=2)
    return wbd.astype(BF16), bfull


def _in_weights(w_in, d_model):
    kvw = N_KV_HEADS * HEAD_DIM
    w = w_in
    c_ki = d_model + 2 * kvw + N_IDX_HEADS * IDX_DIM
    c_wi = c_ki + IDX_DIM
    c_u = c_wi + N_IDX_HEADS
    wki, wwi = w[:, :, c_ki:c_wi], w[:, :, c_wi:c_u]
    wwi_pad = jnp.pad(wwi, ((0, 0), (0, 0), (0, LANES - N_IDX_HEADS)))
    wa = jnp.concatenate([w[:, :, :c_ki], wki, wki, wwi_pad, w[:, :, c_u:]], axis=2)
    wv_t = jnp.swapaxes(w[:, :, d_model + kvw:d_model + 2 * kvw], 1, 2)
    wwi_t = jnp.pad(jnp.swapaxes(wwi, 1, 2), ((0, 0), (0, 16 - N_IDX_HEADS), (0, 0)))
    return wa.astype(BF16), jnp.concatenate([wv_t, wwi_t], axis=1).astype(BF16)


def kernel(x_prompt, x_sample, cache_k, cache_v, cache_kidx, norm1_g, w_in, ln_v_g, ln_v_b, sgu_w, sgu_b, w_out, norm2_g, w_gate, w_up, w_down, final_norm_g):
    B, S, D = x_prompt.shape
    NB, TN, _ = x_sample.shape
    depth, _, P = cache_k.shape[:3]
    assert D == N_HEADS * HEAD_DIM and SGU_CHUNK % TN == 0 and S % SGU_CHUNK == 0
    k_top_p = min(TOPK_MAX, S // 4)
    k_top_s = min(TOPK_MAX, (P + TN) // 4)
    group_dim = D // SGU_GROUPS

    assert S % TOKEN_TILE == 0 and TOKEN_TILE % TN == 0
    pos_p = jnp.arange(S)
    pos_s = jnp.tile(P + jnp.arange(TN), TOKEN_TILE // TN)
    tab_p = _rope_tables(pos_p, HEAD_DIM) + _rope_tables(pos_p, IDX_DIM)
    tab_s = _rope_tables(pos_s, HEAD_DIM) + _rope_tables(pos_s, IDX_DIM)

    wa, wb = _in_weights(w_in, D)
    wo, wg, wu, wd = (t.astype(BF16) for t in (w_out, w_gate, w_up, w_down))
    vec = lambda v: v.reshape(depth, 1, -1)
    g1, g2, lng, lnb = vec(norm1_g), vec(norm2_g), vec(ln_v_g), vec(ln_v_b)
    gf = final_norm_g.reshape(1, -1)
    wmix_p, bmix_p = _mix_weights(sgu_w, sgu_b, SGU_CHUNK, group_dim)
    wmix_s, bmix_s = _mix_weights(sgu_w, sgu_b, TN, group_dim)
    ck = cache_k.reshape(depth, NB, P * N_KV_HEADS, HEAD_DIM)
    cv = cache_v.reshape(depth, NB, P * N_KV_HEADS, HEAD_DIM)

    xp = x_prompt.reshape(B * S, D)
    xs = x_sample.reshape(NB * TN, D)
    def new_buffers(tokens):
        kv = (depth, N_KV_HEADS * tokens, HEAD_DIM)
        return jnp.zeros(kv, F32), jnp.zeros(kv, F32), jnp.zeros((depth, tokens, IDX_DIM), F32)

    new_p, new_s = new_buffers(B * S), new_buffers(NB * TN)
    sgu_v = []
    yp = ys = None
    for l in range(depth):
        final = l == depth - 1

        (q, kf, kb, vf, _, vt, qi, kif, kze, kzo, _, wit, u, vn, sga, sgb) = _inproj(
            xp, l, g1, wa, wb, tab_p, lng, lnb, new_p)
        new_p = (kf, vf, kif)
        a = _attn_prompt(q, qi, wit, kze, kzo, kb, vt, B, S, k_top_p)
        xp, yp = _outffn(xp, a, sga, sgb, u, vn, wmix_p, bmix_p, l, wo, g2, wg, wu, wd, gf, final)

        (q, kf, kb, vf, vb, _, qi, kif, _, _, wi, _, u, vn, sga, sgb) = _inproj(
            xs, l, g1, wa, wb, tab_s, lng, lnb, new_s)
        new_s = (kf, vf, kif)
        qi4 = qi.reshape(NB, TN, N_IDX_HEADS, IDX_DIM).transpose(0, 2, 1, 3)
        a = _attn_sample(q, qi4, wi, ck, cv, cache_kidx, kb, vb, kif, l, k_top_s)
        xs, ys = _outffn(xs, a, sga, sgb, u, vn, wmix_s, bmix_s, l, wo, g2, wg, wu, wd, gf, final)
        sgu_v.append(vn.reshape(NB, TN, D))

    kv_p = (depth, B, S, N_KV_HEADS, HEAD_DIM)
    kv_s = (depth, NB, TN, N_KV_HEADS, HEAD_DIM)
    return (yp.reshape(B, S, D), ys.reshape(NB, TN, D),
            new_p[0].reshape(kv_p), new_p[1].reshape(kv_p), new_p[2].reshape(depth, B, S, IDX_DIM),
            new_s[0].reshape(kv_s), new_s[1].reshape(kv_s), new_s[2].reshape(depth, NB, TN, IDX_DIM),
            jnp.stack(sgu_v))
```

```python
import functools

import jax
import jax.numpy as jnp
from jax import lax
from jax.experimental import pallas as pl
from jax.experimental.pallas import tpu as pltpu

CHUNK = 64
N_HEADS = 8
HEAD_DIM = 128
N_KV_HEADS = 2
KV_GROUP = N_HEADS // N_KV_HEADS
N_IDX_HEADS = 8
IDX_DIM = 64
TOPK_MAX = 256
SGU_CHUNK = 128
SGU_GROUPS = 8
ROPE_THETA = 10000.0
EPS = 1e-6

LANES = 128
TOKEN_TILE = 256
Q_TILE = 256
KEY_TILE = 256
VT_ROWS = HEAD_DIM + 16
FOLD = 8
WINDOW_BITS = 9
VMEM_LIMIT = 56 * 1024 * 1024

INT_MIN = -(2 ** 31)
INT_MAX = 2 ** 31 - 1
MASK_BIAS = -1e30
LOG2E = 1.4426950408889634
Q_SCALE = (HEAD_DIM ** -0.5) * LOG2E

F32 = jnp.float32
BF16 = jnp.bfloat16
NT_DIMS = (((1,), (1,)), ((), ()))


def _const_spec(shape):
    nd = len(shape)
    return pl.BlockSpec(shape, lambda *_: (0,) * nd, pipeline_mode=pl.Buffered(1))


def _sigmoid(x):
    return 1.0 / (1.0 + jnp.exp(-x))


def _rms_norm(x, g):
    return x * lax.rsqrt(jnp.mean(x * x, axis=-1, keepdims=True) + EPS) * g


def _order_key(score):
    bits = pltpu.bitcast(score, jnp.int32)
    return bits ^ ((bits >> 31) & INT_MAX)


def _midpoint(lo, hi):
    return (lo >> 1) + (hi >> 1) + (lo & hi & 1)


_C_Q = 0
_C_K = _C_Q + N_HEADS * HEAD_DIM
_C_V = _C_K + N_KV_HEADS * HEAD_DIM
_C_QI = _C_V + N_KV_HEADS * HEAD_DIM
_C_KI = _C_QI + N_IDX_HEADS * IDX_DIM
_C_WI = _C_KI + LANES
_C_U = _C_WI + LANES


def _inproj_kernel(*refs, d_model, n_alias):
    (x_ref, g_ref, wa_ref, wb_ref, cosh_ref, sinh_ref, cosi_ref, sini_ref,
     lng_ref, lnb_ref) = refs[:10]
    (q_ref, kf_ref, kb_ref, vf_ref, vb_ref, vt_ref, qi_ref, kif_ref, kze_ref, kzo_ref,
     wi_ref, wit_ref, u_ref, vn_ref, sga_ref, sgb_ref) = refs[10 + n_alias:]
    tm = x_ref.shape[0]
    hb = _rms_norm(x_ref[...], g_ref[0]).astype(BF16)

    def proj(c0, width):
        return jnp.dot(hb, wa_ref[0, :, c0:c0 + width], preferred_element_type=F32)

    def head_rows(n):
        return pl.ds(n, tm, stride=N_KV_HEADS)

    cosh, sinh = cosh_ref[...], sinh_ref[...]
    cosi, sini = cosi_ref[...], sini_ref[...]
    lane = lax.broadcasted_iota(jnp.int32, cosi.shape, 1)
    first_half = (lane % IDX_DIM) < (IDX_DIM // 2)

    def rope_head(x):
        return x * cosh + pltpu.roll(x, HEAD_DIM // 2, 1) * sinh

    def rope_idx(x):
        partner = jnp.where(first_half, pltpu.roll(x, LANES - IDX_DIM // 2, 1),
                            pltpu.roll(x, IDX_DIM // 2, 1))
        return x * cosi + partner * sini

    xq = proj(_C_Q, N_HEADS * HEAD_DIM)
    for h in range(N_HEADS):
        sl = slice(h * HEAD_DIM, (h + 1) * HEAD_DIM)
        q_ref[:, sl] = (rope_head(xq[:, sl]) * Q_SCALE).astype(BF16)

    xk = proj(_C_K, N_KV_HEADS * HEAD_DIM)
    for h in range(N_KV_HEADS):
        sl = slice(h * HEAD_DIM, (h + 1) * HEAD_DIM)
        kr = rope_head(xk[:, sl])
        kf_ref[0, head_rows(h), :] = kr
        kb_ref[:, sl] = kr.astype(BF16)

    xv = proj(_C_V, N_KV_HEADS * HEAD_DIM)
    vb_ref[...] = xv.astype(BF16)
    for h in range(N_KV_HEADS):
        vf_ref[0, head_rows(h), :] = xv[:, h * HEAD_DIM:(h + 1) * HEAD_DIM]

    xqi = proj(_C_QI, N_IDX_HEADS * IDX_DIM)
    for p in range(N_IDX_HEADS * IDX_DIM // LANES):
        sl = slice(p * LANES, (p + 1) * LANES)
        qi_ref[:, sl] = (rope_idx(xqi[:, sl]) * (IDX_DIM ** -0.5)).astype(BF16)

    kk = rope_idx(proj(_C_KI, LANES))
    kif_ref[0] = kk[:, :IDX_DIM]
    low = lane < IDX_DIM
    kze_ref[...] = jnp.where(low, kk, 0.0).astype(BF16)
    kzo_ref[...] = jnp.where(low, 0.0, kk).astype(BF16)

    wi_ref[...] = proj(_C_WI, LANES) * (N_IDX_HEADS ** -0.5)

    tb = lax.dot_general(wb_ref[0], hb, NT_DIMS, preferred_element_type=F32)
    kvw = N_KV_HEADS * HEAD_DIM
    ones_rows = (lax.broadcasted_iota(jnp.int32, (VT_ROWS - HEAD_DIM, tm), 0) == 0).astype(BF16)
    for n in range(N_KV_HEADS):
        vt_ref[0, n * VT_ROWS:n * VT_ROWS + HEAD_DIM, :] = (
            tb[n * HEAD_DIM:(n + 1) * HEAD_DIM].astype(BF16))
        vt_ref[0, n * VT_ROWS + HEAD_DIM:(n + 1) * VT_ROWS, :] = ones_rows
    wit_ref[...] = tb[kvw:] * (N_IDX_HEADS ** -0.5)

    u_ref[...] = jax.nn.gelu(proj(_C_U, d_model), approximate=True)
    gv = jax.nn.gelu(proj(_C_U + d_model, d_model), approximate=True)
    mu = jnp.mean(gv, axis=-1, keepdims=True)
    dv = gv - mu
    var = jnp.mean(dv * dv, axis=-1, keepdims=True)
    vn_ref[...] = dv * lax.rsqrt(var + EPS) * lng_ref[0] + lnb_ref[0]
    sga_ref[...] = _sigmoid(proj(_C_U + 2 * d_model, d_model))
    sgb_ref[...] = _sigmoid(proj(_C_U + 3 * d_model, d_model))


def _layer_spec(shape, layer):
    nd = len(shape)
    return pl.BlockSpec((1,) + tuple(shape[1:]), lambda *_: (layer,) + (0,) * (nd - 1),
                        pipeline_mode=pl.Buffered(1))


def _inproj(x, layer, g, wa, wb, tables, lng, lnb, stacked):
    T, D = x.shape
    depth = wa.shape[0]
    tm = TOKEN_TILE
    assert T % tm == 0
    kvw = N_KV_HEADS * HEAD_DIM
    qiw = N_IDX_HEADS * IDX_DIM
    row = lambda w: pl.BlockSpec((tm, w), lambda i: (i, 0))
    kv_rows = N_KV_HEADS * T
    out_shape = (
        jax.ShapeDtypeStruct((T, N_HEADS * HEAD_DIM), BF16),
        jax.ShapeDtypeStruct((depth, kv_rows, HEAD_DIM), F32),
        jax.ShapeDtypeStruct((T, kvw), BF16),
        jax.ShapeDtypeStruct((depth, kv_rows, HEAD_DIM), F32),
        jax.ShapeDtypeStruct((T, kvw), BF16),
        jax.ShapeDtypeStruct((T // tm, N_KV_HEADS * VT_ROWS, tm), BF16),
        jax.ShapeDtypeStruct((T, qiw), BF16),
        jax.ShapeDtypeStruct((depth, T, IDX_DIM), F32),
        jax.ShapeDtypeStruct((T, LANES), BF16),
        jax.ShapeDtypeStruct((T, LANES), BF16),
        jax.ShapeDtypeStruct((T, LANES), F32),
        jax.ShapeDtypeStruct((16, T), F32),
        jax.ShapeDtypeStruct((T, D), F32),
        jax.ShapeDtypeStruct((T, D), F32),
        jax.ShapeDtypeStruct((T, D), F32),
        jax.ShapeDtypeStruct((T, D), F32),
    )
    kv_spec = pl.BlockSpec((1, N_KV_HEADS * tm, HEAD_DIM), lambda i: (layer, i, 0))
    out_specs = (
        row(N_HEADS * HEAD_DIM), kv_spec, row(kvw), kv_spec, row(kvw),
        pl.BlockSpec((1, N_KV_HEADS * VT_ROWS, tm), lambda i: (i, 0, 0)),
        row(qiw), pl.BlockSpec((1, tm, IDX_DIM), lambda i: (layer, i, 0)),
        row(LANES), row(LANES), row(LANES),
        pl.BlockSpec((16, tm), lambda i: (0, i)),
        row(D), row(D), row(D), row(D),
    )
    n_tab = tables[0].shape[0] // tm
    tab = pl.BlockSpec((tm, LANES), lambda i: (i % n_tab, 0))
    in_specs = [row(D), _layer_spec(g.shape, layer), _layer_spec(wa.shape, layer),
                _layer_spec(wb.shape, layer),
                tab, tab, tab, tab,
                _layer_spec(lng.shape, layer), _layer_spec(lnb.shape, layer)]
    args = [x, g, wa, wb, *tables, lng, lnb]
    aliases = {}
    for j, out_idx in enumerate((1, 3, 7)):
        aliases[len(args)] = out_idx
        in_specs.append(pl.BlockSpec(memory_space=pl.ANY))
        args.append(stacked[j])
    return pl.pallas_call(
        functools.partial(_inproj_kernel, d_model=D, n_alias=len(aliases)),
        grid=(T // tm,),
        in_specs=in_specs, out_specs=out_specs, out_shape=out_shape,
        input_output_aliases=aliases,
        compiler_params=pltpu.CompilerParams(dimension_semantics=("arbitrary",),
                                             vmem_limit_bytes=VMEM_LIMIT),
        name="inproj",
    )(*args)


def _attn_prompt_kernel(q_ref, qi_ref, wit_ref, kze_ref, kzo_ref, k_ref, vt_ref, o_ref,
                        keys_ref, hi_ref, lo_ref, f1_ref, f2_ref, kmax_ref, acc_ref, m_ref, mx_ref, s_ref, p_ref,
                        bias_ref, *, k_top):
    tq, kb_sz = Q_TILE, KEY_TILE
    qb = pl.program_id(1)
    nkb = (qb + 1) * (tq // kb_sz)
    q_chunk = (qb * tq + lax.broadcasted_iota(jnp.int32, (1, tq), 1)) // CHUNK
    row_iota = lax.broadcasted_iota(jnp.int32, (kb_sz, 1), 0)

    def key_slice(kb):
        return pl.ds(pl.multiple_of(kb * kb_sz, kb_sz), kb_sz)

    n_blocks = keys_ref.shape[0] // kb_sz
    n_full = nkb // 2

    def clamp(blk):
        return jnp.minimum(blk, n_blocks - 1)

    def idx_dots(blk, slot):
        ks = key_slice(blk)
        kze, kzo = kze_ref[ks, :], kzo_ref[ks, :]
        for p in range(N_IDX_HEADS // 2):
            qp = qi_ref[:, p * LANES:(p + 1) * LANES]
            s_ref[slot, 2 * p] = lax.dot_general(kze, qp, NT_DIMS, preferred_element_type=F32)
            s_ref[slot, 2 * p + 1] = lax.dot_general(kzo, qp, NT_DIMS, preferred_element_type=F32)

    def idx_keys(blk, slot):
        base = pl.multiple_of(blk * kb_sz, kb_sz)
        kmax = kmax_ref[...]
        for c in range(kb_sz // CHUNK):
            rows = slice(c * CHUNK, (c + 1) * CHUNK)
            score = jnp.zeros((CHUNK, tq), F32)
            for j in range(N_IDX_HEADS):
                score = score + wit_ref[j:j + 1, :] * jnp.maximum(s_ref[slot, j, rows, :], 0.0)
            visible = (blk * (kb_sz // CHUNK) + c) <= q_chunk
            key = jnp.where(visible, _order_key(score), INT_MIN)
            ks = pl.ds(base + c * CHUNK, CHUNK)
            keys_ref[ks, :] = key
            hi_ref[ks, :] = (key >> 16).astype(jnp.int16)
            lo_ref[ks, :] = (key ^ 0x8000).astype(jnp.int16)
            kmax = jnp.maximum(kmax, jnp.max(key.reshape(CHUNK // 8, 8, tq), axis=0))
        kmax_ref[...] = kmax

    def score_body(j, carry):
        b0 = 2 * j
        idx_dots(b0 + 1, 1)
        idx_keys(b0, 0)
        idx_dots(clamp(b0 + 2), 0)
        idx_keys(b0 + 1, 1)
        return carry

    kmax_ref[...] = jnp.full(kmax_ref.shape, INT_MIN, jnp.int32)
    idx_dots(0, 0)
    lax.fori_loop(0, n_full, score_body, 0)

    @pl.when(nkb % 2 == 1)
    def _():
        idx_keys(nkb - 1, 0)
        keys_ref[key_slice(nkb), :] = jnp.full((kb_sz, tq), INT_MIN, jnp.int32)

    n_acc = 4
    rows16 = 16

    def fold_rows(x, accs):
        accs = list(accs)
        for r in range(kb_sz // rows16):
            accs[r % n_acc] = accs[r % n_acc] + x[r * rows16:(r + 1) * rows16, :]
        return tuple(accs)

    def zero_accs():
        return tuple(jnp.zeros((rows16, tq), jnp.int16) for _ in range(n_acc))

    def total(accs):
        tot = accs[0].astype(jnp.int32)
        for a in accs[1:]:
            tot = tot + a.astype(jnp.int32)
        return jnp.sum(tot, axis=0, keepdims=True)

    def count_ge(plane_ref, t):
        t16 = jnp.broadcast_to(t, (kb_sz, tq)).astype(jnp.int16)

        def body(kb, accs):
            ge = (plane_ref[key_slice(kb), :] >= t16).astype(jnp.int16)
            return fold_rows(ge, accs)

        return total(lax.fori_loop(0, nkb, body, zero_accs()))

    def bisect16(count, need, lo, hi, count_lo, steps):
        def body(_, carry):
            lo, hi, n_lo = carry
            mid = (lo + hi) >> 1
            n_mid = count(mid)
            ok = n_mid >= need
            return jnp.where(ok, mid, lo), jnp.where(ok, hi, mid), jnp.where(ok, n_mid, n_lo)

        lo, _, n_lo = lax.fori_loop(0, steps, body, (lo, hi, count_lo))
        return lo, n_lo

    floor16 = jnp.full((1, tq), -(2 ** 15), jnp.int32)
    ceil16 = jnp.full((1, tq), 2 ** 15, jnp.int32)
    every = jnp.full((1, tq), keys_ref.shape[0], jnp.int32)
    count_hi = functools.partial(count_ge, hi_ref)
    top16 = (jnp.max(kmax_ref[...], axis=0, keepdims=True) >> 16) + 1
    lo_try = jnp.maximum(top16 - 2 ** WINDOW_BITS, -(2 ** 15) + 1)
    n_try = count_hi(lo_try)
    in_window = n_try >= k_top
    steps = jnp.where(jnp.min(in_window.astype(jnp.int32)) > 0, WINDOW_BITS, 16)
    t_hi, n_hi = bisect16(count_hi, k_top, jnp.where(in_window, lo_try, floor16), top16,
                          jnp.where(in_window, n_try, every), steps)

    t_hi16 = jnp.broadcast_to(t_hi, (kb_sz, tq)).astype(jnp.int16)
    sentinel = jnp.int16(-(2 ** 15))
    n_slots = kb_sz // rows16 // FOLD

    def max_min(a, b):
        a_ge = a >= b
        return jnp.where(a_ge, a, b), jnp.where(a_ge, b, a)

    def tie_body(kb, accs):
        ks = key_slice(kb)
        hi_part = hi_ref[ks, :]
        e = jnp.where(hi_part == t_hi16, lo_ref[ks, :], sentinel)
        lo_ref[ks, :] = e
        g = [e[r * rows16:(r + 1) * rows16, :] for r in range(kb_sz // rows16)]
        n = len(g) // 2
        pairs = [max_min(g[i], g[i + n]) for i in range(n)]
        first, second = [p[0] for p in pairs], [p[1] for p in pairs]
        while n > n_slots:
            n //= 2
            pairs = [max_min(first[i], first[i + n]) for i in range(n)]
            second = [max_min(pairs[i][1], max_min(second[i], second[i + n])[0])[0] for i in range(n)]
            first = [p[0] for p in pairs]
        fs = pl.ds(pl.multiple_of(kb * (n_slots * rows16), n_slots * rows16), n_slots * rows16)
        f1_ref[fs, :] = jnp.concatenate(first, axis=0)
        f2_ref[fs, :] = jnp.concatenate(second, axis=0)
        return fold_rows((hi_part > t_hi16).astype(jnp.int16), accs)

    n_above = total(lax.fori_loop(0, nkb, tie_body, zero_accs()))
    need_lo = k_top - n_above

    def count_folded(t):
        t16 = jnp.broadcast_to(t, (n_slots * rows16, tq)).astype(jnp.int16)

        def body(kb, accs):
            fs = pl.ds(pl.multiple_of(kb * (n_slots * rows16), n_slots * rows16), n_slots * rows16)
            accs = list(accs)
            for i, plane in enumerate((f1_ref, f2_ref)):
                ge = (plane[fs, :] >= t16).astype(jnp.int16)
                for r in range(n_slots):
                    a = (i * n_slots + r) % n_acc
                    accs[a] = accs[a] + ge[r * rows16:(r + 1) * rows16, :]
            return tuple(accs)

        return total(lax.fori_loop(0, nkb, body, zero_accs()))

    def count_lo_pair(t):
        t_next = jnp.minimum(t + 1, 2 ** 15 - 1)
        ta = jnp.broadcast_to(t, (kb_sz, tq)).astype(jnp.int16)
        tb = jnp.broadcast_to(t_next, (kb_sz, tq)).astype(jnp.int16)

        def body(kb, accs):
            plane = lo_ref[key_slice(kb), :]
            return (fold_rows((plane >= ta).astype(jnp.int16), accs[0]),
                    fold_rows((plane >= tb).astype(jnp.int16), accs[1]))

        a, b = lax.fori_loop(0, nkb, body, (zero_accs(), zero_accs()))
        return total(a), jnp.where(t < 2 ** 15 - 1, total(b), 0)

    t_lo, _ = bisect16(count_folded, need_lo, floor16, ceil16, every, 16)
    n_at, n_next = count_lo_pair(t_lo)

    def unfolded_search():
        t, _ = bisect16(functools.partial(count_ge, lo_ref), need_lo, floor16, ceil16, every, 16)
        return (t,) + count_lo_pair(t)

    t_lo, n_at, n_next = lax.cond(jnp.max((n_next >= need_lo).astype(jnp.int32)) > 0,
                                  unfolded_search, lambda: (t_lo, n_at, n_next))
    thr = jnp.maximum((t_hi << 16) + (t_lo + 2 ** 15), INT_MIN + 1)

    n_selected = jnp.where(t_hi > -(2 ** 15),
                           jnp.where(t_lo > -(2 ** 15), n_above + n_at, n_hi), 0)
    need_eq = (need_lo - n_next).astype(F32)
    col_iota = lax.broadcasted_iota(jnp.int32, (1, kb_sz), 1)

    for c in range(tq // LANES):
        cs = slice(c * LANES, (c + 1) * LANES)

        @pl.when(jnp.max(n_selected[:, cs]) > k_top)
        def _(cs=cs):
            tri = jnp.where(col_iota <= row_iota, 1.0, 0.0).astype(BF16)
            thr_c, need_c = thr[:, cs], need_eq[:, cs]

            unroll = 4

            def drop(j, seen):
                blocks = [unroll * j + i for i in range(unroll)]
                keys = [keys_ref[key_slice(clamp(b)), cs] for b in blocks]
                equal = [(key == thr_c) & (b < nkb) for b, key in zip(blocks, keys)]
                local = [jnp.dot(tri, jnp.where(e, 1.0, 0.0).astype(BF16),
                                 preferred_element_type=F32) for e in equal]
                for b, key, e, rank in zip(blocks, keys, equal, local):
                    rank = rank + seen
                    keys_ref[key_slice(clamp(b)), cs] = jnp.where(e & (rank > need_c), key - 1, key)
                    seen = rank[kb_sz - 1:kb_sz, :]
                return seen

            lax.fori_loop(0, (nkb + unroll - 1) // unroll, drop, jnp.zeros((1, LANES), F32))

    m_ref[...] = jnp.full(m_ref.shape, MASK_BIAS, F32)
    acc_ref[...] = jnp.zeros(acc_ref.shape, F32)

    rc = 32
    n_rc = kb_sz // rc

    def fold8(x):
        return x.reshape(rc // 8, 8, tq)

    def qk_dots(blk, slot):
        ks = key_slice(blk)
        for h in range(N_HEADS):
            n = h // KV_GROUP
            k_n = k_ref[ks, n * HEAD_DIM:(n + 1) * HEAD_DIM]
            q_h = q_ref[:, h * HEAD_DIM:(h + 1) * HEAD_DIM]
            s_ref[slot, h] = lax.dot_general(k_n, q_h, NT_DIMS, preferred_element_type=F32)

    def softmax_block(blk, slot):
        thr_b = jnp.where(blk < nkb, thr, INT_MAX)
        bias_ref[slot] = jnp.where(keys_ref[key_slice(clamp(blk)), :] >= thr_b, 0.0, MASK_BIAS)
        for h in range(N_HEADS):
            mx = jnp.full((8, tq), MASK_BIAS, F32)
            for r in range(n_rc):
                rows = slice(r * rc, (r + 1) * rc)
                sc = s_ref[slot, h, rows, :] + bias_ref[slot, rows, :]
                s_ref[slot, h, rows, :] = sc
                mx = jnp.maximum(mx, jnp.max(fold8(sc), axis=0))
            mx_ref[h:h + 1, :] = jnp.max(mx, axis=0, keepdims=True)
        m_old = m_ref[...]
        m_new = jnp.maximum(m_old, mx_ref[...])
        alpha = jnp.exp2(m_old - m_new)
        m_ref[...] = m_new
        for h in range(N_HEADS):
            m_h = jnp.broadcast_to(m_new[h:h + 1, :], (rc, tq))
            for r in range(n_rc):
                rows = slice(r * rc, (r + 1) * rc)
                p_ref[slot, h, rows, :] = jnp.exp2(s_ref[slot, h, rows, :] - m_h).astype(BF16)
        return alpha

    def pv_update(blk, slot, alpha):
        for h in range(N_HEADS):
            n = h // KV_GROUP
            vt_n = vt_ref[blk, n * VT_ROWS:(n + 1) * VT_ROWS, :]
            pv = jnp.dot(vt_n, p_ref[slot, h], preferred_element_type=F32)
            acc_ref[h] = alpha[h:h + 1, :] * acc_ref[h] + pv

    def attn_body(j, carry):
        b0 = 2 * j
        qk_dots(clamp(b0 + 1), 1)
        pv_update(b0, 0, softmax_block(b0, 0))
        qk_dots(clamp(b0 + 2), 0)
        pv_update(clamp(b0 + 1), 1, softmax_block(b0 + 1, 1))
        return carry

    qk_dots(0, 0)
    lax.fori_loop(0, (nkb + 1) // 2, attn_body, 0)

    for h in range(N_HEADS):
        o_t = acc_ref[h, :HEAD_DIM, :] / acc_ref[h, HEAD_DIM:HEAD_DIM + 1, :]
        o_ref[:, h * HEAD_DIM:(h + 1) * HEAD_DIM] = o_t.T


def _attn_prompt(q, qi, wit, kze, kzo, kb, vt, batch, seq, k_top):
    tq = Q_TILE
    assert seq % tq == 0 and Q_TILE % KEY_TILE == 0 and KEY_TILE == TOKEN_TILE
    assert (seq // KEY_TILE) % 4 == 0
    nq = seq // tq
    kvw = N_KV_HEADS * HEAD_DIM
    qrow = lambda w: pl.BlockSpec((tq, w), lambda b, i: (b * nq + i, 0))
    per_batch = lambda w: pl.BlockSpec((seq, w), lambda b, i: (b, 0), pipeline_mode=pl.Buffered(1))
    return pl.pallas_call(
        functools.partial(_attn_prompt_kernel, k_top=k_top),
        grid=(batch, nq),
        in_specs=[qrow(N_HEADS * HEAD_DIM), qrow(N_IDX_HEADS * IDX_DIM),
                  pl.BlockSpec((16, tq), lambda b, i: (0, b * nq + i)),
                  per_batch(LANES), per_batch(LANES), per_batch(kvw),
                  pl.BlockSpec((seq // KEY_TILE, N_KV_HEADS * VT_ROWS, KEY_TILE), lambda b, i: (b, 0, 0),
                               pipeline_mode=pl.Buffered(1))],
        out_specs=qrow(N_HEADS * HEAD_DIM),
        out_shape=jax.ShapeDtypeStruct((batch * seq, N_HEADS * HEAD_DIM), F32),
        scratch_shapes=[pltpu.VMEM((seq, tq), jnp.int32),
                        pltpu.VMEM((seq, tq), jnp.int16),
                        pltpu.VMEM((seq, tq), jnp.int16),
                        pltpu.VMEM((seq // FOLD, tq), jnp.int16),
                        pltpu.VMEM((seq // FOLD, tq), jnp.int16),
                        pltpu.VMEM((8, tq), jnp.int32),
                        pltpu.VMEM((N_HEADS, VT_ROWS, tq), F32),
                        pltpu.VMEM((N_HEADS, tq), F32),
                        pltpu.VMEM((N_HEADS, tq), F32),
                        pltpu.VMEM((2, N_HEADS, KEY_TILE, tq), F32),
                        pltpu.VMEM((2, N_HEADS, KEY_TILE, tq), BF16),
                        pltpu.VMEM((2, KEY_TILE, tq), F32)],
        compiler_params=pltpu.CompilerParams(dimension_semantics=("arbitrary", "arbitrary"),
                                             vmem_limit_bytes=VMEM_LIMIT),
        name="attn_prompt",
    )(q, qi, wit, kze, kzo, kb, vt)


def _attn_sample_kernel(q_ref, qi_ref, wi_ref, ck_ref, cv_ref, cki_ref, kn_ref, vn_ref, kin_ref,
                        o_ref, keyp_ref, keyn_ref, keyt_ref, biasp_ref, *, k_top, past_len, tn):
    group = LANES // tn
    rows = group * tn
    n_cols = past_len // LANES
    kn = kn_ref[...]
    vn = vn_ref[...]
    kin = kin_ref[0].astype(BF16)

    lane = lax.broadcasted_iota(jnp.int32, (tn, LANES), 1)
    pos = lax.broadcasted_iota(jnp.int32, (tn, LANES), 0)
    vis = ((past_len + lane % tn) // CHUNK) <= ((past_len + pos) // CHUNK)
    for j in range(group):
        qj = qi_ref[j].reshape(N_IDX_HEADS * tn, IDX_DIM)
        dp = lax.dot_general(qj, cki_ref[0, j].astype(BF16), NT_DIMS, preferred_element_type=F32)
        dn = lax.dot_general(qj, kin, NT_DIMS, preferred_element_type=F32)
        sp = jnp.zeros((tn, past_len), F32)
        sn = jnp.zeros((tn, LANES), F32)
        for h in range(N_IDX_HEADS):
            w = wi_ref[j * tn:(j + 1) * tn, h:h + 1]
            sp = sp + w * jnp.maximum(dp[h * tn:(h + 1) * tn, :], 0.0)
            sn = sn + w * jnp.maximum(dn[h * tn:(h + 1) * tn, :], 0.0)
        keyp_ref[j * tn:(j + 1) * tn, :] = _order_key(sp)
        own = (lane // tn) == j
        keyn_ref[j * tn:(j + 1) * tn, :] = jnp.where(own & vis, _order_key(sn), INT_MIN)

    for c in range(n_cols):
        keyt_ref[c * LANES:(c + 1) * LANES, :] = keyp_ref[:, c * LANES:(c + 1) * LANES].T
    keyt_ref[past_len:past_len + LANES, :] = keyn_ref[...].T
    n_keys = past_len + LANES

    def bisect_body(_, carry):
        lo, hi = carry
        mid = _midpoint(lo, hi)
        ge = (keyt_ref[...] >= mid).astype(jnp.int32)
        cnt = jnp.sum(jnp.sum(ge.reshape(n_keys // 8, 8, rows), axis=0), axis=0, keepdims=True)
        ok = cnt >= k_top
        return jnp.where(ok, mid, lo), jnp.where(ok, hi, mid)

    thr, _ = lax.fori_loop(
        0, 32, bisect_body,
        (jnp.full((1, rows), INT_MIN + 1, jnp.int32), jnp.full((1, rows), INT_MAX, jnp.int32)))
    thr_col = jnp.broadcast_to(thr, (rows, rows)).T

    key_row = lax.broadcasted_iota(jnp.int32, (n_keys, 1), 0)

    def count_keys(pred):
        hit = pred(keyt_ref[...], key_row).astype(jnp.int32)
        return jnp.sum(jnp.sum(hit.reshape(n_keys // 8, 8, rows), axis=0), axis=0, keepdims=True)

    n_selected = count_keys(lambda key, row: key >= thr)

    @pl.when(jnp.max(n_selected) > k_top)
    def _():
        need_eq = k_top - count_keys(lambda key, row: key > thr)

        def body(_, carry):
            lo, hi = carry
            mid = (lo + hi) >> 1
            ok = count_keys(lambda key, row: (key == thr) & (row < mid)) >= need_eq
            return jnp.where(ok, lo, mid), jnp.where(ok, mid, hi)

        _, cut = lax.fori_loop(0, n_keys.bit_length(), body,
                               (jnp.zeros((1, rows), jnp.int32), jnp.full((1, rows), n_keys, jnp.int32)))
        cut_col = jnp.broadcast_to(cut, (rows, rows)).T
        lane_idx = lax.broadcasted_iota(jnp.int32, (rows, LANES), 1)
        for c in range(n_cols + 1):
            ref, cols = (keyp_ref, slice(c * LANES, (c + 1) * LANES)) if c < n_cols else (keyn_ref, slice(None))
            key = ref[:, cols]
            surplus = (key == thr_col) & (c * LANES + lane_idx >= cut_col)
            ref[:, cols] = jnp.where(surplus, key - 1, key)

    bias_n = jnp.where(keyn_ref[...] >= thr_col, 0.0, MASK_BIAS)
    for c in range(n_cols):
        cols = slice(c * LANES, (c + 1) * LANES)
        biasp_ref[:, cols] = jnp.where(keyp_ref[:, cols] >= thr_col, 0.0, MASK_BIAS)

    for j in range(group):
        bias_pj = biasp_ref[j * tn:(j + 1) * tn, :]
        bias_nj = bias_n[j * tn:(j + 1) * tn, :]
        for n in range(N_KV_HEADS):
            hs = slice(n * HEAD_DIM, (n + 1) * HEAD_DIM)
            k_n = ck_ref[0, j, pl.ds(n, past_len, stride=N_KV_HEADS), :].astype(BF16)
            v_n = cv_ref[0, j, pl.ds(n, past_len, stride=N_KV_HEADS), :].astype(BF16)
            qg = jnp.concatenate(
                [q_ref[j * tn:(j + 1) * tn, (n * KV_GROUP + g) * HEAD_DIM:(n * KV_GROUP + g + 1) * HEAD_DIM]
                 for g in range(KV_GROUP)], axis=0)
            s1 = lax.dot_general(qg, k_n, NT_DIMS, preferred_element_type=F32)
            s2 = lax.dot_general(qg, kn[:, hs], NT_DIMS, preferred_element_type=F32)
            s1 = (s1.reshape(KV_GROUP, tn, past_len) + bias_pj[None]).reshape(KV_GROUP * tn, past_len)
            s2 = (s2.reshape(KV_GROUP, tn, LANES) + bias_nj[None]).reshape(KV_GROUP * tn, LANES)
            m = jnp.maximum(jnp.max(s1, axis=1, keepdims=True), jnp.max(s2, axis=1, keepdims=True))
            p1 = jnp.exp2(s1 - m)
            p2 = jnp.exp2(s2 - m)
            l = jnp.sum(p1, axis=1, keepdims=True) + jnp.sum(p2, axis=1, keepdims=True)
            o = (jnp.dot(p1.astype(BF16), v_n, preferred_element_type=F32)
                 + jnp.dot(p2.astype(BF16), vn[:, hs], preferred_element_type=F32)) / l
            for g in range(KV_GROUP):
                h = n * KV_GROUP + g
                o_ref[j * tn:(j + 1) * tn, h * HEAD_DIM:(h + 1) * HEAD_DIM] = o[g * tn:(g + 1) * tn, :]


def _attn_sample(q, qi4, wi, ck, cv, cki, kb, vb, kif, layer, k_top):
    nb = ck.shape[1]
    past_len = cki.shape[2]
    kvw = N_KV_HEADS * HEAD_DIM
    tn = q.shape[0] // nb
    assert LANES % tn == 0 and tn % 16 == 0 and past_len % LANES == 0
    group = LANES // tn
    assert nb % group == 0
    row = lambda w: pl.BlockSpec((LANES, w), lambda i: (i, 0))
    cache = lambda rows, w: pl.BlockSpec((1, group, rows, w), lambda i: (layer, i, 0, 0))
    return pl.pallas_call(
        functools.partial(_attn_sample_kernel, k_top=k_top, past_len=past_len, tn=tn),
        grid=(nb // group,),
        in_specs=[row(N_HEADS * HEAD_DIM),
                  pl.BlockSpec((group, N_IDX_HEADS, tn, IDX_DIM), lambda i: (i, 0, 0, 0)),
                  row(LANES), cache(N_KV_HEADS * past_len, HEAD_DIM),
                  cache(N_KV_HEADS * past_len, HEAD_DIM), cache(past_len, IDX_DIM),
                  row(kvw), row(kvw),
                  pl.BlockSpec((1, LANES, IDX_DIM), lambda i: (layer, i, 0))],
        out_specs=row(N_HEADS * HEAD_DIM),
        out_shape=jax.ShapeDtypeStruct((nb * tn, N_HEADS * HEAD_DIM), F32),
        scratch_shapes=[pltpu.VMEM((LANES, past_len), jnp.int32),
                        pltpu.VMEM((LANES, LANES), jnp.int32),
                        pltpu.VMEM((past_len + LANES, LANES), jnp.int32),
                        pltpu.VMEM((LANES, past_len), F32)],
        compiler_params=pltpu.CompilerParams(dimension_semantics=("arbitrary",),
                                             vmem_limit_bytes=VMEM_LIMIT),
        name="attn_sample",
    )(q, qi4, wi, ck, cv, cki, kb, vb, kif)


def _outffn_kernel(x_ref, a_ref, sga_ref, sgb_ref, u_ref, vn_ref, wmix_ref, bmix_ref, wo_ref,
                   g2_ref, wg_ref, wu_ref, wd_ref, gf_ref, xo_ref, *rest, final):
    if final:
        y_ref, z_ref = rest
    else:
        (z_ref,) = rest
    tm = x_ref.shape[0]
    group_dim = wmix_ref.shape[-1]
    for c in range(tm // SGU_CHUNK):
        rs = slice(c * SGU_CHUNK, (c + 1) * SGU_CHUNK)
        for g in range(SGU_GROUPS):
            cs = slice(g * group_dim, (g + 1) * group_dim)
            mixed = jnp.dot(wmix_ref[0, g], vn_ref[rs, cs].astype(BF16),
                            preferred_element_type=F32) + bmix_ref[0, :, cs]
            z = sga_ref[rs, cs] * a_ref[rs, cs] + sgb_ref[rs, cs] * (u_ref[rs, cs] * mixed)
            z_ref[rs, cs] = z.astype(BF16)
    x1 = x_ref[...] + jnp.dot(z_ref[...], wo_ref[0], preferred_element_type=F32)
    h2 = _rms_norm(x1, g2_ref[0]).astype(BF16)
    gate = jnp.dot(h2, wg_ref[0], preferred_element_type=F32)
    up = jnp.dot(h2, wu_ref[0], preferred_element_type=F32)
    ff = (gate * _sigmoid(gate) * up).astype(BF16)
    x2 = x1 + jnp.dot(ff, wd_ref[0], preferred_element_type=F32)
    xo_ref[...] = x2
    if final:
        y_ref[...] = _rms_norm(x2, gf_ref[...])


def _outffn(x, a, sga, sgb, u, vn, wmix, bmix, layer, wo, g2, wg, wu, wd, gf, final):
    T, D = x.shape
    tm = TOKEN_TILE
    row = pl.BlockSpec((tm, D), lambda i: (i, 0))
    n_out = 2 if final else 1
    out = pl.pallas_call(
        functools.partial(_outffn_kernel, final=final),
        grid=(T // tm,),
        in_specs=[row] * 6 + [_layer_spec(wmix.shape, layer), _layer_spec(bmix.shape, layer),
                              _layer_spec(wo.shape, layer), _layer_spec(g2.shape, layer),
                              _layer_spec(wg.shape, layer), _layer_spec(wu.shape, layer),
                              _layer_spec(wd.shape, layer), _const_spec((1, D))],
        out_specs=(row,) * n_out,
        out_shape=(jax.ShapeDtypeStruct((T, D), F32),) * n_out,
        scratch_shapes=[pltpu.VMEM((tm, D), BF16)],
        compiler_params=pltpu.CompilerParams(dimension_semantics=("arbitrary",),
                                             vmem_limit_bytes=VMEM_LIMIT),
        name="outffn",
    )(x, a, sga, sgb, u, vn, wmix, bmix, wo, g2, wg, wu, wd, gf)
    return out if final else (out[0], None)


def _rope_tables(pos, d):
    inv = ROPE_THETA ** (-jnp.arange(0, d, 2, dtype=F32) / d)
    ang = pos.astype(F32)[:, None] * inv[None, :]
    cos, sin = jnp.cos(ang), jnp.sin(ang)
    c = jnp.concatenate([cos, cos], axis=-1)
    s = jnp.concatenate([-sin, sin], axis=-1)
    reps = LANES // d
    return jnp.tile(c, (1, reps)), jnp.tile(s, (1, reps))


def _mix_weights(sgu_w, sgu_b, n, group_dim):
    depth = sgu_w.shape[0]
    p = jnp.arange(n)
    mask = (p[None, :] // CHUNK) <= (p[:, None] // CHUNK)
    w = jnp.where(mask[None, None], sgu_w[:, :, :n, :n], 0.0)
    reps = SGU_CHUNK // n
    eye = jnp.eye(reps, dtype=w.dtype)
    wbd = jnp.einsum('ab,lgij->lgaibj', eye, w).reshape(depth, SGU_GROUPS, SGU_CHUNK, SGU_CHUNK)
    b = jnp.tile(sgu_b[:, :, :n], (1, 1, reps))
    bfull = jnp.repeat(jnp.swapaxes(b, 1, 2), group_dim, axis=2)
    return wbd.astype(BF16), bfull


def _in_weights(w_in, d_model):
    kvw = N_KV_HEADS * HEAD_DIM
    w = w_in
    c_ki = d_model + 2 * kvw + N_IDX_HEADS * IDX_DIM
    c_wi = c_ki + IDX_DIM
    c_u = c_wi + N_IDX_HEADS
    wki, wwi = w[:, :, c_ki:c_wi], w[:, :, c_wi:c_u]
    wwi_pad = jnp.pad(wwi, ((0, 0), (0, 0), (0, LANES - N_IDX_HEADS)))
    wa = jnp.concatenate([w[:, :, :c_ki], wki, wki, wwi_pad, w[:, :, c_u:]], axis=2)
    wv_t = jnp.swapaxes(w[:, :, d_model + kvw:d_model + 2 * kvw], 1, 2)
    wwi_t = jnp.pad(jnp.swapaxes(wwi, 1, 2), ((0, 0), (0, 16 - N_IDX_HEADS), (0, 0)))
    return wa.astype(BF16), jnp.concatenate([wv_t, wwi_t], axis=1).astype(BF16)


def kernel(x_prompt, x_sample, cache_k, cache_v, cache_kidx, norm1_g, w_in, ln_v_g, ln_v_b, sgu_w, sgu_b, w_out, norm2_g, w_gate, w_up, w_down, final_norm_g):
    B, S, D = x_prompt.shape
    NB, TN, _ = x_sample.shape
    depth, _, P = cache_k.shape[:3]
    assert D == N_HEADS * HEAD_DIM and SGU_CHUNK % TN == 0 and S % SGU_CHUNK == 0
    k_top_p = min(TOPK_MAX, S // 4)
    k_top_s = min(TOPK_MAX, (P + TN) // 4)
    group_dim = D // SGU_GROUPS

    assert S % TOKEN_TILE == 0 and TOKEN_TILE % TN == 0
    pos_p = jnp.arange(S)
    pos_s = jnp.tile(P + jnp.arange(TN), TOKEN_TILE // TN)
    tab_p = _rope_tables(pos_p, HEAD_DIM) + _rope_tables(pos_p, IDX_DIM)
    tab_s = _rope_tables(pos_s, HEAD_DIM) + _rope_tables(pos_s, IDX_DIM)

    wa, wb = _in_weights(w_in, D)
    wo, wg, wu, wd = (t.astype(BF16) for t in (w_out, w_gate, w_up, w_down))
    vec = lambda v: v.reshape(depth, 1, -1)
    g1, g2, lng, lnb = vec(norm1_g), vec(norm2_g), vec(ln_v_g), vec(ln_v_b)
    gf = final_norm_g.reshape(1, -1)
    wmix_p, bmix_p = _mix_weights(sgu_w, sgu_b, SGU_CHUNK, group_dim)
    wmix_s, bmix_s = _mix_weights(sgu_w, sgu_b, TN, group_dim)
    ck = cache_k.reshape(depth, NB, P * N_KV_HEADS, HEAD_DIM)
    cv = cache_v.reshape(depth, NB, P * N_KV_HEADS, HEAD_DIM)

    xp = x_prompt.reshape(B * S, D)
    xs = x_sample.reshape(NB * TN, D)
    def new_buffers(tokens):
        kv = (depth, N_KV_HEADS * tokens, HEAD_DIM)
        return jnp.zeros(kv, F32), jnp.zeros(kv, F32), jnp.zeros((depth, tokens, IDX_DIM), F32)

    new_p, new_s = new_buffers(B * S), new_buffers(NB * TN)
    sgu_v = []
    yp = ys = None
    for l in range(depth):
        final = l == depth - 1

        (q, kf, kb, vf, _, vt, qi, kif, kze, kzo, _, wit, u, vn, sga, sgb) = _inproj(
            xp, l, g1, wa, wb, tab_p, lng, lnb, new_p)
        new_p = (kf, vf, kif)
        a = _attn_prompt(q, qi, wit, kze, kzo, kb, vt, B, S, k_top_p)
        xp, yp = _outffn(xp, a, sga, sgb, u, vn, wmix_p, bmix_p, l, wo, g2, wg, wu, wd, gf, final)

        (q, kf, kb, vf, vb, _, qi, kif, _, _, wi, _, u, vn, sga, sgb) = _inproj(
            xs, l, g1, wa, wb, tab_s, lng, lnb, new_s)
        new_s = (kf, vf, kif)
        qi4 = qi.reshape(NB, TN, N_IDX_HEADS, IDX_DIM).transpose(0, 2, 1, 3)
        a = _attn_sample(q, qi4, wi, ck, cv, cache_kidx, kb, vb, kif, l, k_top_s)
        xs, ys = _outffn(xs, a, sga, sgb, u, vn, wmix_s, bmix_s, l, wo, g2, wg, wu, wd, gf, final)
        sgu_v.append(vn.reshape(NB, TN, D))

    kv_p = (depth, B, S, N_KV_HEADS, HEAD_DIM)
    kv_s = (depth, NB, TN, N_KV_HEADS, HEAD_DIM)
    return (yp.reshape(B, S, D), ys.reshape(NB, TN, D),
            new_p[0].reshape(kv_p), new_p[1].reshape(kv_p), new_p[2].reshape(depth, B, S, IDX_DIM),
            new_s[0].reshape(kv_s), new_s[1].reshape(kv_s), new_s[2].reshape(depth, NB, TN, IDX_DIM),
            jnp.stack(sgu_v))
```

```python
import functools

import jax
import jax.numpy as jnp
from jax import lax
from jax.experimental import pallas as pl
from jax.experimental.pallas import tpu as pltpu

CHUNK = 64
N_HEADS = 8
HEAD_DIM = 128
N_KV_HEADS = 2
KV_GROUP = N_HEADS // N_KV_HEADS
N_IDX_HEADS = 8
IDX_DIM = 64
TOPK_MAX = 256
SGU_CHUNK = 128
SGU_GROUPS = 8
ROPE_THETA = 10000.0
EPS = 1e-6

LANES = 128
TOKEN_TILE = 256
Q_TILE = 256
KEY_TILE = 256
VT_ROWS = HEAD_DIM + 16
FOLD = 8
WINDOW_BITS = 9
VMEM_LIMIT = 56 * 1024 * 1024

INT_MIN = -(2 ** 31)
INT_MAX = 2 ** 31 - 1
MASK_BIAS = -1e30
LOG2E = 1.4426950408889634
Q_SCALE = (HEAD_DIM ** -0.5) * LOG2E

F32 = jnp.float32
BF16 = jnp.bfloat16
NT_DIMS = (((1,), (1,)), ((), ()))


def _const_spec(shape):
    nd = len(shape)
    return pl.BlockSpec(shape, lambda *_: (0,) * nd, pipeline_mode=pl.Buffered(1))


def _sigmoid(x):
    return 1.0 / (1.0 + jnp.exp(-x))


def _rms_norm(x, g):
    return x * lax.rsqrt(jnp.mean(x * x, axis=-1, keepdims=True) + EPS) * g


def _order_key(score):
    bits = pltpu.bitcast(score, jnp.int32)
    return bits ^ ((bits >> 31) & INT_MAX)


def _midpoint(lo, hi):
    return (lo >> 1) + (hi >> 1) + (lo & hi & 1)


_C_Q = 0
_C_K = _C_Q + N_HEADS * HEAD_DIM
_C_V = _C_K + N_KV_HEADS * HEAD_DIM
_C_QI = _C_V + N_KV_HEADS * HEAD_DIM
_C_KI = _C_QI + N_IDX_HEADS * IDX_DIM
_C_WI = _C_KI + LANES
_C_U = _C_WI + LANES


def _inproj_kernel(*refs, d_model, n_alias):
    (x_ref, g_ref, wa_ref, wb_ref, cosh_ref, sinh_ref, cosi_ref, sini_ref,
     lng_ref, lnb_ref) = refs[:10]
    (q_ref, kf_ref, kb_ref, vf_ref, vb_ref, vt_ref, qi_ref, kif_ref, kze_ref, kzo_ref,
     wi_ref, wit_ref, u_ref, vn_ref, sga_ref, sgb_ref) = refs[10 + n_alias:]
    tm = x_ref.shape[0]
    hb = _rms_norm(x_ref[...], g_ref[0]).astype(BF16)

    def proj(c0, width):
        return jnp.dot(hb, wa_ref[0, :, c0:c0 + width], preferred_element_type=F32)

    def head_rows(n):
        return pl.ds(n, tm, stride=N_KV_HEADS)

    cosh, sinh = cosh_ref[...], sinh_ref[...]
    cosi, sini = cosi_ref[...], sini_ref[...]
    lane = lax.broadcasted_iota(jnp.int32, cosi.shape, 1)
    first_half = (lane % IDX_DIM) < (IDX_DIM // 2)

    def rope_head(x):
        return x * cosh + pltpu.roll(x, HEAD_DIM // 2, 1) * sinh

    def rope_idx(x):
        partner = jnp.where(first_half, pltpu.roll(x, LANES - IDX_DIM // 2, 1),
                            pltpu.roll(x, IDX_DIM // 2, 1))
        return x * cosi + partner * sini

    xq = proj(_C_Q, N_HEADS * HEAD_DIM)
    for h in range(N_HEADS):
        sl = slice(h * HEAD_DIM, (h + 1) * HEAD_DIM)
        q_ref[:, sl] = (rope_head(xq[:, sl]) * Q_SCALE).astype(BF16)

    xk = proj(_C_K, N_KV_HEADS * HEAD_DIM)
    for h in range(N_KV_HEADS):
        sl = slice(h * HEAD_DIM, (h + 1) * HEAD_DIM)
        kr = rope_head(xk[:, sl])
        kf_ref[0, head_rows(h), :] = kr
        kb_ref[:, sl] = kr.astype(BF16)

    xv = proj(_C_V, N_KV_HEADS * HEAD_DIM)
    vb_ref[...] = xv.astype(BF16)
    for h in range(N_KV_HEADS):
        vf_ref[0, head_rows(h), :] = xv[:, h * HEAD_DIM:(h + 1) * HEAD_DIM]

    xqi = proj(_C_QI, N_IDX_HEADS * IDX_DIM)
    for p in range(N_IDX_HEADS * IDX_DIM // LANES):
        sl = slice(p * LANES, (p + 1) * LANES)
        qi_ref[:, sl] = (rope_idx(xqi[:, sl]) * (IDX_DIM ** -0.5)).astype(BF16)

    kk = rope_idx(proj(_C_KI, LANES))
    kif_ref[0] = kk[:, :IDX_DIM]
    low = lane < IDX_DIM
    kze_ref[...] = jnp.where(low, kk, 0.0).astype(BF16)
    kzo_ref[...] = jnp.where(low, 0.0, kk).astype(BF16)

    wi_ref[...] = proj(_C_WI, LANES) * (N_IDX_HEADS ** -0.5)

    tb = lax.dot_general(wb_ref[0], hb, NT_DIMS, preferred_element_type=F32)
    kvw = N_KV_HEADS * HEAD_DIM
    ones_rows = (lax.broadcasted_iota(jnp.int32, (VT_ROWS - HEAD_DIM, tm), 0) == 0).astype(BF16)
    for n in range(N_KV_HEADS):
        vt_ref[0, n * VT_ROWS:n * VT_ROWS + HEAD_DIM, :] = (
            tb[n * HEAD_DIM:(n + 1) * HEAD_DIM].astype(BF16))
        vt_ref[0, n * VT_ROWS + HEAD_DIM:(n + 1) * VT_ROWS, :] = ones_rows
    wit_ref[...] = tb[kvw:] * (N_IDX_HEADS ** -0.5)

    u_ref[...] = jax.nn.gelu(proj(_C_U, d_model), approximate=True)
    gv = jax.nn.gelu(proj(_C_U + d_model, d_model), approximate=True)
    mu = jnp.mean(gv, axis=-1, keepdims=True)
    dv = gv - mu
    var = jnp.mean(dv * dv, axis=-1, keepdims=True)
    vn_ref[...] = dv * lax.rsqrt(var + EPS) * lng_ref[0] + lnb_ref[0]
    sga_ref[...] = _sigmoid(proj(_C_U + 2 * d_model, d_model))
    sgb_ref[...] = _sigmoid(proj(_C_U + 3 * d_model, d_model))


def _layer_spec(shape, layer):
    nd = len(shape)
    return pl.BlockSpec((1,) + tuple(shape[1:]), lambda *_: (layer,) + (0,) * (nd - 1),
                        pipeline_mode=pl.Buffered(1))


def _inproj(x, layer, g, wa, wb, tables, lng, lnb, stacked):
    T, D = x.shape
    depth = wa.shape[0]
    tm = TOKEN_TILE
    assert T % tm == 0
    kvw = N_KV_HEADS * HEAD_DIM
    qiw = N_IDX_HEADS * IDX_DIM
    row = lambda w: pl.BlockSpec((tm, w), lambda i: (i, 0))
    kv_rows = N_KV_HEADS * T
    out_shape = (
        jax.ShapeDtypeStruct((T, N_HEADS * HEAD_DIM), BF16),
        jax.ShapeDtypeStruct((depth, kv_rows, HEAD_DIM), F32),
        jax.ShapeDtypeStruct((T, kvw), BF16),
        jax.ShapeDtypeStruct((depth, kv_rows, HEAD_DIM), F32),
        jax.ShapeDtypeStruct((T, kvw), BF16),
        jax.ShapeDtypeStruct((T // tm, N_KV_HEADS * VT_ROWS, tm), BF16),
        jax.ShapeDtypeStruct((T, qiw), BF16),
        jax.ShapeDtypeStruct((depth, T, IDX_DIM), F32),
        jax.ShapeDtypeStruct((T, LANES), BF16),
        jax.ShapeDtypeStruct((T, LANES), BF16),
        jax.ShapeDtypeStruct((T, LANES), F32),
        jax.ShapeDtypeStruct((16, T), F32),
        jax.ShapeDtypeStruct((T, D), F32),
        jax.ShapeDtypeStruct((T, D), F32),
        jax.ShapeDtypeStruct((T, D), F32),
        jax.ShapeDtypeStruct((T, D), F32),
    )
    kv_spec = pl.BlockSpec((1, N_KV_HEADS * tm, HEAD_DIM), lambda i: (layer, i, 0))
    out_specs = (
        row(N_HEADS * HEAD_DIM), kv_spec, row(kvw), kv_spec, row(kvw),
        pl.BlockSpec((1, N_KV_HEADS * VT_ROWS, tm), lambda i: (i, 0, 0)),
        row(qiw), pl.BlockSpec((1, tm, IDX_DIM), lambda i: (layer, i, 0)),
        row(LANES), row(LANES), row(LANES),
        pl.BlockSpec((16, tm), lambda i: (0, i)),
        row(D), row(D), row(D), row(D),
    )
    n_tab = tables[0].shape[0] // tm
    tab = pl.BlockSpec((tm, LANES), lambda i: (i % n_tab, 0))
    in_specs = [row(D), _layer_spec(g.shape, layer), _layer_spec(wa.shape, layer),
                _layer_spec(wb.shape, layer),
                tab, tab, tab, tab,
                _layer_spec(lng.shape, layer), _layer_spec(lnb.shape, layer)]
    args = [x, g, wa, wb, *tables, lng, lnb]
    aliases = {}
    for j, out_idx in enumerate((1, 3, 7)):
        aliases[len(args)] = out_idx
        in_specs.append(pl.BlockSpec(memory_space=pl.ANY))
        args.append(stacked[j])
    return pl.pallas_call(
        functools.partial(_inproj_kernel, d_model=D, n_alias=len(aliases)),
        grid=(T // tm,),
        in_specs=in_specs, out_specs=out_specs, out_shape=out_shape,
        input_output_aliases=aliases,
        compiler_params=pltpu.CompilerParams(dimension_semantics=("arbitrary",),
                                             vmem_limit_bytes=VMEM_LIMIT),
        name="inproj",
    )(*args)


def _attn_prompt_kernel(q_ref, qi_ref, wit_ref, kze_ref, kzo_ref, k_ref, vt_ref, o_ref,
                        keys_ref, hi_ref, lo_ref, f1_ref, f2_ref, kmax_ref, acc_ref, m_ref, mx_ref, s_ref, p_ref,
                        bias_ref, *, k_top):
    tq, kb_sz = Q_TILE, KEY_TILE
    qb = pl.program_id(1)
    nkb = (qb + 1) * (tq // kb_sz)
    q_chunk = (qb * tq + lax.broadcasted_iota(jnp.int32, (1, tq), 1)) // CHUNK
    row_iota = lax.broadcasted_iota(jnp.int32, (kb_sz, 1), 0)

    def key_slice(kb):
        return pl.ds(pl.multiple_of(kb * kb_sz, kb_sz), kb_sz)

    n_blocks = keys_ref.shape[0] // kb_sz
    n_full = nkb // 2

    def clamp(blk):
        return jnp.minimum(blk, n_blocks - 1)

    def idx_dots(blk, slot):
        ks = key_slice(blk)
        kze, kzo = kze_ref[ks, :], kzo_ref[ks, :]
        for p in range(N_IDX_HEADS // 2):
            qp = qi_ref[:, p * LANES:(p + 1) * LANES]
            s_ref[slot, 2 * p] = lax.dot_general(kze, qp, NT_DIMS, preferred_element_type=F32)
            s_ref[slot, 2 * p + 1] = lax.dot_general(kzo, qp, NT_DIMS, preferred_element_type=F32)

    def idx_keys(blk, slot):
        base = pl.multiple_of(blk * kb_sz, kb_sz)
        kmax = kmax_ref[...]
        for c in range(kb_sz // CHUNK):
            rows = slice(c * CHUNK, (c + 1) * CHUNK)
            score = jnp.zeros((CHUNK, tq), F32)
            for j in range(N_IDX_HEADS):
                score = score + wit_ref[j:j + 1, :] * jnp.maximum(s_ref[slot, j, rows, :], 0.0)
            visible = (blk * (kb_sz // CHUNK) + c) <= q_chunk
            key = jnp.where(visible, _order_key(score), INT_MIN)
            ks = pl.ds(base + c * CHUNK, CHUNK)
            keys_ref[ks, :] = key
            hi_ref[ks, :] = (key >> 16).astype(jnp.int16)
            lo_ref[ks, :] = (key ^ 0x8000).astype(jnp.int16)
            kmax = jnp.maximum(kmax, jnp.max(key.reshape(CHUNK // 8, 8, tq), axis=0))
        kmax_ref[...] = kmax

    def score_body(j, carry):
        b0 = 2 * j
        idx_dots(b0 + 1, 1)
        idx_keys(b0, 0)
        idx_dots(clamp(b0 + 2), 0)
        idx_keys(b0 + 1, 1)
        return carry

    kmax_ref[...] = jnp.full(kmax_ref.shape, INT_MIN, jnp.int32)
    idx_dots(0, 0)
    lax.fori_loop(0, n_full, score_body, 0)

    @pl.when(nkb % 2 == 1)
    def _():
        idx_keys(nkb - 1, 0)
        keys_ref[key_slice(nkb), :] = jnp.full((kb_sz, tq), INT_MIN, jnp.int32)

    n_acc = 4
    rows16 = 16

    def fold_rows(x, accs):
        accs = list(accs)
        for r in range(kb_sz // rows16):
            accs[r % n_acc] = accs[r % n_acc] + x[r * rows16:(r + 1) * rows16, :]
        return tuple(accs)

    def zero_accs(width=tq):
        return tuple(jnp.zeros((rows16, width), jnp.int16) for _ in range(n_acc))

    def total(accs):
        tot = accs[0].astype(jnp.int32)
        for a in accs[1:]:
            tot = tot + a.astype(jnp.int32)
        return jnp.sum(tot, axis=0, keepdims=True)

    def count_ge(plane_ref, t, cols=slice(None)):
        width = t.shape[1]
        t16 = jnp.broadcast_to(t, (kb_sz, width)).astype(jnp.int16)

        def body(kb, accs):
            ge = (plane_ref[key_slice(kb), cols] >= t16).astype(jnp.int16)
            return fold_rows(ge, accs)

        return total(lax.fori_loop(0, nkb, body, zero_accs(width)))

    def narrow(count, need, lo, hi, count_lo, steps):
        def body(_, carry):
            lo, hi, n_lo = carry
            mid = (lo + hi) >> 1
            n_mid = count(mid)
            ok = n_mid >= need
            return jnp.where(ok, mid, lo), jnp.where(ok, hi, mid), jnp.where(ok, n_mid, n_lo)

        return lax.fori_loop(0, steps, body, (lo, hi, count_lo))

    def bisect16(count, need, lo, hi, count_lo, steps):
        lo, _, n_lo = narrow(count, need, lo, hi, count_lo, steps)
        return lo, n_lo

    floor16 = jnp.full((1, tq), -(2 ** 15), jnp.int32)
    ceil16 = jnp.full((1, tq), 2 ** 15, jnp.int32)
    every = jnp.full((1, tq), keys_ref.shape[0], jnp.int32)
    count_hi = functools.partial(count_ge, hi_ref)
    top16 = (jnp.max(kmax_ref[...], axis=0, keepdims=True) >> 16) + 1
    lo_try = jnp.maximum(top16 - 2 ** WINDOW_BITS, -(2 ** 15) + 1)
    n_try = count_hi(lo_try)
    in_window = n_try >= k_top
    lo1, hi1, n1 = narrow(count_hi, k_top, jnp.where(in_window, lo_try, floor16), top16,
                          jnp.where(in_window, n_try, every), WINDOW_BITS)
    stage_one = []
    for c in range(tq // LANES):
        cs = slice(c * LANES, (c + 1) * LANES)
        steps = jnp.where(jnp.max(hi1[:, cs] - lo1[:, cs]) > 1, 16 - WINDOW_BITS, 0)
        stage_one.append(bisect16(functools.partial(count_ge, hi_ref, cols=cs), k_top,
                                  lo1[:, cs], hi1[:, cs], n1[:, cs], steps))
    t_hi = jnp.concatenate([t for t, _ in stage_one], axis=1)
    n_hi = jnp.concatenate([n for _, n in stage_one], axis=1)

    t_hi16 = jnp.broadcast_to(t_hi, (kb_sz, tq)).astype(jnp.int16)
    sentinel = jnp.int16(-(2 ** 15))
    n_slots = kb_sz // rows16 // FOLD

    def max_min(a, b):
        a_ge = a >= b
        return jnp.where(a_ge, a, b), jnp.where(a_ge, b, a)

    def tie_body(kb, accs):
        ks = key_slice(kb)
        hi_part = hi_ref[ks, :]
        e = jnp.where(hi_part == t_hi16, lo_ref[ks, :], sentinel)
        lo_ref[ks, :] = e
        g = [e[r * rows16:(r + 1) * rows16, :] for r in range(kb_sz // rows16)]
        n = len(g) // 2
        pairs = [max_min(g[i], g[i + n]) for i in range(n)]
        first, second = [p[0] for p in pairs], [p[1] for p in pairs]
        while n > n_slots:
            n //= 2
            pairs = [max_min(first[i], first[i + n]) for i in range(n)]
            second = [max_min(pairs[i][1], max_min(second[i], second[i + n])[0])[0] for i in range(n)]
            first = [p[0] for p in pairs]
        fs = pl.ds(pl.multiple_of(kb * (n_slots * rows16), n_slots * rows16), n_slots * rows16)
        f1_ref[fs, :] = jnp.concatenate(first, axis=0)
        f2_ref[fs, :] = jnp.concatenate(second, axis=0)
        return fold_rows((hi_part > t_hi16).astype(jnp.int16), accs)

    n_above = total(lax.fori_loop(0, nkb, tie_body, zero_accs()))
    need_lo = k_top - n_above

    def count_folded(t):
        t16 = jnp.broadcast_to(t, (n_slots * rows16, tq)).astype(jnp.int16)

        def body(kb, accs):
            fs = pl.ds(pl.multiple_of(kb * (n_slots * rows16), n_slots * rows16), n_slots * rows16)
            accs = list(accs)
            for i, plane in enumerate((f1_ref, f2_ref)):
                ge = (plane[fs, :] >= t16).astype(jnp.int16)
                for r in range(n_slots):
                    a = (i * n_slots + r) % n_acc
                    accs[a] = accs[a] + ge[r * rows16:(r + 1) * rows16, :]
            return tuple(accs)

        return total(lax.fori_loop(0, nkb, body, zero_accs()))

    def count_lo_pair(t):
        t_next = jnp.minimum(t + 1, 2 ** 15 - 1)
        ta = jnp.broadcast_to(t, (kb_sz, tq)).astype(jnp.int16)
        tb = jnp.broadcast_to(t_next, (kb_sz, tq)).astype(jnp.int16)

        def body(kb, accs):
            plane = lo_ref[key_slice(kb), :]
            return (fold_rows((plane >= ta).astype(jnp.int16), accs[0]),
                    fold_rows((plane >= tb).astype(jnp.int16), accs[1]))

        a, b = lax.fori_loop(0, nkb, body, (zero_accs(), zero_accs()))
        return total(a), jnp.where(t < 2 ** 15 - 1, total(b), 0)

    t_lo, _ = bisect16(count_folded, need_lo, floor16, ceil16, every, 16)
    n_at, n_next = count_lo_pair(t_lo)

    def unfolded_search():
        t, _ = bisect16(functools.partial(count_ge, lo_ref), need_lo, floor16, ceil16, every, 16)
        return (t,) + count_lo_pair(t)

    t_lo, n_at, n_next = lax.cond(jnp.max((n_next >= need_lo).astype(jnp.int32)) > 0,
                                  unfolded_search, lambda: (t_lo, n_at, n_next))
    thr = jnp.maximum((t_hi << 16) + (t_lo + 2 ** 15), INT_MIN + 1)

    n_selected = jnp.where(t_hi > -(2 ** 15),
                           jnp.where(t_lo > -(2 ** 15), n_above + n_at, n_hi), 0)
    need_eq = (need_lo - n_next).astype(F32)
    col_iota = lax.broadcasted_iota(jnp.int32, (1, kb_sz), 1)

    for c in range(tq // LANES):
        cs = slice(c * LANES, (c + 1) * LANES)

        @pl.when(jnp.max(n_selected[:, cs]) > k_top)
        def _(cs=cs):
            tri = jnp.where(col_iota <= row_iota, 1.0, 0.0).astype(BF16)
            thr_c, need_c = thr[:, cs], need_eq[:, cs]

            unroll = 4

            def drop(j, seen):
                blocks = [unroll * j + i for i in range(unroll)]
                keys = [keys_ref[key_slice(clamp(b)), cs] for b in blocks]
                equal = [(key == thr_c) & (b < nkb) for b, key in zip(blocks, keys)]
                local = [jnp.dot(tri, jnp.where(e, 1.0, 0.0).astype(BF16),
                                 preferred_element_type=F32) for e in equal]
                for b, key, e, rank in zip(blocks, keys, equal, local):
                    rank = rank + seen
                    keys_ref[key_slice(clamp(b)), cs] = jnp.where(e & (rank > need_c), key - 1, key)
                    seen = rank[kb_sz - 1:kb_sz, :]
                return seen

            lax.fori_loop(0, (nkb + unroll - 1) // unroll, drop, jnp.zeros((1, LANES), F32))

    m_ref[...] = jnp.full(m_ref.shape, MASK_BIAS, F32)
    acc_ref[...] = jnp.zeros(acc_ref.shape, F32)

    rc = 32
    n_rc = kb_sz // rc

    def fold8(x):
        return x.reshape(rc // 8, 8, tq)

    def qk_dots(blk, slot):
        ks = key_slice(blk)
        for h in range(N_HEADS):
            n = h // KV_GROUP
            k_n = k_ref[ks, n * HEAD_DIM:(n + 1) * HEAD_DIM]
            q_h = q_ref[:, h * HEAD_DIM:(h + 1) * HEAD_DIM]
            s_ref[slot, h] = lax.dot_general(k_n, q_h, NT_DIMS, preferred_element_type=F32)

    def softmax_block(blk, slot):
        thr_b = jnp.where(blk < nkb, thr, INT_MAX)
        bias_ref[slot] = jnp.where(keys_ref[key_slice(clamp(blk)), :] >= thr_b, 0.0, MASK_BIAS)
        for h in range(N_HEADS):
            mx = jnp.full((8, tq), MASK_BIAS, F32)
            for r in range(n_rc):
                rows = slice(r * rc, (r + 1) * rc)
                sc = s_ref[slot, h, rows, :] + bias_ref[slot, rows, :]
                s_ref[slot, h, rows, :] = sc
                mx = jnp.maximum(mx, jnp.max(fold8(sc), axis=0))
            mx_ref[h:h + 1, :] = jnp.max(mx, axis=0, keepdims=True)
        m_old = m_ref[...]
        m_new = jnp.maximum(m_old, mx_ref[...])
        alpha = jnp.exp2(m_old - m_new)
        m_ref[...] = m_new
        for h in range(N_HEADS):
            m_h = jnp.broadcast_to(m_new[h:h + 1, :], (rc, tq))
            for r in range(n_rc):
                rows = slice(r * rc, (r + 1) * rc)
                p_ref[slot, h, rows, :] = jnp.exp2(s_ref[slot, h, rows, :] - m_h).astype(BF16)
        return alpha

    def pv_update(blk, slot, alpha):
        for h in range(N_HEADS):
            n = h // KV_GROUP
            vt_n = vt_ref[blk, n * VT_ROWS:(n + 1) * VT_ROWS, :]
            pv = jnp.dot(vt_n, p_ref[slot, h], preferred_element_type=F32)
            acc_ref[h] = alpha[h:h + 1, :] * acc_ref[h] + pv

    def attn_body(j, carry):
        b0 = 2 * j
        qk_dots(clamp(b0 + 1), 1)
        pv_update(b0, 0, softmax_block(b0, 0))
        qk_dots(clamp(b0 + 2), 0)
        pv_update(clamp(b0 + 1), 1, softmax_block(b0 + 1, 1))
        return carry

    qk_dots(0, 0)
    lax.fori_loop(0, (nkb + 1) // 2, attn_body, 0)

    for h in range(N_HEADS):
        o_t = acc_ref[h, :HEAD_DIM, :] / acc_ref[h, HEAD_DIM:HEAD_DIM + 1, :]
        o_ref[:, h * HEAD_DIM:(h + 1) * HEAD_DIM] = o_t.T


def _attn_prompt(q, qi, wit, kze, kzo, kb, vt, batch, seq, k_top):
    tq = Q_TILE
    assert seq % tq == 0 and Q_TILE % KEY_TILE == 0 and KEY_TILE == TOKEN_TILE
    assert (seq // KEY_TILE) % 4 == 0
    nq = seq // tq
    kvw = N_KV_HEADS * HEAD_DIM
    qrow = lambda w: pl.BlockSpec((tq, w), lambda b, i: (b * nq + i, 0))
    per_batch = lambda w: pl.BlockSpec((seq, w), lambda b, i: (b, 0), pipeline_mode=pl.Buffered(1))
    return pl.pallas_call(
        functools.partial(_attn_prompt_kernel, k_top=k_top),
        grid=(batch, nq),
        in_specs=[qrow(N_HEADS * HEAD_DIM), qrow(N_IDX_HEADS * IDX_DIM),
                  pl.BlockSpec((16, tq), lambda b, i: (0, b * nq + i)),
                  per_batch(LANES), per_batch(LANES), per_batch(kvw),
                  pl.BlockSpec((seq // KEY_TILE, N_KV_HEADS * VT_ROWS, KEY_TILE), lambda b, i: (b, 0, 0),
                               pipeline_mode=pl.Buffered(1))],
        out_specs=qrow(N_HEADS * HEAD_DIM),
        out_shape=jax.ShapeDtypeStruct((batch * seq, N_HEADS * HEAD_DIM), F32),
        scratch_shapes=[pltpu.VMEM((seq, tq), jnp.int32),
                        pltpu.VMEM((seq, tq), jnp.int16),
                        pltpu.VMEM((seq, tq), jnp.int16),
                        pltpu.VMEM((seq // FOLD, tq), jnp.int16),
                        pltpu.VMEM((seq // FOLD, tq), jnp.int16),
                        pltpu.VMEM((8, tq), jnp.int32),
                        pltpu.VMEM((N_HEADS, VT_ROWS, tq), F32),
                        pltpu.VMEM((N_HEADS, tq), F32),
                        pltpu.VMEM((N_HEADS, tq), F32),
                        pltpu.VMEM((2, N_HEADS, KEY_TILE, tq), F32),
                        pltpu.VMEM((2, N_HEADS, KEY_TILE, tq), BF16),
                        pltpu.VMEM((2, KEY_TILE, tq), F32)],
        compiler_params=pltpu.CompilerParams(dimension_semantics=("arbitrary", "arbitrary"),
                                             vmem_limit_bytes=VMEM_LIMIT),
        name="attn_prompt",
    )(q, qi, wit, kze, kzo, kb, vt)


def _attn_sample_kernel(q_ref, qi_ref, wi_ref, ck_ref, cv_ref, cki_ref, kn_ref, vn_ref, kin_ref,
                        o_ref, keyp_ref, keyn_ref, keyt_ref, biasp_ref, *, k_top, past_len, tn):
    group = LANES // tn
    rows = group * tn
    n_cols = past_len // LANES
    kn = kn_ref[...]
    vn = vn_ref[...]
    kin = kin_ref[0].astype(BF16)

    lane = lax.broadcasted_iota(jnp.int32, (tn, LANES), 1)
    pos = lax.broadcasted_iota(jnp.int32, (tn, LANES), 0)
    vis = ((past_len + lane % tn) // CHUNK) <= ((past_len + pos) // CHUNK)
    for j in range(group):
        qj = qi_ref[j].reshape(N_IDX_HEADS * tn, IDX_DIM)
        dp = lax.dot_general(qj, cki_ref[0, j].astype(BF16), NT_DIMS, preferred_element_type=F32)
        dn = lax.dot_general(qj, kin, NT_DIMS, preferred_element_type=F32)
        sp = jnp.zeros((tn, past_len), F32)
        sn = jnp.zeros((tn, LANES), F32)
        for h in range(N_IDX_HEADS):
            w = wi_ref[j * tn:(j + 1) * tn, h:h + 1]
            sp = sp + w * jnp.maximum(dp[h * tn:(h + 1) * tn, :], 0.0)
            sn = sn + w * jnp.maximum(dn[h * tn:(h + 1) * tn, :], 0.0)
        keyp_ref[j * tn:(j + 1) * tn, :] = _order_key(sp)
        own = (lane // tn) == j
        keyn_ref[j * tn:(j + 1) * tn, :] = jnp.where(own & vis, _order_key(sn), INT_MIN)

    for c in range(n_cols):
        keyt_ref[c * LANES:(c + 1) * LANES, :] = keyp_ref[:, c * LANES:(c + 1) * LANES].T
    keyt_ref[past_len:past_len + LANES, :] = keyn_ref[...].T
    n_keys = past_len + LANES

    def bisect_body(_, carry):
        lo, hi = carry
        mid = _midpoint(lo, hi)
        ge = (keyt_ref[...] >= mid).astype(jnp.int32)
        cnt = jnp.sum(jnp.sum(ge.reshape(n_keys // 8, 8, rows), axis=0), axis=0, keepdims=True)
        ok = cnt >= k_top
        return jnp.where(ok, mid, lo), jnp.where(ok, hi, mid)

    thr, _ = lax.fori_loop(
        0, 32, bisect_body,
        (jnp.full((1, rows), INT_MIN + 1, jnp.int32), jnp.full((1, rows), INT_MAX, jnp.int32)))
    thr_col = jnp.broadcast_to(thr, (rows, rows)).T

    key_row = lax.broadcasted_iota(jnp.int32, (n_keys, 1), 0)

    def count_keys(pred):
        hit = pred(keyt_ref[...], key_row).astype(jnp.int32)
        return jnp.sum(jnp.sum(hit.reshape(n_keys // 8, 8, rows), axis=0), axis=0, keepdims=True)

    n_selected = count_keys(lambda key, row: key >= thr)

    @pl.when(jnp.max(n_selected) > k_top)
    def _():
        need_eq = k_top - count_keys(lambda key, row: key > thr)

        def body(_, carry):
            lo, hi = carry
            mid = (lo + hi) >> 1
            ok = count_keys(lambda key, row: (key == thr) & (row < mid)) >= need_eq
            return jnp.where(ok, lo, mid), jnp.where(ok, mid, hi)

        _, cut = lax.fori_loop(0, n_keys.bit_length(), body,
                               (jnp.zeros((1, rows), jnp.int32), jnp.full((1, rows), n_keys, jnp.int32)))
        cut_col = jnp.broadcast_to(cut, (rows, rows)).T
        lane_idx = lax.broadcasted_iota(jnp.int32, (rows, LANES), 1)
        for c in range(n_cols + 1):
            ref, cols = (keyp_ref, slice(c * LANES, (c + 1) * LANES)) if c < n_cols else (keyn_ref, slice(None))
            key = ref[:, cols]
            surplus = (key == thr_col) & (c * LANES + lane_idx >= cut_col)
            ref[:, cols] = jnp.where(surplus, key - 1, key)

    bias_n = jnp.where(keyn_ref[...] >= thr_col, 0.0, MASK_BIAS)
    for c in range(n_cols):
        cols = slice(c * LANES, (c + 1) * LANES)
        biasp_ref[:, cols] = jnp.where(keyp_ref[:, cols] >= thr_col, 0.0, MASK_BIAS)

    for j in range(group):
        bias_pj = biasp_ref[j * tn:(j + 1) * tn, :]
        bias_nj = bias_n[j * tn:(j + 1) * tn, :]
        for n in range(N_KV_HEADS):
            hs = slice(n * HEAD_DIM, (n + 1) * HEAD_DIM)
            k_n = ck_ref[0, j, pl.ds(n, past_len, stride=N_KV_HEADS), :].astype(BF16)
            v_n = cv_ref[0, j, pl.ds(n, past_len, stride=N_KV_HEADS), :].astype(BF16)
            qg = jnp.concatenate(
                [q_ref[j * tn:(j + 1) * tn, (n * KV_GROUP + g) * HEAD_DIM:(n * KV_GROUP + g + 1) * HEAD_DIM]
                 for g in range(KV_GROUP)], axis=0)
            s1 = lax.dot_general(qg, k_n, NT_DIMS, preferred_element_type=F32)
            s2 = lax.dot_general(qg, kn[:, hs], NT_DIMS, preferred_element_type=F32)
            s1 = (s1.reshape(KV_GROUP, tn, past_len) + bias_pj[None]).reshape(KV_GROUP * tn, past_len)
            s2 = (s2.reshape(KV_GROUP, tn, LANES) + bias_nj[None]).reshape(KV_GROUP * tn, LANES)
            m = jnp.maximum(jnp.max(s1, axis=1, keepdims=True), jnp.max(s2, axis=1, keepdims=True))
            p1 = jnp.exp2(s1 - m)
            p2 = jnp.exp2(s2 - m)
            l = jnp.sum(p1, axis=1, keepdims=True) + jnp.sum(p2, axis=1, keepdims=True)
            o = (jnp.dot(p1.astype(BF16), v_n, preferred_element_type=F32)
                 + jnp.dot(p2.astype(BF16), vn[:, hs], preferred_element_type=F32)) / l
            for g in range(KV_GROUP):
                h = n * KV_GROUP + g
                o_ref[j * tn:(j + 1) * tn, h * HEAD_DIM:(h + 1) * HEAD_DIM] = o[g * tn:(g + 1) * tn, :]


def _attn_sample(q, qi4, wi, ck, cv, cki, kb, vb, kif, layer, k_top):
    nb = ck.shape[1]
    past_len = cki.shape[2]
    kvw = N_KV_HEADS * HEAD_DIM
    tn = q.shape[0] // nb
    assert LANES % tn == 0 and tn % 16 == 0 and past_len % LANES == 0
    group = LANES // tn
    assert nb % group == 0
    row = lambda w: pl.BlockSpec((LANES, w), lambda i: (i, 0))
    cache = lambda rows, w: pl.BlockSpec((1, group, rows, w), lambda i: (layer, i, 0, 0))
    return pl.pallas_call(
        functools.partial(_attn_sample_kernel, k_top=k_top, past_len=past_len, tn=tn),
        grid=(nb // group,),
        in_specs=[row(N_HEADS * HEAD_DIM),
                  pl.BlockSpec((group, N_IDX_HEADS, tn, IDX_DIM), lambda i: (i, 0, 0, 0)),
                  row(LANES), cache(N_KV_HEADS * past_len, HEAD_DIM),
                  cache(N_KV_HEADS * past_len, HEAD_DIM), cache(past_len, IDX_DIM),
                  row(kvw), row(kvw),
                  pl.BlockSpec((1, LANES, IDX_DIM), lambda i: (layer, i, 0))],
        out_specs=row(N_HEADS * HEAD_DIM),
        out_shape=jax.ShapeDtypeStruct((nb * tn, N_HEADS * HEAD_DIM), F32),
        scratch_shapes=[pltpu.VMEM((LANES, past_len), jnp.int32),
                        pltpu.VMEM((LANES, LANES), jnp.int32),
                        pltpu.VMEM((past_len + LANES, LANES), jnp.int32),
                        pltpu.VMEM((LANES, past_len), F32)],
        compiler_params=pltpu.CompilerParams(dimension_semantics=("arbitrary",),
                                             vmem_limit_bytes=VMEM_LIMIT),
        name="attn_sample",
    )(q, qi4, wi, ck, cv, cki, kb, vb, kif)


def _outffn_kernel(x_ref, a_ref, sga_ref, sgb_ref, u_ref, vn_ref, wmix_ref, bmix_ref, wo_ref,
                   g2_ref, wg_ref, wu_ref, wd_ref, gf_ref, xo_ref, *rest, final):
    if final:
        y_ref, z_ref = rest
    else:
        (z_ref,) = rest
    tm = x_ref.shape[0]
    group_dim = wmix_ref.shape[-1]
    for c in range(tm // SGU_CHUNK):
        rs = slice(c * SGU_CHUNK, (c + 1) * SGU_CHUNK)
        for g in range(SGU_GROUPS):
            cs = slice(g * group_dim, (g + 1) * group_dim)
            mixed = jnp.dot(wmix_ref[0, g], vn_ref[rs, cs].astype(BF16),
                            preferred_element_type=F32) + bmix_ref[0, :, cs]
            z = sga_ref[rs, cs] * a_ref[rs, cs] + sgb_ref[rs, cs] * (u_ref[rs, cs] * mixed)
            z_ref[rs, cs] = z.astype(BF16)
    x1 = x_ref[...] + jnp.dot(z_ref[...], wo_ref[0], preferred_element_type=F32)
    h2 = _rms_norm(x1, g2_ref[0]).astype(BF16)
    gate = jnp.dot(h2, wg_ref[0], preferred_element_type=F32)
    up = jnp.dot(h2, wu_ref[0], preferred_element_type=F32)
    ff = (gate * _sigmoid(gate) * up).astype(BF16)
    x2 = x1 + jnp.dot(ff, wd_ref[0], preferred_element_type=F32)
    xo_ref[...] = x2
    if final:
        y_ref[...] = _rms_norm(x2, gf_ref[...])


def _outffn(x, a, sga, sgb, u, vn, wmix, bmix, layer, wo, g2, wg, wu, wd, gf, final):
    T, D = x.shape
    tm = TOKEN_TILE
    row = pl.BlockSpec((tm, D), lambda i: (i, 0))
    n_out = 2 if final else 1
    out = pl.pallas_call(
        functools.partial(_outffn_kernel, final=final),
        grid=(T // tm,),
        in_specs=[row] * 6 + [_layer_spec(wmix.shape, layer), _layer_spec(bmix.shape, layer),
                              _layer_spec(wo.shape, layer), _layer_spec(g2.shape, layer),
                              _layer_spec(wg.shape, layer), _layer_spec(wu.shape, layer),
                              _layer_spec(wd.shape, layer), _const_spec((1, D))],
        out_specs=(row,) * n_out,
        out_shape=(jax.ShapeDtypeStruct((T, D), F32),) * n_out,
        scratch_shapes=[pltpu.VMEM((tm, D), BF16)],
        compiler_params=pltpu.CompilerParams(dimension_semantics=("arbitrary",),
                                             vmem_limit_bytes=VMEM_LIMIT),
        name="outffn",
    )(x, a, sga, sgb, u, vn, wmix, bmix, wo, g2, wg, wu, wd, gf)
    return out if final else (out[0], None)


def _rope_tables(pos, d):
    inv = ROPE_THETA ** (-jnp.arange(0, d, 2, dtype=F32) / d)
    ang = pos.astype(F32)[:, None] * inv[None, :]
    cos, sin = jnp.cos(ang), jnp.sin(ang)
    c = jnp.concatenate([cos, cos], axis=-1)
    s = jnp.concatenate([-sin, sin], axis=-1)
    reps = LANES // d
    return jnp.tile(c, (1, reps)), jnp.tile(s, (1, reps))


def _mix_weights(sgu_w, sgu_b, n, group_dim):
    depth = sgu_w.shape[0]
    p = jnp.arange(n)
    mask = (p[None, :] // CHUNK) <= (p[:, None] // CHUNK)
    w = jnp.where(mask[None, None], sgu_w[:, :, :n, :n], 0.0)
    reps = SGU_CHUNK // n
    eye = jnp.eye(reps, dtype=w.dtype)
    wbd = jnp.einsum('ab,lgij->lgaibj', eye, w).reshape(depth, SGU_GROUPS, SGU_CHUNK, SGU_CHUNK)
    b = jnp.tile(sgu_b[:, :, :n], (1, 1, reps))
    bfull = jnp.repeat(jnp.swapaxes(b, 1, 2), group_dim, axis=2)
    return wbd.astype(BF16), bfull


def _in_weights(w_in, d_model):
    kvw = N_KV_HEADS * HEAD_DIM
    w = w_in
    c_ki = d_model + 2 * kvw + N_IDX_HEADS * IDX_DIM
    c_wi = c_ki + IDX_DIM
    c_u = c_wi + N_IDX_HEADS
    wki, wwi = w[:, :, c_ki:c_wi], w[:, :, c_wi:c_u]
    wwi_pad = jnp.pad(wwi, ((0, 0), (0, 0), (0, LANES - N_IDX_HEADS)))
    wa = jnp.concatenate([w[:, :, :c_ki], wki, wki, wwi_pad, w[:, :, c_u:]], axis=2)
    wv_t = jnp.swapaxes(w[:, :, d_model + kvw:d_model + 2 * kvw], 1, 2)
    wwi_t = jnp.pad(jnp.swapaxes(wwi, 1, 2), ((0, 0), (0, 16 - N_IDX_HEADS), (0, 0)))
    return wa.astype(BF16), jnp.concatenate([wv_t, wwi_t], axis=1).astype(BF16)


def kernel(x_prompt, x_sample, cache_k, cache_v, cache_kidx, norm1_g, w_in, ln_v_g, ln_v_b, sgu_w, sgu_b, w_out, norm2_g, w_gate, w_up, w_down, final_norm_g):
    B, S, D = x_prompt.shape
    NB, TN, _ = x_sample.shape
    depth, _, P = cache_k.shape[:3]
    assert D == N_HEADS * HEAD_DIM and SGU_CHUNK % TN == 0 and S % SGU_CHUNK == 0
    k_top_p = min(TOPK_MAX, S // 4)
    k_top_s = min(TOPK_MAX, (P + TN) // 4)
    group_dim = D // SGU_GROUPS

    assert S % TOKEN_TILE == 0 and TOKEN_TILE % TN == 0
    pos_p = jnp.arange(S)
    pos_s = jnp.tile(P + jnp.arange(TN), TOKEN_TILE // TN)
    tab_p = _rope_tables(pos_p, HEAD_DIM) + _rope_tables(pos_p, IDX_DIM)
    tab_s = _rope_tables(pos_s, HEAD_DIM) + _rope_tables(pos_s, IDX_DIM)

    wa, wb = _in_weights(w_in, D)
    wo, wg, wu, wd = (t.astype(BF16) for t in (w_out, w_gate, w_up, w_down))
    vec = lambda v: v.reshape(depth, 1, -1)
    g1, g2, lng, lnb = vec(norm1_g), vec(norm2_g), vec(ln_v_g), vec(ln_v_b)
    gf = final_norm_g.reshape(1, -1)
    wmix_p, bmix_p = _mix_weights(sgu_w, sgu_b, SGU_CHUNK, group_dim)
    wmix_s, bmix_s = _mix_weights(sgu_w, sgu_b, TN, group_dim)
    ck = cache_k.reshape(depth, NB, P * N_KV_HEADS, HEAD_DIM)
    cv = cache_v.reshape(depth, NB, P * N_KV_HEADS, HEAD_DIM)

    xp = x_prompt.reshape(B * S, D)
    xs = x_sample.reshape(NB * TN, D)
    def new_buffers(tokens):
        kv = (depth, N_KV_HEADS * tokens, HEAD_DIM)
        return jnp.zeros(kv, F32), jnp.zeros(kv, F32), jnp.zeros((depth, tokens, IDX_DIM), F32)

    new_p, new_s = new_buffers(B * S), new_buffers(NB * TN)
    sgu_v = []
    yp = ys = None
    for l in range(depth):
        final = l == depth - 1

        (q, kf, kb, vf, _, vt, qi, kif, kze, kzo, _, wit, u, vn, sga, sgb) = _inproj(
            xp, l, g1, wa, wb, tab_p, lng, lnb, new_p)
        new_p = (kf, vf, kif)
        a = _attn_prompt(q, qi, wit, kze, kzo, kb, vt, B, S, k_top_p)
        xp, yp = _outffn(xp, a, sga, sgb, u, vn, wmix_p, bmix_p, l, wo, g2, wg, wu, wd, gf, final)

        (q, kf, kb, vf, vb, _, qi, kif, _, _, wi, _, u, vn, sga, sgb) = _inproj(
            xs, l, g1, wa, wb, tab_s, lng, lnb, new_s)
        new_s = (kf, vf, kif)
        qi4 = qi.reshape(NB, TN, N_IDX_HEADS, IDX_DIM).transpose(0, 2, 1, 3)
        a = _attn_sample(q, qi4, wi, ck, cv, cache_kidx, kb, vb, kif, l, k_top_s)
        xs, ys = _outffn(xs, a, sga, sgb, u, vn, wmix_s, bmix_s, l, wo, g2, wg, wu, wd, gf, final)
        sgu_v.append(vn.reshape(NB, TN, D))

    kv_p = (depth, B, S, N_KV_HEADS, HEAD_DIM)
    kv_s = (depth, NB, TN, N_KV_HEADS, HEAD_DIM)
    return (yp.reshape(B, S, D), ys.reshape(NB, TN, D),
            new_p[0].reshape(kv_p), new_p[1].reshape(kv_p), new_p[2].reshape(depth, B, S, IDX_DIM),
            new_s[0].reshape(kv_s), new_s[1].reshape(kv_s), new_s[2].reshape(depth, NB, TN, IDX_DIM),
            jnp.stack(sgu_v))
```

```python
import functools

import jax
import jax.numpy as jnp
from jax import lax
from jax.experimental import pallas as pl
from jax.experimental.pallas import tpu as pltpu

CHUNK = 64
N_HEADS = 8
HEAD_DIM = 128
N_KV_HEADS = 2
KV_GROUP = N_HEADS // N_KV_HEADS
N_IDX_HEADS = 8
IDX_DIM = 64
TOPK_MAX = 256
SGU_CHUNK = 128
SGU_GROUPS = 8
ROPE_THETA = 10000.0
EPS = 1e-6

LANES = 128
TOKEN_TILE = 256
Q_TILE = 256
KEY_TILE = 256
VT_ROWS = HEAD_DIM + 16
FOLD = 8
VMEM_LIMIT = 56 * 1024 * 1024

INT_MIN = -(2 ** 31)
INT_MAX = 2 ** 31 - 1
MASK_BIAS = -1e30
LOG2E = 1.4426950408889634
Q_SCALE = (HEAD_DIM ** -0.5) * LOG2E

F32 = jnp.float32
BF16 = jnp.bfloat16
NT_DIMS = (((1,), (1,)), ((), ()))


def _const_spec(shape):
    nd = len(shape)
    return pl.BlockSpec(shape, lambda *_: (0,) * nd, pipeline_mode=pl.Buffered(1))


def _sigmoid(x):
    return 1.0 / (1.0 + jnp.exp(-x))


def _rms_norm(x, g):
    return x * lax.rsqrt(jnp.mean(x * x, axis=-1, keepdims=True) + EPS) * g


def _order_key(score):
    bits = pltpu.bitcast(score, jnp.int32)
    return bits ^ ((bits >> 31) & INT_MAX)


def _midpoint(lo, hi):
    return (lo >> 1) + (hi >> 1) + (lo & hi & 1)


_C_Q = 0
_C_K = _C_Q + N_HEADS * HEAD_DIM
_C_V = _C_K + N_KV_HEADS * HEAD_DIM
_C_QI = _C_V + N_KV_HEADS * HEAD_DIM
_C_KI = _C_QI + N_IDX_HEADS * IDX_DIM
_C_WI = _C_KI + LANES
_C_U = _C_WI + LANES


def _inproj_kernel(*refs, d_model, n_alias):
    (x_ref, g_ref, wa_ref, wb_ref, cosh_ref, sinh_ref, cosi_ref, sini_ref,
     lng_ref, lnb_ref) = refs[:10]
    (q_ref, kf_ref, kb_ref, vf_ref, vb_ref, vt_ref, qi_ref, kif_ref, kze_ref, kzo_ref,
     wi_ref, wit_ref, u_ref, vn_ref, sga_ref, sgb_ref) = refs[10 + n_alias:]
    tm = x_ref.shape[0]
    hb = _rms_norm(x_ref[...], g_ref[0]).astype(BF16)

    def proj(c0, width):
        return jnp.dot(hb, wa_ref[0, :, c0:c0 + width], preferred_element_type=F32)

    def head_rows(n):
        return pl.ds(n, tm, stride=N_KV_HEADS)

    cosh, sinh = cosh_ref[...], sinh_ref[...]
    cosi, sini = cosi_ref[...], sini_ref[...]
    lane = lax.broadcasted_iota(jnp.int32, cosi.shape, 1)
    first_half = (lane % IDX_DIM) < (IDX_DIM // 2)

    def rope_head(x):
        return x * cosh + pltpu.roll(x, HEAD_DIM // 2, 1) * sinh

    def rope_idx(x):
        partner = jnp.where(first_half, pltpu.roll(x, LANES - IDX_DIM // 2, 1),
                            pltpu.roll(x, IDX_DIM // 2, 1))
        return x * cosi + partner * sini

    xq = proj(_C_Q, N_HEADS * HEAD_DIM)
    for h in range(N_HEADS):
        sl = slice(h * HEAD_DIM, (h + 1) * HEAD_DIM)
        q_ref[:, sl] = (rope_head(xq[:, sl]) * Q_SCALE).astype(BF16)

    xk = proj(_C_K, N_KV_HEADS * HEAD_DIM)
    for h in range(N_KV_HEADS):
        sl = slice(h * HEAD_DIM, (h + 1) * HEAD_DIM)
        kr = rope_head(xk[:, sl])
        kf_ref[0, head_rows(h), :] = kr
        kb_ref[:, sl] = kr.astype(BF16)

    xv = proj(_C_V, N_KV_HEADS * HEAD_DIM)
    vb_ref[...] = xv.astype(BF16)
    for h in range(N_KV_HEADS):
        vf_ref[0, head_rows(h), :] = xv[:, h * HEAD_DIM:(h + 1) * HEAD_DIM]

    xqi = proj(_C_QI, N_IDX_HEADS * IDX_DIM)
    for p in range(N_IDX_HEADS * IDX_DIM // LANES):
        sl = slice(p * LANES, (p + 1) * LANES)
        qi_ref[:, sl] = (rope_idx(xqi[:, sl]) * (IDX_DIM ** -0.5)).astype(BF16)

    kk = rope_idx(proj(_C_KI, LANES))
    kif_ref[0] = kk[:, :IDX_DIM]
    low = lane < IDX_DIM
    kze_ref[...] = jnp.where(low, kk, 0.0).astype(BF16)
    kzo_ref[...] = jnp.where(low, 0.0, kk).astype(BF16)

    wi_ref[...] = proj(_C_WI, LANES) * (N_IDX_HEADS ** -0.5)

    tb = lax.dot_general(wb_ref[0], hb, NT_DIMS, preferred_element_type=F32)
    kvw = N_KV_HEADS * HEAD_DIM
    ones_rows = (lax.broadcasted_iota(jnp.int32, (VT_ROWS - HEAD_DIM, tm), 0) == 0).astype(BF16)
    for n in range(N_KV_HEADS):
        vt_ref[0, n * VT_ROWS:n * VT_ROWS + HEAD_DIM, :] = (
            tb[n * HEAD_DIM:(n + 1) * HEAD_DIM].astype(BF16))
        vt_ref[0, n * VT_ROWS + HEAD_DIM:(n + 1) * VT_ROWS, :] = ones_rows
    wit_ref[...] = tb[kvw:] * (N_IDX_HEADS ** -0.5)

    u_ref[...] = jax.nn.gelu(proj(_C_U, d_model), approximate=True)
    gv = jax.nn.gelu(proj(_C_U + d_model, d_model), approximate=True)
    mu = jnp.mean(gv, axis=-1, keepdims=True)
    dv = gv - mu
    var = jnp.mean(dv * dv, axis=-1, keepdims=True)
    vn_ref[...] = dv * lax.rsqrt(var + EPS) * lng_ref[0] + lnb_ref[0]
    sga_ref[...] = _sigmoid(proj(_C_U + 2 * d_model, d_model))
    sgb_ref[...] = _sigmoid(proj(_C_U + 3 * d_model, d_model))


def _layer_spec(shape, layer):
    nd = len(shape)
    return pl.BlockSpec((1,) + tuple(shape[1:]), lambda *_: (layer,) + (0,) * (nd - 1),
                        pipeline_mode=pl.Buffered(1))


def _inproj(x, layer, g, wa, wb, tables, lng, lnb, stacked):
    T, D = x.shape
    depth = wa.shape[0]
    tm = TOKEN_TILE
    assert T % tm == 0
    kvw = N_KV_HEADS * HEAD_DIM
    qiw = N_IDX_HEADS * IDX_DIM
    row = lambda w: pl.BlockSpec((tm, w), lambda i: (i, 0))
    kv_rows = N_KV_HEADS * T
    out_shape = (
        jax.ShapeDtypeStruct((T, N_HEADS * HEAD_DIM), BF16),
        jax.ShapeDtypeStruct((depth, kv_rows, HEAD_DIM), F32),
        jax.ShapeDtypeStruct((T, kvw), BF16),
        jax.ShapeDtypeStruct((depth, kv_rows, HEAD_DIM), F32),
        jax.ShapeDtypeStruct((T, kvw), BF16),
        jax.ShapeDtypeStruct((T // tm, N_KV_HEADS * VT_ROWS, tm), BF16),
        jax.ShapeDtypeStruct((T, qiw), BF16),
        jax.ShapeDtypeStruct((depth, T, IDX_DIM), F32),
        jax.ShapeDtypeStruct((T, LANES), BF16),
        jax.ShapeDtypeStruct((T, LANES), BF16),
        jax.ShapeDtypeStruct((T, LANES), F32),
        jax.ShapeDtypeStruct((16, T), F32),
        jax.ShapeDtypeStruct((T, D), F32),
        jax.ShapeDtypeStruct((T, D), F32),
        jax.ShapeDtypeStruct((T, D), F32),
        jax.ShapeDtypeStruct((T, D), F32),
    )
    kv_spec = pl.BlockSpec((1, N_KV_HEADS * tm, HEAD_DIM), lambda i: (layer, i, 0))
    out_specs = (
        row(N_HEADS * HEAD_DIM), kv_spec, row(kvw), kv_spec, row(kvw),
        pl.BlockSpec((1, N_KV_HEADS * VT_ROWS, tm), lambda i: (i, 0, 0)),
        row(qiw), pl.BlockSpec((1, tm, IDX_DIM), lambda i: (layer, i, 0)),
        row(LANES), row(LANES), row(LANES),
        pl.BlockSpec((16, tm), lambda i: (0, i)),
        row(D), row(D), row(D), row(D),
    )
    n_tab = tables[0].shape[0] // tm
    tab = pl.BlockSpec((tm, LANES), lambda i: (i % n_tab, 0))
    in_specs = [row(D), _layer_spec(g.shape, layer), _layer_spec(wa.shape, layer),
                _layer_spec(wb.shape, layer),
                tab, tab, tab, tab,
                _layer_spec(lng.shape, layer), _layer_spec(lnb.shape, layer)]
    args = [x, g, wa, wb, *tables, lng, lnb]
    aliases = {}
    for j, out_idx in enumerate((1, 3, 7)):
        aliases[len(args)] = out_idx
        in_specs.append(pl.BlockSpec(memory_space=pl.ANY))
        args.append(stacked[j])
    return pl.pallas_call(
        functools.partial(_inproj_kernel, d_model=D, n_alias=len(aliases)),
        grid=(T // tm,),
        in_specs=in_specs, out_specs=out_specs, out_shape=out_shape,
        input_output_aliases=aliases,
        compiler_params=pltpu.CompilerParams(dimension_semantics=("arbitrary",),
                                             vmem_limit_bytes=VMEM_LIMIT),
        name="inproj",
    )(*args)


def _attn_prompt_kernel(q_ref, qi_ref, wit_ref, kze_ref, kzo_ref, k_ref, vt_ref, o_ref,
                        keys_ref, hi_ref, lo_ref, f1_ref, f2_ref, acc_ref, m_ref, mx_ref, s_ref, p_ref,
                        bias_ref, *, k_top):
    tq, kb_sz = Q_TILE, KEY_TILE
    qb = pl.program_id(1)
    nkb = (qb + 1) * (tq // kb_sz)
    q_chunk = (qb * tq + lax.broadcasted_iota(jnp.int32, (1, tq), 1)) // CHUNK
    row_iota = lax.broadcasted_iota(jnp.int32, (kb_sz, 1), 0)

    def key_slice(kb):
        return pl.ds(pl.multiple_of(kb * kb_sz, kb_sz), kb_sz)

    n_blocks = keys_ref.shape[0] // kb_sz
    n_full = nkb // 2

    def clamp(blk):
        return jnp.minimum(blk, n_blocks - 1)

    def idx_dots(blk, slot):
        ks = key_slice(blk)
        kze, kzo = kze_ref[ks, :], kzo_ref[ks, :]
        for p in range(N_IDX_HEADS // 2):
            qp = qi_ref[:, p * LANES:(p + 1) * LANES]
            s_ref[slot, 2 * p] = lax.dot_general(kze, qp, NT_DIMS, preferred_element_type=F32)
            s_ref[slot, 2 * p + 1] = lax.dot_general(kzo, qp, NT_DIMS, preferred_element_type=F32)

    def idx_keys(blk, slot):
        base = pl.multiple_of(blk * kb_sz, kb_sz)
        for c in range(kb_sz // CHUNK):
            rows = slice(c * CHUNK, (c + 1) * CHUNK)
            score = jnp.zeros((CHUNK, tq), F32)
            for j in range(N_IDX_HEADS):
                score = score + wit_ref[j:j + 1, :] * jnp.maximum(s_ref[slot, j, rows, :], 0.0)
            visible = (blk * (kb_sz // CHUNK) + c) <= q_chunk
            key = jnp.where(visible, _order_key(score), INT_MIN)
            ks = pl.ds(base + c * CHUNK, CHUNK)
            keys_ref[ks, :] = key
            hi_ref[ks, :] = (key >> 16).astype(jnp.int16)
            lo_ref[ks, :] = (key ^ 0x8000).astype(jnp.int16)

    def score_body(j, carry):
        b0 = 2 * j
        idx_dots(b0 + 1, 1)
        idx_keys(b0, 0)
        idx_dots(clamp(b0 + 2), 0)
        idx_keys(b0 + 1, 1)
        return carry

    idx_dots(0, 0)
    lax.fori_loop(0, n_full, score_body, 0)

    @pl.when(nkb % 2 == 1)
    def _():
        idx_keys(nkb - 1, 0)
        keys_ref[key_slice(nkb), :] = jnp.full((kb_sz, tq), INT_MIN, jnp.int32)

    n_acc = 4
    rows16 = 16

    def fold_rows(x, accs):
        accs = list(accs)
        for r in range(kb_sz // rows16):
            accs[r % n_acc] = accs[r % n_acc] + x[r * rows16:(r + 1) * rows16, :]
        return tuple(accs)

    def zero_accs():
        return tuple(jnp.zeros((rows16, tq), jnp.int16) for _ in range(n_acc))

    def total(accs):
        tot = accs[0].astype(jnp.int32)
        for a in accs[1:]:
            tot = tot + a.astype(jnp.int32)
        return jnp.sum(tot, axis=0, keepdims=True)

    def count_ge(plane_ref, t):
        t16 = jnp.broadcast_to(t, (kb_sz, tq)).astype(jnp.int16)

        def body(kb, accs):
            ge = (plane_ref[key_slice(kb), :] >= t16).astype(jnp.int16)
            return fold_rows(ge, accs)

        return total(lax.fori_loop(0, nkb, body, zero_accs()))

    def bisect16(count, need, lo, hi, count_lo, steps):
        def body(_, carry):
            lo, hi, n_lo = carry
            mid = (lo + hi) >> 1
            n_mid = count(mid)
            ok = n_mid >= need
            return jnp.where(ok, mid, lo), jnp.where(ok, hi, mid), jnp.where(ok, n_mid, n_lo)

        lo, _, n_lo = lax.fori_loop(0, steps, body, (lo, hi, count_lo))
        return lo, n_lo

    floor16 = jnp.full((1, tq), -(2 ** 15), jnp.int32)
    ceil16 = jnp.full((1, tq), 2 ** 15, jnp.int32)
    every = jnp.full((1, tq), keys_ref.shape[0], jnp.int32)
    t_hi, n_hi = bisect16(functools.partial(count_ge, hi_ref), k_top, floor16, ceil16, every, 16)

    t_hi16 = jnp.broadcast_to(t_hi, (kb_sz, tq)).astype(jnp.int16)
    sentinel = jnp.int16(-(2 ** 15))
    n_slots = kb_sz // rows16 // FOLD

    def max_min(a, b):
        a_ge = a >= b
        return jnp.where(a_ge, a, b), jnp.where(a_ge, b, a)

    def tie_body(kb, accs):
        ks = key_slice(kb)
        hi_part = hi_ref[ks, :]
        e = jnp.where(hi_part == t_hi16, lo_ref[ks, :], sentinel)
        lo_ref[ks, :] = e
        g = [e[r * rows16:(r + 1) * rows16, :] for r in range(kb_sz // rows16)]
        n = len(g) // 2
        pairs = [max_min(g[i], g[i + n]) for i in range(n)]
        first, second = [p[0] for p in pairs], [p[1] for p in pairs]
        while n > n_slots:
            n //= 2
            pairs = [max_min(first[i], first[i + n]) for i in range(n)]
            second = [max_min(pairs[i][1], max_min(second[i], second[i + n])[0])[0] for i in range(n)]
            first = [p[0] for p in pairs]
        fs = pl.ds(pl.multiple_of(kb * (n_slots * rows16), n_slots * rows16), n_slots * rows16)
        f1_ref[fs, :] = jnp.concatenate(first, axis=0)
        f2_ref[fs, :] = jnp.concatenate(second, axis=0)
        return fold_rows((hi_part > t_hi16).astype(jnp.int16), accs)

    n_above = total(lax.fori_loop(0, nkb, tie_body, zero_accs()))
    need_lo = k_top - n_above

    def count_folded(t):
        t16 = jnp.broadcast_to(t, (n_slots * rows16, tq)).astype(jnp.int16)

        def body(kb, accs):
            fs = pl.ds(pl.multiple_of(kb * (n_slots * rows16), n_slots * rows16), n_slots * rows16)
            accs = list(accs)
            for i, plane in enumerate((f1_ref, f2_ref)):
                ge = (plane[fs, :] >= t16).astype(jnp.int16)
                for r in range(n_slots):
                    a = (i * n_slots + r) % n_acc
                    accs[a] = accs[a] + ge[r * rows16:(r + 1) * rows16, :]
            return tuple(accs)

        return total(lax.fori_loop(0, nkb, body, zero_accs()))

    def count_lo_pair(t):
        t_next = jnp.minimum(t + 1, 2 ** 15 - 1)
        ta = jnp.broadcast_to(t, (kb_sz, tq)).astype(jnp.int16)
        tb = jnp.broadcast_to(t_next, (kb_sz, tq)).astype(jnp.int16)

        def body(kb, accs):
            plane = lo_ref[key_slice(kb), :]
            return (fold_rows((plane >= ta).astype(jnp.int16), accs[0]),
                    fold_rows((plane >= tb).astype(jnp.int16), accs[1]))

        a, b = lax.fori_loop(0, nkb, body, (zero_accs(), zero_accs()))
        return total(a), jnp.where(t < 2 ** 15 - 1, total(b), 0)

    t_lo, _ = bisect16(count_folded, need_lo, floor16, ceil16, every, 16)
    n_at, n_next = count_lo_pair(t_lo)

    def unfolded_search():
        t, _ = bisect16(functools.partial(count_ge, lo_ref), need_lo, floor16, ceil16, every, 16)
        return (t,) + count_lo_pair(t)

    t_lo, n_at, n_next = lax.cond(jnp.max((n_next >= need_lo).astype(jnp.int32)) > 0,
                                  unfolded_search, lambda: (t_lo, n_at, n_next))
    thr = jnp.maximum((t_hi << 16) + (t_lo + 2 ** 15), INT_MIN + 1)

    n_selected = jnp.where(t_hi > -(2 ** 15),
                           jnp.where(t_lo > -(2 ** 15), n_above + n_at, n_hi), 0)
    need_eq = (need_lo - n_next).astype(F32)
    col_iota = lax.broadcasted_iota(jnp.int32, (1, kb_sz), 1)

    for c in range(tq // LANES):
        cs = slice(c * LANES, (c + 1) * LANES)

        @pl.when(jnp.max(n_selected[:, cs]) > k_top)
        def _(cs=cs):
            tri = jnp.where(col_iota <= row_iota, 1.0, 0.0).astype(BF16)
            thr_c, need_c = thr[:, cs], need_eq[:, cs]

            unroll = 4

            def drop(j, seen):
                blocks = [unroll * j + i for i in range(unroll)]
                keys = [keys_ref[key_slice(clamp(b)), cs] for b in blocks]
                equal = [(key == thr_c) & (b < nkb) for b, key in zip(blocks, keys)]
                local = [jnp.dot(tri, jnp.where(e, 1.0, 0.0).astype(BF16),
                                 preferred_element_type=F32) for e in equal]
                for b, key, e, rank in zip(blocks, keys, equal, local):
                    rank = rank + seen
                    keys_ref[key_slice(clamp(b)), cs] = jnp.where(e & (rank > need_c), key - 1, key)
                    seen = rank[kb_sz - 1:kb_sz, :]
                return seen

            lax.fori_loop(0, (nkb + unroll - 1) // unroll, drop, jnp.zeros((1, LANES), F32))

    m_ref[...] = jnp.full(m_ref.shape, MASK_BIAS, F32)
    acc_ref[...] = jnp.zeros(acc_ref.shape, F32)

    rc = 32
    n_rc = kb_sz // rc

    def fold8(x):
        return x.reshape(rc // 8, 8, tq)

    def qk_dots(blk, slot):
        ks = key_slice(blk)
        for h in range(N_HEADS):
            n = h // KV_GROUP
            k_n = k_ref[ks, n * HEAD_DIM:(n + 1) * HEAD_DIM]
            q_h = q_ref[:, h * HEAD_DIM:(h + 1) * HEAD_DIM]
            s_ref[slot, h] = lax.dot_general(k_n, q_h, NT_DIMS, preferred_element_type=F32)

    def softmax_block(blk, slot):
        thr_b = jnp.where(blk < nkb, thr, INT_MAX)
        bias_ref[slot] = jnp.where(keys_ref[key_slice(clamp(blk)), :] >= thr_b, 0.0, MASK_BIAS)
        for h in range(N_HEADS):
            mx = jnp.full((8, tq), MASK_BIAS, F32)
            for r in range(n_rc):
                rows = slice(r * rc, (r + 1) * rc)
                sc = s_ref[slot, h, rows, :] + bias_ref[slot, rows, :]
                s_ref[slot, h, rows, :] = sc
                mx = jnp.maximum(mx, jnp.max(fold8(sc), axis=0))
            mx_ref[h:h + 1, :] = jnp.max(mx, axis=0, keepdims=True)
        m_old = m_ref[...]
        m_new = jnp.maximum(m_old, mx_ref[...])
        alpha = jnp.exp2(m_old - m_new)
        m_ref[...] = m_new
        for h in range(N_HEADS):
            m_h = jnp.broadcast_to(m_new[h:h + 1, :], (rc, tq))
            for r in range(n_rc):
                rows = slice(r * rc, (r + 1) * rc)
                p_ref[slot, h, rows, :] = jnp.exp2(s_ref[slot, h, rows, :] - m_h).astype(BF16)
        return alpha

    def pv_update(blk, slot, alpha):
        for h in range(N_HEADS):
            n = h // KV_GROUP
            vt_n = vt_ref[blk, n * VT_ROWS:(n + 1) * VT_ROWS, :]
            pv = jnp.dot(vt_n, p_ref[slot, h], preferred_element_type=F32)
            acc_ref[h] = alpha[h:h + 1, :] * acc_ref[h] + pv

    def attn_body(j, carry):
        b0 = 2 * j
        qk_dots(clamp(b0 + 1), 1)
        pv_update(b0, 0, softmax_block(b0, 0))
        qk_dots(clamp(b0 + 2), 0)
        pv_update(clamp(b0 + 1), 1, softmax_block(b0 + 1, 1))
        return carry

    qk_dots(0, 0)
    lax.fori_loop(0, (nkb + 1) // 2, attn_body, 0)

    for h in range(N_HEADS):
        o_t = acc_ref[h, :HEAD_DIM, :] / acc_ref[h, HEAD_DIM:HEAD_DIM + 1, :]
        o_ref[:, h * HEAD_DIM:(h + 1) * HEAD_DIM] = o_t.T


def _attn_prompt(q, qi, wit, kze, kzo, kb, vt, batch, seq, k_top):
    tq = Q_TILE
    assert seq % tq == 0 and Q_TILE % KEY_TILE == 0 and KEY_TILE == TOKEN_TILE
    assert (seq // KEY_TILE) % 4 == 0
    nq = seq // tq
    kvw = N_KV_HEADS * HEAD_DIM
    qrow = lambda w: pl.BlockSpec((tq, w), lambda b, i: (b * nq + i, 0))
    per_batch = lambda w: pl.BlockSpec((seq, w), lambda b, i: (b, 0), pipeline_mode=pl.Buffered(1))
    return pl.pallas_call(
        functools.partial(_attn_prompt_kernel, k_top=k_top),
        grid=(batch, nq),
        in_specs=[qrow(N_HEADS * HEAD_DIM), qrow(N_IDX_HEADS * IDX_DIM),
                  pl.BlockSpec((16, tq), lambda b, i: (0, b * nq + i)),
                  per_batch(LANES), per_batch(LANES), per_batch(kvw),
                  pl.BlockSpec((seq // KEY_TILE, N_KV_HEADS * VT_ROWS, KEY_TILE), lambda b, i: (b, 0, 0),
                               pipeline_mode=pl.Buffered(1))],
        out_specs=qrow(N_HEADS * HEAD_DIM),
        out_shape=jax.ShapeDtypeStruct((batch * seq, N_HEADS * HEAD_DIM), F32),
        scratch_shapes=[pltpu.VMEM((seq, tq), jnp.int32),
                        pltpu.VMEM((seq, tq), jnp.int16),
                        pltpu.VMEM((seq, tq), jnp.int16),
                        pltpu.VMEM((seq // FOLD, tq), jnp.int16),
                        pltpu.VMEM((seq // FOLD, tq), jnp.int16),
                        pltpu.VMEM((N_HEADS, VT_ROWS, tq), F32),
                        pltpu.VMEM((N_HEADS, tq), F32),
                        pltpu.VMEM((N_HEADS, tq), F32),
                        pltpu.VMEM((2, N_HEADS, KEY_TILE, tq), F32),
                        pltpu.VMEM((2, N_HEADS, KEY_TILE, tq), BF16),
                        pltpu.VMEM((2, KEY_TILE, tq), F32)],
        compiler_params=pltpu.CompilerParams(dimension_semantics=("arbitrary", "arbitrary"),
                                             vmem_limit_bytes=VMEM_LIMIT),
        name="attn_prompt",
    )(q, qi, wit, kze, kzo, kb, vt)


def _attn_sample_kernel(q_ref, qi_ref, wi_ref, ck_ref, cv_ref, cki_ref, kn_ref, vn_ref, kin_ref,
                        o_ref, keyp_ref, keyn_ref, keyt_ref, biasp_ref, *, k_top, past_len, tn):
    group = LANES // tn
    rows = group * tn
    n_cols = past_len // LANES
    kn = kn_ref[...]
    vn = vn_ref[...]
    kin = kin_ref[0].astype(BF16)

    lane = lax.broadcasted_iota(jnp.int32, (tn, LANES), 1)
    pos = lax.broadcasted_iota(jnp.int32, (tn, LANES), 0)
    vis = ((past_len + lane % tn) // CHUNK) <= ((past_len + pos) // CHUNK)
    for j in range(group):
        qj = qi_ref[j].reshape(N_IDX_HEADS * tn, IDX_DIM)
        dp = lax.dot_general(qj, cki_ref[0, j].astype(BF16), NT_DIMS, preferred_element_type=F32)
        dn = lax.dot_general(qj, kin, NT_DIMS, preferred_element_type=F32)
        sp = jnp.zeros((tn, past_len), F32)
        sn = jnp.zeros((tn, LANES), F32)
        for h in range(N_IDX_HEADS):
            w = wi_ref[j * tn:(j + 1) * tn, h:h + 1]
            sp = sp + w * jnp.maximum(dp[h * tn:(h + 1) * tn, :], 0.0)
            sn = sn + w * jnp.maximum(dn[h * tn:(h + 1) * tn, :], 0.0)
        keyp_ref[j * tn:(j + 1) * tn, :] = _order_key(sp)
        own = (lane // tn) == j
        keyn_ref[j * tn:(j + 1) * tn, :] = jnp.where(own & vis, _order_key(sn), INT_MIN)

    for c in range(n_cols):
        keyt_ref[c * LANES:(c + 1) * LANES, :] = keyp_ref[:, c * LANES:(c + 1) * LANES].T
    keyt_ref[past_len:past_len + LANES, :] = keyn_ref[...].T
    n_keys = past_len + LANES

    def bisect_body(_, carry):
        lo, hi = carry
        mid = _midpoint(lo, hi)
        ge = (keyt_ref[...] >= mid).astype(jnp.int32)
        cnt = jnp.sum(jnp.sum(ge.reshape(n_keys // 8, 8, rows), axis=0), axis=0, keepdims=True)
        ok = cnt >= k_top
        return jnp.where(ok, mid, lo), jnp.where(ok, hi, mid)

    thr, _ = lax.fori_loop(
        0, 32, bisect_body,
        (jnp.full((1, rows), INT_MIN + 1, jnp.int32), jnp.full((1, rows), INT_MAX, jnp.int32)))
    thr_col = jnp.broadcast_to(thr, (rows, rows)).T

    key_row = lax.broadcasted_iota(jnp.int32, (n_keys, 1), 0)

    def count_keys(pred):
        hit = pred(keyt_ref[...], key_row).astype(jnp.int32)
        return jnp.sum(jnp.sum(hit.reshape(n_keys // 8, 8, rows), axis=0), axis=0, keepdims=True)

    n_selected = count_keys(lambda key, row: key >= thr)

    @pl.when(jnp.max(n_selected) > k_top)
    def _():
        need_eq = k_top - count_keys(lambda key, row: key > thr)

        def body(_, carry):
            lo, hi = carry
            mid = (lo + hi) >> 1
            ok = count_keys(lambda key, row: (key == thr) & (row < mid)) >= need_eq
            return jnp.where(ok, lo, mid), jnp.where(ok, mid, hi)

        _, cut = lax.fori_loop(0, n_keys.bit_length(), body,
                               (jnp.zeros((1, rows), jnp.int32), jnp.full((1, rows), n_keys, jnp.int32)))
        cut_col = jnp.broadcast_to(cut, (rows, rows)).T
        lane_idx = lax.broadcasted_iota(jnp.int32, (rows, LANES), 1)
        for c in range(n_cols + 1):
            ref, cols = (keyp_ref, slice(c * LANES, (c + 1) * LANES)) if c < n_cols else (keyn_ref, slice(None))
            key = ref[:, cols]
            surplus = (key == thr_col) & (c * LANES + lane_idx >= cut_col)
            ref[:, cols] = jnp.where(surplus, key - 1, key)

    bias_n = jnp.where(keyn_ref[...] >= thr_col, 0.0, MASK_BIAS)
    for c in range(n_cols):
        cols = slice(c * LANES, (c + 1) * LANES)
        biasp_ref[:, cols] = jnp.where(keyp_ref[:, cols] >= thr_col, 0.0, MASK_BIAS)

    for j in range(group):
        bias_pj = biasp_ref[j * tn:(j + 1) * tn, :]
        bias_nj = bias_n[j * tn:(j + 1) * tn, :]
        for n in range(N_KV_HEADS):
            hs = slice(n * HEAD_DIM, (n + 1) * HEAD_DIM)
            k_n = ck_ref[0, j, pl.ds(n, past_len, stride=N_KV_HEADS), :].astype(BF16)
            v_n = cv_ref[0, j, pl.ds(n, past_len, stride=N_KV_HEADS), :].astype(BF16)
            qg = jnp.concatenate(
                [q_ref[j * tn:(j + 1) * tn, (n * KV_GROUP + g) * HEAD_DIM:(n * KV_GROUP + g + 1) * HEAD_DIM]
                 for g in range(KV_GROUP)], axis=0)
            s1 = lax.dot_general(qg, k_n, NT_DIMS, preferred_element_type=F32)
            s2 = lax.dot_general(qg, kn[:, hs], NT_DIMS, preferred_element_type=F32)
            s1 = (s1.reshape(KV_GROUP, tn, past_len) + bias_pj[None]).reshape(KV_GROUP * tn, past_len)
            s2 = (s2.reshape(KV_GROUP, tn, LANES) + bias_nj[None]).reshape(KV_GROUP * tn, LANES)
            m = jnp.maximum(jnp.max(s1, axis=1, keepdims=True), jnp.max(s2, axis=1, keepdims=True))
            p1 = jnp.exp2(s1 - m)
            p2 = jnp.exp2(s2 - m)
            l = jnp.sum(p1, axis=1, keepdims=True) + jnp.sum(p2, axis=1, keepdims=True)
            o = (jnp.dot(p1.astype(BF16), v_n, preferred_element_type=F32)
                 + jnp.dot(p2.astype(BF16), vn[:, hs], preferred_element_type=F32)) / l
            for g in range(KV_GROUP):
                h = n * KV_GROUP + g
                o_ref[j * tn:(j + 1) * tn, h * HEAD_DIM:(h + 1) * HEAD_DIM] = o[g * tn:(g + 1) * tn, :]


def _attn_sample(q, qi4, wi, ck, cv, cki, kb, vb, kif, layer, k_top):
    nb = ck.shape[1]
    past_len = cki.shape[2]
    kvw = N_KV_HEADS * HEAD_DIM
    tn = q.shape[0] // nb
    assert LANES % tn == 0 and tn % 16 == 0 and past_len % LANES == 0
    group = LANES // tn
    assert nb % group == 0
    row = lambda w: pl.BlockSpec((LANES, w), lambda i: (i, 0))
    cache = lambda rows, w: pl.BlockSpec((1, group, rows, w), lambda i: (layer, i, 0, 0))
    return pl.pallas_call(
        functools.partial(_attn_sample_kernel, k_top=k_top, past_len=past_len, tn=tn),
        grid=(nb // group,),
        in_specs=[row(N_HEADS * HEAD_DIM),
                  pl.BlockSpec((group, N_IDX_HEADS, tn, IDX_DIM), lambda i: (i, 0, 0, 0)),
                  row(LANES), cache(N_KV_HEADS * past_len, HEAD_DIM),
                  cache(N_KV_HEADS * past_len, HEAD_DIM), cache(past_len, IDX_DIM),
                  row(kvw), row(kvw),
                  pl.BlockSpec((1, LANES, IDX_DIM), lambda i: (layer, i, 0))],
        out_specs=row(N_HEADS * HEAD_DIM),
        out_shape=jax.ShapeDtypeStruct((nb * tn, N_HEADS * HEAD_DIM), F32),
        scratch_shapes=[pltpu.VMEM((LANES, past_len), jnp.int32),
                        pltpu.VMEM((LANES, LANES), jnp.int32),
                        pltpu.VMEM((past_len + LANES, LANES), jnp.int32),
                        pltpu.VMEM((LANES, past_len), F32)],
        compiler_params=pltpu.CompilerParams(dimension_semantics=("arbitrary",),
                                             vmem_limit_bytes=VMEM_LIMIT),
        name="attn_sample",
    )(q, qi4, wi, ck, cv, cki, kb, vb, kif)


def _outffn_kernel(x_ref, a_ref, sga_ref, sgb_ref, u_ref, vn_ref, wmix_ref, bmix_ref, wo_ref,
                   g2_ref, wg_ref, wu_ref, wd_ref, gf_ref, xo_ref, *rest, final):
    if final:
        y_ref, z_ref = rest
    else:
        (z_ref,) = rest
    tm = x_ref.shape[0]
    group_dim = wmix_ref.shape[-1]
    for c in range(tm // SGU_CHUNK):
        rs = slice(c * SGU_CHUNK, (c + 1) * SGU_CHUNK)
        for g in range(SGU_GROUPS):
            cs = slice(g * group_dim, (g + 1) * group_dim)
            mixed = jnp.dot(wmix_ref[0, g], vn_ref[rs, cs].astype(BF16),
                            preferred_element_type=F32) + bmix_ref[0, :, cs]
            z = sga_ref[rs, cs] * a_ref[rs, cs] + sgb_ref[rs, cs] * (u_ref[rs, cs] * mixed)
            z_ref[rs, cs] = z.astype(BF16)
    x1 = x_ref[...] + jnp.dot(z_ref[...], wo_ref[0], preferred_element_type=F32)
    h2 = _rms_norm(x1, g2_ref[0]).astype(BF16)
    gate = jnp.dot(h2, wg_ref[0], preferred_element_type=F32)
    up = jnp.dot(h2, wu_ref[0], preferred_element_type=F32)
    ff = (gate * _sigmoid(gate) * up).astype(BF16)
    x2 = x1 + jnp.dot(ff, wd_ref[0], preferred_element_type=F32)
    xo_ref[...] = x2
    if final:
        y_ref[...] = _rms_norm(x2, gf_ref[...])


def _outffn(x, a, sga, sgb, u, vn, wmix, bmix, layer, wo, g2, wg, wu, wd, gf, final):
    T, D = x.shape
    tm = TOKEN_TILE
    row = pl.BlockSpec((tm, D), lambda i: (i, 0))
    n_out = 2 if final else 1
    out = pl.pallas_call(
        functools.partial(_outffn_kernel, final=final),
        grid=(T // tm,),
        in_specs=[row] * 6 + [_layer_spec(wmix.shape, layer), _layer_spec(bmix.shape, layer),
                              _layer_spec(wo.shape, layer), _layer_spec(g2.shape, layer),
                              _layer_spec(wg.shape, layer), _layer_spec(wu.shape, layer),
                              _layer_spec(wd.shape, layer), _const_spec((1, D))],
        out_specs=(row,) * n_out,
        out_shape=(jax.ShapeDtypeStruct((T, D), F32),) * n_out,
        scratch_shapes=[pltpu.VMEM((tm, D), BF16)],
        compiler_params=pltpu.CompilerParams(dimension_semantics=("arbitrary",),
                                             vmem_limit_bytes=VMEM_LIMIT),
        name="outffn",
    )(x, a, sga, sgb, u, vn, wmix, bmix, wo, g2, wg, wu, wd, gf)
    return out if final else (out[0], None)


def _rope_tables(pos, d):
    inv = ROPE_THETA ** (-jnp.arange(0, d, 2, dtype=F32) / d)
    ang = pos.astype(F32)[:, None] * inv[None, :]
    cos, sin = jnp.cos(ang), jnp.sin(ang)
    c = jnp.concatenate([cos, cos], axis=-1)
    s = jnp.concatenate([-sin, sin], axis=-1)
    reps = LANES // d
    return jnp.tile(c, (1, reps)), jnp.tile(s, (1, reps))


def _mix_weights(sgu_w, sgu_b, n, group_dim):
    depth = sgu_w.shape[0]
    p = jnp.arange(n)
    mask = (p[None, :] // CHUNK) <= (p[:, None] // CHUNK)
    w = jnp.where(mask[None, None], sgu_w[:, :, :n, :n], 0.0)
    reps = SGU_CHUNK // n
    eye = jnp.eye(reps, dtype=w.dtype)
    wbd = jnp.einsum('ab,lgij->lgaibj', eye, w).reshape(depth, SGU_GROUPS, SGU_CHUNK, SGU_CHUNK)
    b = jnp.tile(sgu_b[:, :, :n], (1, 1, reps))
    bfull = jnp.repeat(jnp.swapaxes(b, 1, 2), group_dim, axis=2)
    return wbd.astype(BF16), bfull


def _in_weights(w_in, d_model):
    kvw = N_KV_HEADS * HEAD_DIM
    w = w_in
    c_ki = d_model + 2 * kvw + N_IDX_HEADS * IDX_DIM
    c_wi = c_ki + IDX_DIM
    c_u = c_wi + N_IDX_HEADS
    wki, wwi = w[:, :, c_ki:c_wi], w[:, :, c_wi:c_u]
    wwi_pad = jnp.pad(wwi, ((0, 0), (0, 0), (0, LANES - N_IDX_HEADS)))
    wa = jnp.concatenate([w[:, :, :c_ki], wki, wki, wwi_pad, w[:, :, c_u:]], axis=2)
    wv_t = jnp.swapaxes(w[:, :, d_model + kvw:d_model + 2 * kvw], 1, 2)
    wwi_t = jnp.pad(jnp.swapaxes(wwi, 1, 2), ((0, 0), (0, 16 - N_IDX_HEADS), (0, 0)))
    return wa.astype(BF16), jnp.concatenate([wv_t, wwi_t], axis=1).astype(BF16)


def kernel(x_prompt, x_sample, cache_k, cache_v, cache_kidx, norm1_g, w_in, ln_v_g, ln_v_b, sgu_w, sgu_b, w_out, norm2_g, w_gate, w_up, w_down, final_norm_g):
    B, S, D = x_prompt.shape
    NB, TN, _ = x_sample.shape
    depth, _, P = cache_k.shape[:3]
    assert D == N_HEADS * HEAD_DIM and SGU_CHUNK % TN == 0 and S % SGU_CHUNK == 0
    k_top_p = min(TOPK_MAX, S // 4)
    k_top_s = min(TOPK_MAX, (P + TN) // 4)
    group_dim = D // SGU_GROUPS

    assert S % TOKEN_TILE == 0 and TOKEN_TILE % TN == 0
    pos_p = jnp.arange(S)
    pos_s = jnp.tile(P + jnp.arange(TN), TOKEN_TILE // TN)
    tab_p = _rope_tables(pos_p, HEAD_DIM) + _rope_tables(pos_p, IDX_DIM)
    tab_s = _rope_tables(pos_s, HEAD_DIM) + _rope_tables(pos_s, IDX_DIM)

    wa, wb = _in_weights(w_in, D)
    wo, wg, wu, wd = (t.astype(BF16) for t in (w_out, w_gate, w_up, w_down))
    vec = lambda v: v.reshape(depth, 1, -1)
    g1, g2, lng, lnb = vec(norm1_g), vec(norm2_g), vec(ln_v_g), vec(ln_v_b)
    gf = final_norm_g.reshape(1, -1)
    wmix_p, bmix_p = _mix_weights(sgu_w, sgu_b, SGU_CHUNK, group_dim)
    wmix_s, bmix_s = _mix_weights(sgu_w, sgu_b, TN, group_dim)
    ck = cache_k.reshape(depth, NB, P * N_KV_HEADS, HEAD_DIM)
    cv = cache_v.reshape(depth, NB, P * N_KV_HEADS, HEAD_DIM)

    xp = x_prompt.reshape(B * S, D)
    xs = x_sample.reshape(NB * TN, D)
    def new_buffers(tokens):
        kv = (depth, N_KV_HEADS * tokens, HEAD_DIM)
        return jnp.zeros(kv, F32), jnp.zeros(kv, F32), jnp.zeros((depth, tokens, IDX_DIM), F32)

    new_p, new_s = new_buffers(B * S), new_buffers(NB * TN)
    sgu_v = []
    yp = ys = None
    for l in range(depth):
        final = l == depth - 1

        (q, kf, kb, vf, _, vt, qi, kif, kze, kzo, _, wit, u, vn, sga, sgb) = _inproj(
            xp, l, g1, wa, wb, tab_p, lng, lnb, new_p)
        new_p = (kf, vf, kif)
        a = _attn_prompt(q, qi, wit, kze, kzo, kb, vt, B, S, k_top_p)
        xp, yp = _outffn(xp, a, sga, sgb, u, vn, wmix_p, bmix_p, l, wo, g2, wg, wu, wd, gf, final)

        (q, kf, kb, vf, vb, _, qi, kif, _, _, wi, _, u, vn, sga, sgb) = _inproj(
            xs, l, g1, wa, wb, tab_s, lng, lnb, new_s)
        new_s = (kf, vf, kif)
        qi4 = qi.reshape(NB, TN, N_IDX_HEADS, IDX_DIM).transpose(0, 2, 1, 3)
        a = _attn_sample(q, qi4, wi, ck, cv, cache_kidx, kb, vb, kif, l, k_top_s)
        xs, ys = _outffn(xs, a, sga, sgb, u, vn, wmix_s, bmix_s, l, wo, g2, wg, wu, wd, gf, final)
        sgu_v.append(vn.reshape(NB, TN, D))

    kv_p = (depth, B, S, N_KV_HEADS, HEAD_DIM)
    kv_s = (depth, NB, TN, N_KV_HEADS, HEAD_DIM)
    return (yp.reshape(B, S, D), ys.reshape(NB, TN, D),
            new_p[0].reshape(kv_p), new_p[1].reshape(kv_p), new_p[2].reshape(depth, B, S, IDX_DIM),
            new_s[0].reshape(kv_s), new_s[1].reshape(kv_s), new_s[2].reshape(depth, NB, TN, IDX_DIM),
            jnp.stack(sgu_v))
```

```python
import functools

import jax
import jax.numpy as jnp
from jax import lax
from jax.experimental import pallas as pl
from jax.experimental.pallas import tpu as pltpu

CHUNK = 64
N_HEADS = 8
HEAD_DIM = 128
N_KV_HEADS = 2
KV_GROUP = N_HEADS // N_KV_HEADS
N_IDX_HEADS = 8
IDX_DIM = 64
TOPK_MAX = 256
SGU_CHUNK = 128
SGU_GROUPS = 8
ROPE_THETA = 10000.0
EPS = 1e-6

LANES = 128
TOKEN_TILE = 256
Q_TILE = 256
KEY_TILE = 256
VT_ROWS = HEAD_DIM + 16
FOLD_UNROLL = 4
FOLD = 8
VMEM_LIMIT = 56 * 1024 * 1024

INT_MIN = -(2 ** 31)
INT_MAX = 2 ** 31 - 1
MASK_BIAS = -1e30
LOG2E = 1.4426950408889634
Q_SCALE = (HEAD_DIM ** -0.5) * LOG2E

F32 = jnp.float32
BF16 = jnp.bfloat16
NT_DIMS = (((1,), (1,)), ((), ()))


def _const_spec(shape):
    nd = len(shape)
    return pl.BlockSpec(shape, lambda *_: (0,) * nd, pipeline_mode=pl.Buffered(1))


def _sigmoid(x):
    return 1.0 / (1.0 + jnp.exp(-x))


def _rms_norm(x, g):
    return x * lax.rsqrt(jnp.mean(x * x, axis=-1, keepdims=True) + EPS) * g


def _order_key(score):
    bits = pltpu.bitcast(score, jnp.int32)
    return bits ^ ((bits >> 31) & INT_MAX)


def _midpoint(lo, hi):
    return (lo >> 1) + (hi >> 1) + (lo & hi & 1)


_C_Q = 0
_C_K = _C_Q + N_HEADS * HEAD_DIM
_C_V = _C_K + N_KV_HEADS * HEAD_DIM
_C_QI = _C_V + N_KV_HEADS * HEAD_DIM
_C_KI = _C_QI + N_IDX_HEADS * IDX_DIM
_C_WI = _C_KI + LANES
_C_U = _C_WI + LANES


def _inproj_kernel(*refs, d_model, n_alias):
    (x_ref, g_ref, wa_ref, wb_ref, cosh_ref, sinh_ref, cosi_ref, sini_ref,
     lng_ref, lnb_ref) = refs[:10]
    (q_ref, kf_ref, kb_ref, vf_ref, vb_ref, vt_ref, qi_ref, kif_ref, kze_ref, kzo_ref,
     wi_ref, wit_ref, u_ref, vn_ref, sga_ref, sgb_ref) = refs[10 + n_alias:]
    tm = x_ref.shape[0]
    hb = _rms_norm(x_ref[...], g_ref[0]).astype(BF16)

    def proj(c0, width):
        return jnp.dot(hb, wa_ref[0, :, c0:c0 + width], preferred_element_type=F32)

    def head_rows(n):
        return pl.ds(n, tm, stride=N_KV_HEADS)

    cosh, sinh = cosh_ref[...], sinh_ref[...]
    cosi, sini = cosi_ref[...], sini_ref[...]
    lane = lax.broadcasted_iota(jnp.int32, cosi.shape, 1)
    first_half = (lane % IDX_DIM) < (IDX_DIM // 2)

    def rope_head(x):
        return x * cosh + pltpu.roll(x, HEAD_DIM // 2, 1) * sinh

    def rope_idx(x):
        partner = jnp.where(first_half, pltpu.roll(x, LANES - IDX_DIM // 2, 1),
                            pltpu.roll(x, IDX_DIM // 2, 1))
        return x * cosi + partner * sini

    xq = proj(_C_Q, N_HEADS * HEAD_DIM)
    for h in range(N_HEADS):
        sl = slice(h * HEAD_DIM, (h + 1) * HEAD_DIM)
        q_ref[:, sl] = (rope_head(xq[:, sl]) * Q_SCALE).astype(BF16)

    xk = proj(_C_K, N_KV_HEADS * HEAD_DIM)
    for h in range(N_KV_HEADS):
        sl = slice(h * HEAD_DIM, (h + 1) * HEAD_DIM)
        kr = rope_head(xk[:, sl])
        kf_ref[0, head_rows(h), :] = kr
        kb_ref[:, sl] = kr.astype(BF16)

    xv = proj(_C_V, N_KV_HEADS * HEAD_DIM)
    vb_ref[...] = xv.astype(BF16)
    for h in range(N_KV_HEADS):
        vf_ref[0, head_rows(h), :] = xv[:, h * HEAD_DIM:(h + 1) * HEAD_DIM]

    xqi = proj(_C_QI, N_IDX_HEADS * IDX_DIM)
    for p in range(N_IDX_HEADS * IDX_DIM // LANES):
        sl = slice(p * LANES, (p + 1) * LANES)
        qi_ref[:, sl] = (rope_idx(xqi[:, sl]) * (IDX_DIM ** -0.5)).astype(BF16)

    kk = rope_idx(proj(_C_KI, LANES))
    kif_ref[0] = kk[:, :IDX_DIM]
    low = lane < IDX_DIM
    kze_ref[...] = jnp.where(low, kk, 0.0).astype(BF16)
    kzo_ref[...] = jnp.where(low, 0.0, kk).astype(BF16)

    wi_ref[...] = proj(_C_WI, LANES) * (N_IDX_HEADS ** -0.5)

    tb = lax.dot_general(wb_ref[0], hb, NT_DIMS, preferred_element_type=F32)
    kvw = N_KV_HEADS * HEAD_DIM
    ones_rows = (lax.broadcasted_iota(jnp.int32, (VT_ROWS - HEAD_DIM, tm), 0) == 0).astype(BF16)
    for n in range(N_KV_HEADS):
        vt_ref[0, n * VT_ROWS:n * VT_ROWS + HEAD_DIM, :] = (
            tb[n * HEAD_DIM:(n + 1) * HEAD_DIM].astype(BF16))
        vt_ref[0, n * VT_ROWS + HEAD_DIM:(n + 1) * VT_ROWS, :] = ones_rows
    wit_ref[...] = tb[kvw:] * (N_IDX_HEADS ** -0.5)

    u_ref[...] = jax.nn.gelu(proj(_C_U, d_model), approximate=True)
    gv = jax.nn.gelu(proj(_C_U + d_model, d_model), approximate=True)
    mu = jnp.mean(gv, axis=-1, keepdims=True)
    dv = gv - mu
    var = jnp.mean(dv * dv, axis=-1, keepdims=True)
    vn_ref[...] = dv * lax.rsqrt(var + EPS) * lng_ref[0] + lnb_ref[0]
    sga_ref[...] = _sigmoid(proj(_C_U + 2 * d_model, d_model))
    sgb_ref[...] = _sigmoid(proj(_C_U + 3 * d_model, d_model))


def _layer_spec(shape, layer):
    nd = len(shape)
    return pl.BlockSpec((1,) + tuple(shape[1:]), lambda *_: (layer,) + (0,) * (nd - 1),
                        pipeline_mode=pl.Buffered(1))


def _inproj(x, layer, g, wa, wb, tables, lng, lnb, stacked):
    T, D = x.shape
    depth = wa.shape[0]
    tm = TOKEN_TILE
    assert T % tm == 0
    kvw = N_KV_HEADS * HEAD_DIM
    qiw = N_IDX_HEADS * IDX_DIM
    row = lambda w: pl.BlockSpec((tm, w), lambda i: (i, 0))
    kv_rows = N_KV_HEADS * T
    out_shape = (
        jax.ShapeDtypeStruct((T, N_HEADS * HEAD_DIM), BF16),
        jax.ShapeDtypeStruct((depth, kv_rows, HEAD_DIM), F32),
        jax.ShapeDtypeStruct((T, kvw), BF16),
        jax.ShapeDtypeStruct((depth, kv_rows, HEAD_DIM), F32),
        jax.ShapeDtypeStruct((T, kvw), BF16),
        jax.ShapeDtypeStruct((T // tm, N_KV_HEADS * VT_ROWS, tm), BF16),
        jax.ShapeDtypeStruct((T, qiw), BF16),
        jax.ShapeDtypeStruct((depth, T, IDX_DIM), F32),
        jax.ShapeDtypeStruct((T, LANES), BF16),
        jax.ShapeDtypeStruct((T, LANES), BF16),
        jax.ShapeDtypeStruct((T, LANES), F32),
        jax.ShapeDtypeStruct((16, T), F32),
        jax.ShapeDtypeStruct((T, D), F32),
        jax.ShapeDtypeStruct((T, D), F32),
        jax.ShapeDtypeStruct((T, D), F32),
        jax.ShapeDtypeStruct((T, D), F32),
    )
    kv_spec = pl.BlockSpec((1, N_KV_HEADS * tm, HEAD_DIM), lambda i: (layer, i, 0))
    out_specs = (
        row(N_HEADS * HEAD_DIM), kv_spec, row(kvw), kv_spec, row(kvw),
        pl.BlockSpec((1, N_KV_HEADS * VT_ROWS, tm), lambda i: (i, 0, 0)),
        row(qiw), pl.BlockSpec((1, tm, IDX_DIM), lambda i: (layer, i, 0)),
        row(LANES), row(LANES), row(LANES),
        pl.BlockSpec((16, tm), lambda i: (0, i)),
        row(D), row(D), row(D), row(D),
    )
    n_tab = tables[0].shape[0] // tm
    tab = pl.BlockSpec((tm, LANES), lambda i: (i % n_tab, 0))
    in_specs = [row(D), _layer_spec(g.shape, layer), _layer_spec(wa.shape, layer),
                _layer_spec(wb.shape, layer),
                tab, tab, tab, tab,
                _layer_spec(lng.shape, layer), _layer_spec(lnb.shape, layer)]
    args = [x, g, wa, wb, *tables, lng, lnb]
    aliases = {}
    for j, out_idx in enumerate((1, 3, 7)):
        aliases[len(args)] = out_idx
        in_specs.append(pl.BlockSpec(memory_space=pl.ANY))
        args.append(stacked[j])
    return pl.pallas_call(
        functools.partial(_inproj_kernel, d_model=D, n_alias=len(aliases)),
        grid=(T // tm,),
        in_specs=in_specs, out_specs=out_specs, out_shape=out_shape,
        input_output_aliases=aliases,
        compiler_params=pltpu.CompilerParams(dimension_semantics=("arbitrary",),
                                             vmem_limit_bytes=VMEM_LIMIT),
        name="inproj",
    )(*args)


def _attn_prompt_kernel(q_ref, qi_ref, wit_ref, kze_ref, kzo_ref, k_ref, vt_ref, o_ref,
                        keys_ref, hi_ref, lo_ref, f1_ref, f2_ref, acc_ref, m_ref, mx_ref, s_ref, p_ref,
                        bias_ref, *, k_top):
    tq, kb_sz = Q_TILE, KEY_TILE
    qb = pl.program_id(1)
    nkb = (qb + 1) * (tq // kb_sz)
    q_chunk = (qb * tq + lax.broadcasted_iota(jnp.int32, (1, tq), 1)) // CHUNK
    row_iota = lax.broadcasted_iota(jnp.int32, (kb_sz, 1), 0)

    def key_slice(kb):
        return pl.ds(pl.multiple_of(kb * kb_sz, kb_sz), kb_sz)

    n_blocks = keys_ref.shape[0] // kb_sz
    n_full = nkb // 2

    def clamp(blk):
        return jnp.minimum(blk, n_blocks - 1)

    def idx_dots(blk, slot):
        ks = key_slice(blk)
        kze, kzo = kze_ref[ks, :], kzo_ref[ks, :]
        for p in range(N_IDX_HEADS // 2):
            qp = qi_ref[:, p * LANES:(p + 1) * LANES]
            s_ref[slot, 2 * p] = lax.dot_general(kze, qp, NT_DIMS, preferred_element_type=F32)
            s_ref[slot, 2 * p + 1] = lax.dot_general(kzo, qp, NT_DIMS, preferred_element_type=F32)

    def idx_keys(blk, slot):
        base = pl.multiple_of(blk * kb_sz, kb_sz)
        for c in range(kb_sz // CHUNK):
            rows = slice(c * CHUNK, (c + 1) * CHUNK)
            score = jnp.zeros((CHUNK, tq), F32)
            for j in range(N_IDX_HEADS):
                score = score + wit_ref[j:j + 1, :] * jnp.maximum(s_ref[slot, j, rows, :], 0.0)
            visible = (blk * (kb_sz // CHUNK) + c) <= q_chunk
            key = jnp.where(visible, _order_key(score), INT_MIN)
            ks = pl.ds(base + c * CHUNK, CHUNK)
            keys_ref[ks, :] = key
            hi_ref[ks, :] = (key >> 16).astype(jnp.int16)
            lo_ref[ks, :] = (key ^ 0x8000).astype(jnp.int16)

    def score_body(j, carry):
        b0 = 2 * j
        idx_dots(b0 + 1, 1)
        idx_keys(b0, 0)
        idx_dots(clamp(b0 + 2), 0)
        idx_keys(b0 + 1, 1)
        return carry

    idx_dots(0, 0)
    lax.fori_loop(0, n_full, score_body, 0)

    @pl.when(nkb % 2 == 1)
    def _():
        idx_keys(nkb - 1, 0)
        keys_ref[key_slice(nkb), :] = jnp.full((kb_sz, tq), INT_MIN, jnp.int32)

    n_acc = 4
    rows16 = 16

    def fold_rows(x, accs):
        accs = list(accs)
        for r in range(kb_sz // rows16):
            accs[r % n_acc] = accs[r % n_acc] + x[r * rows16:(r + 1) * rows16, :]
        return tuple(accs)

    def zero_accs():
        return tuple(jnp.zeros((rows16, tq), jnp.int16) for _ in range(n_acc))

    def total(accs):
        tot = accs[0].astype(jnp.int32)
        for a in accs[1:]:
            tot = tot + a.astype(jnp.int32)
        return jnp.sum(tot, axis=0, keepdims=True)

    def count_ge(plane_ref, t):
        t16 = jnp.broadcast_to(t, (kb_sz, tq)).astype(jnp.int16)

        def body(kb, accs):
            ge = (plane_ref[key_slice(kb), :] >= t16).astype(jnp.int16)
            return fold_rows(ge, accs)

        return total(lax.fori_loop(0, nkb, body, zero_accs()))

    def bisect16(count, need, lo, hi, count_lo, steps):
        def body(_, carry):
            lo, hi, n_lo = carry
            mid = (lo + hi) >> 1
            n_mid = count(mid)
            ok = n_mid >= need
            return jnp.where(ok, mid, lo), jnp.where(ok, hi, mid), jnp.where(ok, n_mid, n_lo)

        lo, _, n_lo = lax.fori_loop(0, steps, body, (lo, hi, count_lo))
        return lo, n_lo

    floor16 = jnp.full((1, tq), -(2 ** 15), jnp.int32)
    ceil16 = jnp.full((1, tq), 2 ** 15, jnp.int32)
    every = jnp.full((1, tq), keys_ref.shape[0], jnp.int32)
    t_hi, n_hi = bisect16(functools.partial(count_ge, hi_ref), k_top, floor16, ceil16, every, 16)

    t_hi16 = jnp.broadcast_to(t_hi, (kb_sz, tq)).astype(jnp.int16)
    sentinel = jnp.int16(-(2 ** 15))
    n_slots = kb_sz // rows16 // FOLD

    def max_min(a, b):
        a_ge = a >= b
        return jnp.where(a_ge, a, b), jnp.where(a_ge, b, a)

    def tie_body(kb, accs):
        ks = key_slice(kb)
        hi_part = hi_ref[ks, :]
        e = jnp.where(hi_part == t_hi16, lo_ref[ks, :], sentinel)
        lo_ref[ks, :] = e
        g = [e[r * rows16:(r + 1) * rows16, :] for r in range(kb_sz // rows16)]
        n = len(g) // 2
        pairs = [max_min(g[i], g[i + n]) for i in range(n)]
        first, second = [p[0] for p in pairs], [p[1] for p in pairs]
        while n > n_slots:
            n //= 2
            pairs = [max_min(first[i], first[i + n]) for i in range(n)]
            second = [max_min(pairs[i][1], max_min(second[i], second[i + n])[0])[0] for i in range(n)]
            first = [p[0] for p in pairs]
        fs = pl.ds(pl.multiple_of(kb * (n_slots * rows16), n_slots * rows16), n_slots * rows16)
        f1_ref[fs, :] = jnp.concatenate(first, axis=0)
        f2_ref[fs, :] = jnp.concatenate(second, axis=0)
        return fold_rows((hi_part > t_hi16).astype(jnp.int16), accs)

    n_above = total(lax.fori_loop(0, nkb, tie_body, zero_accs()))
    need_lo = k_top - n_above
    for i in range(FOLD_UNROLL - 1):
        @pl.when(nkb + i < n_blocks)
        def _(i=i):
            fs = pl.ds(pl.multiple_of((nkb + i) * (n_slots * rows16), n_slots * rows16), n_slots * rows16)
            f1_ref[fs, :] = jnp.full((n_slots * rows16, tq), sentinel, jnp.int16)
            f2_ref[fs, :] = jnp.full((n_slots * rows16, tq), sentinel, jnp.int16)

    def count_folded(t):
        t16 = jnp.broadcast_to(t, (n_slots * rows16, tq)).astype(jnp.int16)

        def body(j, accs):
            accs = list(accs)
            for u in range(FOLD_UNROLL):
                kb = j * FOLD_UNROLL + u
                fs = pl.ds(pl.multiple_of(kb * (n_slots * rows16), n_slots * rows16), n_slots * rows16)
                for i, plane in enumerate((f1_ref, f2_ref)):
                    ge = (plane[fs, :] >= t16).astype(jnp.int16)
                    for r in range(n_slots):
                        a = (i * n_slots + r) % n_acc
                        accs[a] = accs[a] + ge[r * rows16:(r + 1) * rows16, :]
            return tuple(accs)

        return total(lax.fori_loop(0, (nkb + FOLD_UNROLL - 1) // FOLD_UNROLL, body, zero_accs()))

    def count_lo_pair(t):
        t_next = jnp.minimum(t + 1, 2 ** 15 - 1)
        ta = jnp.broadcast_to(t, (kb_sz, tq)).astype(jnp.int16)
        tb = jnp.broadcast_to(t_next, (kb_sz, tq)).astype(jnp.int16)

        def body(kb, accs):
            plane = lo_ref[key_slice(kb), :]
            return (fold_rows((plane >= ta).astype(jnp.int16), accs[0]),
                    fold_rows((plane >= tb).astype(jnp.int16), accs[1]))

        a, b = lax.fori_loop(0, nkb, body, (zero_accs(), zero_accs()))
        return total(a), jnp.where(t < 2 ** 15 - 1, total(b), 0)

    t_lo, _ = bisect16(count_folded, need_lo, floor16, ceil16, every, 16)
    n_at, n_next = count_lo_pair(t_lo)

    def unfolded_search():
        t, _ = bisect16(functools.partial(count_ge, lo_ref), need_lo, floor16, ceil16, every, 16)
        return (t,) + count_lo_pair(t)

    t_lo, n_at, n_next = lax.cond(jnp.max((n_next >= need_lo).astype(jnp.int32)) > 0,
                                  unfolded_search, lambda: (t_lo, n_at, n_next))
    thr = jnp.maximum((t_hi << 16) + (t_lo + 2 ** 15), INT_MIN + 1)

    n_selected = jnp.where(t_hi > -(2 ** 15),
                           jnp.where(t_lo > -(2 ** 15), n_above + n_at, n_hi), 0)
    need_eq = (need_lo - n_next).astype(F32)
    col_iota = lax.broadcasted_iota(jnp.int32, (1, kb_sz), 1)

    for c in range(tq // LANES):
        cs = slice(c * LANES, (c + 1) * LANES)

        @pl.when(jnp.max(n_selected[:, cs]) > k_top)
        def _(cs=cs):
            tri = jnp.where(col_iota <= row_iota, 1.0, 0.0).astype(BF16)
            thr_c, need_c = thr[:, cs], need_eq[:, cs]

            unroll = 4

            def drop(j, seen):
                blocks = [unroll * j + i for i in range(unroll)]
                keys = [keys_ref[key_slice(clamp(b)), cs] for b in blocks]
                equal = [(key == thr_c) & (b < nkb) for b, key in zip(blocks, keys)]
                local = [jnp.dot(tri, jnp.where(e, 1.0, 0.0).astype(BF16),
                                 preferred_element_type=F32) for e in equal]
                for b, key, e, rank in zip(blocks, keys, equal, local):
                    rank = rank + seen
                    keys_ref[key_slice(clamp(b)), cs] = jnp.where(e & (rank > need_c), key - 1, key)
                    seen = rank[kb_sz - 1:kb_sz, :]
                return seen

            lax.fori_loop(0, (nkb + unroll - 1) // unroll, drop, jnp.zeros((1, LANES), F32))

    m_ref[...] = jnp.full(m_ref.shape, MASK_BIAS, F32)
    acc_ref[...] = jnp.zeros(acc_ref.shape, F32)

    rc = 32
    n_rc = kb_sz // rc

    def fold8(x):
        return x.reshape(rc // 8, 8, tq)

    def qk_dots(blk, slot):
        ks = key_slice(blk)
        for h in range(N_HEADS):
            n = h // KV_GROUP
            k_n = k_ref[ks, n * HEAD_DIM:(n + 1) * HEAD_DIM]
            q_h = q_ref[:, h * HEAD_DIM:(h + 1) * HEAD_DIM]
            s_ref[slot, h] = lax.dot_general(k_n, q_h, NT_DIMS, preferred_element_type=F32)

    def softmax_block(blk, slot):
        thr_b = jnp.where(blk < nkb, thr, INT_MAX)
        bias_ref[slot] = jnp.where(keys_ref[key_slice(clamp(blk)), :] >= thr_b, 0.0, MASK_BIAS)
        for h in range(N_HEADS):
            mx = jnp.full((8, tq), MASK_BIAS, F32)
            for r in range(n_rc):
                rows = slice(r * rc, (r + 1) * rc)
                sc = s_ref[slot, h, rows, :] + bias_ref[slot, rows, :]
                s_ref[slot, h, rows, :] = sc
                mx = jnp.maximum(mx, jnp.max(fold8(sc), axis=0))
            mx_ref[h:h + 1, :] = jnp.max(mx, axis=0, keepdims=True)
        m_old = m_ref[...]
        m_new = jnp.maximum(m_old, mx_ref[...])
        alpha = jnp.exp2(m_old - m_new)
        m_ref[...] = m_new
        for h in range(N_HEADS):
            m_h = jnp.broadcast_to(m_new[h:h + 1, :], (rc, tq))
            for r in range(n_rc):
                rows = slice(r * rc, (r + 1) * rc)
                p_ref[slot, h, rows, :] = jnp.exp2(s_ref[slot, h, rows, :] - m_h).astype(BF16)
        return alpha

    def pv_update(blk, slot, alpha):
        for h in range(N_HEADS):
            n = h // KV_GROUP
            vt_n = vt_ref[blk, n * VT_ROWS:(n + 1) * VT_ROWS, :]
            pv = jnp.dot(vt_n, p_ref[slot, h], preferred_element_type=F32)
            acc_ref[h] = alpha[h:h + 1, :] * acc_ref[h] + pv

    def attn_body(j, carry):
        b0 = 2 * j
        qk_dots(clamp(b0 + 1), 1)
        pv_update(b0, 0, softmax_block(b0, 0))
        qk_dots(clamp(b0 + 2), 0)
        pv_update(clamp(b0 + 1), 1, softmax_block(b0 + 1, 1))
        return carry

    qk_dots(0, 0)
    lax.fori_loop(0, (nkb + 1) // 2, attn_body, 0)

    for h in range(N_HEADS):
        o_t = acc_ref[h, :HEAD_DIM, :] / acc_ref[h, HEAD_DIM:HEAD_DIM + 1, :]
        o_ref[:, h * HEAD_DIM:(h + 1) * HEAD_DIM] = o_t.T


def _attn_prompt(q, qi, wit, kze, kzo, kb, vt, batch, seq, k_top):
    tq = Q_TILE
    assert seq % tq == 0 and Q_TILE % KEY_TILE == 0 and KEY_TILE == TOKEN_TILE
    assert (seq // KEY_TILE) % 4 == 0
    nq = seq // tq
    kvw = N_KV_HEADS * HEAD_DIM
    qrow = lambda w: pl.BlockSpec((tq, w), lambda b, i: (b * nq + i, 0))
    per_batch = lambda w: pl.BlockSpec((seq, w), lambda b, i: (b, 0), pipeline_mode=pl.Buffered(1))
    return pl.pallas_call(
        functools.partial(_attn_prompt_kernel, k_top=k_top),
        grid=(batch, nq),
        in_specs=[qrow(N_HEADS * HEAD_DIM), qrow(N_IDX_HEADS * IDX_DIM),
                  pl.BlockSpec((16, tq), lambda b, i: (0, b * nq + i)),
                  per_batch(LANES), per_batch(LANES), per_batch(kvw),
                  pl.BlockSpec((seq // KEY_TILE, N_KV_HEADS * VT_ROWS, KEY_TILE), lambda b, i: (b, 0, 0),
                               pipeline_mode=pl.Buffered(1))],
        out_specs=qrow(N_HEADS * HEAD_DIM),
        out_shape=jax.ShapeDtypeStruct((batch * seq, N_HEADS * HEAD_DIM), F32),
        scratch_shapes=[pltpu.VMEM((seq, tq), jnp.int32),
                        pltpu.VMEM((seq, tq), jnp.int16),
                        pltpu.VMEM((seq, tq), jnp.int16),
                        pltpu.VMEM((seq // FOLD, tq), jnp.int16),
                        pltpu.VMEM((seq // FOLD, tq), jnp.int16),
                        pltpu.VMEM((N_HEADS, VT_ROWS, tq), F32),
                        pltpu.VMEM((N_HEADS, tq), F32),
                        pltpu.VMEM((N_HEADS, tq), F32),
                        pltpu.VMEM((2, N_HEADS, KEY_TILE, tq), F32),
                        pltpu.VMEM((2, N_HEADS, KEY_TILE, tq), BF16),
                        pltpu.VMEM((2, KEY_TILE, tq), F32)],
        compiler_params=pltpu.CompilerParams(dimension_semantics=("arbitrary", "arbitrary"),
                                             vmem_limit_bytes=VMEM_LIMIT),
        name="attn_prompt",
    )(q, qi, wit, kze, kzo, kb, vt)


def _attn_sample_kernel(q_ref, qi_ref, wi_ref, ck_ref, cv_ref, cki_ref, kn_ref, vn_ref, kin_ref,
                        o_ref, keyp_ref, keyn_ref, keyt_ref, biasp_ref, *, k_top, past_len, tn):
    group = LANES // tn
    rows = group * tn
    n_cols = past_len // LANES
    kn = kn_ref[...]
    vn = vn_ref[...]
    kin = kin_ref[0].astype(BF16)

    lane = lax.broadcasted_iota(jnp.int32, (tn, LANES), 1)
    pos = lax.broadcasted_iota(jnp.int32, (tn, LANES), 0)
    vis = ((past_len + lane % tn) // CHUNK) <= ((past_len + pos) // CHUNK)
    for j in range(group):
        qj = qi_ref[j].reshape(N_IDX_HEADS * tn, IDX_DIM)
        dp = lax.dot_general(qj, cki_ref[0, j].astype(BF16), NT_DIMS, preferred_element_type=F32)
        dn = lax.dot_general(qj, kin, NT_DIMS, preferred_element_type=F32)
        sp = jnp.zeros((tn, past_len), F32)
        sn = jnp.zeros((tn, LANES), F32)
        for h in range(N_IDX_HEADS):
            w = wi_ref[j * tn:(j + 1) * tn, h:h + 1]
            sp = sp + w * jnp.maximum(dp[h * tn:(h + 1) * tn, :], 0.0)
            sn = sn + w * jnp.maximum(dn[h * tn:(h + 1) * tn, :], 0.0)
        keyp_ref[j * tn:(j + 1) * tn, :] = _order_key(sp)
        own = (lane // tn) == j
        keyn_ref[j * tn:(j + 1) * tn, :] = jnp.where(own & vis, _order_key(sn), INT_MIN)

    for c in range(n_cols):
        keyt_ref[c * LANES:(c + 1) * LANES, :] = keyp_ref[:, c * LANES:(c + 1) * LANES].T
    keyt_ref[past_len:past_len + LANES, :] = keyn_ref[...].T
    n_keys = past_len + LANES

    def bisect_body(_, carry):
        lo, hi = carry
        mid = _midpoint(lo, hi)
        ge = (keyt_ref[...] >= mid).astype(jnp.int32)
        cnt = jnp.sum(jnp.sum(ge.reshape(n_keys // 8, 8, rows), axis=0), axis=0, keepdims=True)
        ok = cnt >= k_top
        return jnp.where(ok, mid, lo), jnp.where(ok, hi, mid)

    thr, _ = lax.fori_loop(
        0, 32, bisect_body,
        (jnp.full((1, rows), INT_MIN + 1, jnp.int32), jnp.full((1, rows), INT_MAX, jnp.int32)))
    thr_col = jnp.broadcast_to(thr, (rows, rows)).T

    key_row = lax.broadcasted_iota(jnp.int32, (n_keys, 1), 0)

    def count_keys(pred):
        hit = pred(keyt_ref[...], key_row).astype(jnp.int32)
        return jnp.sum(jnp.sum(hit.reshape(n_keys // 8, 8, rows), axis=0), axis=0, keepdims=True)

    n_selected = count_keys(lambda key, row: key >= thr)

    @pl.when(jnp.max(n_selected) > k_top)
    def _():
        need_eq = k_top - count_keys(lambda key, row: key > thr)

        def body(_, carry):
            lo, hi = carry
            mid = (lo + hi) >> 1
            ok = count_keys(lambda key, row: (key == thr) & (row < mid)) >= need_eq
            return jnp.where(ok, lo, mid), jnp.where(ok, mid, hi)

        _, cut = lax.fori_loop(0, n_keys.bit_length(), body,
                               (jnp.zeros((1, rows), jnp.int32), jnp.full((1, rows), n_keys, jnp.int32)))
        cut_col = jnp.broadcast_to(cut, (rows, rows)).T
        lane_idx = lax.broadcasted_iota(jnp.int32, (rows, LANES), 1)
        for c in range(n_cols + 1):
            ref, cols = (keyp_ref, slice(c * LANES, (c + 1) * LANES)) if c < n_cols else (keyn_ref, slice(None))
            key = ref[:, cols]
            surplus = (key == thr_col) & (c * LANES + lane_idx >= cut_col)
            ref[:, cols] = jnp.where(surplus, key - 1, key)

    bias_n = jnp.where(keyn_ref[...] >= thr_col, 0.0, MASK_BIAS)
    for c in range(n_cols):
        cols = slice(c * LANES, (c + 1) * LANES)
        biasp_ref[:, cols] = jnp.where(keyp_ref[:, cols] >= thr_col, 0.0, MASK_BIAS)

    for j in range(group):
        bias_pj = biasp_ref[j * tn:(j + 1) * tn, :]
        bias_nj = bias_n[j * tn:(j + 1) * tn, :]
        for n in range(N_KV_HEADS):
            hs = slice(n * HEAD_DIM, (n + 1) * HEAD_DIM)
            k_n = ck_ref[0, j, pl.ds(n, past_len, stride=N_KV_HEADS), :].astype(BF16)
            v_n = cv_ref[0, j, pl.ds(n, past_len, stride=N_KV_HEADS), :].astype(BF16)
            qg = jnp.concatenate(
                [q_ref[j * tn:(j + 1) * tn, (n * KV_GROUP + g) * HEAD_DIM:(n * KV_GROUP + g + 1) * HEAD_DIM]
                 for g in range(KV_GROUP)], axis=0)
            s1 = lax.dot_general(qg, k_n, NT_DIMS, preferred_element_type=F32)
            s2 = lax.dot_general(qg, kn[:, hs], NT_DIMS, preferred_element_type=F32)
            s1 = (s1.reshape(KV_GROUP, tn, past_len) + bias_pj[None]).reshape(KV_GROUP * tn, past_len)
            s2 = (s2.reshape(KV_GROUP, tn, LANES) + bias_nj[None]).reshape(KV_GROUP * tn, LANES)
            m = jnp.maximum(jnp.max(s1, axis=1, keepdims=True), jnp.max(s2, axis=1, keepdims=True))
            p1 = jnp.exp2(s1 - m)
            p2 = jnp.exp2(s2 - m)
            l = jnp.sum(p1, axis=1, keepdims=True) + jnp.sum(p2, axis=1, keepdims=True)
            o = (jnp.dot(p1.astype(BF16), v_n, preferred_element_type=F32)
                 + jnp.dot(p2.astype(BF16), vn[:, hs], preferred_element_type=F32)) / l
            for g in range(KV_GROUP):
                h = n * KV_GROUP + g
                o_ref[j * tn:(j + 1) * tn, h * HEAD_DIM:(h + 1) * HEAD_DIM] = o[g * tn:(g + 1) * tn, :]


def _attn_sample(q, qi4, wi, ck, cv, cki, kb, vb, kif, layer, k_top):
    nb = ck.shape[1]
    past_len = cki.shape[2]
    kvw = N_KV_HEADS * HEAD_DIM
    tn = q.shape[0] // nb
    assert LANES % tn == 0 and tn % 16 == 0 and past_len % LANES == 0
    group = LANES // tn
    assert nb % group == 0
    row = lambda w: pl.BlockSpec((LANES, w), lambda i: (i, 0))
    cache = lambda rows, w: pl.BlockSpec((1, group, rows, w), lambda i: (layer, i, 0, 0))
    return pl.pallas_call(
        functools.partial(_attn_sample_kernel, k_top=k_top, past_len=past_len, tn=tn),
        grid=(nb // group,),
        in_specs=[row(N_HEADS * HEAD_DIM),
                  pl.BlockSpec((group, N_IDX_HEADS, tn, IDX_DIM), lambda i: (i, 0, 0, 0)),
                  row(LANES), cache(N_KV_HEADS * past_len, HEAD_DIM),
                  cache(N_KV_HEADS * past_len, HEAD_DIM), cache(past_len, IDX_DIM),
                  row(kvw), row(kvw),
                  pl.BlockSpec((1, LANES, IDX_DIM), lambda i: (layer, i, 0))],
        out_specs=row(N_HEADS * HEAD_DIM),
        out_shape=jax.ShapeDtypeStruct((nb * tn, N_HEADS * HEAD_DIM), F32),
        scratch_shapes=[pltpu.VMEM((LANES, past_len), jnp.int32),
                        pltpu.VMEM((LANES, LANES), jnp.int32),
                        pltpu.VMEM((past_len + LANES, LANES), jnp.int32),
                        pltpu.VMEM((LANES, past_len), F32)],
        compiler_params=pltpu.CompilerParams(dimension_semantics=("arbitrary",),
                                             vmem_limit_bytes=VMEM_LIMIT),
        name="attn_sample",
    )(q, qi4, wi, ck, cv, cki, kb, vb, kif)


def _outffn_kernel(x_ref, a_ref, sga_ref, sgb_ref, u_ref, vn_ref, wmix_ref, bmix_ref, wo_ref,
                   g2_ref, wg_ref, wu_ref, wd_ref, gf_ref, xo_ref, *rest, final):
    if final:
        y_ref, z_ref = rest
    else:
        (z_ref,) = rest
    tm = x_ref.shape[0]
    group_dim = wmix_ref.shape[-1]
    for c in range(tm // SGU_CHUNK):
        rs = slice(c * SGU_CHUNK, (c + 1) * SGU_CHUNK)
        for g in range(SGU_GROUPS):
            cs = slice(g * group_dim, (g + 1) * group_dim)
            mixed = jnp.dot(wmix_ref[0, g], vn_ref[rs, cs].astype(BF16),
                            preferred_element_type=F32) + bmix_ref[0, :, cs]
            z = sga_ref[rs, cs] * a_ref[rs, cs] + sgb_ref[rs, cs] * (u_ref[rs, cs] * mixed)
            z_ref[rs, cs] = z.astype(BF16)
    x1 = x_ref[...] + jnp.dot(z_ref[...], wo_ref[0], preferred_element_type=F32)
    h2 = _rms_norm(x1, g2_ref[0]).astype(BF16)
    gate = jnp.dot(h2, wg_ref[0], preferred_element_type=F32)
    up = jnp.dot(h2, wu_ref[0], preferred_element_type=F32)
    ff = (gate * _sigmoid(gate) * up).astype(BF16)
    x2 = x1 + jnp.dot(ff, wd_ref[0], preferred_element_type=F32)
    xo_ref[...] = x2
    if final:
        y_ref[...] = _rms_norm(x2, gf_ref[...])


def _outffn(x, a, sga, sgb, u, vn, wmix, bmix, layer, wo, g2, wg, wu, wd, gf, final):
    T, D = x.shape
    tm = TOKEN_TILE
    row = pl.BlockSpec((tm, D), lambda i: (i, 0))
    n_out = 2 if final else 1
    out = pl.pallas_call(
        functools.partial(_outffn_kernel, final=final),
        grid=(T // tm,),
        in_specs=[row] * 6 + [_layer_spec(wmix.shape, layer), _layer_spec(bmix.shape, layer),
                              _layer_spec(wo.shape, layer), _layer_spec(g2.shape, layer),
                              _layer_spec(wg.shape, layer), _layer_spec(wu.shape, layer),
                              _layer_spec(wd.shape, layer), _const_spec((1, D))],
        out_specs=(row,) * n_out,
        out_shape=(jax.ShapeDtypeStruct((T, D), F32),) * n_out,
        scratch_shapes=[pltpu.VMEM((tm, D), BF16)],
        compiler_params=pltpu.CompilerParams(dimension_semantics=("arbitrary",),
                                             vmem_limit_bytes=VMEM_LIMIT),
        name="outffn",
    )(x, a, sga, sgb, u, vn, wmix, bmix, wo, g2, wg, wu, wd, gf)
    return out if final else (out[0], None)


def _rope_tables(pos, d):
    inv = ROPE_THETA ** (-jnp.arange(0, d, 2, dtype=F32) / d)
    ang = pos.astype(F32)[:, None] * inv[None, :]
    cos, sin = jnp.cos(ang), jnp.sin(ang)
    c = jnp.concatenate([cos, cos], axis=-1)
    s = jnp.concatenate([-sin, sin], axis=-1)
    reps = LANES // d
    return jnp.tile(c, (1, reps)), jnp.tile(s, (1, reps))


def _mix_weights(sgu_w, sgu_b, n, group_dim):
    depth = sgu_w.shape[0]
    p = jnp.arange(n)
    mask = (p[None, :] // CHUNK) <= (p[:, None] // CHUNK)
    w = jnp.where(mask[None, None], sgu_w[:, :, :n, :n], 0.0)
    reps = SGU_CHUNK // n
    eye = jnp.eye(reps, dtype=w.dtype)
    wbd = jnp.einsum('ab,lgij->lgaibj', eye, w).reshape(depth, SGU_GROUPS, SGU_CHUNK, SGU_CHUNK)
    b = jnp.tile(sgu_b[:, :, :n], (1, 1, reps))
    bfull = jnp.repeat(jnp.swapaxes(b, 1, 2), group_dim, axis=2)
    return wbd.astype(BF16), bfull


def _in_weights(w_in, d_model):
    kvw = N_KV_HEADS * HEAD_DIM
    w = w_in
    c_ki = d_model + 2 * kvw + N_IDX_HEADS * IDX_DIM
    c_wi = c_ki + IDX_DIM
    c_u = c_wi + N_IDX_HEADS
    wki, wwi = w[:, :, c_ki:c_wi], w[:, :, c_wi:c_u]
    wwi_pad = jnp.pad(wwi, ((0, 0), (0, 0), (0, LANES - N_IDX_HEADS)))
    wa = jnp.concatenate([w[:, :, :c_ki], wki, wki, wwi_pad, w[:, :, c_u:]], axis=2)
    wv_t = jnp.swapaxes(w[:, :, d_model + kvw:d_model + 2 * kvw], 1, 2)
    wwi_t = jnp.pad(jnp.swapaxes(wwi, 1, 2), ((0, 0), (0, 16 - N_IDX_HEADS), (0, 0)))
    return wa.astype(BF16), jnp.concatenate([wv_t, wwi_t], axis=1).astype(BF16)


def kernel(x_prompt, x_sample, cache_k, cache_v, cache_kidx, norm1_g, w_in, ln_v_g, ln_v_b, sgu_w, sgu_b, w_out, norm2_g, w_gate, w_up, w_down, final_norm_g):
    B, S, D = x_prompt.shape
    NB, TN, _ = x_sample.shape
    depth, _, P = cache_k.shape[:3]
    assert D == N_HEADS * HEAD_DIM and SGU_CHUNK % TN == 0 and S % SGU_CHUNK == 0
    k_top_p = min(TOPK_MAX, S // 4)
    k_top_s = min(TOPK_MAX, (P + TN) // 4)
    group_dim = D // SGU_GROUPS

    assert S % TOKEN_TILE == 0 and TOKEN_TILE % TN == 0
    pos_p = jnp.arange(S)
    pos_s = jnp.tile(P + jnp.arange(TN), TOKEN_TILE // TN)
    tab_p = _rope_tables(pos_p, HEAD_DIM) + _rope_tables(pos_p, IDX_DIM)
    tab_s = _rope_tables(pos_s, HEAD_DIM) + _rope_tables(pos_s, IDX_DIM)

    wa, wb = _in_weights(w_in, D)
    wo, wg, wu, wd = (t.astype(BF16) for t in (w_out, w_gate, w_up, w_down))
    vec = lambda v: v.reshape(depth, 1, -1)
    g1, g2, lng, lnb = vec(norm1_g), vec(norm2_g), vec(ln_v_g), vec(ln_v_b)
    gf = final_norm_g.reshape(1, -1)
    wmix_p, bmix_p = _mix_weights(sgu_w, sgu_b, SGU_CHUNK, group_dim)
    wmix_s, bmix_s = _mix_weights(sgu_w, sgu_b, TN, group_dim)
    ck = cache_k.reshape(depth, NB, P * N_KV_HEADS, HEAD_DIM)
    cv = cache_v.reshape(depth, NB, P * N_KV_HEADS, HEAD_DIM)

    xp = x_prompt.reshape(B * S, D)
    xs = x_sample.reshape(NB * TN, D)
    def new_buffers(tokens):
        kv = (depth, N_KV_HEADS * tokens, HEAD_DIM)
        return jnp.zeros(kv, F32), jnp.zeros(kv, F32), jnp.zeros((depth, tokens, IDX_DIM), F32)

    new_p, new_s = new_buffers(B * S), new_buffers(NB * TN)
    sgu_v = []
    yp = ys = None
    for l in range(depth):
        final = l == depth - 1

        (q, kf, kb, vf, _, vt, qi, kif, kze, kzo, _, wit, u, vn, sga, sgb) = _inproj(
            xp, l, g1, wa, wb, tab_p, lng, lnb, new_p)
        new_p = (kf, vf, kif)
        a = _attn_prompt(q, qi, wit, kze, kzo, kb, vt, B, S, k_top_p)
        xp, yp = _outffn(xp, a, sga, sgb, u, vn, wmix_p, bmix_p, l, wo, g2, wg, wu, wd, gf, final)

        (q, kf, kb, vf, vb, _, qi, kif, _, _, wi, _, u, vn, sga, sgb) = _inproj(
            xs, l, g1, wa, wb, tab_s, lng, lnb, new_s)
        new_s = (kf, vf, kif)
        qi4 = qi.reshape(NB, TN, N_IDX_HEADS, IDX_DIM).transpose(0, 2, 1, 3)
        a = _attn_sample(q, qi4, wi, ck, cv, cache_kidx, kb, vb, kif, l, k_top_s)
        xs, ys = _outffn(xs, a, sga, sgb, u, vn, wmix_s, bmix_s, l, wo, g2, wg, wu, wd, gf, final)
        sgu_v.append(vn.reshape(NB, TN, D))

    kv_p = (depth, B, S, N_KV_HEADS, HEAD_DIM)
    kv_s = (depth, NB, TN, N_KV_HEADS, HEAD_DIM)
    return (yp.reshape(B, S, D), ys.reshape(NB, TN, D),
            new_p[0].reshape(kv_p), new_p[1].reshape(kv_p), new_p[2].reshape(depth, B, S, IDX_DIM),
            new_s[0].reshape(kv_s), new_s[1].reshape(kv_s), new_s[2].reshape(depth, NB, TN, IDX_DIM),
            jnp.stack(sgu_v))
```
